```python
import jax
import jax.numpy as jnp
from jax import lax
import numpy as np

D_MODEL = 1024
BATCH = 32
SEQ = 256
DEPTH = 2
DEC_BATCH = 2
DEC_SEQ = 2048
PAST_LEN = 512

GRID_W = 64
N_EVEN = (DEPTH + 1) // 2
N_ODD = DEPTH // 2
EPS = 1e-6

LRU_WIDTH = 3 * D_MODEL // 4
LRU_HEADS = 12
LRU_HEAD_DIM = LRU_WIDTH // LRU_HEADS
LRU_CONV = 4
LRU_CONV_LEFT = 2
LRU_CONV_RIGHT = LRU_CONV - 1 - LRU_CONV_LEFT
LRU_C = 8.0
FNET_WIDTH = D_MODEL // 4
FNET_GROUPS = 4
FNET_GROUP_DIM = FNET_WIDTH // FNET_GROUPS
IN0 = 2 * LRU_WIDTH + FNET_WIDTH
MIX0 = LRU_WIDTH + FNET_WIDTH

SGU_WIDTH = D_MODEL // 2
SGU_GROUPS = 4
SGU_GROUP_DIM = SGU_WIDTH // SGU_GROUPS
CHUNK = 128
CONV_WIDTH = D_MODEL // 2
CONV_K = 31
CONV_PAD = CONV_K // 2
IN1 = 2 * SGU_WIDTH + 2 * CONV_WIDTH
MIX1 = SGU_WIDTH + CONV_WIDTH

N_EXPERTS = 32
TOP_K = 4
D_FF = D_MODEL
SWIGLU_ALPHA = 1.702
SWIGLU_LIMIT = 7.0
MOE_BLOCK = 256

kernel_name = "hybrid_rglru_fnet_sgu_conformer_moe_diffusion_step"


def _rmsnorm(x, g):
    xf = x.astype(jnp.float32)
    y = xf * lax.rsqrt(jnp.mean(xf * xf, axis=-1, keepdims=True) + EPS)
    return (y * g.astype(jnp.float32)).astype(x.dtype)


def _layernorm(x, g, b):
    xf = x.astype(jnp.float32)
    xc = xf - jnp.mean(xf, axis=-1, keepdims=True)
    var = jnp.mean(xc * xc, axis=-1, keepdims=True)
    return (xc * lax.rsqrt(var + EPS) * g.astype(jnp.float32) + b.astype(jnp.float32)).astype(x.dtype)


def _adaln(cond, w, b):
    m = jax.nn.silu(cond.astype(jnp.float32)) @ w.astype(jnp.float32) + b.astype(jnp.float32)
    return m.reshape(m.shape[:-1] + (1, 6, D_MODEL))


def _depthwise_conv(x, w, b, pad_left, pad_right):
    y = lax.conv_general_dilated(
        x, w[:, None, :].astype(x.dtype), (1,), [(pad_left, pad_right)],
        dimension_numbers=("NWC", "WIO", "NWC"), feature_group_count=x.shape[-1])
    return y + b.astype(x.dtype)


def _linear_scan(a, bx, h0, reverse):
    def combine(left, right):
        a_l, b_l = left
        a_r, b_r = right
        return a_l * a_r, a_r * b_l + b_r
    a_cum, b_cum = lax.associative_scan(combine, (a, bx), reverse=reverse, axis=1)
    return a_cum * h0[:, None, :] + b_cum


def _rglru_direction(xc, w_r, b_r, w_i, b_i, lam, h0, reverse):
    bsz, t_len, _ = xc.shape
    xf = xc.astype(jnp.float32)
    xh = xf.reshape(bsz, t_len, LRU_HEADS, LRU_HEAD_DIM)
    pre_r = jnp.einsum("bthi,hij->bthj", xh, w_r.astype(jnp.float32)).reshape(bsz, t_len, LRU_WIDTH)
    pre_i = jnp.einsum("bthi,hij->bthj", xh, w_i.astype(jnp.float32)).reshape(bsz, t_len, LRU_WIDTH)
    r = jax.nn.sigmoid(pre_r + b_r.astype(jnp.float32))
    i = jax.nn.sigmoid(pre_i + b_i.astype(jnp.float32))
    log_a = -LRU_C * r * jax.nn.softplus(-lam.astype(jnp.float32))
    a = jnp.exp(log_a)
    bx = jnp.sqrt(-jnp.expm1(2.0 * log_a)) * (i * xf)
    return _linear_scan(a, bx, h0, reverse)


def _mixer_rglru_fourier(hn, w_in, conv_w, conv_b, w_r, b_r, w_i, b_i, lam, w_out, h0_f, h0_b):
    bsz, t_len, _ = hn.shape
    proj = hn @ w_in
    x_rec = proj[..., :LRU_WIDTH]
    x_gate = proj[..., LRU_WIDTH:2 * LRU_WIDTH]
    x_four = proj[..., 2 * LRU_WIDTH:]
    xc = _depthwise_conv(x_rec, conv_w, conv_b, LRU_CONV_LEFT, LRU_CONV_RIGHT)
    h_f = _rglru_direction(xc, w_r[0], b_r[0], w_i[0], b_i[0], lam[0], h0_f, False)
    h_b = _rglru_direction(xc, w_r[1], b_r[1], w_i[1], b_i[1], lam[1], h0_b, True)
    y_rec = (h_f + h_b).astype(hn.dtype) * jax.nn.gelu(x_gate)
    z = x_four.astype(jnp.float32).reshape(bsz, t_len, FNET_GROUPS, FNET_GROUP_DIM)
    y_four = jnp.real(jnp.fft.fft2(z, axes=(1, 3), norm="ortho"))
    y_four = y_four.reshape(bsz, t_len, FNET_WIDTH).astype(hn.dtype)
    out = jnp.concatenate([y_rec, y_four], axis=-1) @ w_out
    return out, h_f, h_b


def _mixer_sgu_conformer(hn, w_in, ln_g, ln_b, w_s, b_s, dw_w, dw_b, cln_g, cln_b, w_out):
    bsz, t_len, _ = hn.shape
    proj = hn @ w_in
    z = jax.nn.gelu(proj[..., :2 * SGU_WIDTH])
    u = z[..., :SGU_WIDTH]
    v = _layernorm(z[..., SGU_WIDTH:], ln_g, ln_b)
    vb = v.reshape(bsz, t_len // CHUNK, CHUNK, SGU_GROUPS, SGU_GROUP_DIM)
    s = jnp.einsum("gpq,bnqgc->bnpgc", w_s.astype(vb.dtype), vb) + b_s.T.astype(vb.dtype)[:, :, None]
    y_sgu = u * s.reshape(bsz, t_len, SGU_WIDTH)
    conv_in = proj[..., 2 * SGU_WIDTH:]
    glu = conv_in[..., :CONV_WIDTH] * jax.nn.sigmoid(conv_in[..., CONV_WIDTH:])
    d = _depthwise_conv(glu, dw_w, dw_b, CONV_PAD, CONV_PAD)
    y_conv = jax.nn.silu(_layernorm(d, cln_g, cln_b))
    return jnp.concatenate([y_sgu, y_conv], axis=-1) @ w_out


def _moe(x, w_router, b_router, w1, b1, w2, b2):
    lead = x.shape[:-1]
    xt = x.reshape(-1, D_MODEL)
    n_tok = xt.shape[0]
    logits = xt.astype(jnp.float32) @ w_router.astype(jnp.float32) + b_router.astype(jnp.float32)
    top_logit, top_e = lax.top_k(logits, TOP_K)
    gates = jax.nn.softmax(top_logit, axis=-1)
    flat_e = top_e.reshape(-1)
    n_assign = n_tok * TOP_K
    order = jnp.argsort(flat_e)
    sorted_e = flat_e[order]
    sorted_tok = (order // TOP_K).astype(jnp.int32)
    sorted_gate = gates.reshape(-1)[order]
    counts = jnp.bincount(flat_e, length=N_EXPERTS)
    padded = (counts + MOE_BLOCK - 1) // MOE_BLOCK * MOE_BLOCK
    pad_end = jnp.cumsum(padded)
    pad_start = pad_end - padded
    start = jnp.cumsum(counts) - counts
    dest = pad_start[sorted_e] + jnp.arange(n_assign) - start[sorted_e]
    n_blocks = -(-n_assign // MOE_BLOCK) + N_EXPERTS
    n_slots = n_blocks * MOE_BLOCK
    slot_tok = jnp.full((n_slots,), n_tok, jnp.int32).at[dest].set(sorted_tok)
    slot_gate = jnp.zeros((n_slots,), jnp.float32).at[dest].set(sorted_gate)
    block_e = jnp.minimum(jnp.searchsorted(pad_end, jnp.arange(n_blocks) * MOE_BLOCK, side="right"),
                          N_EXPERTS - 1)
    x_pad = jnp.concatenate([xt, jnp.zeros((1, D_MODEL), xt.dtype)], axis=0)
    xb = x_pad[slot_tok].reshape(n_blocks, MOE_BLOCK, D_MODEL)

    def expert_block(args):
        xblk, e = args
        h = xblk @ w1[e] + b1[e]
        h_glu = jnp.minimum(h[:, 0::2], SWIGLU_LIMIT)
        h_lin = jnp.clip(h[:, 1::2], -SWIGLU_LIMIT, SWIGLU_LIMIT)
        act = h_glu * jax.nn.sigmoid(SWIGLU_ALPHA * h_glu) * (h_lin + 1.0)
        return act @ w2[e] + b2[e]

    yb = lax.map(expert_block, (xb, block_e))
    y_slots = yb.reshape(n_slots, D_MODEL) * slot_gate[:, None].astype(yb.dtype)
    y = jnp.zeros((n_tok + 1, D_MODEL), yb.dtype).at[slot_tok].add(y_slots)[:n_tok]
    return y.reshape(lead + (D_MODEL,)).astype(x.dtype)


def setup_inputs(seed: int = 0) -> dict:
    key = jax.random.key(seed)
    ks = iter(jax.random.split(key, 40))

    def nrm(shape, scale):
        return jax.random.normal(next(ks), shape, jnp.float32) * scale

    u = jax.random.uniform(next(ks), (N_EVEN, 2, LRU_WIDTH), jnp.float32, 0.9, 0.999)
    a0 = u ** (1.0 / LRU_C)
    lru_lambda = jnp.log(a0) - jnp.log1p(-a0)
    return {
        "x_prompt": nrm((BATCH, SEQ, D_MODEL), 1.0),
        "x_sample": nrm((DEC_BATCH, DEC_SEQ, D_MODEL), 1.0),
        "state_rglru": nrm((DEC_BATCH, N_EVEN, 2, LRU_WIDTH), 0.5),
        "c": nrm((DEC_BATCH, D_MODEL), 1.0),
        "c_ctx": nrm((D_MODEL,), 1.0),
        "norm_mix_g": 1.0 + nrm((DEPTH, D_MODEL), 0.02),
        "norm_ffn_g": 1.0 + nrm((DEPTH, D_MODEL), 0.02),
        "w_mod": nrm((DEPTH, D_MODEL, 6 * D_MODEL), 0.5 * D_MODEL ** -0.5),
        "b_mod": nrm((DEPTH, 6 * D_MODEL), 0.02),
        "w_in0": nrm((N_EVEN, D_MODEL, IN0), D_MODEL ** -0.5),
        "lru_conv_w": nrm((N_EVEN, LRU_CONV, LRU_WIDTH), LRU_CONV ** -0.5),
        "lru_conv_b": nrm((N_EVEN, LRU_WIDTH), 0.02),
        "lru_w_r": nrm((N_EVEN, 2, LRU_HEADS, LRU_HEAD_DIM, LRU_HEAD_DIM), LRU_HEAD_DIM ** -0.5),
        "lru_b_r": nrm((N_EVEN, 2, LRU_WIDTH), 0.02),
        "lru_w_i": nrm((N_EVEN, 2, LRU_HEADS, LRU_HEAD_DIM, LRU_HEAD_DIM), LRU_HEAD_DIM ** -0.5),
        "lru_b_i": nrm((N_EVEN, 2, LRU_WIDTH), 0.02),
        "lru_lambda": lru_lambda,
        "w_out0": nrm((N_EVEN, MIX0, D_MODEL), MIX0 ** -0.5),
        "w_in1": nrm((N_ODD, D_MODEL, IN1), D_MODEL ** -0.5),
        "sgu_ln_g": 1.0 + nrm((N_ODD, SGU_WIDTH), 0.02),
        "sgu_ln_b": nrm((N_ODD, SGU_WIDTH), 0.02),
        "sgu_w_s": nrm((N_ODD, SGU_GROUPS, CHUNK, CHUNK), CHUNK ** -0.5),
        "sgu_b_s": 1.0 + nrm((N_ODD, SGU_GROUPS, CHUNK), 0.02),
        "conv_dw_w": nrm((N_ODD, CONV_K, CONV_WIDTH), CONV_K ** -0.5),
        "conv_dw_b": nrm((N_ODD, CONV_WIDTH), 0.02),
        "conv_ln_g": 1.0 + nrm((N_ODD, CONV_WIDTH), 0.02),
        "conv_ln_b": nrm((N_ODD, CONV_WIDTH), 0.02),
        "w_out1": nrm((N_ODD, MIX1, D_MODEL), MIX1 ** -0.5),
        "w_router": nrm((DEPTH, D_MODEL, N_EXPERTS), D_MODEL ** -0.5),
        "b_router": nrm((DEPTH, N_EXPERTS), 0.01),
        "w1": nrm((DEPTH, N_EXPERTS, D_MODEL, 2 * D_FF), D_MODEL ** -0.5),
        "b1": nrm((DEPTH, N_EXPERTS, 2 * D_FF), 0.02),
        "w2": nrm((DEPTH, N_EXPERTS, D_FF, D_MODEL), D_FF ** -0.5),
        "b2": nrm((DEPTH, N_EXPERTS, D_MODEL), 0.02),
        "final_norm_g": 1.0 + nrm((D_MODEL,), 0.02),
    }


def reference(x_prompt, x_sample, state_rglru, c, c_ctx, norm_mix_g, norm_ffn_g, w_mod, b_mod,
              w_in0, lru_conv_w, lru_conv_b, lru_w_r, lru_b_r, lru_w_i, lru_b_i, lru_lambda, w_out0,
              w_in1, sgu_ln_g, sgu_ln_b, sgu_w_s, sgu_b_s, conv_dw_w, conv_dw_b, conv_ln_g, conv_ln_b,
              w_out1, w_router, b_router, w1, b1, w2, b2, final_norm_g):
    h_p = x_prompt
    h_s = x_sample
    ctx_states = []
    for layer in range(DEPTH):
        m_p = _adaln(c_ctx, w_mod[layer], b_mod[layer]).astype(h_p.dtype)
        m_s = _adaln(c, w_mod[layer], b_mod[layer]).astype(h_s.dtype)
        hn_p = _rmsnorm(h_p, norm_mix_g[layer]) * (1.0 + m_p[..., 1, :]) + m_p[..., 0, :]
        hn_s = _rmsnorm(h_s, norm_mix_g[layer]) * (1.0 + m_s[..., 1, :]) + m_s[..., 0, :]
        j = layer // 2
        if layer % 2 == 0:
            params = (w_in0[j], lru_conv_w[j], lru_conv_b[j], lru_w_r[j], lru_b_r[j],
                      lru_w_i[j], lru_b_i[j], lru_lambda[j], w_out0[j])
            zero = jnp.zeros((h_p.shape[0], LRU_WIDTH), jnp.float32)
            mix_p, hf_p, hb_p = _mixer_rglru_fourier(hn_p, *params, zero, zero)
            ctx_states.append(jnp.stack([hf_p[:, -1], hb_p[:, 0]], axis=1))
            st = state_rglru[:, j].astype(jnp.float32)
            mix_s, _, _ = _mixer_rglru_fourier(hn_s, *params, st[:, 0], st[:, 1])
        else:
            params = (w_in1[j], sgu_ln_g[j], sgu_ln_b[j], sgu_w_s[j], sgu_b_s[j], conv_dw_w[j],
                      conv_dw_b[j], conv_ln_g[j], conv_ln_b[j], w_out1[j])
            mix_p = _mixer_sgu_conformer(hn_p, *params)
            mix_s = _mixer_sgu_conformer(hn_s, *params)
        h_p = h_p + m_p[..., 2, :] * mix_p
        h_s = h_s + m_s[..., 2, :] * mix_s
        moe_params = (w_router[layer], b_router[layer], w1[layer], b1[layer], w2[layer], b2[layer])
        fn_p = _rmsnorm(h_p, norm_ffn_g[layer]) * (1.0 + m_p[..., 4, :]) + m_p[..., 3, :]
        fn_s = _rmsnorm(h_s, norm_ffn_g[layer]) * (1.0 + m_s[..., 4, :]) + m_s[..., 3, :]
        h_p = h_p + m_p[..., 5, :] * _moe(fn_p, *moe_params)
        h_s = h_s + m_s[..., 5, :] * _moe(fn_s, *moe_params)
    y_prompt = _rmsnorm(h_p, final_norm_g)
    y_sample = _rmsnorm(h_s, final_norm_g)
    new_state_rglru = jnp.stack(ctx_states, axis=1).astype(x_prompt.dtype)
    return (y_prompt, y_sample, new_state_rglru)
```

```python
import functools

import numpy as np
import jax
import jax.numpy as jnp
from jax import lax
from jax.experimental import pallas as pl
from jax.experimental.pallas import tpu as pltpu

F32 = jnp.float32
BF16 = jnp.bfloat16

D = 1024
N_PROMPT_SEQ = 32
T_PROMPT = 256
N_SAMPLE_SEQ = 2
T_SAMPLE = 2048
N_P = N_PROMPT_SEQ * T_PROMPT
N_S = N_SAMPLE_SEQ * T_SAMPLE
N = N_P + N_S
EPS = 1e-6

TILE = 256
N_TILES = N // TILE
P_TILES = N_P // TILE
S_TILES = T_SAMPLE // TILE
TM_PROJ = 512
N_MOD = 8

LRU_W = 768
LRU_HEADS = 12
LRU_HD = 64
LRU_K = 4
LRU_LEFT = 2
LRU_C = 8.0
LRU_BLK = 256
FN_W = 256
FN_G = 4
FN_GD = 64
IN0 = 2 * LRU_W + FN_W

SGU_W = 512
SGU_G = 4
CHUNK = 128
CONV_W = 512
CONV_K = 31
CONV_PAD = 15
CONV_HALO = 16
IN1 = 2 * SGU_W + 2 * CONV_W

N_EXP = 32
TOP_K = 4
D_FF = 1024
ALPHA = 1.702
LIMIT = 7.0
MOE_BLOCK = 256
N_ASSIGN = N * TOP_K
N_BLOCKS = N_ASSIGN // MOE_BLOCK + N_EXP
N_SLOTS = N_BLOCKS * MOE_BLOCK

VMEM_LIMIT = 56 * 1024 * 1024


def _cp(sem, vmem=None):
    return pltpu.CompilerParams(dimension_semantics=sem, vmem_limit_bytes=vmem)


def _dot(a, b):
    return jnp.dot(a, b, preferred_element_type=F32)


def _split(x):
    hi = x.astype(BF16)
    lo = (x - hi.astype(F32)).astype(BF16)
    return hi, lo


def _dot3(a, b):
    ah, al = _split(a)
    bh, bl = _split(b)
    return _dot(ah, bh) + _dot(al, bh) + _dot(ah, bl)


def _dot3_nt(a, b):
    dn = (((1,), (1,)), ((), ()))
    d = lambda x, y: lax.dot_general(x, y, dn, preferred_element_type=F32)
    ah, al = _split(a)
    bh, bl = _split(b)
    return d(ah, bh) + d(al, bh) + d(ah, bl)


def _sigmoid(x):
    return 1.0 / (1.0 + jnp.exp(-x))


def _gelu(x):
    return 0.5 * x * (1.0 + jnp.tanh(0.7978845608028654 * (x + 0.044715 * (x * x * x))))


def _rms_mod(x, g, shift, scale):
    ms = jnp.mean(x * x, axis=-1, keepdims=True)
    y = x * lax.rsqrt(ms + EPS) * g
    return y * (1.0 + scale) + shift


def _layernorm(x, g, b):
    xc = x - jnp.mean(x, axis=-1, keepdims=True)
    var = jnp.mean(xc * xc, axis=-1, keepdims=True)
    return xc * lax.rsqrt(var + EPS) * g + b


def _tile_mod_row(r):
    return jnp.where(r < P_TILES, 0, 1 + (r - P_TILES) // S_TILES)


def _tile_is_seq_start(r):
    return (r < P_TILES) | ((r - P_TILES) % S_TILES == 0)


def _tile_is_seq_end(r):
    return (r < P_TILES) | ((r - P_TILES) % S_TILES == S_TILES - 1)


MOD_TN = 512


def _adaln_kernel(cond_ref, w_ref, b_ref, o_ref):
    cond = cond_ref[...]
    s = cond * _sigmoid(cond)
    o_ref[...] = _dot3(s, w_ref[...]) + b_ref[...]


def _adaln(cond8, w_mod, b_mod):
    depth = w_mod.shape[0]
    out = pl.pallas_call(
        _adaln_kernel,
        out_shape=jax.ShapeDtypeStruct((depth, N_MOD, 6 * D), F32),
        grid=(depth, 6 * D // MOD_TN),
        in_specs=[
            pl.BlockSpec((N_MOD, D), lambda l, j: (0, 0)),
            pl.BlockSpec((None, D, MOD_TN), lambda l, j: (l, 0, j)),
            pl.BlockSpec((None, 1, MOD_TN), lambda l, j: (l, 0, j)),
        ],
        out_specs=pl.BlockSpec((None, N_MOD, MOD_TN), lambda l, j: (l, 0, j)),
        compiler_params=_cp(("arbitrary", "arbitrary")),
        name="adaln",
    )(cond8, w_mod, b_mod.reshape(depth, 1, 6 * D))
    return out.reshape(depth, N_MOD, 6, D)


def _inproj_kernel(x_ref, g_ref, m_ref, w_ref, o_ref):
    hn = _rms_mod(x_ref[...], g_ref[...], m_ref[0:1, :], m_ref[1:2, :])
    o_ref[...] = _dot(hn.astype(BF16), w_ref[...])


def _inproj(h, g, mod, w_bf16):
    n_out = w_bf16.shape[1]
    p_steps = N_P // TM_PROJ
    s_steps = T_SAMPLE // TM_PROJ

    def mod_row(i):
        return jnp.where(i < p_steps, 0, 1 + (i - p_steps) // s_steps)

    return pl.pallas_call(
        _inproj_kernel,
        out_shape=jax.ShapeDtypeStruct((N, n_out), F32),
        grid=(N // TM_PROJ,),
        in_specs=[
            pl.BlockSpec((TM_PROJ, D), lambda i: (i, 0)),
            pl.BlockSpec((1, D), lambda i: (0, 0)),
            pl.BlockSpec((None, 6, D), lambda i: (mod_row(i), 0, 0)),
            pl.BlockSpec((D, n_out), lambda i: (0, 0)),
        ],
        out_specs=pl.BlockSpec((TM_PROJ, n_out), lambda i: (i, 0)),
        compiler_params=_cp(("arbitrary",), VMEM_LIMIT),
        name="inproj",
    )(h, g.reshape(1, D), mod, w_bf16)


LRU_HALO = 8


def _lru_tile(d, s):
    return jnp.where(d == 0, s, N_TILES - 1 - s)


def _lru_kernel(x_ref, prev_ref, next_ref, cw_ref, cb_ref, wg_ref, br_ref, bi_ref, lam_ref, h0_ref,
                o_ref, ext_ref, a_ref, carry_ref):
    d = pl.program_id(0)
    r = _lru_tile(d, pl.program_id(1))
    start = _tile_is_seq_start(r)
    end = _tile_is_seq_end(r)

    ext_ref[0:LRU_HALO, :] = jnp.where(start, 0.0, prev_ref[...])
    ext_ref[LRU_HALO:LRU_HALO + TILE, :] = x_ref[...]
    ext_ref[LRU_HALO + TILE:, :] = jnp.where(end, 0.0, next_ref[...])
    xc = cb_ref[...] + jnp.zeros((TILE, LRU_W), F32)
    for k in range(LRU_K):
        off = LRU_HALO - LRU_LEFT + k
        xc = xc + ext_ref[off:off + TILE, :] * cw_ref[k:k + 1, :]

    xcb = xc.astype(BF16)
    pre_r, pre_i = [], []
    for blk in range(LRU_W // LRU_BLK):
        g = _dot(xcb[:, blk * LRU_BLK:(blk + 1) * LRU_BLK], wg_ref[blk])
        pre_r.append(g[:, :LRU_BLK])
        pre_i.append(g[:, LRU_BLK:])
    gate_r = _sigmoid(jnp.concatenate(pre_r, axis=1) + br_ref[...])
    gate_i = _sigmoid(jnp.concatenate(pre_i, axis=1) + bi_ref[...])
    neg_lam = -lam_ref[...]
    softplus = jnp.maximum(neg_lam, 0.0) + jnp.log1p(jnp.exp(-jnp.abs(neg_lam)))
    log_a = (-LRU_C) * gate_r * softplus
    a = jnp.exp(log_a)
    a_ref[...] = a
    o_ref[...] = jnp.sqrt(-jnp.tanh(log_a) * (a * a + 1.0)) * (gate_i * xc)

    fresh = jnp.where(d == 0, start, end)
    h_init = jnp.where(fresh, h0_ref[pl.ds(_tile_mod_row(r), 1), :], carry_ref[...])

    def step(t, h):
        tt = jnp.where(d == 0, t, TILE - 1 - t)
        h = a_ref[pl.ds(tt, 1), :] * h + o_ref[pl.ds(tt, 1), :]
        o_ref[pl.ds(tt, 1), :] = h
        return h

    carry_ref[...] = lax.fori_loop(0, TILE, step, h_init, unroll=8)


def _lru(proj0, conv_w, conv_b, wg, b_r, b_i, lam, h0):
    n_halo_blocks = N // LRU_HALO
    per_tile = TILE // LRU_HALO
    tile = lambda d, s: _lru_tile(d, s)
    return pl.pallas_call(
        _lru_kernel,
        out_shape=jax.ShapeDtypeStruct((2, N, LRU_W), F32),
        grid=(2, N_TILES),
        in_specs=[
            pl.BlockSpec((TILE, LRU_W), lambda d, s: (tile(d, s), 0)),
            pl.BlockSpec((LRU_HALO, LRU_W), lambda d, s: (jnp.maximum(tile(d, s) * per_tile - 1, 0), 0)),
            pl.BlockSpec((LRU_HALO, LRU_W),
                         lambda d, s: (jnp.minimum((tile(d, s) + 1) * per_tile, n_halo_blocks - 1), 0)),
            pl.BlockSpec((LRU_K, LRU_W), lambda d, s: (0, 0)),
            pl.BlockSpec((1, LRU_W), lambda d, s: (0, 0)),
            pl.BlockSpec((None, LRU_W // LRU_BLK, LRU_BLK, 2 * LRU_BLK), lambda d, s: (d, 0, 0, 0)),
            pl.BlockSpec((None, 1, LRU_W), lambda d, s: (d, 0, 0)),
            pl.BlockSpec((None, 1, LRU_W), lambda d, s: (d, 0, 0)),
            pl.BlockSpec((None, 1, LRU_W), lambda d, s: (d, 0, 0)),
            pl.BlockSpec((None, N_MOD, LRU_W), lambda d, s: (d, 0, 0)),
        ],
        out_specs=pl.BlockSpec((None, TILE, LRU_W), lambda d, s: (d, tile(d, s), 0)),
        scratch_shapes=[
            pltpu.VMEM((TILE + 2 * LRU_HALO, LRU_W), F32),
            pltpu.VMEM((TILE, LRU_W), F32),
            pltpu.VMEM((1, LRU_W), F32),
        ],
        compiler_params=_cp(("arbitrary", "arbitrary"), VMEM_LIMIT),
        name="rglru_scan",
    )(proj0, proj0, proj0, conv_w, conv_b.reshape(1, LRU_W), wg,
      b_r.reshape(2, 1, LRU_W), b_i.reshape(2, 1, LRU_W), lam.reshape(2, 1, LRU_W), h0)


def _dft_tables(n, scale):
    k = np.arange(n, dtype=np.int64)
    ang = 2.0 * np.pi * ((k[:, None] * k[None, :]) % n).astype(np.float64) / n
    return np.cos(ang) * scale, np.sin(ang) * scale


def _channel_tables():
    c, s = _dft_tables(FN_GD, FN_GD ** -0.5)
    eye = np.eye(FN_G)
    return (jnp.asarray(np.kron(eye, c), BF16), jnp.asarray(np.kron(eye, s), BF16))


def _time_tables(t_len):
    c, s = _dft_tables(t_len, t_len ** -0.5)
    return jnp.asarray(c, BF16), jnp.asarray(s, BF16)


def _fourier_prompt_kernel(z_ref, cc_ref, sc_ref, ct_ref, st_ref, o_ref):
    z = z_ref[...].astype(BF16)
    zc = _dot(z, cc_ref[...]).astype(BF16)
    zs = _dot(z, sc_ref[...]).astype(BF16)
    o_ref[...] = _dot(ct_ref[...], zc) - _dot(st_ref[...], zs)


def _fourier_prompt(proj0):
    cc, sc = _channel_tables()
    ct, st = _time_tables(T_PROMPT)
    const = lambda b: (0, 0)
    return pl.pallas_call(
        _fourier_prompt_kernel,
        out_shape=jax.ShapeDtypeStruct((N_P, FN_W), F32),
        grid=(N_PROMPT_SEQ,),
        in_specs=[
            pl.BlockSpec((T_PROMPT, FN_W), lambda b: (b, 2 * LRU_W // FN_W)),
            pl.BlockSpec((FN_W, FN_W), const),
            pl.BlockSpec((FN_W, FN_W), const),
            pl.BlockSpec((T_PROMPT, T_PROMPT), const),
            pl.BlockSpec((T_PROMPT, T_PROMPT), const),
        ],
        out_specs=pl.BlockSpec((T_PROMPT, FN_W), lambda b: (b, 0)),
        compiler_params=_cp(("arbitrary",)),
        name="fourier_prompt",
    )(proj0, cc, sc, ct, st)


def _fourier_sample_kernel(z_ref, cc_ref, sc_ref, ct_ref, st_ref, o_ref, zc_ref, zs_ref):
    @pl.when(pl.program_id(1) == 0)
    def _():
        z = z_ref[...].astype(BF16)
        zc_ref[...] = _dot(z, cc_ref[...]).astype(BF16)
        zs_ref[...] = _dot(z, sc_ref[...]).astype(BF16)

    o_ref[...] = _dot(ct_ref[...], zc_ref[...]) - _dot(st_ref[...], zs_ref[...])


def _fourier_sample(proj0):
    cc, sc = _channel_tables()
    ct, st = _time_tables(T_SAMPLE)
    const = lambda b, i: (0, 0)
    first_seq_block = N_P // T_SAMPLE
    return pl.pallas_call(
        _fourier_sample_kernel,
        out_shape=jax.ShapeDtypeStruct((N_S, FN_W), F32),
        grid=(N_SAMPLE_SEQ, S_TILES),
        in_specs=[
            pl.BlockSpec((T_SAMPLE, FN_W), lambda b, i: (first_seq_block + b, 2 * LRU_W // FN_W)),
            pl.BlockSpec((FN_W, FN_W), const),
            pl.BlockSpec((FN_W, FN_W), const),
            pl.BlockSpec((TILE, T_SAMPLE), lambda b, i: (i, 0)),
            pl.BlockSpec((TILE, T_SAMPLE), lambda b, i: (i, 0)),
        ],
        out_specs=pl.BlockSpec((TILE, FN_W), lambda b, i: (b * S_TILES + i, 0)),
        scratch_shapes=[pltpu.VMEM((T_SAMPLE, FN_W), BF16), pltpu.VMEM((T_SAMPLE, FN_W), BF16)],
        compiler_params=_cp(("arbitrary", "arbitrary"), VMEM_LIMIT),
        name="fourier_sample",
    )(proj0, cc, sc, ct, st)


def _epilogue(mix, h_ref, m_ref, gf_ref, wr_ref, brt_ref, hout_ref, fn_ref, tope_ref, gate_ref):
    h_new = h_ref[...] + m_ref[2:3, :] * mix
    hout_ref[...] = h_new
    fn = _rms_mod(h_new, gf_ref[...], m_ref[3:4, :], m_ref[4:5, :])
    fn_ref[...] = fn
    logits = _dot3_nt(wr_ref[...], fn) + brt_ref[...]
    iota = lax.broadcasted_iota(jnp.int32, logits.shape, 0)
    vals, idxs = [], []
    for _ in range(TOP_K):
        m = jnp.max(logits, axis=0, keepdims=True)
        idx = jnp.min(jnp.where(logits == m, iota, N_EXP), axis=0, keepdims=True)
        vals.append(m)
        idxs.append(idx)
        logits = jnp.where(iota == idx, -jnp.inf, logits)
    exps = [jnp.exp(v - vals[0]) for v in vals]
    denom = exps[0] + exps[1] + exps[2] + exps[3]
    for k in range(TOP_K):
        tope_ref[k:k + 1, :] = idxs[k]
        gate_ref[k:k + 1, :] = exps[k] / denom


_EPI_OUT_SHAPES = (
    jax.ShapeDtypeStruct((N, D), F32),
    jax.ShapeDtypeStruct((N, D), F32),
    jax.ShapeDtypeStruct((TOP_K, N), jnp.int32),
    jax.ShapeDtypeStruct((TOP_K, N), F32),
)


def _epi_in_specs():
    return [
        pl.BlockSpec((TILE, D), lambda r: (r, 0)),
        pl.BlockSpec((None, 6, D), lambda r: (_tile_mod_row(r), 0, 0)),
        pl.BlockSpec((1, D), lambda r: (0, 0)),
        pl.BlockSpec((N_EXP, D), lambda r: (0, 0)),
        pl.BlockSpec((N_EXP, 1), lambda r: (0, 0)),
    ]


def _epi_out_specs():
    return [
        pl.BlockSpec((TILE, D), lambda r: (r, 0)),
        pl.BlockSpec((TILE, D), lambda r: (r, 0)),
        pl.BlockSpec((TOP_K, TILE), lambda r: (0, r)),
        pl.BlockSpec((TOP_K, TILE), lambda r: (0, r)),
    ]


def _post0_kernel(hs_ref, xg_ref, yf_ref, wo_ref, h_ref, m_ref, gf_ref, wr_ref, brt_ref,
                  hout_ref, fn_ref, tope_ref, gate_ref):
    y_rec = (hs_ref[0] + hs_ref[1]) * _gelu(xg_ref[...])
    mix = (_dot(y_rec.astype(BF16), wo_ref[0:LRU_W, :])
           + _dot(yf_ref[...].astype(BF16), wo_ref[LRU_W:, :]))
    _epilogue(mix, h_ref, m_ref, gf_ref, wr_ref, brt_ref, hout_ref, fn_ref, tope_ref, gate_ref)


def _post0(hs, proj0, yf, w_out_bf16, h, mod, g_ffn, w_router_t, b_router):
    return pl.pallas_call(
        _post0_kernel,
        out_shape=_EPI_OUT_SHAPES,
        grid=(N_TILES,),
        in_specs=[
            pl.BlockSpec((2, TILE, LRU_W), lambda r: (0, r, 0)),
            pl.BlockSpec((TILE, LRU_W), lambda r: (r, 1)),
            pl.BlockSpec((TILE, FN_W), lambda r: (r, 0)),
            pl.BlockSpec((D, D), lambda r: (0, 0)),
        ] + _epi_in_specs(),
        out_specs=_epi_out_specs(),
        compiler_params=_cp(("arbitrary",), VMEM_LIMIT),
        name="post_rglru_fourier",
    )(hs, proj0, yf, w_out_bf16, h, mod, g_ffn.reshape(1, D), w_router_t, b_router.reshape(N_EXP, 1))


def _glu(x):
    return x[:, :CONV_W] * _sigmoid(x[:, CONV_W:])


def _post1_kernel(p_ref, prev_ref, next_ref, lng_ref, lnb_ref, ws_ref, bs_ref, dww_ref, dwb_ref,
                  clg_ref, clb_ref, wo_ref, h_ref, m_ref, gf_ref, wr_ref, brt_ref,
                  hout_ref, fn_ref, tope_ref, gate_ref, ext_ref):
    r = pl.program_id(0)
    z = _gelu(p_ref[:, 0:2 * SGU_W])
    u = z[:, :SGU_W]
    v = _layernorm(z[:, SGU_W:], lng_ref[...], lnb_ref[...]).astype(BF16)
    rows = []
    for n in range(TILE // CHUNK):
        cols = []
        for g in range(SGU_G):
            vb = v[n * CHUNK:(n + 1) * CHUNK, g * CHUNK:(g + 1) * CHUNK]
            cols.append(_dot(ws_ref[g], vb))
        rows.append(jnp.concatenate(cols, axis=1) + bs_ref[...])
    y_sgu = u * jnp.concatenate(rows, axis=0)

    ext_ref[0:CONV_HALO, :] = jnp.where(_tile_is_seq_start(r), 0.0, _glu(prev_ref[...]))
    ext_ref[CONV_HALO:CONV_HALO + TILE, :] = _glu(p_ref[:, 2 * SGU_W:])
    ext_ref[CONV_HALO + TILE:, :] = jnp.where(_tile_is_seq_end(r), 0.0, _glu(next_ref[...]))
    dconv = dwb_ref[...] + jnp.zeros((TILE, CONV_W), F32)
    for k in range(CONV_K):
        off = CONV_HALO - CONV_PAD + k
        dconv = dconv + ext_ref[off:off + TILE, :] * dww_ref[k:k + 1, :]
    ln = _layernorm(dconv, clg_ref[...], clb_ref[...])
    y_conv = ln * _sigmoid(ln)

    mix = (_dot(y_sgu.astype(BF16), wo_ref[0:SGU_W, :])
           + _dot(y_conv.astype(BF16), wo_ref[SGU_W:, :]))
    _epilogue(mix, h_ref, m_ref, gf_ref, wr_ref, brt_ref, hout_ref, fn_ref, tope_ref, gate_ref)


def _post1(proj1, ln_g, ln_b, ws_bf16, bs_full, dw_w, dw_b, cln_g, cln_b, w_out_bf16,
           h, mod, g_ffn, w_router_t, b_router):
    per_tile = TILE // CONV_HALO
    n_halo_blocks = N // CONV_HALO
    const2 = lambda r: (0, 0)
    row = lambda a: a.reshape(1, -1)
    return pl.pallas_call(
        _post1_kernel,
        out_shape=_EPI_OUT_SHAPES,
        grid=(N_TILES,),
        in_specs=[
            pl.BlockSpec((TILE, IN1), lambda r: (r, 0)),
            pl.BlockSpec((CONV_HALO, 2 * CONV_W), lambda r: (jnp.maximum(r * per_tile - 1, 0), 1)),
            pl.BlockSpec((CONV_HALO, 2 * CONV_W),
                         lambda r: (jnp.minimum((r + 1) * per_tile, n_halo_blocks - 1), 1)),
            pl.BlockSpec((1, SGU_W), const2),
            pl.BlockSpec((1, SGU_W), const2),
            pl.BlockSpec((SGU_G, CHUNK, CHUNK), lambda r: (0, 0, 0)),
            pl.BlockSpec((CHUNK, SGU_W), const2),
            pl.BlockSpec((CONV_K, CONV_W), const2),
            pl.BlockSpec((1, CONV_W), const2),
            pl.BlockSpec((1, CONV_W), const2),
            pl.BlockSpec((1, CONV_W), const2),
            pl.BlockSpec((D, D), const2),
        ] + _epi_in_specs(),
        out_specs=_epi_out_specs(),
        scratch_shapes=[pltpu.VMEM((TILE + 2 * CONV_HALO, CONV_W), F32)],
        compiler_params=_cp(("arbitrary",), VMEM_LIMIT),
        name="post_sgu_conformer",
    )(proj1, proj1, proj1, row(ln_g), row(ln_b), ws_bf16, bs_full, dw_w, row(dw_b), row(cln_g), row(cln_b),
      w_out_bf16, h, mod, g_ffn.reshape(1, D), w_router_t, b_router.reshape(N_EXP, 1))


def _routing_tables(top_e):
    e_flat = top_e.reshape(-1)
    onehot = (e_flat[None, :] == jnp.arange(N_EXP, dtype=jnp.int32)[:, None]).astype(jnp.int32)
    incl = jnp.cumsum(onehot, axis=1)
    rank = jnp.sum((incl - onehot) * onehot, axis=0)
    counts = incl[:, -1]
    padded = (counts + MOE_BLOCK - 1) // MOE_BLOCK * MOE_BLOCK
    pad_end = jnp.cumsum(padded)
    pad_start = pad_end - padded
    pos = (pad_start[e_flat] + rank).astype(jnp.int32)
    n_used = (pad_end[-1] // MOE_BLOCK).astype(jnp.int32)
    blk = jnp.minimum(jnp.arange(N_BLOCKS, dtype=jnp.int32), n_used - 1)
    block_e = jnp.searchsorted(pad_end, blk * MOE_BLOCK, side="right").astype(jnp.int32)
    tok = jnp.tile(jnp.arange(N, dtype=jnp.int32), TOP_K)
    slot_tok = jnp.zeros((N_SLOTS,), jnp.int32).at[pos].set(tok)
    return pos, slot_tok, block_e, n_used.reshape(1)


def _moe_kernel(be_ref, nu_ref, st_ref, x_hbm, w1_ref, b1g_ref, b1l_ref, w2_ref, b2_ref, perm_ref,
                o_ref, xbuf, sem, w1s, w2s):
    i = pl.program_id(0)
    n_used = nu_ref[0]

    def row_copy(tok, slot, j):
        return pltpu.make_async_copy(x_hbm.at[pl.ds(tok, 1), :], xbuf.at[slot, pl.ds(j, 1), :], sem.at[slot])

    def issue(blk, slot):
        def body(j, carry):
            row_copy(st_ref[blk * MOE_BLOCK + j], slot, j).start()
            return carry
        lax.fori_loop(0, MOE_BLOCK, body, 0, unroll=8)

    @pl.when(i == 0)
    def _():
        issue(0, 0)

    @pl.when(i + 1 < n_used)
    def _():
        issue(i + 1, (i + 1) % 2)

    e = be_ref[i]
    e_prev = be_ref[jnp.maximum(i - 1, 0)]

    @pl.when((i < n_used) & ((i == 0) | (e != e_prev)))
    def _():
        for c in range(2 * D_FF // 256):
            wc = w1_ref[:, c * 256:(c + 1) * 256].astype(BF16)
            w1s[:, c * 256:(c + 1) * 256] = _dot(wc, perm_ref[...]).astype(BF16)
        w2s[...] = w2_ref[...].astype(BF16)

    @pl.when(i < n_used)
    def _():
        slot = i % 2
        pltpu.make_async_copy(x_hbm.at[pl.ds(0, MOE_BLOCK), :], xbuf.at[slot], sem.at[slot]).wait()
        x = xbuf[slot].astype(BF16)
        hid = _dot(x, w1s[...])
        acts = []
        for c in range(D_FF // 128):
            h_glu = hid[:, c * 256:c * 256 + 128] + b1g_ref[:, c * 128:(c + 1) * 128]
            h_lin = hid[:, c * 256 + 128:(c + 1) * 256] + b1l_ref[:, c * 128:(c + 1) * 128]
            h_glu = jnp.minimum(h_glu, LIMIT)
            h_lin = jnp.clip(h_lin, -LIMIT, LIMIT)
            acts.append((h_glu * _sigmoid(ALPHA * h_glu) * (h_lin + 1.0)).astype(BF16))
        act = jnp.concatenate(acts, axis=1)
        o_ref[...] = _dot(act, w2s[...]) + b2_ref[...]

    @pl.when(i >= n_used)
    def _():
        o_ref[...] = jnp.zeros_like(o_ref)


def _deinterleave_matrix():
    p = np.zeros((256, 256), np.float32)
    m = np.arange(128)
    p[2 * m, m] = 1.0
    p[2 * m + 1, 128 + m] = 1.0
    return jnp.asarray(p, BF16)


def _moe_experts(fn, slot_tok, block_e, n_used, w1, b1, w2, b2):
    b1g = b1[:, 0::2].reshape(N_EXP, 1, D_FF)
    b1l = b1[:, 1::2].reshape(N_EXP, 1, D_FF)
    ex = lambda i, be, nu, st: (be[i], 0, 0)
    grid_spec = pltpu.PrefetchScalarGridSpec(
        num_scalar_prefetch=3,
        grid=(N_BLOCKS,),
        in_specs=[
            pl.BlockSpec(memory_space=pl.ANY),
            pl.BlockSpec((None, D, 2 * D_FF), ex),
            pl.BlockSpec((None, 1, D_FF), ex),
            pl.BlockSpec((None, 1, D_FF), ex),
            pl.BlockSpec((None, D_FF, D), ex),
            pl.BlockSpec((None, 1, D), ex),
            pl.BlockSpec((256, 256), lambda i, be, nu, st: (0, 0)),
        ],
        out_specs=pl.BlockSpec((MOE_BLOCK, D), lambda i, be, nu, st: (i, 0)),
        scratch_shapes=[
            pltpu.VMEM((2, MOE_BLOCK, D), F32),
            pltpu.SemaphoreType.DMA((2,)),
            pltpu.VMEM((D, 2 * D_FF), BF16),
            pltpu.VMEM((D_FF, D), BF16),
        ],
    )
    return pl.pallas_call(
        _moe_kernel,
        out_shape=jax.ShapeDtypeStruct((N_SLOTS, D), F32),
        grid_spec=grid_spec,
        compiler_params=_cp(("arbitrary",), VMEM_LIMIT),
        name="moe_experts",
    )(block_e, n_used, slot_tok, fn, w1, b1g, b1l, w2, b2.reshape(N_EXP, 1, D), _deinterleave_matrix())


def _combine_kernel(pos_ref, yb_hbm, g_ref, h_ref, m_ref, fg_ref, o_ref, ybuf, sem, *, final):
    i = pl.program_id(0)

    def issue(tile, slot):
        def body(j, carry):
            for k in range(TOP_K):
                row = pos_ref[k * N + tile * TILE + j]
                pltpu.make_async_copy(yb_hbm.at[pl.ds(row, 1), :], ybuf.at[slot, k, pl.ds(j, 1), :],
                                      sem.at[slot]).start()
            return carry
        lax.fori_loop(0, TILE, body, 0, unroll=4)

    @pl.when(i == 0)
    def _():
        issue(0, 0)

    @pl.when(i + 1 < N_TILES)
    def _():
        issue(i + 1, (i + 1) % 2)

    slot = i % 2
    for k in range(TOP_K):
        pltpu.make_async_copy(yb_hbm.at[pl.ds(0, TILE), :], ybuf.at[slot, k], sem.at[slot]).wait()
    g = g_ref[...]
    y = ybuf[slot, 0] * g[:, 0:1]
    for k in range(1, TOP_K):
        y = y + ybuf[slot, k] * g[:, k:k + 1]
    h_new = h_ref[...] + m_ref[5:6, :] * y
    if final:
        ms = jnp.mean(h_new * h_new, axis=-1, keepdims=True)
        h_new = h_new * lax.rsqrt(ms + EPS) * fg_ref[...]
    o_ref[...] = h_new


def _moe_combine(yb, pos, gates_nt, h, mod, final_g, final):
    grid_spec = pltpu.PrefetchScalarGridSpec(
        num_scalar_prefetch=1,
        grid=(N_TILES,),
        in_specs=[
            pl.BlockSpec(memory_space=pl.ANY),
            pl.BlockSpec((TILE, TOP_K), lambda i, pos: (i, 0)),
            pl.BlockSpec((TILE, D), lambda i, pos: (i, 0)),
            pl.BlockSpec((None, 6, D), lambda i, pos: (_tile_mod_row(i), 0, 0)),
            pl.BlockSpec((1, D), lambda i, pos: (0, 0)),
        ],
        out_specs=pl.BlockSpec((TILE, D), lambda i, pos: (i, 0)),
        scratch_shapes=[pltpu.VMEM((2, TOP_K, TILE, D), F32), pltpu.SemaphoreType.DMA((2,))],
    )
    return pl.pallas_call(
        functools.partial(_combine_kernel, final=final),
        out_shape=jax.ShapeDtypeStruct((N, D), F32),
        grid_spec=grid_spec,
        compiler_params=_cp(("arbitrary",), VMEM_LIMIT),
        name="moe_combine",
    )(pos, yb, gates_nt, h, mod, final_g.reshape(1, D))


def _moe(fn, top_e, gates, h, mod, w1, b1, w2, b2, final_g, final):
    pos, slot_tok, block_e, n_used = _routing_tables(top_e)
    yb = _moe_experts(fn, slot_tok, block_e, n_used, w1, b1, w2, b2)
    return _moe_combine(yb, pos, gates.T, h, mod, final_g, final)


def _gate_weights(w_r, w_i):
    per = LRU_BLK // LRU_HD
    eye = jnp.eye(per, dtype=F32)

    def blockdiag(w):
        w = w.reshape(2, LRU_HEADS // per, per, LRU_HD, LRU_HD)
        full = jnp.einsum("dgaij,ab->dgaibj", w, eye)
        return full.reshape(2, LRU_HEADS // per, LRU_BLK, LRU_BLK)

    return jnp.concatenate([blockdiag(w_r), blockdiag(w_i)], axis=-1).astype(BF16)


def kernel(x_prompt, x_sample, state_rglru, c, c_ctx, norm_mix_g, norm_ffn_g, w_mod, b_mod, w_in0, lru_conv_w, lru_conv_b, lru_w_r, lru_b_r, lru_w_i, lru_b_i, lru_lambda, w_out0, w_in1, sgu_ln_g, sgu_ln_b, sgu_w_s, sgu_b_s, conv_dw_w, conv_dw_b, conv_ln_g, conv_ln_b, w_out1, w_router, b_router, w1, b1, w2, b2, final_norm_g):
    h = jnp.concatenate([x_prompt.reshape(N_P, D), x_sample.reshape(N_S, D)], axis=0)
    cond8 = jnp.concatenate([c_ctx[None, :], c, jnp.zeros((N_MOD - 1 - N_SAMPLE_SEQ, D), F32)], axis=0)
    mod = _adaln(cond8, w_mod, b_mod)

    proj0 = _inproj(h, norm_mix_g[0], mod[0], w_in0[0].astype(BF16))
    st = state_rglru[:, 0].astype(F32)
    h0 = jnp.zeros((2, N_MOD, LRU_W), F32).at[:, 1:1 + N_SAMPLE_SEQ].set(jnp.swapaxes(st, 0, 1))
    hs = _lru(proj0, lru_conv_w[0], lru_conv_b[0], _gate_weights(lru_w_r[0], lru_w_i[0]),
              lru_b_r[0], lru_b_i[0], lru_lambda[0], h0)
    yf = jnp.concatenate([_fourier_prompt(proj0), _fourier_sample(proj0)], axis=0)
    h, fn, top_e, gates = _post0(hs, proj0, yf, w_out0[0].astype(BF16), h, mod[0], norm_ffn_g[0],
                                 w_router[0].T, b_router[0])
    h = _moe(fn, top_e, gates, h, mod[0], w1[0], b1[0], w2[0], b2[0], final_norm_g, False)

    proj1 = _inproj(h, norm_mix_g[1], mod[1], w_in1[0].astype(BF16))
    bs_full = jnp.repeat(sgu_b_s[0].T, CHUNK, axis=1)
    h, fn, top_e, gates = _post1(proj1, sgu_ln_g[0], sgu_ln_b[0], sgu_w_s[0].astype(BF16), bs_full,
                                 conv_dw_w[0], conv_dw_b[0], conv_ln_g[0], conv_ln_b[0],
                                 w_out1[0].astype(BF16), h, mod[1], norm_ffn_g[1], w_router[1].T, b_router[1])
    y = _moe(fn, top_e, gates, h, mod[1], w1[1], b1[1], w2[1], b2[1], final_norm_g, True)

    y_prompt = y[:N_P].reshape(N_PROMPT_SEQ, T_PROMPT, D)
    y_sample = y[N_P:].reshape(N_SAMPLE_SEQ, T_SAMPLE, D)
    hs_p = hs[:, :N_P].reshape(2, N_PROMPT_SEQ, T_PROMPT, LRU_W)
    new_state = jnp.stack([hs_p[0, :, -1], hs_p[1, :, 0]], axis=1)[:, None]
    return (y_prompt, y_sample, new_state.astype(x_prompt.dtype))
```

```python
import functools

import numpy as np
import jax
import jax.numpy as jnp
from jax import lax
from jax.experimental import pallas as pl
from jax.experimental.pallas import tpu as pltpu

F32 = jnp.float32
BF16 = jnp.bfloat16

D = 1024
N_PROMPT_SEQ = 32
T_PROMPT = 256
N_SAMPLE_SEQ = 2
T_SAMPLE = 2048
N_P = N_PROMPT_SEQ * T_PROMPT
N_S = N_SAMPLE_SEQ * T_SAMPLE
N = N_P + N_S
EPS = 1e-6

TILE = 256
N_TILES = N // TILE
P_TILES = N_P // TILE
S_TILES = T_SAMPLE // TILE
TM_PROJ = 512
N_MOD = 8

LRU_W = 768
LRU_HEADS = 12
LRU_HD = 64
LRU_K = 4
LRU_LEFT = 2
LRU_C = 8.0
LRU_BLK = 256
FN_W = 256
FN_G = 4
FN_GD = 64
IN0 = 2 * LRU_W + FN_W

SGU_W = 512
SGU_G = 4
CHUNK = 128
CONV_W = 512
CONV_K = 31
CONV_PAD = 15
CONV_HALO = 16
IN1 = 2 * SGU_W + 2 * CONV_W

N_EXP = 32
TOP_K = 4
D_FF = 1024
ALPHA = 1.702
LIMIT = 7.0
MOE_BLOCK = 256
N_ASSIGN = N * TOP_K
N_BLOCKS = N_ASSIGN // MOE_BLOCK + N_EXP
N_SLOTS = N_BLOCKS * MOE_BLOCK
ROW_TILE = D // 128

VMEM_LIMIT = 56 * 1024 * 1024


def _cp(sem, vmem=None):
    return pltpu.CompilerParams(dimension_semantics=sem, vmem_limit_bytes=vmem)


def _dot(a, b):
    return jnp.dot(a, b, preferred_element_type=F32)


def _split(x):
    hi = x.astype(BF16)
    lo = (x - hi.astype(F32)).astype(BF16)
    return hi, lo


def _dot3(a, b):
    ah, al = _split(a)
    bh, bl = _split(b)
    return _dot(ah, bh) + _dot(al, bh) + _dot(ah, bl)


def _dot3_nt(a, b):
    dn = (((1,), (1,)), ((), ()))
    d = lambda x, y: lax.dot_general(x, y, dn, preferred_element_type=F32)
    ah, al = _split(a)
    bh, bl = _split(b)
    return d(ah, bh) + d(al, bh) + d(ah, bl)


def _sigmoid(x):
    return 1.0 / (1.0 + jnp.exp(-x))


def _gelu(x):
    return 0.5 * x * (1.0 + jnp.tanh(0.7978845608028654 * (x + 0.044715 * (x * x * x))))


def _rms_mod(x, g, shift, scale):
    ms = jnp.mean(x * x, axis=-1, keepdims=True)
    y = x * lax.rsqrt(ms + EPS) * g
    return y * (1.0 + scale) + shift


def _layernorm(x, g, b):
    xc = x - jnp.mean(x, axis=-1, keepdims=True)
    var = jnp.mean(xc * xc, axis=-1, keepdims=True)
    return xc * lax.rsqrt(var + EPS) * g + b


def _tile_mod_row(r):
    return jnp.where(r < P_TILES, 0, 1 + (r - P_TILES) // S_TILES)


def _tile_is_seq_start(r):
    return (r < P_TILES) | ((r - P_TILES) % S_TILES == 0)


def _tile_is_seq_end(r):
    return (r < P_TILES) | ((r - P_TILES) % S_TILES == S_TILES - 1)


MOD_TN = 512


def _adaln_kernel(cond_ref, w_ref, b_ref, o_ref):
    cond = cond_ref[...]
    s = cond * _sigmoid(cond)
    o_ref[...] = _dot3(s, w_ref[...]) + b_ref[...]


def _adaln(cond8, w_mod, b_mod):
    depth = w_mod.shape[0]
    out = pl.pallas_call(
        _adaln_kernel,
        out_shape=jax.ShapeDtypeStruct((depth, N_MOD, 6 * D), F32),
        grid=(depth, 6 * D // MOD_TN),
        in_specs=[
            pl.BlockSpec((N_MOD, D), lambda l, j: (0, 0)),
            pl.BlockSpec((None, D, MOD_TN), lambda l, j: (l, 0, j)),
            pl.BlockSpec((None, 1, MOD_TN), lambda l, j: (l, 0, j)),
        ],
        out_specs=pl.BlockSpec((None, N_MOD, MOD_TN), lambda l, j: (l, 0, j)),
        compiler_params=_cp(("arbitrary", "arbitrary")),
        name="adaln",
    )(cond8, w_mod, b_mod.reshape(depth, 1, 6 * D))
    return out.reshape(depth, N_MOD, 6, D)


def _inproj_kernel(x_ref, g_ref, m_ref, w_ref, o_ref):
    hn = _rms_mod(x_ref[...], g_ref[...], m_ref[0:1, :], m_ref[1:2, :])
    o_ref[...] = _dot(hn.astype(BF16), w_ref[...])


def _inproj(h, g, mod, w_bf16):
    n_out = w_bf16.shape[1]
    p_steps = N_P // TM_PROJ
    s_steps = T_SAMPLE // TM_PROJ

    def mod_row(i):
        return jnp.where(i < p_steps, 0, 1 + (i - p_steps) // s_steps)

    return pl.pallas_call(
        _inproj_kernel,
        out_shape=jax.ShapeDtypeStruct((N, n_out), F32),
        grid=(N // TM_PROJ,),
        in_specs=[
            pl.BlockSpec((TM_PROJ, D), lambda i: (i, 0)),
            pl.BlockSpec((1, D), lambda i: (0, 0)),
            pl.BlockSpec((None, 6, D), lambda i: (mod_row(i), 0, 0)),
            pl.BlockSpec((D, n_out), lambda i: (0, 0)),
        ],
        out_specs=pl.BlockSpec((TM_PROJ, n_out), lambda i: (i, 0)),
        compiler_params=_cp(("arbitrary",), VMEM_LIMIT),
        name="inproj",
    )(h, g.reshape(1, D), mod, w_bf16)


LRU_HALO = 8


def _lru_tile(d, s):
    return jnp.where(d == 0, s, N_TILES - 1 - s)


def _lru_kernel(x_ref, prev_ref, next_ref, cw_ref, cb_ref, wg_ref, br_ref, bi_ref, lam_ref, h0_ref,
                o_ref, ext_ref, a_ref, carry_ref):
    d = pl.program_id(0)
    r = _lru_tile(d, pl.program_id(1))
    start = _tile_is_seq_start(r)
    end = _tile_is_seq_end(r)

    ext_ref[0:LRU_HALO, :] = jnp.where(start, 0.0, prev_ref[...])
    ext_ref[LRU_HALO:LRU_HALO + TILE, :] = x_ref[...]
    ext_ref[LRU_HALO + TILE:, :] = jnp.where(end, 0.0, next_ref[...])
    xc = cb_ref[...] + jnp.zeros((TILE, LRU_W), F32)
    for k in range(LRU_K):
        off = LRU_HALO - LRU_LEFT + k
        xc = xc + ext_ref[off:off + TILE, :] * cw_ref[k:k + 1, :]

    xcb = xc.astype(BF16)
    pre_r, pre_i = [], []
    for blk in range(LRU_W // LRU_BLK):
        g = _dot(xcb[:, blk * LRU_BLK:(blk + 1) * LRU_BLK], wg_ref[blk])
        pre_r.append(g[:, :LRU_BLK])
        pre_i.append(g[:, LRU_BLK:])
    gate_r = _sigmoid(jnp.concatenate(pre_r, axis=1) + br_ref[...])
    gate_i = _sigmoid(jnp.concatenate(pre_i, axis=1) + bi_ref[...])
    neg_lam = -lam_ref[...]
    softplus = jnp.maximum(neg_lam, 0.0) + jnp.log1p(jnp.exp(-jnp.abs(neg_lam)))
    log_a = (-LRU_C) * gate_r * softplus
    a = jnp.exp(log_a)
    a_ref[...] = a
    o_ref[...] = jnp.sqrt(-jnp.tanh(log_a) * (a * a + 1.0)) * (gate_i * xc)

    fresh = jnp.where(d == 0, start, end)
    h_init = jnp.where(fresh, h0_ref[pl.ds(_tile_mod_row(r), 1), :], carry_ref[...])

    def step(t, h):
        tt = jnp.where(d == 0, t, TILE - 1 - t)
        h = a_ref[pl.ds(tt, 1), :] * h + o_ref[pl.ds(tt, 1), :]
        o_ref[pl.ds(tt, 1), :] = h
        return h

    carry_ref[...] = lax.fori_loop(0, TILE, step, h_init, unroll=8)


def _lru(proj0, conv_w, conv_b, wg, b_r, b_i, lam, h0):
    n_halo_blocks = N // LRU_HALO
    per_tile = TILE // LRU_HALO
    tile = lambda d, s: _lru_tile(d, s)
    return pl.pallas_call(
        _lru_kernel,
        out_shape=jax.ShapeDtypeStruct((2, N, LRU_W), F32),
        grid=(2, N_TILES),
        in_specs=[
            pl.BlockSpec((TILE, LRU_W), lambda d, s: (tile(d, s), 0)),
            pl.BlockSpec((LRU_HALO, LRU_W), lambda d, s: (jnp.maximum(tile(d, s) * per_tile - 1, 0), 0)),
            pl.BlockSpec((LRU_HALO, LRU_W),
                         lambda d, s: (jnp.minimum((tile(d, s) + 1) * per_tile, n_halo_blocks - 1), 0)),
            pl.BlockSpec((LRU_K, LRU_W), lambda d, s: (0, 0)),
            pl.BlockSpec((1, LRU_W), lambda d, s: (0, 0)),
            pl.BlockSpec((None, LRU_W // LRU_BLK, LRU_BLK, 2 * LRU_BLK), lambda d, s: (d, 0, 0, 0)),
            pl.BlockSpec((None, 1, LRU_W), lambda d, s: (d, 0, 0)),
            pl.BlockSpec((None, 1, LRU_W), lambda d, s: (d, 0, 0)),
            pl.BlockSpec((None, 1, LRU_W), lambda d, s: (d, 0, 0)),
            pl.BlockSpec((None, N_MOD, LRU_W), lambda d, s: (d, 0, 0)),
        ],
        out_specs=pl.BlockSpec((None, TILE, LRU_W), lambda d, s: (d, tile(d, s), 0)),
        scratch_shapes=[
            pltpu.VMEM((TILE + 2 * LRU_HALO, LRU_W), F32),
            pltpu.VMEM((TILE, LRU_W), F32),
            pltpu.VMEM((1, LRU_W), F32),
        ],
        compiler_params=_cp(("arbitrary", "arbitrary"), VMEM_LIMIT),
        name="rglru_scan",
    )(proj0, proj0, proj0, conv_w, conv_b.reshape(1, LRU_W), wg,
      b_r.reshape(2, 1, LRU_W), b_i.reshape(2, 1, LRU_W), lam.reshape(2, 1, LRU_W), h0)


def _dft_tables(n, scale):
    k = np.arange(n, dtype=np.int64)
    ang = 2.0 * np.pi * ((k[:, None] * k[None, :]) % n).astype(np.float64) / n
    return np.cos(ang) * scale, np.sin(ang) * scale


def _channel_tables():
    c, s = _dft_tables(FN_GD, FN_GD ** -0.5)
    eye = np.eye(FN_G)
    return (jnp.asarray(np.kron(eye, c), BF16), jnp.asarray(np.kron(eye, s), BF16))


def _time_tables(t_len):
    c, s = _dft_tables(t_len, t_len ** -0.5)
    return jnp.asarray(c, BF16), jnp.asarray(s, BF16)


def _fourier_prompt_kernel(z_ref, cc_ref, sc_ref, ct_ref, st_ref, o_ref):
    z = z_ref[...].astype(BF16)
    zc = _dot(z, cc_ref[...]).astype(BF16)
    zs = _dot(z, sc_ref[...]).astype(BF16)
    o_ref[...] = _dot(ct_ref[...], zc) - _dot(st_ref[...], zs)


def _fourier_prompt(proj0):
    cc, sc = _channel_tables()
    ct, st = _time_tables(T_PROMPT)
    const = lambda b: (0, 0)
    return pl.pallas_call(
        _fourier_prompt_kernel,
        out_shape=jax.ShapeDtypeStruct((N_P, FN_W), F32),
        grid=(N_PROMPT_SEQ,),
        in_specs=[
            pl.BlockSpec((T_PROMPT, FN_W), lambda b: (b, 2 * LRU_W // FN_W)),
            pl.BlockSpec((FN_W, FN_W), const),
            pl.BlockSpec((FN_W, FN_W), const),
            pl.BlockSpec((T_PROMPT, T_PROMPT), const),
            pl.BlockSpec((T_PROMPT, T_PROMPT), const),
        ],
        out_specs=pl.BlockSpec((T_PROMPT, FN_W), lambda b: (b, 0)),
        compiler_params=_cp(("arbitrary",)),
        name="fourier_prompt",
    )(proj0, cc, sc, ct, st)


def _fourier_sample_kernel(z_ref, cc_ref, sc_ref, ct_ref, st_ref, o_ref, zc_ref, zs_ref):
    @pl.when(pl.program_id(1) == 0)
    def _():
        z = z_ref[...].astype(BF16)
        zc_ref[...] = _dot(z, cc_ref[...]).astype(BF16)
        zs_ref[...] = _dot(z, sc_ref[...]).astype(BF16)

    o_ref[...] = _dot(ct_ref[...], zc_ref[...]) - _dot(st_ref[...], zs_ref[...])


def _fourier_sample(proj0):
    cc, sc = _channel_tables()
    ct, st = _time_tables(T_SAMPLE)
    const = lambda b, i: (0, 0)
    first_seq_block = N_P // T_SAMPLE
    return pl.pallas_call(
        _fourier_sample_kernel,
        out_shape=jax.ShapeDtypeStruct((N_S, FN_W), F32),
        grid=(N_SAMPLE_SEQ, S_TILES),
        in_specs=[
            pl.BlockSpec((T_SAMPLE, FN_W), lambda b, i: (first_seq_block + b, 2 * LRU_W // FN_W)),
            pl.BlockSpec((FN_W, FN_W), const),
            pl.BlockSpec((FN_W, FN_W), const),
            pl.BlockSpec((TILE, T_SAMPLE), lambda b, i: (i, 0)),
            pl.BlockSpec((TILE, T_SAMPLE), lambda b, i: (i, 0)),
        ],
        out_specs=pl.BlockSpec((TILE, FN_W), lambda b, i: (b * S_TILES + i, 0)),
        scratch_shapes=[pltpu.VMEM((T_SAMPLE, FN_W), BF16), pltpu.VMEM((T_SAMPLE, FN_W), BF16)],
        compiler_params=_cp(("arbitrary", "arbitrary"), VMEM_LIMIT),
        name="fourier_sample",
    )(proj0, cc, sc, ct, st)


def _epilogue(mix, h_ref, m_ref, gf_ref, wr_ref, brt_ref, hout_ref, fn_ref, tope_ref, gate_ref):
    h_new = h_ref[...] + m_ref[2:3, :] * mix
    hout_ref[...] = h_new
    fn = _rms_mod(h_new, gf_ref[...], m_ref[3:4, :], m_ref[4:5, :])
    for s in range(ROW_TILE):
        fn_ref[pl.ds(s, TILE, stride=ROW_TILE), :] = fn[:, s * 128:(s + 1) * 128]
    logits = _dot3_nt(wr_ref[...], fn) + brt_ref[...]
    iota = lax.broadcasted_iota(jnp.int32, logits.shape, 0)
    vals, idxs = [], []
    for _ in range(TOP_K):
        m = jnp.max(logits, axis=0, keepdims=True)
        idx = jnp.min(jnp.where(logits == m, iota, N_EXP), axis=0, keepdims=True)
        vals.append(m)
        idxs.append(idx)
        logits = jnp.where(iota == idx, -jnp.inf, logits)
    exps = [jnp.exp(v - vals[0]) for v in vals]
    denom = exps[0] + exps[1] + exps[2] + exps[3]
    for k in range(TOP_K):
        tope_ref[k:k + 1, :] = idxs[k]
        gate_ref[k:k + 1, :] = exps[k] / denom


_EPI_OUT_SHAPES = (
    jax.ShapeDtypeStruct((N, D), F32),
    jax.ShapeDtypeStruct((N * ROW_TILE, 128), F32),
    jax.ShapeDtypeStruct((TOP_K, N), jnp.int32),
    jax.ShapeDtypeStruct((TOP_K, N), F32),
)


def _epi_in_specs():
    return [
        pl.BlockSpec((TILE, D), lambda r: (r, 0)),
        pl.BlockSpec((None, 6, D), lambda r: (_tile_mod_row(r), 0, 0)),
        pl.BlockSpec((1, D), lambda r: (0, 0)),
        pl.BlockSpec((N_EXP, D), lambda r: (0, 0)),
        pl.BlockSpec((N_EXP, 1), lambda r: (0, 0)),
    ]


def _epi_out_specs():
    return [
        pl.BlockSpec((TILE, D), lambda r: (r, 0)),
        pl.BlockSpec((TILE * ROW_TILE, 128), lambda r: (r, 0)),
        pl.BlockSpec((TOP_K, TILE), lambda r: (0, r)),
        pl.BlockSpec((TOP_K, TILE), lambda r: (0, r)),
    ]


def _post0_kernel(hs_ref, xg_ref, yf_ref, wo_ref, h_ref, m_ref, gf_ref, wr_ref, brt_ref,
                  hout_ref, fn_ref, tope_ref, gate_ref):
    y_rec = (hs_ref[0] + hs_ref[1]) * _gelu(xg_ref[...])
    mix = (_dot(y_rec.astype(BF16), wo_ref[0:LRU_W, :])
           + _dot(yf_ref[...].astype(BF16), wo_ref[LRU_W:, :]))
    _epilogue(mix, h_ref, m_ref, gf_ref, wr_ref, brt_ref, hout_ref, fn_ref, tope_ref, gate_ref)


def _post0(hs, proj0, yf, w_out_bf16, h, mod, g_ffn, w_router_t, b_router):
    return pl.pallas_call(
        _post0_kernel,
        out_shape=_EPI_OUT_SHAPES,
        grid=(N_TILES,),
        in_specs=[
            pl.BlockSpec((2, TILE, LRU_W), lambda r: (0, r, 0)),
            pl.BlockSpec((TILE, LRU_W), lambda r: (r, 1)),
            pl.BlockSpec((TILE, FN_W), lambda r: (r, 0)),
            pl.BlockSpec((D, D), lambda r: (0, 0)),
        ] + _epi_in_specs(),
        out_specs=_epi_out_specs(),
        compiler_params=_cp(("arbitrary",), VMEM_LIMIT),
        name="post_rglru_fourier",
    )(hs, proj0, yf, w_out_bf16, h, mod, g_ffn.reshape(1, D), w_router_t, b_router.reshape(N_EXP, 1))


def _glu(x):
    return x[:, :CONV_W] * _sigmoid(x[:, CONV_W:])


def _post1_kernel(p_ref, prev_ref, next_ref, lng_ref, lnb_ref, ws_ref, bs_ref, dww_ref, dwb_ref,
                  clg_ref, clb_ref, wo_ref, h_ref, m_ref, gf_ref, wr_ref, brt_ref,
                  hout_ref, fn_ref, tope_ref, gate_ref, ext_ref):
    r = pl.program_id(0)
    z = _gelu(p_ref[:, 0:2 * SGU_W])
    u = z[:, :SGU_W]
    v = _layernorm(z[:, SGU_W:], lng_ref[...], lnb_ref[...]).astype(BF16)
    rows = []
    for n in range(TILE // CHUNK):
        cols = []
        for g in range(SGU_G):
            vb = v[n * CHUNK:(n + 1) * CHUNK, g * CHUNK:(g + 1) * CHUNK]
            cols.append(_dot(ws_ref[g], vb))
        rows.append(jnp.concatenate(cols, axis=1) + bs_ref[...])
    y_sgu = u * jnp.concatenate(rows, axis=0)

    ext_ref[0:CONV_HALO, :] = jnp.where(_tile_is_seq_start(r), 0.0, _glu(prev_ref[...]))
    ext_ref[CONV_HALO:CONV_HALO + TILE, :] = _glu(p_ref[:, 2 * SGU_W:])
    ext_ref[CONV_HALO + TILE:, :] = jnp.where(_tile_is_seq_end(r), 0.0, _glu(next_ref[...]))
    dconv = dwb_ref[...] + jnp.zeros((TILE, CONV_W), F32)
    for k in range(CONV_K):
        off = CONV_HALO - CONV_PAD + k
        dconv = dconv + ext_ref[off:off + TILE, :] * dww_ref[k:k + 1, :]
    ln = _layernorm(dconv, clg_ref[...], clb_ref[...])
    y_conv = ln * _sigmoid(ln)

    mix = (_dot(y_sgu.astype(BF16), wo_ref[0:SGU_W, :])
           + _dot(y_conv.astype(BF16), wo_ref[SGU_W:, :]))
    _epilogue(mix, h_ref, m_ref, gf_ref, wr_ref, brt_ref, hout_ref, fn_ref, tope_ref, gate_ref)


def _post1(proj1, ln_g, ln_b, ws_bf16, bs_full, dw_w, dw_b, cln_g, cln_b, w_out_bf16,
           h, mod, g_ffn, w_router_t, b_router):
    per_tile = TILE // CONV_HALO
    n_halo_blocks = N // CONV_HALO
    const2 = lambda r: (0, 0)
    row = lambda a: a.reshape(1, -1)
    return pl.pallas_call(
        _post1_kernel,
        out_shape=_EPI_OUT_SHAPES,
        grid=(N_TILES,),
        in_specs=[
            pl.BlockSpec((TILE, IN1), lambda r: (r, 0)),
            pl.BlockSpec((CONV_HALO, 2 * CONV_W), lambda r: (jnp.maximum(r * per_tile - 1, 0), 1)),
            pl.BlockSpec((CONV_HALO, 2 * CONV_W),
                         lambda r: (jnp.minimum((r + 1) * per_tile, n_halo_blocks - 1), 1)),
            pl.BlockSpec((1, SGU_W), const2),
            pl.BlockSpec((1, SGU_W), const2),
            pl.BlockSpec((SGU_G, CHUNK, CHUNK), lambda r: (0, 0, 0)),
            pl.BlockSpec((CHUNK, SGU_W), const2),
            pl.BlockSpec((CONV_K, CONV_W), const2),
            pl.BlockSpec((1, CONV_W), const2),
            pl.BlockSpec((1, CONV_W), const2),
            pl.BlockSpec((1, CONV_W), const2),
            pl.BlockSpec((D, D), const2),
        ] + _epi_in_specs(),
        out_specs=_epi_out_specs(),
        scratch_shapes=[pltpu.VMEM((TILE + 2 * CONV_HALO, CONV_W), F32)],
        compiler_params=_cp(("arbitrary",), VMEM_LIMIT),
        name="post_sgu_conformer",
    )(proj1, proj1, proj1, row(ln_g), row(ln_b), ws_bf16, bs_full, dw_w, row(dw_b), row(cln_g), row(cln_b),
      w_out_bf16, h, mod, g_ffn.reshape(1, D), w_router_t, b_router.reshape(N_EXP, 1))


RANK_TILE = 512
PAD_BASE = N_ASSIGN


def _rank_kernel(e_ref, tri_ref, rank_ref, cnt_ref, run_ref):
    @pl.when(pl.program_id(0) == 0)
    def _():
        run_ref[...] = jnp.zeros_like(run_ref)

    iota = lax.broadcasted_iota(jnp.int32, (N_EXP, RANK_TILE), 0)
    run = run_ref[...]
    for k in range(TOP_K):
        onehot = jnp.where(iota == e_ref[k:k + 1, :], 1.0, 0.0)
        incl = _dot(onehot.astype(BF16), tri_ref[...])
        rank = jnp.sum(onehot * (incl - 1.0 + run), axis=0, keepdims=True)
        rank_ref[k:k + 1, :] = rank.astype(jnp.int32)
        run = run + incl[:, RANK_TILE - 1:RANK_TILE]
    run_ref[...] = run
    cnt_ref[...] = run.astype(jnp.int32)


def _rank(top_e):
    tri = jnp.asarray(np.triu(np.ones((RANK_TILE, RANK_TILE), np.float32)), BF16)
    return pl.pallas_call(
        _rank_kernel,
        out_shape=(jax.ShapeDtypeStruct((TOP_K, N), jnp.int32), jax.ShapeDtypeStruct((N_EXP, 1), jnp.int32)),
        grid=(N // RANK_TILE,),
        in_specs=[
            pl.BlockSpec((TOP_K, RANK_TILE), lambda i: (0, i)),
            pl.BlockSpec((RANK_TILE, RANK_TILE), lambda i: (0, 0)),
        ],
        out_specs=[
            pl.BlockSpec((TOP_K, RANK_TILE), lambda i: (0, i)),
            pl.BlockSpec((N_EXP, 1), lambda i: (0, 0)),
        ],
        scratch_shapes=[pltpu.VMEM((N_EXP, 1), F32)],
        compiler_params=_cp(("arbitrary",)),
        name="moe_rank",
    )(top_e, tri)


def _inverse_kernel(pos_ref, cnt_ref, pstart_ref, pend_ref, out_ref):
    def fill(s, carry):
        out_ref[s] = PAD_BASE + (s & (2 * MOE_BLOCK - 1))
        return carry

    def per_expert(e, carry):
        lax.fori_loop(pstart_ref[e] + cnt_ref[e], pend_ref[e], fill, 0)
        return carry

    lax.fori_loop(0, N_EXP, per_expert, 0)
    lax.fori_loop(pend_ref[N_EXP - 1], N_SLOTS, fill, 0)
    for k in range(TOP_K):
        def body(t, carry, k=k):
            out_ref[pos_ref[k * N + t]] = t * TOP_K + k
            return carry
        lax.fori_loop(0, N, body, 0, unroll=8)


def _inverse(pos_flat, counts, pad_start, pad_end):
    smem = pl.BlockSpec(memory_space=pltpu.SMEM)
    return pl.pallas_call(
        _inverse_kernel,
        out_shape=jax.ShapeDtypeStruct((N_SLOTS,), jnp.int32),
        in_specs=[smem, smem, smem, smem],
        out_specs=smem,
        name="moe_inverse",
    )(pos_flat, counts, pad_start, pad_end)


def _routing_tables(top_e):
    rank, counts = _rank(top_e)
    counts = counts.reshape(N_EXP)
    padded = (counts + MOE_BLOCK - 1) // MOE_BLOCK * MOE_BLOCK
    pad_end = jnp.cumsum(padded).astype(jnp.int32)
    pad_start = pad_end - padded
    experts = jnp.arange(N_EXP, dtype=jnp.int32)[:, None, None]
    pos = rank + jnp.sum(jnp.where(top_e[None] == experts, pad_start[:, None, None], 0), axis=0)
    pos = pos.astype(jnp.int32).reshape(-1)
    n_used = pad_end[-1] // MOE_BLOCK
    blk = jnp.minimum(jnp.arange(N_BLOCKS, dtype=jnp.int32), n_used - 1)
    block_e = jnp.sum(pad_end[None, :] <= (blk * MOE_BLOCK)[:, None], axis=1).astype(jnp.int32)
    slot_a = _inverse(pos, counts, pad_start, pad_end)
    return pos, slot_a, block_e, n_used.reshape(1)


def _moe_kernel(be_ref, nu_ref, sa_ref, x_hbm, w1_ref, b1g_ref, b1l_ref, w2_ref, b2_ref, perm_ref,
                o_ref, xbuf, sem, w1s, w2s):
    i = pl.program_id(0)
    n_used = nu_ref[0]
    slot = i % 2

    def tile_copy(blk, j, dst_slot):
        tok = jnp.minimum(sa_ref[blk * MOE_BLOCK + j] >> 2, N - 1)
        src = x_hbm.at[pl.ds(pl.multiple_of(tok * ROW_TILE, ROW_TILE), ROW_TILE), :]
        dst = xbuf.at[dst_slot, pl.ds(pl.multiple_of(j * ROW_TILE, ROW_TILE), ROW_TILE), :]
        return pltpu.make_async_copy(src, dst, sem.at[dst_slot])

    @pl.when(i == 0)
    def _():
        def body(j, carry):
            tile_copy(0, j, 0).start()
            return carry
        lax.fori_loop(0, MOE_BLOCK, body, 0, unroll=8)

    e = be_ref[i]
    e_prev = be_ref[jnp.maximum(i - 1, 0)]

    @pl.when((i < n_used) & ((i == 0) | (e != e_prev)))
    def _():
        for c in range(2 * D_FF // 256):
            wc = w1_ref[:, c * 256:(c + 1) * 256].astype(BF16)
            w1s[:, c * 256:(c + 1) * 256] = _dot(wc, perm_ref[...]).astype(BF16)
        w2s[...] = w2_ref[...].astype(BF16)

    def block(prefetch_next):
        pltpu.make_async_copy(x_hbm.at[pl.ds(0, MOE_BLOCK * ROW_TILE), :], xbuf.at[slot], sem.at[slot]).wait()
        if prefetch_next:
            for j in range(MOE_BLOCK):
                tile_copy(i + 1, j, 1 - slot).start()
        x = jnp.concatenate([xbuf[slot, pl.ds(s, MOE_BLOCK, stride=ROW_TILE), :] for s in range(ROW_TILE)],
                            axis=1).astype(BF16)
        hid = _dot(x, w1s[...])
        acts = []
        for c in range(D_FF // 128):
            h_glu = hid[:, c * 256:c * 256 + 128] + b1g_ref[:, c * 128:(c + 1) * 128]
            h_lin = hid[:, c * 256 + 128:(c + 1) * 256] + b1l_ref[:, c * 128:(c + 1) * 128]
            h_glu = jnp.minimum(h_glu, LIMIT)
            h_lin = jnp.clip(h_lin, -LIMIT, LIMIT)
            acts.append((h_glu * _sigmoid(ALPHA * h_glu) * (h_lin + 1.0)).astype(BF16))
        y = _dot(jnp.concatenate(acts, axis=1), w2s[...]) + b2_ref[...]
        for s in range(ROW_TILE):
            o_ref[pl.ds(s, MOE_BLOCK, stride=ROW_TILE), :] = y[:, s * 128:(s + 1) * 128]

    @pl.when(i + 1 < n_used)
    def _():
        block(True)

    @pl.when(i + 1 == n_used)
    def _():
        block(False)

    @pl.when(i >= n_used)
    def _():
        o_ref[...] = jnp.zeros_like(o_ref)


def _deinterleave_matrix():
    p = np.zeros((256, 256), np.float32)
    m = np.arange(128)
    p[2 * m, m] = 1.0
    p[2 * m + 1, 128 + m] = 1.0
    return jnp.asarray(p, BF16)


def _moe_experts(layer, fn_tiles, slot_a, block_e, n_used, w1, b1, w2, b2):
    b1g = b1[layer][:, 0::2].reshape(N_EXP, 1, D_FF)
    b1l = b1[layer][:, 1::2].reshape(N_EXP, 1, D_FF)
    ex = lambda i, be, nu, sa: (be[i], 0, 0)
    lex = lambda i, be, nu, sa: (layer, be[i], 0, 0)
    grid_spec = pltpu.PrefetchScalarGridSpec(
        num_scalar_prefetch=3,
        grid=(N_BLOCKS,),
        in_specs=[
            pl.BlockSpec(memory_space=pl.ANY),
            pl.BlockSpec((None, None, D, 2 * D_FF), lex),
            pl.BlockSpec((None, 1, D_FF), ex),
            pl.BlockSpec((None, 1, D_FF), ex),
            pl.BlockSpec((None, None, D_FF, D), lex),
            pl.BlockSpec((None, 1, D), ex),
            pl.BlockSpec((256, 256), lambda i, be, nu, sa: (0, 0)),
        ],
        out_specs=pl.BlockSpec((MOE_BLOCK * ROW_TILE, 128), lambda i, be, nu, sa: (i, 0)),
        scratch_shapes=[
            pltpu.VMEM((2, MOE_BLOCK * ROW_TILE, 128), F32),
            pltpu.SemaphoreType.DMA((2,)),
            pltpu.VMEM((D, 2 * D_FF), BF16),
            pltpu.VMEM((D_FF, D), BF16),
        ],
    )
    return pl.pallas_call(
        _moe_kernel,
        out_shape=jax.ShapeDtypeStruct((N_SLOTS * ROW_TILE, 128), F32),
        grid_spec=grid_spec,
        compiler_params=_cp(("arbitrary",), VMEM_LIMIT),
        name="moe_experts",
    )(block_e, n_used, slot_a, fn_tiles, w1, b1g, b1l, w2, b2[layer].reshape(N_EXP, 1, D), _deinterleave_matrix())


def _combine_kernel(pos_ref, yb_hbm, g_ref, h_ref, m_ref, fg_ref, o_ref, ybuf, sem, *, final):
    i = pl.program_id(0)
    slot = i % 2

    def tile_copy(tile, j, k, dst_slot):
        row = pos_ref[k * N + tile * TILE + j]
        src = yb_hbm.at[pl.ds(pl.multiple_of(row * ROW_TILE, ROW_TILE), ROW_TILE), :]
        dst = ybuf.at[dst_slot, k, pl.ds(pl.multiple_of(j * ROW_TILE, ROW_TILE), ROW_TILE), :]
        return pltpu.make_async_copy(src, dst, sem.at[dst_slot])

    @pl.when(i == 0)
    def _():
        def body(j, carry):
            for k in range(TOP_K):
                tile_copy(0, j, k, 0).start()
            return carry
        lax.fori_loop(0, TILE, body, 0, unroll=4)

    def tile_step(prefetch_next):
        for k in range(TOP_K):
            pltpu.make_async_copy(yb_hbm.at[pl.ds(0, TILE * ROW_TILE), :], ybuf.at[slot, k], sem.at[slot]).wait()
        if prefetch_next:
            for j in range(TILE):
                for k in range(TOP_K):
                    tile_copy(i + 1, j, k, 1 - slot).start()
        g = g_ref[...]
        cols = []
        for s in range(ROW_TILE):
            y = ybuf[slot, 0, pl.ds(s, TILE, stride=ROW_TILE), :] * g[:, 0:1]
            for k in range(1, TOP_K):
                y = y + ybuf[slot, k, pl.ds(s, TILE, stride=ROW_TILE), :] * g[:, k:k + 1]
            cols.append(y)
        h_new = h_ref[...] + m_ref[5:6, :] * jnp.concatenate(cols, axis=1)
        if final:
            ms = jnp.mean(h_new * h_new, axis=-1, keepdims=True)
            h_new = h_new * lax.rsqrt(ms + EPS) * fg_ref[...]
        o_ref[...] = h_new

    @pl.when(i + 1 < N_TILES)
    def _():
        tile_step(True)

    @pl.when(i + 1 == N_TILES)
    def _():
        tile_step(False)


def _moe_combine(yb_tiles, pos, gates_nt, h, mod, final_g, final):
    grid_spec = pltpu.PrefetchScalarGridSpec(
        num_scalar_prefetch=1,
        grid=(N_TILES,),
        in_specs=[
            pl.BlockSpec(memory_space=pl.ANY),
            pl.BlockSpec((TILE, TOP_K), lambda i, pos: (i, 0)),
            pl.BlockSpec((TILE, D), lambda i, pos: (i, 0)),
            pl.BlockSpec((None, 6, D), lambda i, pos: (_tile_mod_row(i), 0, 0)),
            pl.BlockSpec((1, D), lambda i, pos: (0, 0)),
        ],
        out_specs=pl.BlockSpec((TILE, D), lambda i, pos: (i, 0)),
        scratch_shapes=[pltpu.VMEM((2, TOP_K, TILE * ROW_TILE, 128), F32), pltpu.SemaphoreType.DMA((2,))],
    )
    return pl.pallas_call(
        functools.partial(_combine_kernel, final=final),
        out_shape=jax.ShapeDtypeStruct((N, D), F32),
        grid_spec=grid_spec,
        compiler_params=_cp(("arbitrary",), VMEM_LIMIT),
        name="moe_combine",
    )(pos, yb_tiles, gates_nt, h, mod, final_g.reshape(1, D))


def _moe(layer, fn_tiles, top_e, gates, h, mod, w1, b1, w2, b2, final_g, final):
    pos, slot_a, block_e, n_used = _routing_tables(top_e)
    yb_tiles = _moe_experts(layer, fn_tiles, slot_a, block_e, n_used, w1, b1, w2, b2)
    return _moe_combine(yb_tiles, pos, gates.T, h, mod, final_g, final)


def _gate_weights(w_r, w_i):
    per = LRU_BLK // LRU_HD
    eye = jnp.eye(per, dtype=F32)

    def blockdiag(w):
        w = w.reshape(2, LRU_HEADS // per, per, LRU_HD, LRU_HD)
        full = jnp.einsum("dgaij,ab->dgaibj", w, eye)
        return full.reshape(2, LRU_HEADS // per, LRU_BLK, LRU_BLK)

    return jnp.concatenate([blockdiag(w_r), blockdiag(w_i)], axis=-1).astype(BF16)


def kernel(x_prompt, x_sample, state_rglru, c, c_ctx, norm_mix_g, norm_ffn_g, w_mod, b_mod, w_in0, lru_conv_w, lru_conv_b, lru_w_r, lru_b_r, lru_w_i, lru_b_i, lru_lambda, w_out0, w_in1, sgu_ln_g, sgu_ln_b, sgu_w_s, sgu_b_s, conv_dw_w, conv_dw_b, conv_ln_g, conv_ln_b, w_out1, w_router, b_router, w1, b1, w2, b2, final_norm_g):
    h = jnp.concatenate([x_prompt.reshape(N_P, D), x_sample.reshape(N_S, D)], axis=0)
    cond8 = jnp.concatenate([c_ctx[None, :], c, jnp.zeros((N_MOD - 1 - N_SAMPLE_SEQ, D), F32)], axis=0)
    mod = _adaln(cond8, w_mod, b_mod)

    proj0 = _inproj(h, norm_mix_g[0], mod[0], w_in0[0].astype(BF16))
    st = state_rglru[:, 0].astype(F32)
    h0 = jnp.zeros((2, N_MOD, LRU_W), F32).at[:, 1:1 + N_SAMPLE_SEQ].set(jnp.swapaxes(st, 0, 1))
    hs = _lru(proj0, lru_conv_w[0], lru_conv_b[0], _gate_weights(lru_w_r[0], lru_w_i[0]),
              lru_b_r[0], lru_b_i[0], lru_lambda[0], h0)
    yf = jnp.concatenate([_fourier_prompt(proj0), _fourier_sample(proj0)], axis=0)
    h, fn, top_e, gates = _post0(hs, proj0, yf, w_out0[0].astype(BF16), h, mod[0], norm_ffn_g[0],
                                 w_router[0].T, b_router[0])
    h = _moe(0, fn, top_e, gates, h, mod[0], w1, b1, w2, b2, final_norm_g, False)

    proj1 = _inproj(h, norm_mix_g[1], mod[1], w_in1[0].astype(BF16))
    bs_full = jnp.repeat(sgu_b_s[0].T, CHUNK, axis=1)
    h, fn, top_e, gates = _post1(proj1, sgu_ln_g[0], sgu_ln_b[0], sgu_w_s[0].astype(BF16), bs_full,
                                 conv_dw_w[0], conv_dw_b[0], conv_ln_g[0], conv_ln_b[0],
                                 w_out1[0].astype(BF16), h, mod[1], norm_ffn_g[1], w_router[1].T, b_router[1])
    y = _moe(1, fn, top_e, gates, h, mod[1], w1, b1, w2, b2, final_norm_g, True)

    y_prompt = y[:N_P].reshape(N_PROMPT_SEQ, T_PROMPT, D)
    y_sample = y[N_P:].reshape(N_SAMPLE_SEQ, T_SAMPLE, D)
    hs_p = hs[:, :N_P].reshape(2, N_PROMPT_SEQ, T_PROMPT, LRU_W)
    new_state = jnp.stack([hs_p[0, :, -1], hs_p[1, :, 0]], axis=1)[:, None]
    return (y_prompt, y_sample, new_state.astype(x_prompt.dtype))
```

```python
import functools

import numpy as np
import jax
import jax.numpy as jnp
from jax import lax
from jax.experimental import pallas as pl
from jax.experimental.pallas import tpu as pltpu

F32 = jnp.float32
BF16 = jnp.bfloat16

D = 1024
N_PROMPT_SEQ = 32
T_PROMPT = 256
N_SAMPLE_SEQ = 2
T_SAMPLE = 2048
N_P = N_PROMPT_SEQ * T_PROMPT
N_S = N_SAMPLE_SEQ * T_SAMPLE
N = N_P + N_S
EPS = 1e-6

TILE = 256
N_TILES = N // TILE
P_TILES = N_P // TILE
S_TILES = T_SAMPLE // TILE
TM_PROJ = 512
N_MOD = 8

LRU_W = 768
LRU_HEADS = 12
LRU_HD = 64
LRU_K = 4
LRU_LEFT = 2
LRU_C = 8.0
LRU_BLK = 256
FN_W = 256
FN_G = 4
FN_GD = 64
IN0 = 2 * LRU_W + FN_W

SGU_W = 512
SGU_G = 4
CHUNK = 128
CONV_W = 512
CONV_K = 31
CONV_PAD = 15
CONV_HALO = 16
IN1 = 2 * SGU_W + 2 * CONV_W

N_EXP = 32
TOP_K = 4
D_FF = 1024
ALPHA = 1.702
LIMIT = 7.0
MOE_BLOCK = 256
N_ASSIGN = N * TOP_K
N_BLOCKS = N_ASSIGN // MOE_BLOCK + N_EXP
N_SLOTS = N_BLOCKS * MOE_BLOCK
ROW_TILE = D // 128

VMEM_LIMIT = 56 * 1024 * 1024


def _cp(sem, vmem=None):
    return pltpu.CompilerParams(dimension_semantics=sem, vmem_limit_bytes=vmem)


def _dot(a, b):
    return jnp.dot(a, b, preferred_element_type=F32)


def _split(x):
    hi = x.astype(BF16)
    lo = (x - hi.astype(F32)).astype(BF16)
    return hi, lo


def _dot3(a, b):
    ah, al = _split(a)
    bh, bl = _split(b)
    return _dot(ah, bh) + _dot(al, bh) + _dot(ah, bl)


def _dot3_nt(a, b):
    dn = (((1,), (1,)), ((), ()))
    d = lambda x, y: lax.dot_general(x, y, dn, preferred_element_type=F32)
    ah, al = _split(a)
    bh, bl = _split(b)
    return d(ah, bh) + d(al, bh) + d(ah, bl)


def _sigmoid(x):
    return 1.0 / (1.0 + jnp.exp(-x))


def _gelu(x):
    return 0.5 * x * (1.0 + jnp.tanh(0.7978845608028654 * (x + 0.044715 * (x * x * x))))


def _rms_mod(x, g, shift, scale):
    ms = jnp.mean(x * x, axis=-1, keepdims=True)
    y = x * lax.rsqrt(ms + EPS) * g
    return y * (1.0 + scale) + shift


def _layernorm(x, g, b):
    xc = x - jnp.mean(x, axis=-1, keepdims=True)
    var = jnp.mean(xc * xc, axis=-1, keepdims=True)
    return xc * lax.rsqrt(var + EPS) * g + b


def _tile_mod_row(r):
    return jnp.where(r < P_TILES, 0, 1 + (r - P_TILES) // S_TILES)


def _tile_is_seq_start(r):
    return (r < P_TILES) | ((r - P_TILES) % S_TILES == 0)


def _tile_is_seq_end(r):
    return (r < P_TILES) | ((r - P_TILES) % S_TILES == S_TILES - 1)


MOD_TN = 512


def _adaln_kernel(cond_ref, w_ref, b_ref, o_ref):
    cond = cond_ref[...]
    s = cond * _sigmoid(cond)
    o_ref[...] = _dot3(s, w_ref[...]) + b_ref[...]


def _adaln(cond8, w_mod, b_mod):
    depth = w_mod.shape[0]
    out = pl.pallas_call(
        _adaln_kernel,
        out_shape=jax.ShapeDtypeStruct((depth, N_MOD, 6 * D), F32),
        grid=(depth, 6 * D // MOD_TN),
        in_specs=[
            pl.BlockSpec((N_MOD, D), lambda l, j: (0, 0)),
            pl.BlockSpec((None, D, MOD_TN), lambda l, j: (l, 0, j)),
            pl.BlockSpec((None, 1, MOD_TN), lambda l, j: (l, 0, j)),
        ],
        out_specs=pl.BlockSpec((None, N_MOD, MOD_TN), lambda l, j: (l, 0, j)),
        compiler_params=_cp(("arbitrary", "arbitrary")),
        name="adaln",
    )(cond8, w_mod, b_mod.reshape(depth, 1, 6 * D))
    return out.reshape(depth, N_MOD, 6, D)


PROJ_P_STEPS = N_P // TM_PROJ
PROJ_S_STEPS = T_SAMPLE // TM_PROJ


def _stream_specs(rows, p_steps, h):
    if isinstance(h, tuple):
        hp, hl = h
        first_latent = 0
    else:
        hp = hl = h
        first_latent = p_steps
    specs = [
        pl.BlockSpec((rows, D), lambda i, *_: (jnp.minimum(i, p_steps - 1), 0)),
        pl.BlockSpec((rows, D), lambda i, *_: (jnp.maximum(i - p_steps, 0) + first_latent, 0)),
    ]
    return specs, (hp, hl)


def _inproj_kernel(xp_ref, xl_ref, g_ref, m_ref, w_ref, o_ref):
    x = jnp.where(pl.program_id(0) < PROJ_P_STEPS, xp_ref[...], xl_ref[...])
    hn = _rms_mod(x, g_ref[...], m_ref[0:1, :], m_ref[1:2, :])
    o_ref[...] = _dot(hn.astype(BF16), w_ref[...])


def _inproj(h, g, mod, w_bf16):
    n_out = w_bf16.shape[1]

    def mod_row(i):
        return jnp.where(i < PROJ_P_STEPS, 0, 1 + (i - PROJ_P_STEPS) // PROJ_S_STEPS)

    h_specs, h_args = _stream_specs(TM_PROJ, PROJ_P_STEPS, h)
    return pl.pallas_call(
        _inproj_kernel,
        out_shape=jax.ShapeDtypeStruct((N, n_out), F32),
        grid=(N // TM_PROJ,),
        in_specs=h_specs + [
            pl.BlockSpec((1, D), lambda i: (0, 0)),
            pl.BlockSpec((None, 6, D), lambda i: (mod_row(i), 0, 0)),
            pl.BlockSpec((D, n_out), lambda i: (0, 0)),
        ],
        out_specs=pl.BlockSpec((TM_PROJ, n_out), lambda i: (i, 0)),
        compiler_params=_cp(("arbitrary",), VMEM_LIMIT),
        name="inproj",
    )(*h_args, g.reshape(1, D), mod, w_bf16)


LRU_HALO = 8


def _lru_tile(d, s):
    return jnp.where(d == 0, s, N_TILES - 1 - s)


def _lru_kernel(x_ref, prev_ref, next_ref, cw_ref, cb_ref, wg_ref, br_ref, bi_ref, lam_ref, h0_ref,
                o_ref, state_ref, ext_ref, a_ref, carry_ref):
    d = pl.program_id(0)
    r = _lru_tile(d, pl.program_id(1))
    start = _tile_is_seq_start(r)
    end = _tile_is_seq_end(r)

    ext_ref[0:LRU_HALO, :] = jnp.where(start, 0.0, prev_ref[...])
    ext_ref[LRU_HALO:LRU_HALO + TILE, :] = x_ref[...]
    ext_ref[LRU_HALO + TILE:, :] = jnp.where(end, 0.0, next_ref[...])
    xc = cb_ref[...] + jnp.zeros((TILE, LRU_W), F32)
    for k in range(LRU_K):
        off = LRU_HALO - LRU_LEFT + k
        xc = xc + ext_ref[off:off + TILE, :] * cw_ref[k:k + 1, :]

    xcb = xc.astype(BF16)
    pre_r, pre_i = [], []
    for blk in range(LRU_W // LRU_BLK):
        g = _dot(xcb[:, blk * LRU_BLK:(blk + 1) * LRU_BLK], wg_ref[blk])
        pre_r.append(g[:, :LRU_BLK])
        pre_i.append(g[:, LRU_BLK:])
    gate_r = _sigmoid(jnp.concatenate(pre_r, axis=1) + br_ref[...])
    gate_i = _sigmoid(jnp.concatenate(pre_i, axis=1) + bi_ref[...])
    neg_lam = -lam_ref[...]
    softplus = jnp.maximum(neg_lam, 0.0) + jnp.log1p(jnp.exp(-jnp.abs(neg_lam)))
    log_a = (-LRU_C) * gate_r * softplus
    a = jnp.exp(log_a)
    a_ref[...] = a
    o_ref[...] = jnp.sqrt(-jnp.tanh(log_a) * (a * a + 1.0)) * (gate_i * xc)

    fresh = jnp.where(d == 0, start, end)
    h_init = jnp.where(fresh, h0_ref[pl.ds(_tile_mod_row(r), 1), :], carry_ref[...])

    def step(t, h):
        tt = jnp.where(d == 0, t, TILE - 1 - t)
        h = a_ref[pl.ds(tt, 1), :] * h + o_ref[pl.ds(tt, 1), :]
        o_ref[pl.ds(tt, 1), :] = h
        return h

    h_last = lax.fori_loop(0, TILE, step, h_init, unroll=8)
    carry_ref[...] = h_last

    @pl.when(r < P_TILES)
    def _():
        state_ref[...] = h_last


def _lru(proj0, conv_w, conv_b, wg, b_r, b_i, lam, h0):
    n_halo_blocks = N // LRU_HALO
    per_tile = TILE // LRU_HALO
    tile = lambda d, s: _lru_tile(d, s)
    return pl.pallas_call(
        _lru_kernel,
        out_shape=(jax.ShapeDtypeStruct((2, N, LRU_W), F32),
                   jax.ShapeDtypeStruct((2, N_PROMPT_SEQ, 1, LRU_W), F32)),
        grid=(2, N_TILES),
        in_specs=[
            pl.BlockSpec((TILE, LRU_W), lambda d, s: (tile(d, s), 0)),
            pl.BlockSpec((LRU_HALO, LRU_W), lambda d, s: (jnp.maximum(tile(d, s) * per_tile - 1, 0), 0)),
            pl.BlockSpec((LRU_HALO, LRU_W),
                         lambda d, s: (jnp.minimum((tile(d, s) + 1) * per_tile, n_halo_blocks - 1), 0)),
            pl.BlockSpec((LRU_K, LRU_W), lambda d, s: (0, 0)),
            pl.BlockSpec((1, LRU_W), lambda d, s: (0, 0)),
            pl.BlockSpec((None, LRU_W // LRU_BLK, LRU_BLK, 2 * LRU_BLK), lambda d, s: (d, 0, 0, 0)),
            pl.BlockSpec((None, 1, LRU_W), lambda d, s: (d, 0, 0)),
            pl.BlockSpec((None, 1, LRU_W), lambda d, s: (d, 0, 0)),
            pl.BlockSpec((None, 1, LRU_W), lambda d, s: (d, 0, 0)),
            pl.BlockSpec((None, N_MOD, LRU_W), lambda d, s: (d, 0, 0)),
        ],
        out_specs=[
            pl.BlockSpec((None, TILE, LRU_W), lambda d, s: (d, tile(d, s), 0)),
            pl.BlockSpec((None, None, 1, LRU_W), lambda d, s: (d, jnp.minimum(tile(d, s), P_TILES - 1), 0, 0)),
        ],
        scratch_shapes=[
            pltpu.VMEM((TILE + 2 * LRU_HALO, LRU_W), F32),
            pltpu.VMEM((TILE, LRU_W), F32),
            pltpu.VMEM((1, LRU_W), F32),
        ],
        compiler_params=_cp(("arbitrary", "arbitrary"), VMEM_LIMIT),
        name="rglru_scan",
    )(proj0, proj0, proj0, conv_w, conv_b.reshape(1, LRU_W), wg,
      b_r.reshape(2, 1, LRU_W), b_i.reshape(2, 1, LRU_W), lam.reshape(2, 1, LRU_W), h0)


def _dft_tables(n, scale):
    k = np.arange(n, dtype=np.int64)
    ang = 2.0 * np.pi * ((k[:, None] * k[None, :]) % n).astype(np.float64) / n
    return np.cos(ang) * scale, np.sin(ang) * scale


def _channel_tables():
    c, s = _dft_tables(FN_GD, FN_GD ** -0.5)
    eye = np.eye(FN_G)
    return (jnp.asarray(np.kron(eye, c), BF16), jnp.asarray(np.kron(eye, s), BF16))


def _time_tables(t_len):
    c, s = _dft_tables(t_len, t_len ** -0.5)
    return jnp.asarray(c, BF16), jnp.asarray(s, BF16)


def _fourier_prompt_kernel(z_ref, cc_ref, sc_ref, ct_ref, st_ref, o_ref):
    z = z_ref[...].astype(BF16)
    zc = _dot(z, cc_ref[...]).astype(BF16)
    zs = _dot(z, sc_ref[...]).astype(BF16)
    o_ref[...] = _dot(ct_ref[...], zc) - _dot(st_ref[...], zs)


def _fourier_prompt(proj0):
    cc, sc = _channel_tables()
    ct, st = _time_tables(T_PROMPT)
    const = lambda b: (0, 0)
    return pl.pallas_call(
        _fourier_prompt_kernel,
        out_shape=jax.ShapeDtypeStruct((N_P, FN_W), F32),
        grid=(N_PROMPT_SEQ,),
        in_specs=[
            pl.BlockSpec((T_PROMPT, FN_W), lambda b: (b, 2 * LRU_W // FN_W)),
            pl.BlockSpec((FN_W, FN_W), const),
            pl.BlockSpec((FN_W, FN_W), const),
            pl.BlockSpec((T_PROMPT, T_PROMPT), const),
            pl.BlockSpec((T_PROMPT, T_PROMPT), const),
        ],
        out_specs=pl.BlockSpec((T_PROMPT, FN_W), lambda b: (b, 0)),
        compiler_params=_cp(("arbitrary",)),
        name="fourier_prompt",
    )(proj0, cc, sc, ct, st)


def _fourier_sample_kernel(z_ref, cc_ref, sc_ref, ct_ref, st_ref, o_ref, zc_ref, zs_ref):
    @pl.when(pl.program_id(1) == 0)
    def _():
        z = z_ref[...].astype(BF16)
        zc_ref[...] = _dot(z, cc_ref[...]).astype(BF16)
        zs_ref[...] = _dot(z, sc_ref[...]).astype(BF16)

    o_ref[...] = _dot(ct_ref[...], zc_ref[...]) - _dot(st_ref[...], zs_ref[...])


def _fourier_sample(proj0):
    cc, sc = _channel_tables()
    ct, st = _time_tables(T_SAMPLE)
    const = lambda b, i: (0, 0)
    first_seq_block = N_P // T_SAMPLE
    return pl.pallas_call(
        _fourier_sample_kernel,
        out_shape=jax.ShapeDtypeStruct((N_S, FN_W), F32),
        grid=(N_SAMPLE_SEQ, S_TILES),
        in_specs=[
            pl.BlockSpec((T_SAMPLE, FN_W), lambda b, i: (first_seq_block + b, 2 * LRU_W // FN_W)),
            pl.BlockSpec((FN_W, FN_W), const),
            pl.BlockSpec((FN_W, FN_W), const),
            pl.BlockSpec((TILE, T_SAMPLE), lambda b, i: (i, 0)),
            pl.BlockSpec((TILE, T_SAMPLE), lambda b, i: (i, 0)),
        ],
        out_specs=pl.BlockSpec((TILE, FN_W), lambda b, i: (b * S_TILES + i, 0)),
        scratch_shapes=[pltpu.VMEM((T_SAMPLE, FN_W), BF16), pltpu.VMEM((T_SAMPLE, FN_W), BF16)],
        compiler_params=_cp(("arbitrary", "arbitrary"), VMEM_LIMIT),
        name="fourier_sample",
    )(proj0, cc, sc, ct, st)


def _epilogue(mix, hp_ref, hl_ref, m_ref, gf_ref, wr_ref, brt_ref, hout_ref, fn_ref, tope_ref, gate_ref):
    h = jnp.where(pl.program_id(0) < P_TILES, hp_ref[...], hl_ref[...])
    h_new = h + m_ref[2:3, :] * mix
    hout_ref[...] = h_new
    fn = _rms_mod(h_new, gf_ref[...], m_ref[3:4, :], m_ref[4:5, :])
    for s in range(ROW_TILE):
        fn_ref[pl.ds(s, TILE, stride=ROW_TILE), :] = fn[:, s * 128:(s + 1) * 128]
    logits = _dot3_nt(wr_ref[...], fn) + brt_ref[...]
    iota = lax.broadcasted_iota(jnp.int32, logits.shape, 0)
    vals, idxs = [], []
    for _ in range(TOP_K):
        m = jnp.max(logits, axis=0, keepdims=True)
        idx = jnp.min(jnp.where(logits == m, iota, N_EXP), axis=0, keepdims=True)
        vals.append(m)
        idxs.append(idx)
        logits = jnp.where(iota == idx, -jnp.inf, logits)
    exps = [jnp.exp(v - vals[0]) for v in vals]
    denom = exps[0] + exps[1] + exps[2] + exps[3]
    for k in range(TOP_K):
        tope_ref[k:k + 1, :] = idxs[k]
        gate_ref[k:k + 1, :] = exps[k] / denom


_EPI_OUT_SHAPES = (
    jax.ShapeDtypeStruct((N, D), F32),
    jax.ShapeDtypeStruct((N * ROW_TILE, 128), F32),
    jax.ShapeDtypeStruct((TOP_K, N), jnp.int32),
    jax.ShapeDtypeStruct((TOP_K, N), F32),
)


def _epi_in_specs(h):
    h_specs, _ = _stream_specs(TILE, P_TILES, h)
    return h_specs + [
        pl.BlockSpec((None, 6, D), lambda r: (_tile_mod_row(r), 0, 0)),
        pl.BlockSpec((1, D), lambda r: (0, 0)),
        pl.BlockSpec((N_EXP, D), lambda r: (0, 0)),
        pl.BlockSpec((N_EXP, 1), lambda r: (0, 0)),
    ]


def _epi_out_specs():
    return [
        pl.BlockSpec((TILE, D), lambda r: (r, 0)),
        pl.BlockSpec((TILE * ROW_TILE, 128), lambda r: (r, 0)),
        pl.BlockSpec((TOP_K, TILE), lambda r: (0, r)),
        pl.BlockSpec((TOP_K, TILE), lambda r: (0, r)),
    ]


def _post0_kernel(hs_ref, xg_ref, yfp_ref, yfl_ref, wo_ref, hp_ref, hl_ref, m_ref, gf_ref, wr_ref, brt_ref,
                  hout_ref, fn_ref, tope_ref, gate_ref):
    y_rec = (hs_ref[0] + hs_ref[1]) * _gelu(xg_ref[...])
    y_four = jnp.where(pl.program_id(0) < P_TILES, yfp_ref[...], yfl_ref[...])
    mix = (_dot(y_rec.astype(BF16), wo_ref[0:LRU_W, :])
           + _dot(y_four.astype(BF16), wo_ref[LRU_W:, :]))
    _epilogue(mix, hp_ref, hl_ref, m_ref, gf_ref, wr_ref, brt_ref, hout_ref, fn_ref, tope_ref, gate_ref)


def _post0(hs, proj0, yf_prompt, yf_latent, w_out_bf16, h, mod, g_ffn, w_router_t, b_router):
    return pl.pallas_call(
        _post0_kernel,
        out_shape=_EPI_OUT_SHAPES,
        grid=(N_TILES,),
        in_specs=[
            pl.BlockSpec((2, TILE, LRU_W), lambda r: (0, r, 0)),
            pl.BlockSpec((TILE, LRU_W), lambda r: (r, 1)),
            pl.BlockSpec((TILE, FN_W), lambda r: (jnp.minimum(r, P_TILES - 1), 0)),
            pl.BlockSpec((TILE, FN_W), lambda r: (jnp.maximum(r - P_TILES, 0), 0)),
            pl.BlockSpec((D, D), lambda r: (0, 0)),
        ] + _epi_in_specs(h),
        out_specs=_epi_out_specs(),
        compiler_params=_cp(("arbitrary",), VMEM_LIMIT),
        name="post_rglru_fourier",
    )(hs, proj0, yf_prompt, yf_latent, w_out_bf16, *_stream_specs(TILE, P_TILES, h)[1], mod, g_ffn.reshape(1, D), w_router_t,
      b_router.reshape(N_EXP, 1))


def _glu(x):
    return x[:, :CONV_W] * _sigmoid(x[:, CONV_W:])


def _post1_kernel(p_ref, prev_ref, next_ref, lng_ref, lnb_ref, ws_ref, bs_ref, dww_ref, dwb_ref,
                  clg_ref, clb_ref, wo_ref, hp_ref, hl_ref, m_ref, gf_ref, wr_ref, brt_ref,
                  hout_ref, fn_ref, tope_ref, gate_ref, ext_ref):
    r = pl.program_id(0)
    z = _gelu(p_ref[:, 0:2 * SGU_W])
    u = z[:, :SGU_W]
    v = _layernorm(z[:, SGU_W:], lng_ref[...], lnb_ref[...]).astype(BF16)
    rows = []
    for n in range(TILE // CHUNK):
        cols = []
        for g in range(SGU_G):
            vb = v[n * CHUNK:(n + 1) * CHUNK, g * CHUNK:(g + 1) * CHUNK]
            cols.append(_dot(ws_ref[g], vb))
        rows.append(jnp.concatenate(cols, axis=1) + bs_ref[...])
    y_sgu = u * jnp.concatenate(rows, axis=0)

    ext_ref[0:CONV_HALO, :] = jnp.where(_tile_is_seq_start(r), 0.0, _glu(prev_ref[...]))
    ext_ref[CONV_HALO:CONV_HALO + TILE, :] = _glu(p_ref[:, 2 * SGU_W:])
    ext_ref[CONV_HALO + TILE:, :] = jnp.where(_tile_is_seq_end(r), 0.0, _glu(next_ref[...]))
    dconv = dwb_ref[...] + jnp.zeros((TILE, CONV_W), F32)
    for k in range(CONV_K):
        off = CONV_HALO - CONV_PAD + k
        dconv = dconv + ext_ref[off:off + TILE, :] * dww_ref[k:k + 1, :]
    ln = _layernorm(dconv, clg_ref[...], clb_ref[...])
    y_conv = ln * _sigmoid(ln)

    mix = (_dot(y_sgu.astype(BF16), wo_ref[0:SGU_W, :])
           + _dot(y_conv.astype(BF16), wo_ref[SGU_W:, :]))
    _epilogue(mix, hp_ref, hl_ref, m_ref, gf_ref, wr_ref, brt_ref, hout_ref, fn_ref, tope_ref, gate_ref)


def _post1(proj1, ln_g, ln_b, ws_bf16, bs_full, dw_w, dw_b, cln_g, cln_b, w_out_bf16,
           h, mod, g_ffn, w_router_t, b_router):
    per_tile = TILE // CONV_HALO
    n_halo_blocks = N // CONV_HALO
    const2 = lambda r: (0, 0)
    row = lambda a: a.reshape(1, -1)
    return pl.pallas_call(
        _post1_kernel,
        out_shape=_EPI_OUT_SHAPES,
        grid=(N_TILES,),
        in_specs=[
            pl.BlockSpec((TILE, IN1), lambda r: (r, 0)),
            pl.BlockSpec((CONV_HALO, 2 * CONV_W), lambda r: (jnp.maximum(r * per_tile - 1, 0), 1)),
            pl.BlockSpec((CONV_HALO, 2 * CONV_W),
                         lambda r: (jnp.minimum((r + 1) * per_tile, n_halo_blocks - 1), 1)),
            pl.BlockSpec((1, SGU_W), const2),
            pl.BlockSpec((1, SGU_W), const2),
            pl.BlockSpec((SGU_G, CHUNK, CHUNK), lambda r: (0, 0, 0)),
            pl.BlockSpec((CHUNK, SGU_W), const2),
            pl.BlockSpec((CONV_K, CONV_W), const2),
            pl.BlockSpec((1, CONV_W), const2),
            pl.BlockSpec((1, CONV_W), const2),
            pl.BlockSpec((1, CONV_W), const2),
            pl.BlockSpec((D, D), const2),
        ] + _epi_in_specs(h),
        out_specs=_epi_out_specs(),
        scratch_shapes=[pltpu.VMEM((TILE + 2 * CONV_HALO, CONV_W), F32)],
        compiler_params=_cp(("arbitrary",), VMEM_LIMIT),
        name="post_sgu_conformer",
    )(proj1, proj1, proj1, row(ln_g), row(ln_b), ws_bf16, bs_full, dw_w, row(dw_b), row(cln_g), row(cln_b),
      w_out_bf16, *_stream_specs(TILE, P_TILES, h)[1], mod, g_ffn.reshape(1, D), w_router_t,
      b_router.reshape(N_EXP, 1))


RANK_TILE = 512
PAD_BASE = N_ASSIGN


def _rank_kernel(e_ref, tri_ref, rank_ref, cnt_ref, run_ref):
    @pl.when(pl.program_id(0) == 0)
    def _():
        run_ref[...] = jnp.zeros_like(run_ref)

    iota = lax.broadcasted_iota(jnp.int32, (N_EXP, RANK_TILE), 0)
    run = run_ref[...]
    for k in range(TOP_K):
        onehot = jnp.where(iota == e_ref[k:k + 1, :], 1.0, 0.0)
        incl = _dot(onehot.astype(BF16), tri_ref[...])
        rank = jnp.sum(onehot * (incl - 1.0 + run), axis=0, keepdims=True)
        rank_ref[k:k + 1, :] = rank.astype(jnp.int32)
        run = run + incl[:, RANK_TILE - 1:RANK_TILE]
    run_ref[...] = run
    cnt_ref[...] = run.astype(jnp.int32)


def _rank(top_e):
    tri = jnp.asarray(np.triu(np.ones((RANK_TILE, RANK_TILE), np.float32)), BF16)
    return pl.pallas_call(
        _rank_kernel,
        out_shape=(jax.ShapeDtypeStruct((TOP_K, N), jnp.int32), jax.ShapeDtypeStruct((N_EXP, 1), jnp.int32)),
        grid=(N // RANK_TILE,),
        in_specs=[
            pl.BlockSpec((TOP_K, RANK_TILE), lambda i: (0, i)),
            pl.BlockSpec((RANK_TILE, RANK_TILE), lambda i: (0, 0)),
        ],
        out_specs=[
            pl.BlockSpec((TOP_K, RANK_TILE), lambda i: (0, i)),
            pl.BlockSpec((N_EXP, 1), lambda i: (0, 0)),
        ],
        scratch_shapes=[pltpu.VMEM((N_EXP, 1), F32)],
        compiler_params=_cp(("arbitrary",)),
        name="moe_rank",
    )(top_e, tri)


def _inverse_kernel(pos_ref, pend_ref, out_ref):
    def fill_block(start):
        def fill(j, carry):
            s = start + j
            out_ref[s] = PAD_BASE + (s & (2 * MOE_BLOCK - 1))
            return carry
        lax.fori_loop(0, MOE_BLOCK, fill, 0, unroll=16)

    def per_expert(e, carry):
        fill_block(jnp.maximum(pend_ref[e] - MOE_BLOCK, 0))
        return carry

    def idle_block(b, carry):
        fill_block(b * MOE_BLOCK)
        return carry

    lax.fori_loop(0, N_EXP, per_expert, 0)
    lax.fori_loop(pend_ref[N_EXP - 1] // MOE_BLOCK, N_BLOCKS, idle_block, 0)

    def body(t, carry):
        for k in range(TOP_K):
            out_ref[pos_ref[k * N + t]] = t * TOP_K + k
        return carry

    lax.fori_loop(0, N, body, 0, unroll=8)


def _inverse(pos_flat, pad_end):
    smem = pl.BlockSpec(memory_space=pltpu.SMEM)
    return pl.pallas_call(
        _inverse_kernel,
        out_shape=jax.ShapeDtypeStruct((N_SLOTS,), jnp.int32),
        in_specs=[smem, smem],
        out_specs=smem,
        name="moe_inverse",
    )(pos_flat, pad_end)


def _routing_tables(top_e):
    rank, counts = _rank(top_e)
    counts = counts.reshape(N_EXP)
    padded = (counts + MOE_BLOCK - 1) // MOE_BLOCK * MOE_BLOCK
    pad_end = jnp.cumsum(padded).astype(jnp.int32)
    pad_start = pad_end - padded
    experts = jnp.arange(N_EXP, dtype=jnp.int32)[:, None, None]
    pos = rank + jnp.sum(jnp.where(top_e[None] == experts, pad_start[:, None, None], 0), axis=0)
    pos = pos.astype(jnp.int32).reshape(-1)
    n_used = pad_end[-1] // MOE_BLOCK
    blk = jnp.minimum(jnp.arange(N_BLOCKS, dtype=jnp.int32), n_used - 1)
    block_e = jnp.sum(pad_end[None, :] <= (blk * MOE_BLOCK)[:, None], axis=1).astype(jnp.int32)
    slot_a = _inverse(pos, pad_end)
    return pos, slot_a, block_e, n_used.reshape(1)


def _moe_kernel(be_ref, nu_ref, sa_ref, x_hbm, w1_ref, b1g_ref, b1l_ref, w2_ref, b2_ref, perm_ref,
                o_ref, xbuf, sem, w1s, w2s):
    i = pl.program_id(0)
    n_used = nu_ref[0]
    slot = i % 2

    def tile_copy(blk, j, dst_slot):
        tok = jnp.minimum(sa_ref[blk * MOE_BLOCK + j] >> 2, N - 1)
        src = x_hbm.at[pl.ds(pl.multiple_of(tok * ROW_TILE, ROW_TILE), ROW_TILE), :]
        dst = xbuf.at[dst_slot, pl.ds(pl.multiple_of(j * ROW_TILE, ROW_TILE), ROW_TILE), :]
        return pltpu.make_async_copy(src, dst, sem.at[dst_slot])

    @pl.when(i == 0)
    def _():
        def body(j, carry):
            tile_copy(0, j, 0).start()
            return carry
        lax.fori_loop(0, MOE_BLOCK, body, 0, unroll=8)

    e = be_ref[i]
    e_prev = be_ref[jnp.maximum(i - 1, 0)]

    @pl.when((i < n_used) & ((i == 0) | (e != e_prev)))
    def _():
        for c in range(2 * D_FF // 256):
            wc = w1_ref[:, c * 256:(c + 1) * 256].astype(BF16)
            w1s[:, c * 256:(c + 1) * 256] = _dot(wc, perm_ref[...]).astype(BF16)
        w2s[...] = w2_ref[...].astype(BF16)

    def block(prefetch_next):
        pltpu.make_async_copy(x_hbm.at[pl.ds(0, MOE_BLOCK * ROW_TILE), :], xbuf.at[slot], sem.at[slot]).wait()
        if prefetch_next:
            for j in range(MOE_BLOCK):
                tile_copy(i + 1, j, 1 - slot).start()
        x = jnp.concatenate([xbuf[slot, pl.ds(s, MOE_BLOCK, stride=ROW_TILE), :] for s in range(ROW_TILE)],
                            axis=1).astype(BF16)
        hid = _dot(x, w1s[...])
        acts = []
        for c in range(D_FF // 128):
            h_glu = hid[:, c * 256:c * 256 + 128] + b1g_ref[:, c * 128:(c + 1) * 128]
            h_lin = hid[:, c * 256 + 128:(c + 1) * 256] + b1l_ref[:, c * 128:(c + 1) * 128]
            h_glu = jnp.minimum(h_glu, LIMIT)
            h_lin = jnp.clip(h_lin, -LIMIT, LIMIT)
            acts.append((h_glu * _sigmoid(ALPHA * h_glu) * (h_lin + 1.0)).astype(BF16))
        y = _dot(jnp.concatenate(acts, axis=1), w2s[...]) + b2_ref[...]
        for s in range(ROW_TILE):
            o_ref[pl.ds(s, MOE_BLOCK, stride=ROW_TILE), :] = y[:, s * 128:(s + 1) * 128]

    @pl.when(i + 1 < n_used)
    def _():
        block(True)

    @pl.when(i + 1 == n_used)
    def _():
        block(False)

    @pl.when(i >= n_used)
    def _():
        o_ref[...] = jnp.zeros_like(o_ref)


def _deinterleave_matrix():
    p = np.zeros((256, 256), np.float32)
    m = np.arange(128)
    p[2 * m, m] = 1.0
    p[2 * m + 1, 128 + m] = 1.0
    return jnp.asarray(p, BF16)


def _moe_experts(layer, fn_tiles, slot_a, block_e, n_used, w1, b1, w2, b2):
    b1g = b1[layer][:, 0::2].reshape(N_EXP, 1, D_FF)
    b1l = b1[layer][:, 1::2].reshape(N_EXP, 1, D_FF)
    ex = lambda i, be, nu, sa: (be[i], 0, 0)
    lex = lambda i, be, nu, sa: (layer, be[i], 0, 0)
    grid_spec = pltpu.PrefetchScalarGridSpec(
        num_scalar_prefetch=3,
        grid=(N_BLOCKS,),
        in_specs=[
            pl.BlockSpec(memory_space=pl.ANY),
            pl.BlockSpec((None, None, D, 2 * D_FF), lex),
            pl.BlockSpec((None, 1, D_FF), ex),
            pl.BlockSpec((None, 1, D_FF), ex),
            pl.BlockSpec((None, None, D_FF, D), lex),
            pl.BlockSpec((None, 1, D), ex),
            pl.BlockSpec((256, 256), lambda i, be, nu, sa: (0, 0)),
        ],
        out_specs=pl.BlockSpec((MOE_BLOCK * ROW_TILE, 128), lambda i, be, nu, sa: (i, 0)),
        scratch_shapes=[
            pltpu.VMEM((2, MOE_BLOCK * ROW_TILE, 128), F32),
            pltpu.SemaphoreType.DMA((2,)),
            pltpu.VMEM((D, 2 * D_FF), BF16),
            pltpu.VMEM((D_FF, D), BF16),
        ],
    )
    return pl.pallas_call(
        _moe_kernel,
        out_shape=jax.ShapeDtypeStruct((N_SLOTS * ROW_TILE, 128), F32),
        grid_spec=grid_spec,
        compiler_params=_cp(("arbitrary",), VMEM_LIMIT),
        name="moe_experts",
    )(block_e, n_used, slot_a, fn_tiles, w1, b1g, b1l, w2, b2[layer].reshape(N_EXP, 1, D), _deinterleave_matrix())


def _combine_kernel(pos_ref, yb_hbm, g_ref, h_ref, m_ref, fg_ref, *out_and_scratch, final):
    if final:
        yp_ref, yl_ref, ybuf, sem = out_and_scratch
    else:
        o_ref, ybuf, sem = out_and_scratch
    i = pl.program_id(0)
    slot = i % 2

    def tile_copy(tile, j, k, dst_slot):
        row = pos_ref[k * N + tile * TILE + j]
        src = yb_hbm.at[pl.ds(pl.multiple_of(row * ROW_TILE, ROW_TILE), ROW_TILE), :]
        dst = ybuf.at[dst_slot, k, pl.ds(pl.multiple_of(j * ROW_TILE, ROW_TILE), ROW_TILE), :]
        return pltpu.make_async_copy(src, dst, sem.at[dst_slot])

    @pl.when(i == 0)
    def _():
        def body(j, carry):
            for k in range(TOP_K):
                tile_copy(0, j, k, 0).start()
            return carry
        lax.fori_loop(0, TILE, body, 0, unroll=4)

    def tile_step(prefetch_next):
        for k in range(TOP_K):
            pltpu.make_async_copy(yb_hbm.at[pl.ds(0, TILE * ROW_TILE), :], ybuf.at[slot, k], sem.at[slot]).wait()
        if prefetch_next:
            for j in range(TILE):
                for k in range(TOP_K):
                    tile_copy(i + 1, j, k, 1 - slot).start()
        g = g_ref[...]
        cols = []
        for s in range(ROW_TILE):
            y = ybuf[slot, 0, pl.ds(s, TILE, stride=ROW_TILE), :] * g[:, 0:1]
            for k in range(1, TOP_K):
                y = y + ybuf[slot, k, pl.ds(s, TILE, stride=ROW_TILE), :] * g[:, k:k + 1]
            cols.append(y)
        h_new = h_ref[...] + m_ref[5:6, :] * jnp.concatenate(cols, axis=1)
        if final:
            ms = jnp.mean(h_new * h_new, axis=-1, keepdims=True)
            y = h_new * lax.rsqrt(ms + EPS) * fg_ref[...]

            @pl.when(i < P_TILES)
            def _():
                yp_ref[...] = y

            @pl.when(i >= P_TILES)
            def _():
                yl_ref[...] = y
        else:
            o_ref[...] = h_new

    @pl.when(i + 1 < N_TILES)
    def _():
        tile_step(True)

    @pl.when(i + 1 == N_TILES)
    def _():
        tile_step(False)


def _moe_combine(yb_tiles, pos, gates_nt, h, mod, final_g, final):
    if final:
        out_shape = (jax.ShapeDtypeStruct((N_P, D), F32), jax.ShapeDtypeStruct((N_S, D), F32))
        out_specs = [
            pl.BlockSpec((TILE, D), lambda i, pos: (jnp.minimum(i, P_TILES - 1), 0)),
            pl.BlockSpec((TILE, D), lambda i, pos: (jnp.maximum(i - P_TILES, 0), 0)),
        ]
    else:
        out_shape = jax.ShapeDtypeStruct((N, D), F32)
        out_specs = pl.BlockSpec((TILE, D), lambda i, pos: (i, 0))
    grid_spec = pltpu.PrefetchScalarGridSpec(
        num_scalar_prefetch=1,
        grid=(N_TILES,),
        in_specs=[
            pl.BlockSpec(memory_space=pl.ANY),
            pl.BlockSpec((TILE, TOP_K), lambda i, pos: (i, 0)),
            pl.BlockSpec((TILE, D), lambda i, pos: (i, 0)),
            pl.BlockSpec((None, 6, D), lambda i, pos: (_tile_mod_row(i), 0, 0)),
            pl.BlockSpec((1, D), lambda i, pos: (0, 0)),
        ],
        out_specs=out_specs,
        scratch_shapes=[pltpu.VMEM((2, TOP_K, TILE * ROW_TILE, 128), F32), pltpu.SemaphoreType.DMA((2,))],
    )
    return pl.pallas_call(
        functools.partial(_combine_kernel, final=final),
        out_shape=out_shape,
        grid_spec=grid_spec,
        compiler_params=_cp(("arbitrary",), VMEM_LIMIT),
        name="moe_combine",
    )(pos, yb_tiles, gates_nt, h, mod, final_g.reshape(1, D))


def _moe(layer, fn_tiles, top_e, gates, h, mod, w1, b1, w2, b2, final_g, final):
    pos, slot_a, block_e, n_used = _routing_tables(top_e)
    yb_tiles = _moe_experts(layer, fn_tiles, slot_a, block_e, n_used, w1, b1, w2, b2)
    return _moe_combine(yb_tiles, pos, gates.T, h, mod, final_g, final)


def _gate_weights(w_r, w_i):
    per = LRU_BLK // LRU_HD
    eye = jnp.eye(per, dtype=F32)

    def blockdiag(w):
        w = w.reshape(2, LRU_HEADS // per, per, LRU_HD, LRU_HD)
        full = jnp.einsum("dgaij,ab->dgaibj", w, eye)
        return full.reshape(2, LRU_HEADS // per, LRU_BLK, LRU_BLK)

    return jnp.concatenate([blockdiag(w_r), blockdiag(w_i)], axis=-1).astype(BF16)


def kernel(x_prompt, x_sample, state_rglru, c, c_ctx, norm_mix_g, norm_ffn_g, w_mod, b_mod, w_in0, lru_conv_w, lru_conv_b, lru_w_r, lru_b_r, lru_w_i, lru_b_i, lru_lambda, w_out0, w_in1, sgu_ln_g, sgu_ln_b, sgu_w_s, sgu_b_s, conv_dw_w, conv_dw_b, conv_ln_g, conv_ln_b, w_out1, w_router, b_router, w1, b1, w2, b2, final_norm_g):
    h = (x_prompt.reshape(N_P, D), x_sample.reshape(N_S, D))
    cond8 = jnp.concatenate([c_ctx[None, :], c, jnp.zeros((N_MOD - 1 - N_SAMPLE_SEQ, D), F32)], axis=0)
    mod = _adaln(cond8, w_mod, b_mod)

    proj0 = _inproj(h, norm_mix_g[0], mod[0], w_in0[0].astype(BF16))
    st = state_rglru[:, 0].astype(F32)
    h0 = jnp.zeros((2, N_MOD, LRU_W), F32).at[:, 1:1 + N_SAMPLE_SEQ].set(jnp.swapaxes(st, 0, 1))
    hs, ctx_state = _lru(proj0, lru_conv_w[0], lru_conv_b[0], _gate_weights(lru_w_r[0], lru_w_i[0]),
              lru_b_r[0], lru_b_i[0], lru_lambda[0], h0)
    h, fn, top_e, gates = _post0(hs, proj0, _fourier_prompt(proj0), _fourier_sample(proj0), w_out0[0].astype(BF16), h, mod[0], norm_ffn_g[0],
                                 w_router[0].T, b_router[0])
    h = _moe(0, fn, top_e, gates, h, mod[0], w1, b1, w2, b2, final_norm_g, False)

    proj1 = _inproj(h, norm_mix_g[1], mod[1], w_in1[0].astype(BF16))
    bs_full = jnp.repeat(sgu_b_s[0].T, CHUNK, axis=1)
    h, fn, top_e, gates = _post1(proj1, sgu_ln_g[0], sgu_ln_b[0], sgu_w_s[0].astype(BF16), bs_full,
                                 conv_dw_w[0], conv_dw_b[0], conv_ln_g[0], conv_ln_b[0],
                                 w_out1[0].astype(BF16), h, mod[1], norm_ffn_g[1], w_router[1].T, b_router[1])
    y_p, y_l = _moe(1, fn, top_e, gates, h, mod[1], w1, b1, w2, b2, final_norm_g, True)

    y_prompt = y_p.reshape(N_PROMPT_SEQ, T_PROMPT, D)
    y_sample = y_l.reshape(N_SAMPLE_SEQ, T_SAMPLE, D)
    new_state = jnp.transpose(ctx_state, (1, 2, 0, 3))
    return (y_prompt, y_sample, new_state.astype(x_prompt.dtype))
```

```python
import functools

import numpy as np
import jax
import jax.numpy as jnp
from jax import lax
from jax.experimental import pallas as pl
from jax.experimental.pallas import tpu as pltpu

F32 = jnp.float32
BF16 = jnp.bfloat16

D = 1024
N_PROMPT_SEQ = 32
T_PROMPT = 256
N_SAMPLE_SEQ = 2
T_SAMPLE = 2048
N_P = N_PROMPT_SEQ * T_PROMPT
N_S = N_SAMPLE_SEQ * T_SAMPLE
N = N_P + N_S
EPS = 1e-6

TILE = 256
N_TILES = N // TILE
P_TILES = N_P // TILE
S_TILES = T_SAMPLE // TILE
TM_PROJ = 512
N_MOD = 8

LRU_W = 768
LRU_HEADS = 12
LRU_HD = 64
LRU_K = 4
LRU_LEFT = 2
LRU_C = 8.0
LRU_BLK = 256
FN_W = 256
FN_G = 4
FN_GD = 64
IN0 = 2 * LRU_W + FN_W

SGU_W = 512
SGU_G = 4
CHUNK = 128
CONV_W = 512
CONV_K = 31
CONV_PAD = 15
CONV_HALO = 16
IN1 = 2 * SGU_W + 2 * CONV_W

N_EXP = 32
TOP_K = 4
D_FF = 1024
ALPHA = 1.702
LIMIT = 7.0
MOE_BLOCK = 256
N_ASSIGN = N * TOP_K
N_BLOCKS = N_ASSIGN // MOE_BLOCK + N_EXP
N_SLOTS = N_BLOCKS * MOE_BLOCK
ROW_TILE = D // 128
PACK_ROWS = D // 2 // 128
PACK_PAIR = 2 * PACK_ROWS

VMEM_LIMIT = 56 * 1024 * 1024
MOE_VMEM_LIMIT = 60 * 1024 * 1024


def _cp(sem, vmem=None):
    return pltpu.CompilerParams(dimension_semantics=sem, vmem_limit_bytes=vmem)


def _dot(a, b):
    return jnp.dot(a, b, preferred_element_type=F32)


def _split(x):
    hi = x.astype(BF16)
    lo = (x - hi.astype(F32)).astype(BF16)
    return hi, lo


def _dot3(a, b):
    ah, al = _split(a)
    bh, bl = _split(b)
    return _dot(ah, bh) + _dot(al, bh) + _dot(ah, bl)


def _dot3_nt(a, b):
    dn = (((1,), (1,)), ((), ()))
    d = lambda x, y: lax.dot_general(x, y, dn, preferred_element_type=F32)
    ah, al = _split(a)
    bh, bl = _split(b)
    return d(ah, bh) + d(al, bh) + d(ah, bl)


def _sigmoid(x):
    return 1.0 / (1.0 + jnp.exp(-x))


def _gelu(x):
    return 0.5 * x * (1.0 + jnp.tanh(0.7978845608028654 * (x + 0.044715 * (x * x * x))))


def _rms_mod(x, g, shift, scale):
    ms = jnp.mean(x * x, axis=-1, keepdims=True)
    y = x * lax.rsqrt(ms + EPS) * g
    return y * (1.0 + scale) + shift


def _layernorm(x, g, b):
    xc = x - jnp.mean(x, axis=-1, keepdims=True)
    var = jnp.mean(xc * xc, axis=-1, keepdims=True)
    return xc * lax.rsqrt(var + EPS) * g + b


def _tile_mod_row(r):
    return jnp.where(r < P_TILES, 0, 1 + (r - P_TILES) // S_TILES)


def _tile_is_seq_start(r):
    return (r < P_TILES) | ((r - P_TILES) % S_TILES == 0)


def _tile_is_seq_end(r):
    return (r < P_TILES) | ((r - P_TILES) % S_TILES == S_TILES - 1)


MOD_TN = 512


def _adaln_kernel(cond_ref, w_ref, b_ref, o_ref):
    cond = cond_ref[...]
    s = cond * _sigmoid(cond)
    o_ref[...] = _dot3(s, w_ref[...]) + b_ref[...]


def _adaln(cond8, w_mod, b_mod):
    depth = w_mod.shape[0]
    out = pl.pallas_call(
        _adaln_kernel,
        out_shape=jax.ShapeDtypeStruct((depth, N_MOD, 6 * D), F32),
        grid=(depth, 6 * D // MOD_TN),
        in_specs=[
            pl.BlockSpec((N_MOD, D), lambda l, j: (0, 0)),
            pl.BlockSpec((None, D, MOD_TN), lambda l, j: (l, 0, j)),
            pl.BlockSpec((None, 1, MOD_TN), lambda l, j: (l, 0, j)),
        ],
        out_specs=pl.BlockSpec((None, N_MOD, MOD_TN), lambda l, j: (l, 0, j)),
        compiler_params=_cp(("arbitrary", "arbitrary")),
        name="adaln",
    )(cond8, w_mod, b_mod.reshape(depth, 1, 6 * D))
    return out.reshape(depth, N_MOD, 6, D)


PROJ_P_STEPS = N_P // TM_PROJ
PROJ_S_STEPS = T_SAMPLE // TM_PROJ


def _stream_specs(rows, p_steps, h):
    if isinstance(h, tuple):
        hp, hl = h
        first_latent = 0
    else:
        hp = hl = h
        first_latent = p_steps
    specs = [
        pl.BlockSpec((rows, D), lambda i, *_: (jnp.minimum(i, p_steps - 1), 0)),
        pl.BlockSpec((rows, D), lambda i, *_: (jnp.maximum(i - p_steps, 0) + first_latent, 0)),
    ]
    return specs, (hp, hl)


def _inproj_kernel(xp_ref, xl_ref, g_ref, m_ref, w_ref, o_ref):
    x = jnp.where(pl.program_id(0) < PROJ_P_STEPS, xp_ref[...], xl_ref[...])
    hn = _rms_mod(x, g_ref[...], m_ref[0:1, :], m_ref[1:2, :])
    o_ref[...] = _dot(hn.astype(BF16), w_ref[...])


def _inproj(h, g, mod, w_bf16):
    n_out = w_bf16.shape[1]

    def mod_row(i):
        return jnp.where(i < PROJ_P_STEPS, 0, 1 + (i - PROJ_P_STEPS) // PROJ_S_STEPS)

    h_specs, h_args = _stream_specs(TM_PROJ, PROJ_P_STEPS, h)
    return pl.pallas_call(
        _inproj_kernel,
        out_shape=jax.ShapeDtypeStruct((N, n_out), F32),
        grid=(N // TM_PROJ,),
        in_specs=h_specs + [
            pl.BlockSpec((1, D), lambda i: (0, 0)),
            pl.BlockSpec((None, 6, D), lambda i: (mod_row(i), 0, 0)),
            pl.BlockSpec((D, n_out), lambda i: (0, 0)),
        ],
        out_specs=pl.BlockSpec((TM_PROJ, n_out), lambda i: (i, 0)),
        compiler_params=_cp(("arbitrary",), VMEM_LIMIT),
        name="inproj",
    )(*h_args, g.reshape(1, D), mod, w_bf16)


LRU_HALO = 8


def _lru_tile(d, s):
    return jnp.where(d == 0, s, N_TILES - 1 - s)


def _lru_kernel(x_ref, prev_ref, next_ref, cw_ref, cb_ref, wg_ref, br_ref, bi_ref, lam_ref, h0_ref,
                o_ref, state_ref, ext_ref, a_ref, carry_ref):
    d = pl.program_id(0)
    r = _lru_tile(d, pl.program_id(1))
    start = _tile_is_seq_start(r)
    end = _tile_is_seq_end(r)

    ext_ref[0:LRU_HALO, :] = jnp.where(start, 0.0, prev_ref[...])
    ext_ref[LRU_HALO:LRU_HALO + TILE, :] = x_ref[...]
    ext_ref[LRU_HALO + TILE:, :] = jnp.where(end, 0.0, next_ref[...])
    xc = cb_ref[...] + jnp.zeros((TILE, LRU_W), F32)
    for k in range(LRU_K):
        off = LRU_HALO - LRU_LEFT + k
        xc = xc + ext_ref[off:off + TILE, :] * cw_ref[k:k + 1, :]

    xcb = xc.astype(BF16)
    pre_r, pre_i = [], []
    for blk in range(LRU_W // LRU_BLK):
        g = _dot(xcb[:, blk * LRU_BLK:(blk + 1) * LRU_BLK], wg_ref[blk])
        pre_r.append(g[:, :LRU_BLK])
        pre_i.append(g[:, LRU_BLK:])
    gate_r = _sigmoid(jnp.concatenate(pre_r, axis=1) + br_ref[...])
    gate_i = _sigmoid(jnp.concatenate(pre_i, axis=1) + bi_ref[...])
    neg_lam = -lam_ref[...]
    softplus = jnp.maximum(neg_lam, 0.0) + jnp.log1p(jnp.exp(-jnp.abs(neg_lam)))
    log_a = (-LRU_C) * gate_r * softplus
    a = jnp.exp(log_a)
    a_ref[...] = a
    o_ref[...] = jnp.sqrt(-jnp.tanh(log_a) * (a * a + 1.0)) * (gate_i * xc)

    fresh = jnp.where(d == 0, start, end)
    h_init = jnp.where(fresh, h0_ref[pl.ds(_tile_mod_row(r), 1), :], carry_ref[...])

    def step(t, h):
        tt = jnp.where(d == 0, t, TILE - 1 - t)
        h = a_ref[pl.ds(tt, 1), :] * h + o_ref[pl.ds(tt, 1), :]
        o_ref[pl.ds(tt, 1), :] = h
        return h

    h_last = lax.fori_loop(0, TILE, step, h_init, unroll=8)
    carry_ref[...] = h_last

    @pl.when(r < P_TILES)
    def _():
        state_ref[...] = h_last


def _lru(proj0, conv_w, conv_b, wg, b_r, b_i, lam, h0):
    n_halo_blocks = N // LRU_HALO
    per_tile = TILE // LRU_HALO
    tile = lambda d, s: _lru_tile(d, s)
    return pl.pallas_call(
        _lru_kernel,
        out_shape=(jax.ShapeDtypeStruct((2, N, LRU_W), F32),
                   jax.ShapeDtypeStruct((2, N_PROMPT_SEQ, 1, LRU_W), F32)),
        grid=(2, N_TILES),
        in_specs=[
            pl.BlockSpec((TILE, LRU_W), lambda d, s: (tile(d, s), 0)),
            pl.BlockSpec((LRU_HALO, LRU_W), lambda d, s: (jnp.maximum(tile(d, s) * per_tile - 1, 0), 0)),
            pl.BlockSpec((LRU_HALO, LRU_W),
                         lambda d, s: (jnp.minimum((tile(d, s) + 1) * per_tile, n_halo_blocks - 1), 0)),
            pl.BlockSpec((LRU_K, LRU_W), lambda d, s: (0, 0)),
            pl.BlockSpec((1, LRU_W), lambda d, s: (0, 0)),
            pl.BlockSpec((None, LRU_W // LRU_BLK, LRU_BLK, 2 * LRU_BLK), lambda d, s: (d, 0, 0, 0)),
            pl.BlockSpec((None, 1, LRU_W), lambda d, s: (d, 0, 0)),
            pl.BlockSpec((None, 1, LRU_W), lambda d, s: (d, 0, 0)),
            pl.BlockSpec((None, 1, LRU_W), lambda d, s: (d, 0, 0)),
            pl.BlockSpec((None, N_MOD, LRU_W), lambda d, s: (d, 0, 0)),
        ],
        out_specs=[
            pl.BlockSpec((None, TILE, LRU_W), lambda d, s: (d, tile(d, s), 0)),
            pl.BlockSpec((None, None, 1, LRU_W), lambda d, s: (d, jnp.minimum(tile(d, s), P_TILES - 1), 0, 0)),
        ],
        scratch_shapes=[
            pltpu.VMEM((TILE + 2 * LRU_HALO, LRU_W), F32),
            pltpu.VMEM((TILE, LRU_W), F32),
            pltpu.VMEM((1, LRU_W), F32),
        ],
        compiler_params=_cp(("arbitrary", "arbitrary"), VMEM_LIMIT),
        name="rglru_scan",
    )(proj0, proj0, proj0, conv_w, conv_b.reshape(1, LRU_W), wg,
      b_r.reshape(2, 1, LRU_W), b_i.reshape(2, 1, LRU_W), lam.reshape(2, 1, LRU_W), h0)


def _dft_tables(n, scale):
    k = np.arange(n, dtype=np.int64)
    ang = 2.0 * np.pi * ((k[:, None] * k[None, :]) % n).astype(np.float64) / n
    return np.cos(ang) * scale, np.sin(ang) * scale


def _channel_tables():
    c, s = _dft_tables(FN_GD, FN_GD ** -0.5)
    eye = np.eye(FN_G)
    return (jnp.asarray(np.kron(eye, c), BF16), jnp.asarray(np.kron(eye, s), BF16))


def _time_tables(t_len):
    c, s = _dft_tables(t_len, t_len ** -0.5)
    return jnp.asarray(c, BF16), jnp.asarray(s, BF16)


def _fourier_prompt_kernel(z_ref, cc_ref, sc_ref, ct_ref, st_ref, o_ref):
    z = z_ref[...].astype(BF16)
    zc = _dot(z, cc_ref[...]).astype(BF16)
    zs = _dot(z, sc_ref[...]).astype(BF16)
    o_ref[...] = _dot(ct_ref[...], zc) - _dot(st_ref[...], zs)


def _fourier_prompt(proj0):
    cc, sc = _channel_tables()
    ct, st = _time_tables(T_PROMPT)
    const = lambda b: (0, 0)
    return pl.pallas_call(
        _fourier_prompt_kernel,
        out_shape=jax.ShapeDtypeStruct((N_P, FN_W), F32),
        grid=(N_PROMPT_SEQ,),
        in_specs=[
            pl.BlockSpec((T_PROMPT, FN_W), lambda b: (b, 2 * LRU_W // FN_W)),
            pl.BlockSpec((FN_W, FN_W), const),
            pl.BlockSpec((FN_W, FN_W), const),
            pl.BlockSpec((T_PROMPT, T_PROMPT), const),
            pl.BlockSpec((T_PROMPT, T_PROMPT), const),
        ],
        out_specs=pl.BlockSpec((T_PROMPT, FN_W), lambda b: (b, 0)),
        compiler_params=_cp(("arbitrary",)),
        name="fourier_prompt",
    )(proj0, cc, sc, ct, st)


def _fourier_sample_kernel(z_ref, cc_ref, sc_ref, ct_ref, st_ref, o_ref, zc_ref, zs_ref):
    @pl.when(pl.program_id(1) == 0)
    def _():
        z = z_ref[...].astype(BF16)
        zc_ref[...] = _dot(z, cc_ref[...]).astype(BF16)
        zs_ref[...] = _dot(z, sc_ref[...]).astype(BF16)

    o_ref[...] = _dot(ct_ref[...], zc_ref[...]) - _dot(st_ref[...], zs_ref[...])


def _fourier_sample(proj0):
    cc, sc = _channel_tables()
    ct, st = _time_tables(T_SAMPLE)
    const = lambda b, i: (0, 0)
    first_seq_block = N_P // T_SAMPLE
    return pl.pallas_call(
        _fourier_sample_kernel,
        out_shape=jax.ShapeDtypeStruct((N_S, FN_W), F32),
        grid=(N_SAMPLE_SEQ, S_TILES),
        in_specs=[
            pl.BlockSpec((T_SAMPLE, FN_W), lambda b, i: (first_seq_block + b, 2 * LRU_W // FN_W)),
            pl.BlockSpec((FN_W, FN_W), const),
            pl.BlockSpec((FN_W, FN_W), const),
            pl.BlockSpec((TILE, T_SAMPLE), lambda b, i: (i, 0)),
            pl.BlockSpec((TILE, T_SAMPLE), lambda b, i: (i, 0)),
        ],
        out_specs=pl.BlockSpec((TILE, FN_W), lambda b, i: (b * S_TILES + i, 0)),
        scratch_shapes=[pltpu.VMEM((T_SAMPLE, FN_W), BF16), pltpu.VMEM((T_SAMPLE, FN_W), BF16)],
        compiler_params=_cp(("arbitrary", "arbitrary"), VMEM_LIMIT),
        name="fourier_sample",
    )(proj0, cc, sc, ct, st)


def _epilogue(mix, hp_ref, hl_ref, m_ref, gf_ref, wr_ref, brt_ref, hout_ref, fn_ref, tope_ref, gate_ref):
    h = jnp.where(pl.program_id(0) < P_TILES, hp_ref[...], hl_ref[...])
    h_new = h + m_ref[2:3, :] * mix
    hout_ref[...] = h_new
    fn = _rms_mod(h_new, gf_ref[...], m_ref[3:4, :], m_ref[4:5, :])
    hi = lax.bitcast_convert_type(fn[:, :D // 2].astype(BF16).astype(F32), jnp.uint32)
    lo = lax.bitcast_convert_type(fn[:, D // 2:].astype(BF16).astype(F32), jnp.uint32)
    packed = hi | (lo >> 16)
    for q in range(PACK_ROWS):
        fn_ref[pl.ds(q, TILE, stride=PACK_ROWS), :] = packed[:, q * 128:(q + 1) * 128]
    logits = _dot3_nt(wr_ref[...], fn) + brt_ref[...]
    iota = lax.broadcasted_iota(jnp.int32, logits.shape, 0)
    vals, idxs = [], []
    for _ in range(TOP_K):
        m = jnp.max(logits, axis=0, keepdims=True)
        idx = jnp.min(jnp.where(logits == m, iota, N_EXP), axis=0, keepdims=True)
        vals.append(m)
        idxs.append(idx)
        logits = jnp.where(iota == idx, -jnp.inf, logits)
    exps = [jnp.exp(v - vals[0]) for v in vals]
    denom = exps[0] + exps[1] + exps[2] + exps[3]
    for k in range(TOP_K):
        tope_ref[k:k + 1, :] = idxs[k]
        gate_ref[k:k + 1, :] = exps[k] / denom


_EPI_OUT_SHAPES = (
    jax.ShapeDtypeStruct((N, D), F32),
    jax.ShapeDtypeStruct((N * PACK_ROWS, 128), jnp.uint32),
    jax.ShapeDtypeStruct((TOP_K, N), jnp.int32),
    jax.ShapeDtypeStruct((TOP_K, N), F32),
)


def _epi_in_specs(h):
    h_specs, _ = _stream_specs(TILE, P_TILES, h)
    return h_specs + [
        pl.BlockSpec((None, 6, D), lambda r: (_tile_mod_row(r), 0, 0)),
        pl.BlockSpec((1, D), lambda r: (0, 0)),
        pl.BlockSpec((N_EXP, D), lambda r: (0, 0)),
        pl.BlockSpec((N_EXP, 1), lambda r: (0, 0)),
    ]


def _epi_out_specs():
    return [
        pl.BlockSpec((TILE, D), lambda r: (r, 0)),
        pl.BlockSpec((TILE * PACK_ROWS, 128), lambda r: (r, 0)),
        pl.BlockSpec((TOP_K, TILE), lambda r: (0, r)),
        pl.BlockSpec((TOP_K, TILE), lambda r: (0, r)),
    ]


def _post0_kernel(hs_ref, xg_ref, yfp_ref, yfl_ref, wo_ref, hp_ref, hl_ref, m_ref, gf_ref, wr_ref, brt_ref,
                  hout_ref, fn_ref, tope_ref, gate_ref):
    y_rec = (hs_ref[0] + hs_ref[1]) * _gelu(xg_ref[...])
    y_four = jnp.where(pl.program_id(0) < P_TILES, yfp_ref[...], yfl_ref[...])
    mix = (_dot(y_rec.astype(BF16), wo_ref[0:LRU_W, :])
           + _dot(y_four.astype(BF16), wo_ref[LRU_W:, :]))
    _epilogue(mix, hp_ref, hl_ref, m_ref, gf_ref, wr_ref, brt_ref, hout_ref, fn_ref, tope_ref, gate_ref)


def _post0(hs, proj0, yf_prompt, yf_latent, w_out_bf16, h, mod, g_ffn, w_router_t, b_router):
    return pl.pallas_call(
        _post0_kernel,
        out_shape=_EPI_OUT_SHAPES,
        grid=(N_TILES,),
        in_specs=[
            pl.BlockSpec((2, TILE, LRU_W), lambda r: (0, r, 0)),
            pl.BlockSpec((TILE, LRU_W), lambda r: (r, 1)),
            pl.BlockSpec((TILE, FN_W), lambda r: (jnp.minimum(r, P_TILES - 1), 0)),
            pl.BlockSpec((TILE, FN_W), lambda r: (jnp.maximum(r - P_TILES, 0), 0)),
            pl.BlockSpec((D, D), lambda r: (0, 0)),
        ] + _epi_in_specs(h),
        out_specs=_epi_out_specs(),
        compiler_params=_cp(("arbitrary",), VMEM_LIMIT),
        name="post_rglru_fourier",
    )(hs, proj0, yf_prompt, yf_latent, w_out_bf16, *_stream_specs(TILE, P_TILES, h)[1], mod, g_ffn.reshape(1, D), w_router_t,
      b_router.reshape(N_EXP, 1))


def _glu(x):
    return x[:, :CONV_W] * _sigmoid(x[:, CONV_W:])


def _post1_kernel(p_ref, prev_ref, next_ref, lng_ref, lnb_ref, ws_ref, bs_ref, dww_ref, dwb_ref,
                  clg_ref, clb_ref, wo_ref, hp_ref, hl_ref, m_ref, gf_ref, wr_ref, brt_ref,
                  hout_ref, fn_ref, tope_ref, gate_ref, ext_ref):
    r = pl.program_id(0)
    z = _gelu(p_ref[:, 0:2 * SGU_W])
    u = z[:, :SGU_W]
    v = _layernorm(z[:, SGU_W:], lng_ref[...], lnb_ref[...]).astype(BF16)
    rows = []
    for n in range(TILE // CHUNK):
        cols = []
        for g in range(SGU_G):
            vb = v[n * CHUNK:(n + 1) * CHUNK, g * CHUNK:(g + 1) * CHUNK]
            cols.append(_dot(ws_ref[g], vb))
        rows.append(jnp.concatenate(cols, axis=1) + bs_ref[...])
    y_sgu = u * jnp.concatenate(rows, axis=0)

    ext_ref[0:CONV_HALO, :] = jnp.where(_tile_is_seq_start(r), 0.0, _glu(prev_ref[...]))
    ext_ref[CONV_HALO:CONV_HALO + TILE, :] = _glu(p_ref[:, 2 * SGU_W:])
    ext_ref[CONV_HALO + TILE:, :] = jnp.where(_tile_is_seq_end(r), 0.0, _glu(next_ref[...]))
    dconv = dwb_ref[...] + jnp.zeros((TILE, CONV_W), F32)
    for k in range(CONV_K):
        off = CONV_HALO - CONV_PAD + k
        dconv = dconv + ext_ref[off:off + TILE, :] * dww_ref[k:k + 1, :]
    ln = _layernorm(dconv, clg_ref[...], clb_ref[...])
    y_conv = ln * _sigmoid(ln)

    mix = (_dot(y_sgu.astype(BF16), wo_ref[0:SGU_W, :])
           + _dot(y_conv.astype(BF16), wo_ref[SGU_W:, :]))
    _epilogue(mix, hp_ref, hl_ref, m_ref, gf_ref, wr_ref, brt_ref, hout_ref, fn_ref, tope_ref, gate_ref)


def _post1(proj1, ln_g, ln_b, ws_bf16, bs_full, dw_w, dw_b, cln_g, cln_b, w_out_bf16,
           h, mod, g_ffn, w_router_t, b_router):
    per_tile = TILE // CONV_HALO
    n_halo_blocks = N // CONV_HALO
    const2 = lambda r: (0, 0)
    row = lambda a: a.reshape(1, -1)
    return pl.pallas_call(
        _post1_kernel,
        out_shape=_EPI_OUT_SHAPES,
        grid=(N_TILES,),
        in_specs=[
            pl.BlockSpec((TILE, IN1), lambda r: (r, 0)),
            pl.BlockSpec((CONV_HALO, 2 * CONV_W), lambda r: (jnp.maximum(r * per_tile - 1, 0), 1)),
            pl.BlockSpec((CONV_HALO, 2 * CONV_W),
                         lambda r: (jnp.minimum((r + 1) * per_tile, n_halo_blocks - 1), 1)),
            pl.BlockSpec((1, SGU_W), const2),
            pl.BlockSpec((1, SGU_W), const2),
            pl.BlockSpec((SGU_G, CHUNK, CHUNK), lambda r: (0, 0, 0)),
            pl.BlockSpec((CHUNK, SGU_W), const2),
            pl.BlockSpec((CONV_K, CONV_W), const2),
            pl.BlockSpec((1, CONV_W), const2),
            pl.BlockSpec((1, CONV_W), const2),
            pl.BlockSpec((1, CONV_W), const2),
            pl.BlockSpec((D, D), const2),
        ] + _epi_in_specs(h),
        out_specs=_epi_out_specs(),
        scratch_shapes=[pltpu.VMEM((TILE + 2 * CONV_HALO, CONV_W), F32)],
        compiler_params=_cp(("arbitrary",), VMEM_LIMIT),
        name="post_sgu_conformer",
    )(proj1, proj1, proj1, row(ln_g), row(ln_b), ws_bf16, bs_full, dw_w, row(dw_b), row(cln_g), row(cln_b),
      w_out_bf16, *_stream_specs(TILE, P_TILES, h)[1], mod, g_ffn.reshape(1, D), w_router_t,
      b_router.reshape(N_EXP, 1))


RANK_TILE = 512
PAD_BASE = N_ASSIGN


def _rank_kernel(e_ref, tri_ref, rank_ref, cnt_ref, run_ref):
    @pl.when(pl.program_id(0) == 0)
    def _():
        run_ref[...] = jnp.zeros_like(run_ref)

    iota = lax.broadcasted_iota(jnp.int32, (N_EXP, RANK_TILE), 0)
    run = run_ref[...]
    for k in range(TOP_K):
        onehot = jnp.where(iota == e_ref[k:k + 1, :], 1.0, 0.0)
        incl = _dot(onehot.astype(BF16), tri_ref[...])
        rank = jnp.sum(onehot * (incl - 1.0 + run), axis=0, keepdims=True)
        rank_ref[k:k + 1, :] = rank.astype(jnp.int32)
        run = run + incl[:, RANK_TILE - 1:RANK_TILE]
    run_ref[...] = run
    cnt_ref[...] = run.astype(jnp.int32)


def _rank(top_e):
    tri = jnp.asarray(np.triu(np.ones((RANK_TILE, RANK_TILE), np.float32)), BF16)
    return pl.pallas_call(
        _rank_kernel,
        out_shape=(jax.ShapeDtypeStruct((TOP_K, N), jnp.int32), jax.ShapeDtypeStruct((N_EXP, 1), jnp.int32)),
        grid=(N // RANK_TILE,),
        in_specs=[
            pl.BlockSpec((TOP_K, RANK_TILE), lambda i: (0, i)),
            pl.BlockSpec((RANK_TILE, RANK_TILE), lambda i: (0, 0)),
        ],
        out_specs=[
            pl.BlockSpec((TOP_K, RANK_TILE), lambda i: (0, i)),
            pl.BlockSpec((N_EXP, 1), lambda i: (0, 0)),
        ],
        scratch_shapes=[pltpu.VMEM((N_EXP, 1), F32)],
        compiler_params=_cp(("arbitrary",)),
        name="moe_rank",
    )(top_e, tri)


def _inverse_kernel(pos_ref, pend_ref, out_ref):
    def fill_block(start):
        def fill(j, carry):
            s = start + j
            out_ref[s] = PAD_BASE + (s & (2 * MOE_BLOCK - 1))
            return carry
        lax.fori_loop(0, MOE_BLOCK, fill, 0, unroll=16)

    def per_expert(e, carry):
        fill_block(jnp.maximum(pend_ref[e] - MOE_BLOCK, 0))
        return carry

    def idle_block(b, carry):
        fill_block(b * MOE_BLOCK)
        return carry

    lax.fori_loop(0, N_EXP, per_expert, 0)
    lax.fori_loop(pend_ref[N_EXP - 1] // MOE_BLOCK, N_BLOCKS, idle_block, 0)

    def body(t, carry):
        for k in range(TOP_K):
            out_ref[pos_ref[k * N + t]] = t * TOP_K + k
        return carry

    lax.fori_loop(0, N, body, 0, unroll=8)


def _inverse(pos_flat, pad_end):
    smem = pl.BlockSpec(memory_space=pltpu.SMEM)
    return pl.pallas_call(
        _inverse_kernel,
        out_shape=jax.ShapeDtypeStruct((N_SLOTS,), jnp.int32),
        in_specs=[smem, smem],
        out_specs=smem,
        name="moe_inverse",
    )(pos_flat, pad_end)


def _routing_tables(top_e):
    rank, counts = _rank(top_e)
    counts = counts.reshape(N_EXP)
    padded = (counts + MOE_BLOCK - 1) // MOE_BLOCK * MOE_BLOCK
    pad_end = jnp.cumsum(padded).astype(jnp.int32)
    pad_start = pad_end - padded
    experts = jnp.arange(N_EXP, dtype=jnp.int32)[:, None, None]
    pos = rank + jnp.sum(jnp.where(top_e[None] == experts, pad_start[:, None, None], 0), axis=0)
    pos = pos.astype(jnp.int32).reshape(-1)
    n_used = pad_end[-1] // MOE_BLOCK
    blk = jnp.minimum(jnp.arange(N_BLOCKS, dtype=jnp.int32), n_used - 1)
    block_e = jnp.sum(pad_end[None, :] <= (blk * MOE_BLOCK)[:, None], axis=1).astype(jnp.int32)
    slot_a = _inverse(pos, pad_end)
    ids = jnp.arange(N_EXP, dtype=jnp.int32)
    later_used = (ids[None, :] > ids[:, None]) & (counts[None, :] > 0)
    next_expert = jnp.min(jnp.where(later_used, ids[None, :], N_EXP), axis=1).astype(jnp.int32)
    return pos, slot_a, block_e, n_used.reshape(1), next_expert


def _moe_kernel(be_ref, nu_ref, sa_ref, nxt_ref, x_hbm, w1_hbm, b1g_ref, b1l_ref, w2_hbm, b2_ref, perm_ref,
                o_ref, xs, xbuf, w1f, w2f, w1s, w2s, sem, *, layer):
    i = pl.program_id(0)
    n_used = nu_ref[0]
    slot = i % 2
    e = be_ref[i]
    e_prev = be_ref[jnp.maximum(i - 1, 0)]
    low_half = lax.broadcasted_iota(jnp.int32, (PACK_PAIR, 128), 0) < PACK_ROWS

    def weight_copies(expert):
        return (pltpu.make_async_copy(w1_hbm.at[layer, expert], w1f, sem.at[1]),
                pltpu.make_async_copy(w2_hbm.at[layer, expert], w2f, sem.at[2]))

    def token_tile(a):
        tok = jnp.minimum(a >> 2, N - 1)
        tile = xs[pl.ds(pl.multiple_of((tok >> 1) * PACK_PAIR, PACK_PAIR), PACK_PAIR), :]
        return tile, pltpu.roll(tile, PACK_ROWS, 0), tok & 1

    def gather_pair(blk, jj, dst_slot):
        t0, r0, half0 = token_tile(sa_ref[blk * MOE_BLOCK + 2 * jj])
        t1, r1, half1 = token_tile(sa_ref[blk * MOE_BLOCK + 2 * jj + 1])
        lower = jnp.where(half0 == 0, t0, r0)
        upper = jnp.where(half1 == 1, t1, r1)
        start = jj * PACK_PAIR if isinstance(jj, int) else pl.multiple_of(jj * PACK_PAIR, PACK_PAIR)
        xbuf[dst_slot, pl.ds(start, PACK_PAIR), :] = jnp.where(low_half, lower, upper)

    @pl.when(i == 0)
    def _():
        xs_copy = pltpu.make_async_copy(x_hbm, xs, sem.at[0])
        xs_copy.start()
        for c in weight_copies(e):
            c.start()
        xs_copy.wait()

        def body(jj, carry):
            gather_pair(0, jj, 0)
            return carry
        lax.fori_loop(0, MOE_BLOCK // 2, body, 0, unroll=4)

    @pl.when((i < n_used) & ((i == 0) | (e != e_prev)))
    def _():
        for c in weight_copies(e):
            c.wait()
        for c in range(2 * D_FF // 256):
            wc = w1f[:, c * 256:(c + 1) * 256].astype(BF16)
            w1s[:, c * 256:(c + 1) * 256] = _dot(wc, perm_ref[...]).astype(BF16)
        w2s[...] = w2f[...].astype(BF16)
        e_next = nxt_ref[e]

        @pl.when(e_next < N_EXP)
        def _():
            for c in weight_copies(e_next):
                c.start()

    def block(prefetch_next):
        halves = [[], []]
        for q in range(PACK_ROWS):
            words = xbuf[slot, pl.ds(q, MOE_BLOCK, stride=PACK_ROWS), :]
            halves[0].append(lax.bitcast_convert_type(words & jnp.uint32(0xFFFF0000), F32).astype(BF16))
            halves[1].append(lax.bitcast_convert_type(words << 16, F32).astype(BF16))
        x = jnp.concatenate(halves[0] + halves[1], axis=1)
        if prefetch_next:
            for jj in range(MOE_BLOCK // 2):
                gather_pair(i + 1, jj, 1 - slot)
        hid = _dot(x, w1s[...])
        acts = []
        for c in range(D_FF // 128):
            h_glu = hid[:, c * 256:c * 256 + 128] + b1g_ref[:, c * 128:(c + 1) * 128]
            h_lin = hid[:, c * 256 + 128:(c + 1) * 256] + b1l_ref[:, c * 128:(c + 1) * 128]
            h_glu = jnp.minimum(h_glu, LIMIT)
            h_lin = jnp.clip(h_lin, -LIMIT, LIMIT)
            acts.append((h_glu * _sigmoid(ALPHA * h_glu) * (h_lin + 1.0)).astype(BF16))
        y = _dot(jnp.concatenate(acts, axis=1), w2s[...]) + b2_ref[...]
        for s in range(ROW_TILE):
            o_ref[pl.ds(s, MOE_BLOCK, stride=ROW_TILE), :] = y[:, s * 128:(s + 1) * 128]

    @pl.when(i + 1 < n_used)
    def _():
        block(True)

    @pl.when(i + 1 == n_used)
    def _():
        block(False)

    @pl.when(i >= n_used)
    def _():
        o_ref[...] = jnp.zeros_like(o_ref)


def _deinterleave_matrix():
    p = np.zeros((256, 256), np.float32)
    m = np.arange(128)
    p[2 * m, m] = 1.0
    p[2 * m + 1, 128 + m] = 1.0
    return jnp.asarray(p, BF16)


def _moe_experts(layer, fn_packed, slot_a, block_e, n_used, next_expert, w1, b1, w2, b2):
    b1g = b1[layer][:, 0::2].reshape(N_EXP, 1, D_FF)
    b1l = b1[layer][:, 1::2].reshape(N_EXP, 1, D_FF)
    ex = lambda i, be, nu, sa, nxt: (be[i], 0, 0)
    grid_spec = pltpu.PrefetchScalarGridSpec(
        num_scalar_prefetch=4,
        grid=(N_BLOCKS,),
        in_specs=[
            pl.BlockSpec(memory_space=pl.ANY),
            pl.BlockSpec(memory_space=pl.ANY),
            pl.BlockSpec((None, 1, D_FF), ex),
            pl.BlockSpec((None, 1, D_FF), ex),
            pl.BlockSpec(memory_space=pl.ANY),
            pl.BlockSpec((None, 1, D), ex),
            pl.BlockSpec((256, 256), lambda i, be, nu, sa, nxt: (0, 0)),
        ],
        out_specs=pl.BlockSpec((MOE_BLOCK * ROW_TILE, 128), lambda i, be, nu, sa, nxt: (i, 0)),
        scratch_shapes=[
            pltpu.VMEM((N * PACK_ROWS, 128), jnp.uint32),
            pltpu.VMEM((2, MOE_BLOCK * PACK_ROWS, 128), jnp.uint32),
            pltpu.VMEM((D, 2 * D_FF), F32),
            pltpu.VMEM((D_FF, D), F32),
            pltpu.VMEM((D, 2 * D_FF), BF16),
            pltpu.VMEM((D_FF, D), BF16),
            pltpu.SemaphoreType.DMA((3,)),
        ],
    )
    return pl.pallas_call(
        functools.partial(_moe_kernel, layer=layer),
        out_shape=jax.ShapeDtypeStruct((N_SLOTS * ROW_TILE, 128), F32),
        grid_spec=grid_spec,
        compiler_params=_cp(("arbitrary",), MOE_VMEM_LIMIT),
        name="moe_experts",
    )(block_e, n_used, slot_a, next_expert, fn_packed, w1, b1g, b1l, w2, b2[layer].reshape(N_EXP, 1, D),
      _deinterleave_matrix())


def _combine_kernel(pos_ref, yb_hbm, g_ref, h_ref, m_ref, fg_ref, *out_and_scratch, final):
    if final:
        yp_ref, yl_ref, ybuf, sem = out_and_scratch
    else:
        o_ref, ybuf, sem = out_and_scratch
    i = pl.program_id(0)
    slot = i % 2

    def tile_copy(tile, j, k, dst_slot):
        row = pos_ref[k * N + tile * TILE + j]
        src = yb_hbm.at[pl.ds(pl.multiple_of(row * ROW_TILE, ROW_TILE), ROW_TILE), :]
        dst = ybuf.at[dst_slot, k, pl.ds(pl.multiple_of(j * ROW_TILE, ROW_TILE), ROW_TILE), :]
        return pltpu.make_async_copy(src, dst, sem.at[dst_slot])

    @pl.when(i == 0)
    def _():
        def body(j, carry):
            for k in range(TOP_K):
                tile_copy(0, j, k, 0).start()
            return carry
        lax.fori_loop(0, TILE, body, 0, unroll=4)

    def tile_step(prefetch_next):
        for k in range(TOP_K):
            pltpu.make_async_copy(yb_hbm.at[pl.ds(0, TILE * ROW_TILE), :], ybuf.at[slot, k], sem.at[slot]).wait()
        if prefetch_next:
            for j in range(TILE):
                for k in range(TOP_K):
                    tile_copy(i + 1, j, k, 1 - slot).start()
        g = g_ref[...]
        cols = []
        for s in range(ROW_TILE):
            y = ybuf[slot, 0, pl.ds(s, TILE, stride=ROW_TILE), :] * g[:, 0:1]
            for k in range(1, TOP_K):
                y = y + ybuf[slot, k, pl.ds(s, TILE, stride=ROW_TILE), :] * g[:, k:k + 1]
            cols.append(y)
        h_new = h_ref[...] + m_ref[5:6, :] * jnp.concatenate(cols, axis=1)
        if final:
            ms = jnp.mean(h_new * h_new, axis=-1, keepdims=True)
            y = h_new * lax.rsqrt(ms + EPS) * fg_ref[...]

            @pl.when(i < P_TILES)
            def _():
                yp_ref[...] = y

            @pl.when(i >= P_TILES)
            def _():
                yl_ref[...] = y
        else:
            o_ref[...] = h_new

    @pl.when(i + 1 < N_TILES)
    def _():
        tile_step(True)

    @pl.when(i + 1 == N_TILES)
    def _():
        tile_step(False)


def _moe_combine(yb_tiles, pos, gates_nt, h, mod, final_g, final):
    if final:
        out_shape = (jax.ShapeDtypeStruct((N_P, D), F32), jax.ShapeDtypeStruct((N_S, D), F32))
        out_specs = [
            pl.BlockSpec((TILE, D), lambda i, pos: (jnp.minimum(i, P_TILES - 1), 0)),
            pl.BlockSpec((TILE, D), lambda i, pos: (jnp.maximum(i - P_TILES, 0), 0)),
        ]
    else:
        out_shape = jax.ShapeDtypeStruct((N, D), F32)
        out_specs = pl.BlockSpec((TILE, D), lambda i, pos: (i, 0))
    grid_spec = pltpu.PrefetchScalarGridSpec(
        num_scalar_prefetch=1,
        grid=(N_TILES,),
        in_specs=[
            pl.BlockSpec(memory_space=pl.ANY),
            pl.BlockSpec((TILE, TOP_K), lambda i, pos: (i, 0)),
            pl.BlockSpec((TILE, D), lambda i, pos: (i, 0)),
            pl.BlockSpec((None, 6, D), lambda i, pos: (_tile_mod_row(i), 0, 0)),
            pl.BlockSpec((1, D), lambda i, pos: (0, 0)),
        ],
        out_specs=out_specs,
        scratch_shapes=[pltpu.VMEM((2, TOP_K, TILE * ROW_TILE, 128), F32), pltpu.SemaphoreType.DMA((2,))],
    )
    return pl.pallas_call(
        functools.partial(_combine_kernel, final=final),
        out_shape=out_shape,
        grid_spec=grid_spec,
        compiler_params=_cp(("arbitrary",), VMEM_LIMIT),
        name="moe_combine",
    )(pos, yb_tiles, gates_nt, h, mod, final_g.reshape(1, D))


def _moe(layer, fn_packed, top_e, gates, h, mod, w1, b1, w2, b2, final_g, final):
    pos, slot_a, block_e, n_used, next_expert = _routing_tables(top_e)
    yb_tiles = _moe_experts(layer, fn_packed, slot_a, block_e, n_used, next_expert, w1, b1, w2, b2)
    return _moe_combine(yb_tiles, pos, gates.T, h, mod, final_g, final)


def _gate_weights(w_r, w_i):
    per = LRU_BLK // LRU_HD
    eye = jnp.eye(per, dtype=F32)

    def blockdiag(w):
        w = w.reshape(2, LRU_HEADS // per, per, LRU_HD, LRU_HD)
        full = jnp.einsum("dgaij,ab->dgaibj", w, eye)
        return full.reshape(2, LRU_HEADS // per, LRU_BLK, LRU_BLK)

    return jnp.concatenate([blockdiag(w_r), blockdiag(w_i)], axis=-1).astype(BF16)


def kernel(x_prompt, x_sample, state_rglru, c, c_ctx, norm_mix_g, norm_ffn_g, w_mod, b_mod, w_in0, lru_conv_w, lru_conv_b, lru_w_r, lru_b_r, lru_w_i, lru_b_i, lru_lambda, w_out0, w_in1, sgu_ln_g, sgu_ln_b, sgu_w_s, sgu_b_s, conv_dw_w, conv_dw_b, conv_ln_g, conv_ln_b, w_out1, w_router, b_router, w1, b1, w2, b2, final_norm_g):
    h = (x_prompt.reshape(N_P, D), x_sample.reshape(N_S, D))
    cond8 = jnp.concatenate([c_ctx[None, :], c, jnp.zeros((N_MOD - 1 - N_SAMPLE_SEQ, D), F32)], axis=0)
    mod = _adaln(cond8, w_mod, b_mod)

    proj0 = _inproj(h, norm_mix_g[0], mod[0], w_in0[0].astype(BF16))
    st = state_rglru[:, 0].astype(F32)
    h0 = jnp.zeros((2, N_MOD, LRU_W), F32).at[:, 1:1 + N_SAMPLE_SEQ].set(jnp.swapaxes(st, 0, 1))
    hs, ctx_state = _lru(proj0, lru_conv_w[0], lru_conv_b[0], _gate_weights(lru_w_r[0], lru_w_i[0]),
              lru_b_r[0], lru_b_i[0], lru_lambda[0], h0)
    h, fn, top_e, gates = _post0(hs, proj0, _fourier_prompt(proj0), _fourier_sample(proj0), w_out0[0].astype(BF16), h, mod[0], norm_ffn_g[0],
                                 w_router[0].T, b_router[0])
    h = _moe(0, fn, top_e, gates, h, mod[0], w1, b1, w2, b2, final_norm_g, False)

    proj1 = _inproj(h, norm_mix_g[1], mod[1], w_in1[0].astype(BF16))
    bs_full = jnp.repeat(sgu_b_s[0].T, CHUNK, axis=1)
    h, fn, top_e, gates = _post1(proj1, sgu_ln_g[0], sgu_ln_b[0], sgu_w_s[0].astype(BF16), bs_full,
                                 conv_dw_w[0], conv_dw_b[0], conv_ln_g[0], conv_ln_b[0],
                                 w_out1[0].astype(BF16), h, mod[1], norm_ffn_g[1], w_router[1].T, b_router[1])
    y_p, y_l = _moe(1, fn, top_e, gates, h, mod[1], w1, b1, w2, b2, final_norm_g, True)

    y_prompt = y_p.reshape(N_PROMPT_SEQ, T_PROMPT, D)
    y_sample = y_l.reshape(N_SAMPLE_SEQ, T_SAMPLE, D)
    new_state = jnp.transpose(ctx_state, (1, 2, 0, 3))
    return (y_prompt, y_sample, new_state.astype(x_prompt.dtype))
```

```python
import functools

import numpy as np
import jax
import jax.numpy as jnp
from jax import lax
from jax.experimental import pallas as pl
from jax.experimental.pallas import tpu as pltpu

F32 = jnp.float32
BF16 = jnp.bfloat16

D = 1024
N_PROMPT_SEQ = 32
T_PROMPT = 256
N_SAMPLE_SEQ = 2
T_SAMPLE = 2048
N_P = N_PROMPT_SEQ * T_PROMPT
N_S = N_SAMPLE_SEQ * T_SAMPLE
N = N_P + N_S
EPS = 1e-6

TILE = 256
N_TILES = N // TILE
P_TILES = N_P // TILE
S_TILES = T_SAMPLE // TILE
TM_PROJ = 512
N_MOD = 8

LRU_W = 768
LRU_HEADS = 12
LRU_HD = 64
LRU_K = 4
LRU_LEFT = 2
LRU_C = 8.0
LRU_BLK = 256
FN_W = 256
FN_G = 4
FN_GD = 64
IN0 = 2 * LRU_W + FN_W

SGU_W = 512
SGU_G = 4
CHUNK = 128
CONV_W = 512
CONV_K = 31
CONV_PAD = 15
CONV_HALO = 16
IN1 = 2 * SGU_W + 2 * CONV_W

N_EXP = 32
TOP_K = 4
D_FF = 1024
ALPHA = 1.702
LIMIT = 7.0
MOE_BLOCK = 256
N_ASSIGN = N * TOP_K
N_BLOCKS = N_ASSIGN // MOE_BLOCK + N_EXP
N_SLOTS = N_BLOCKS * MOE_BLOCK
ROW_TILE = D // 128
PACK_ROWS = D // 2 // 128
PACK_PAIR = 2 * PACK_ROWS

VMEM_LIMIT = 56 * 1024 * 1024
MOE_VMEM_LIMIT = 60 * 1024 * 1024


def _cp(sem, vmem=None):
    return pltpu.CompilerParams(dimension_semantics=sem, vmem_limit_bytes=vmem)


def _dot(a, b):
    return jnp.dot(a, b, preferred_element_type=F32)


def _split(x):
    hi = x.astype(BF16)
    lo = (x - hi.astype(F32)).astype(BF16)
    return hi, lo


def _dot3(a, b):
    ah, al = _split(a)
    bh, bl = _split(b)
    return _dot(ah, bh) + _dot(al, bh) + _dot(ah, bl)


def _dot3_nt(a, b):
    dn = (((1,), (1,)), ((), ()))
    d = lambda x, y: lax.dot_general(x, y, dn, preferred_element_type=F32)
    ah, al = _split(a)
    bh, bl = _split(b)
    return d(ah, bh) + d(al, bh) + d(ah, bl)


def _sigmoid(x):
    return 1.0 / (1.0 + jnp.exp(-x))


def _gelu(x):
    return 0.5 * x * (1.0 + jnp.tanh(0.7978845608028654 * (x + 0.044715 * (x * x * x))))


def _rms_mod(x, g, shift, scale):
    ms = jnp.mean(x * x, axis=-1, keepdims=True)
    y = x * lax.rsqrt(ms + EPS) * g
    return y * (1.0 + scale) + shift


def _layernorm(x, g, b):
    xc = x - jnp.mean(x, axis=-1, keepdims=True)
    var = jnp.mean(xc * xc, axis=-1, keepdims=True)
    return xc * lax.rsqrt(var + EPS) * g + b


def _tile_mod_row(r):
    return jnp.where(r < P_TILES, 0, 1 + (r - P_TILES) // S_TILES)


def _tile_is_seq_start(r):
    return (r < P_TILES) | ((r - P_TILES) % S_TILES == 0)


def _tile_is_seq_end(r):
    return (r < P_TILES) | ((r - P_TILES) % S_TILES == S_TILES - 1)


MOD_TN = 512


def _adaln_kernel(cond_ref, w_ref, b_ref, o_ref):
    cond = cond_ref[...]
    s = cond * _sigmoid(cond)
    o_ref[...] = _dot3(s, w_ref[...]) + b_ref[...]


def _adaln(cond8, w_mod, b_mod):
    depth = w_mod.shape[0]
    out = pl.pallas_call(
        _adaln_kernel,
        out_shape=jax.ShapeDtypeStruct((depth, N_MOD, 6 * D), F32),
        grid=(depth, 6 * D // MOD_TN),
        in_specs=[
            pl.BlockSpec((N_MOD, D), lambda l, j: (0, 0)),
            pl.BlockSpec((None, D, MOD_TN), lambda l, j: (l, 0, j)),
            pl.BlockSpec((None, 1, MOD_TN), lambda l, j: (l, 0, j)),
        ],
        out_specs=pl.BlockSpec((None, N_MOD, MOD_TN), lambda l, j: (l, 0, j)),
        compiler_params=_cp(("arbitrary", "arbitrary")),
        name="adaln",
    )(cond8, w_mod, b_mod.reshape(depth, 1, 6 * D))
    return out.reshape(depth, N_MOD, 6, D)


PROJ_P_STEPS = N_P // TM_PROJ
PROJ_S_STEPS = T_SAMPLE // TM_PROJ


def _stream_specs(rows, p_steps, h):
    if isinstance(h, tuple):
        hp, hl = h
        first_latent = 0
    else:
        hp = hl = h
        first_latent = p_steps
    specs = [
        pl.BlockSpec((rows, D), lambda i, *_: (jnp.minimum(i, p_steps - 1), 0)),
        pl.BlockSpec((rows, D), lambda i, *_: (jnp.maximum(i - p_steps, 0) + first_latent, 0)),
    ]
    return specs, (hp, hl)


def _inproj_kernel(xp_ref, xl_ref, g_ref, m_ref, w_ref, o_ref):
    x = jnp.where(pl.program_id(0) < PROJ_P_STEPS, xp_ref[...], xl_ref[...])
    hn = _rms_mod(x, g_ref[...], m_ref[0:1, :], m_ref[1:2, :])
    o_ref[...] = _dot(hn.astype(BF16), w_ref[...])


def _inproj(h, g, mod, w_bf16):
    n_out = w_bf16.shape[1]

    def mod_row(i):
        return jnp.where(i < PROJ_P_STEPS, 0, 1 + (i - PROJ_P_STEPS) // PROJ_S_STEPS)

    h_specs, h_args = _stream_specs(TM_PROJ, PROJ_P_STEPS, h)
    return pl.pallas_call(
        _inproj_kernel,
        out_shape=jax.ShapeDtypeStruct((N, n_out), F32),
        grid=(N // TM_PROJ,),
        in_specs=h_specs + [
            pl.BlockSpec((1, D), lambda i: (0, 0)),
            pl.BlockSpec((None, 6, D), lambda i: (mod_row(i), 0, 0)),
            pl.BlockSpec((D, n_out), lambda i: (0, 0)),
        ],
        out_specs=pl.BlockSpec((TM_PROJ, n_out), lambda i: (i, 0)),
        compiler_params=_cp(("arbitrary",), VMEM_LIMIT),
        name="inproj",
    )(*h_args, g.reshape(1, D), mod, w_bf16)


LRU_HALO = 8


def _lru_tile(d, s):
    return jnp.where(d == 0, s, N_TILES - 1 - s)


def _lru_kernel(x_ref, prev_ref, next_ref, cw_ref, cb_ref, wg_ref, br_ref, bi_ref, lam_ref, h0_ref,
                o_ref, state_ref, ext_ref, a_ref, carry_ref):
    d = pl.program_id(0)
    r = _lru_tile(d, pl.program_id(1))
    start = _tile_is_seq_start(r)
    end = _tile_is_seq_end(r)

    ext_ref[0:LRU_HALO, :] = jnp.where(start, 0.0, prev_ref[...])
    ext_ref[LRU_HALO:LRU_HALO + TILE, :] = x_ref[...]
    ext_ref[LRU_HALO + TILE:, :] = jnp.where(end, 0.0, next_ref[...])
    xc = cb_ref[...] + jnp.zeros((TILE, LRU_W), F32)
    for k in range(LRU_K):
        off = LRU_HALO - LRU_LEFT + k
        xc = xc + ext_ref[off:off + TILE, :] * cw_ref[k:k + 1, :]

    xcb = xc.astype(BF16)
    pre_r, pre_i = [], []
    for blk in range(LRU_W // LRU_BLK):
        g = _dot(xcb[:, blk * LRU_BLK:(blk + 1) * LRU_BLK], wg_ref[blk])
        pre_r.append(g[:, :LRU_BLK])
        pre_i.append(g[:, LRU_BLK:])
    gate_r = _sigmoid(jnp.concatenate(pre_r, axis=1) + br_ref[...])
    gate_i = _sigmoid(jnp.concatenate(pre_i, axis=1) + bi_ref[...])
    neg_lam = -lam_ref[...]
    softplus = jnp.maximum(neg_lam, 0.0) + jnp.log1p(jnp.exp(-jnp.abs(neg_lam)))
    log_a = (-LRU_C) * gate_r * softplus
    a = jnp.exp(log_a)
    a_ref[...] = a
    o_ref[...] = jnp.sqrt(-jnp.tanh(log_a) * (a * a + 1.0)) * (gate_i * xc)

    fresh = jnp.where(d == 0, start, end)
    h_init = jnp.where(fresh, h0_ref[pl.ds(_tile_mod_row(r), 1), :], carry_ref[...])

    def step(t, h):
        tt = jnp.where(d == 0, t, TILE - 1 - t)
        h = a_ref[pl.ds(tt, 1), :] * h + o_ref[pl.ds(tt, 1), :]
        o_ref[pl.ds(tt, 1), :] = h
        return h

    h_last = lax.fori_loop(0, TILE, step, h_init, unroll=8)
    carry_ref[...] = h_last

    @pl.when(r < P_TILES)
    def _():
        state_ref[...] = h_last


def _lru(proj0, conv_w, conv_b, wg, b_r, b_i, lam, h0):
    n_halo_blocks = N // LRU_HALO
    per_tile = TILE // LRU_HALO
    tile = lambda d, s: _lru_tile(d, s)
    return pl.pallas_call(
        _lru_kernel,
        out_shape=(jax.ShapeDtypeStruct((2, N, LRU_W), F32),
                   jax.ShapeDtypeStruct((2, N_PROMPT_SEQ, 1, LRU_W), F32)),
        grid=(2, N_TILES),
        in_specs=[
            pl.BlockSpec((TILE, LRU_W), lambda d, s: (tile(d, s), 0)),
            pl.BlockSpec((LRU_HALO, LRU_W), lambda d, s: (jnp.maximum(tile(d, s) * per_tile - 1, 0), 0)),
            pl.BlockSpec((LRU_HALO, LRU_W),
                         lambda d, s: (jnp.minimum((tile(d, s) + 1) * per_tile, n_halo_blocks - 1), 0)),
            pl.BlockSpec((LRU_K, LRU_W), lambda d, s: (0, 0)),
            pl.BlockSpec((1, LRU_W), lambda d, s: (0, 0)),
            pl.BlockSpec((None, LRU_W // LRU_BLK, LRU_BLK, 2 * LRU_BLK), lambda d, s: (d, 0, 0, 0)),
            pl.BlockSpec((None, 1, LRU_W), lambda d, s: (d, 0, 0)),
            pl.BlockSpec((None, 1, LRU_W), lambda d, s: (d, 0, 0)),
            pl.BlockSpec((None, 1, LRU_W), lambda d, s: (d, 0, 0)),
            pl.BlockSpec((None, N_MOD, LRU_W), lambda d, s: (d, 0, 0)),
        ],
        out_specs=[
            pl.BlockSpec((None, TILE, LRU_W), lambda d, s: (d, tile(d, s), 0)),
            pl.BlockSpec((None, None, 1, LRU_W), lambda d, s: (d, jnp.minimum(tile(d, s), P_TILES - 1), 0, 0)),
        ],
        scratch_shapes=[
            pltpu.VMEM((TILE + 2 * LRU_HALO, LRU_W), F32),
            pltpu.VMEM((TILE, LRU_W), F32),
            pltpu.VMEM((1, LRU_W), F32),
        ],
        compiler_params=_cp(("arbitrary", "arbitrary"), VMEM_LIMIT),
        name="rglru_scan",
    )(proj0, proj0, proj0, conv_w, conv_b.reshape(1, LRU_W), wg,
      b_r.reshape(2, 1, LRU_W), b_i.reshape(2, 1, LRU_W), lam.reshape(2, 1, LRU_W), h0)


def _dft_tables(n, scale):
    k = np.arange(n, dtype=np.int64)
    ang = 2.0 * np.pi * ((k[:, None] * k[None, :]) % n).astype(np.float64) / n
    return np.cos(ang) * scale, np.sin(ang) * scale


def _channel_tables():
    c, s = _dft_tables(FN_GD, FN_GD ** -0.5)
    eye = np.eye(FN_G)
    return (jnp.asarray(np.kron(eye, c), BF16), jnp.asarray(np.kron(eye, s), BF16))


def _time_tables(t_len):
    c, s = _dft_tables(t_len, t_len ** -0.5)
    return jnp.asarray(c, BF16), jnp.asarray(s, BF16)


def _fourier_prompt_kernel(z_ref, cc_ref, sc_ref, ct_ref, st_ref, o_ref):
    z = z_ref[...].astype(BF16)
    zc = _dot(z, cc_ref[...]).astype(BF16)
    zs = _dot(z, sc_ref[...]).astype(BF16)
    o_ref[...] = _dot(ct_ref[...], zc) - _dot(st_ref[...], zs)


def _fourier_prompt(proj0):
    cc, sc = _channel_tables()
    ct, st = _time_tables(T_PROMPT)
    const = lambda b: (0, 0)
    return pl.pallas_call(
        _fourier_prompt_kernel,
        out_shape=jax.ShapeDtypeStruct((N_P, FN_W), F32),
        grid=(N_PROMPT_SEQ,),
        in_specs=[
            pl.BlockSpec((T_PROMPT, FN_W), lambda b: (b, 2 * LRU_W // FN_W)),
            pl.BlockSpec((FN_W, FN_W), const),
            pl.BlockSpec((FN_W, FN_W), const),
            pl.BlockSpec((T_PROMPT, T_PROMPT), const),
            pl.BlockSpec((T_PROMPT, T_PROMPT), const),
        ],
        out_specs=pl.BlockSpec((T_PROMPT, FN_W), lambda b: (b, 0)),
        compiler_params=_cp(("arbitrary",)),
        name="fourier_prompt",
    )(proj0, cc, sc, ct, st)


def _fourier_sample_kernel(z_ref, cc_ref, sc_ref, ct_ref, st_ref, o_ref, zc_ref, zs_ref):
    @pl.when(pl.program_id(1) == 0)
    def _():
        z = z_ref[...].astype(BF16)
        zc_ref[...] = _dot(z, cc_ref[...]).astype(BF16)
        zs_ref[...] = _dot(z, sc_ref[...]).astype(BF16)

    o_ref[...] = _dot(ct_ref[...], zc_ref[...]) - _dot(st_ref[...], zs_ref[...])


def _fourier_sample(proj0):
    cc, sc = _channel_tables()
    ct, st = _time_tables(T_SAMPLE)
    const = lambda b, i: (0, 0)
    first_seq_block = N_P // T_SAMPLE
    return pl.pallas_call(
        _fourier_sample_kernel,
        out_shape=jax.ShapeDtypeStruct((N_S, FN_W), F32),
        grid=(N_SAMPLE_SEQ, S_TILES),
        in_specs=[
            pl.BlockSpec((T_SAMPLE, FN_W), lambda b, i: (first_seq_block + b, 2 * LRU_W // FN_W)),
            pl.BlockSpec((FN_W, FN_W), const),
            pl.BlockSpec((FN_W, FN_W), const),
            pl.BlockSpec((TILE, T_SAMPLE), lambda b, i: (i, 0)),
            pl.BlockSpec((TILE, T_SAMPLE), lambda b, i: (i, 0)),
        ],
        out_specs=pl.BlockSpec((TILE, FN_W), lambda b, i: (b * S_TILES + i, 0)),
        scratch_shapes=[pltpu.VMEM((T_SAMPLE, FN_W), BF16), pltpu.VMEM((T_SAMPLE, FN_W), BF16)],
        compiler_params=_cp(("arbitrary", "arbitrary"), VMEM_LIMIT),
        name="fourier_sample",
    )(proj0, cc, sc, ct, st)


def _epilogue(mix, hp_ref, hl_ref, m_ref, gf_ref, wr_ref, brt_ref, hout_ref, fn_ref, tope_ref, gate_ref):
    h = jnp.where(pl.program_id(0) < P_TILES, hp_ref[...], hl_ref[...])
    h_new = h + m_ref[2:3, :] * mix
    hout_ref[...] = h_new
    fn = _rms_mod(h_new, gf_ref[...], m_ref[3:4, :], m_ref[4:5, :])
    hi = lax.bitcast_convert_type(fn[:, :D // 2].astype(BF16).astype(F32), jnp.uint32)
    lo = lax.bitcast_convert_type(fn[:, D // 2:].astype(BF16).astype(F32), jnp.uint32)
    packed = hi | (lo >> 16)
    for q in range(PACK_ROWS):
        fn_ref[pl.ds(q, TILE, stride=PACK_ROWS), :] = packed[:, q * 128:(q + 1) * 128]
    logits = _dot3_nt(wr_ref[...], fn) + brt_ref[...]
    iota = lax.broadcasted_iota(jnp.int32, logits.shape, 0)
    vals, idxs = [], []
    for _ in range(TOP_K):
        m = jnp.max(logits, axis=0, keepdims=True)
        idx = jnp.min(jnp.where(logits == m, iota, N_EXP), axis=0, keepdims=True)
        vals.append(m)
        idxs.append(idx)
        logits = jnp.where(iota == idx, -jnp.inf, logits)
    exps = [jnp.exp(v - vals[0]) for v in vals]
    denom = exps[0] + exps[1] + exps[2] + exps[3]
    for k in range(TOP_K):
        tope_ref[k:k + 1, :] = idxs[k]
        gate_ref[k:k + 1, :] = exps[k] / denom


_EPI_OUT_SHAPES = (
    jax.ShapeDtypeStruct((N, D), F32),
    jax.ShapeDtypeStruct((N * PACK_ROWS, 128), jnp.uint32),
    jax.ShapeDtypeStruct((TOP_K, N), jnp.int32),
    jax.ShapeDtypeStruct((TOP_K, N), F32),
)


def _epi_in_specs(h):
    h_specs, _ = _stream_specs(TILE, P_TILES, h)
    return h_specs + [
        pl.BlockSpec((None, 6, D), lambda r: (_tile_mod_row(r), 0, 0)),
        pl.BlockSpec((1, D), lambda r: (0, 0)),
        pl.BlockSpec((N_EXP, D), lambda r: (0, 0)),
        pl.BlockSpec((N_EXP, 1), lambda r: (0, 0)),
    ]


def _epi_out_specs():
    return [
        pl.BlockSpec((TILE, D), lambda r: (r, 0)),
        pl.BlockSpec((TILE * PACK_ROWS, 128), lambda r: (r, 0)),
        pl.BlockSpec((TOP_K, TILE), lambda r: (0, r)),
        pl.BlockSpec((TOP_K, TILE), lambda r: (0, r)),
    ]


def _post0_kernel(hs_ref, xg_ref, yfp_ref, yfl_ref, wo_ref, hp_ref, hl_ref, m_ref, gf_ref, wr_ref, brt_ref,
                  hout_ref, fn_ref, tope_ref, gate_ref):
    y_rec = (hs_ref[0] + hs_ref[1]) * _gelu(xg_ref[...])
    y_four = jnp.where(pl.program_id(0) < P_TILES, yfp_ref[...], yfl_ref[...])
    mix = (_dot(y_rec.astype(BF16), wo_ref[0:LRU_W, :])
           + _dot(y_four.astype(BF16), wo_ref[LRU_W:, :]))
    _epilogue(mix, hp_ref, hl_ref, m_ref, gf_ref, wr_ref, brt_ref, hout_ref, fn_ref, tope_ref, gate_ref)


def _post0(hs, proj0, yf_prompt, yf_latent, w_out_bf16, h, mod, g_ffn, w_router_t, b_router):
    return pl.pallas_call(
        _post0_kernel,
        out_shape=_EPI_OUT_SHAPES,
        grid=(N_TILES,),
        in_specs=[
            pl.BlockSpec((2, TILE, LRU_W), lambda r: (0, r, 0)),
            pl.BlockSpec((TILE, LRU_W), lambda r: (r, 1)),
            pl.BlockSpec((TILE, FN_W), lambda r: (jnp.minimum(r, P_TILES - 1), 0)),
            pl.BlockSpec((TILE, FN_W), lambda r: (jnp.maximum(r - P_TILES, 0), 0)),
            pl.BlockSpec((D, D), lambda r: (0, 0)),
        ] + _epi_in_specs(h),
        out_specs=_epi_out_specs(),
        compiler_params=_cp(("arbitrary",), VMEM_LIMIT),
        name="post_rglru_fourier",
    )(hs, proj0, yf_prompt, yf_latent, w_out_bf16, *_stream_specs(TILE, P_TILES, h)[1], mod, g_ffn.reshape(1, D), w_router_t,
      b_router.reshape(N_EXP, 1))


def _glu(x):
    return x[:, :CONV_W] * _sigmoid(x[:, CONV_W:])


def _post1_kernel(p_ref, prev_ref, next_ref, lng_ref, lnb_ref, ws_ref, bs_ref, dww_ref, dwb_ref,
                  clg_ref, clb_ref, wo_ref, hp_ref, hl_ref, m_ref, gf_ref, wr_ref, brt_ref,
                  hout_ref, fn_ref, tope_ref, gate_ref, ext_ref):
    r = pl.program_id(0)
    z = _gelu(p_ref[:, 0:2 * SGU_W])
    u = z[:, :SGU_W]
    v = _layernorm(z[:, SGU_W:], lng_ref[...], lnb_ref[...]).astype(BF16)
    rows = []
    for n in range(TILE // CHUNK):
        cols = []
        for g in range(SGU_G):
            vb = v[n * CHUNK:(n + 1) * CHUNK, g * CHUNK:(g + 1) * CHUNK]
            cols.append(_dot(ws_ref[g], vb))
        rows.append(jnp.concatenate(cols, axis=1) + bs_ref[...])
    y_sgu = u * jnp.concatenate(rows, axis=0)

    ext_ref[0:CONV_HALO, :] = jnp.where(_tile_is_seq_start(r), 0.0, _glu(prev_ref[...]))
    ext_ref[CONV_HALO:CONV_HALO + TILE, :] = _glu(p_ref[:, 2 * SGU_W:])
    ext_ref[CONV_HALO + TILE:, :] = jnp.where(_tile_is_seq_end(r), 0.0, _glu(next_ref[...]))
    dconv = dwb_ref[...] + jnp.zeros((TILE, CONV_W), F32)
    for k in range(CONV_K):
        off = CONV_HALO - CONV_PAD + k
        dconv = dconv + ext_ref[off:off + TILE, :] * dww_ref[k:k + 1, :]
    ln = _layernorm(dconv, clg_ref[...], clb_ref[...])
    y_conv = ln * _sigmoid(ln)

    mix = (_dot(y_sgu.astype(BF16), wo_ref[0:SGU_W, :])
           + _dot(y_conv.astype(BF16), wo_ref[SGU_W:, :]))
    _epilogue(mix, hp_ref, hl_ref, m_ref, gf_ref, wr_ref, brt_ref, hout_ref, fn_ref, tope_ref, gate_ref)


def _post1(proj1, ln_g, ln_b, ws_bf16, bs_full, dw_w, dw_b, cln_g, cln_b, w_out_bf16,
           h, mod, g_ffn, w_router_t, b_router):
    per_tile = TILE // CONV_HALO
    n_halo_blocks = N // CONV_HALO
    const2 = lambda r: (0, 0)
    row = lambda a: a.reshape(1, -1)
    return pl.pallas_call(
        _post1_kernel,
        out_shape=_EPI_OUT_SHAPES,
        grid=(N_TILES,),
        in_specs=[
            pl.BlockSpec((TILE, IN1), lambda r: (r, 0)),
            pl.BlockSpec((CONV_HALO, 2 * CONV_W), lambda r: (jnp.maximum(r * per_tile - 1, 0), 1)),
            pl.BlockSpec((CONV_HALO, 2 * CONV_W),
                         lambda r: (jnp.minimum((r + 1) * per_tile, n_halo_blocks - 1), 1)),
            pl.BlockSpec((1, SGU_W), const2),
            pl.BlockSpec((1, SGU_W), const2),
            pl.BlockSpec((SGU_G, CHUNK, CHUNK), lambda r: (0, 0, 0)),
            pl.BlockSpec((CHUNK, SGU_W), const2),
            pl.BlockSpec((CONV_K, CONV_W), const2),
            pl.BlockSpec((1, CONV_W), const2),
            pl.BlockSpec((1, CONV_W), const2),
            pl.BlockSpec((1, CONV_W), const2),
            pl.BlockSpec((D, D), const2),
        ] + _epi_in_specs(h),
        out_specs=_epi_out_specs(),
        scratch_shapes=[pltpu.VMEM((TILE + 2 * CONV_HALO, CONV_W), F32)],
        compiler_params=_cp(("arbitrary",), VMEM_LIMIT),
        name="post_sgu_conformer",
    )(proj1, proj1, proj1, row(ln_g), row(ln_b), ws_bf16, bs_full, dw_w, row(dw_b), row(cln_g), row(cln_b),
      w_out_bf16, *_stream_specs(TILE, P_TILES, h)[1], mod, g_ffn.reshape(1, D), w_router_t,
      b_router.reshape(N_EXP, 1))


RANK_TILE = 512
PAD_BASE = N_ASSIGN
PLACEHOLDER_BASE = PAD_BASE + 2 * MOE_BLOCK
Y_ROWS = PLACEHOLDER_BASE + 2 * MOE_BLOCK


def _rank_kernel(e_ref, tri_ref, rank_ref, cnt_ref, run_ref):
    @pl.when(pl.program_id(0) == 0)
    def _():
        run_ref[...] = jnp.zeros_like(run_ref)

    iota = lax.broadcasted_iota(jnp.int32, (N_EXP, RANK_TILE), 0)
    run = run_ref[...]
    for k in range(TOP_K):
        onehot = jnp.where(iota == e_ref[k:k + 1, :], 1.0, 0.0)
        incl = _dot(onehot.astype(BF16), tri_ref[...])
        rank = jnp.sum(onehot * (incl - 1.0 + run), axis=0, keepdims=True)
        rank_ref[k:k + 1, :] = rank.astype(jnp.int32)
        run = run + incl[:, RANK_TILE - 1:RANK_TILE]
    run_ref[...] = run
    cnt_ref[...] = run.astype(jnp.int32)


def _rank(top_e):
    tri = jnp.asarray(np.triu(np.ones((RANK_TILE, RANK_TILE), np.float32)), BF16)
    return pl.pallas_call(
        _rank_kernel,
        out_shape=(jax.ShapeDtypeStruct((TOP_K, N), jnp.int32), jax.ShapeDtypeStruct((N_EXP, 1), jnp.int32)),
        grid=(N // RANK_TILE,),
        in_specs=[
            pl.BlockSpec((TOP_K, RANK_TILE), lambda i: (0, i)),
            pl.BlockSpec((RANK_TILE, RANK_TILE), lambda i: (0, 0)),
        ],
        out_specs=[
            pl.BlockSpec((TOP_K, RANK_TILE), lambda i: (0, i)),
            pl.BlockSpec((N_EXP, 1), lambda i: (0, 0)),
        ],
        scratch_shapes=[pltpu.VMEM((N_EXP, 1), F32)],
        compiler_params=_cp(("arbitrary",)),
        name="moe_rank",
    )(top_e, tri)


def _inverse_kernel(pos_ref, pend_ref, out_ref):
    def fill_block(start):
        def fill(j, carry):
            s = start + j
            out_ref[s] = PAD_BASE + (s & (2 * MOE_BLOCK - 1))
            return carry
        lax.fori_loop(0, MOE_BLOCK, fill, 0, unroll=16)

    def per_expert(e, carry):
        fill_block(jnp.maximum(pend_ref[e] - MOE_BLOCK, 0))
        return carry

    def idle_block(b, carry):
        fill_block(b * MOE_BLOCK)
        return carry

    lax.fori_loop(0, N_EXP, per_expert, 0)
    lax.fori_loop(pend_ref[N_EXP - 1] // MOE_BLOCK, N_BLOCKS, idle_block, 0)

    def body(t, carry):
        for k in range(TOP_K):
            out_ref[pos_ref[k * N + t]] = t * TOP_K + k
        return carry

    lax.fori_loop(0, N, body, 0, unroll=8)


def _inverse(pos_flat, pad_end):
    smem = pl.BlockSpec(memory_space=pltpu.SMEM)
    return pl.pallas_call(
        _inverse_kernel,
        out_shape=jax.ShapeDtypeStruct((N_SLOTS,), jnp.int32),
        in_specs=[smem, smem],
        out_specs=smem,
        name="moe_inverse",
    )(pos_flat, pad_end)


def _routing_tables(top_e):
    rank, counts = _rank(top_e)
    counts = counts.reshape(N_EXP)
    padded = (counts + MOE_BLOCK - 1) // MOE_BLOCK * MOE_BLOCK
    pad_end = jnp.cumsum(padded).astype(jnp.int32)
    pad_start = pad_end - padded
    experts = jnp.arange(N_EXP, dtype=jnp.int32)[:, None, None]
    pos = rank + jnp.sum(jnp.where(top_e[None] == experts, pad_start[:, None, None], 0), axis=0)
    pos = pos.astype(jnp.int32).reshape(-1)
    n_used = pad_end[-1] // MOE_BLOCK
    blk = jnp.minimum(jnp.arange(N_BLOCKS, dtype=jnp.int32), n_used - 1)
    block_e = jnp.sum(pad_end[None, :] <= (blk * MOE_BLOCK)[:, None], axis=1).astype(jnp.int32)
    slot_a = _inverse(pos, pad_end)
    ids = jnp.arange(N_EXP, dtype=jnp.int32)
    later_used = (ids[None, :] > ids[:, None]) & (counts[None, :] > 0)
    next_expert = jnp.min(jnp.where(later_used, ids[None, :], N_EXP), axis=1).astype(jnp.int32)
    return slot_a, block_e, n_used.reshape(1), next_expert


def _moe_kernel(be_ref, nu_ref, sa_ref, nxt_ref, x_hbm, w1_hbm, b1g_ref, b1l_ref, w2_hbm, b2_ref, perm_ref,
                y_hbm, xs, xbuf, w1f, w2f, w1s, w2s, sem, ybuf, osem, *, layer):
    i = pl.program_id(0)
    n_used = nu_ref[0]
    slot = i % 2
    e = be_ref[i]
    e_prev = be_ref[jnp.maximum(i - 1, 0)]
    low_half = lax.broadcasted_iota(jnp.int32, (PACK_PAIR, 128), 0) < PACK_ROWS

    def weight_copies(expert):
        return (pltpu.make_async_copy(w1_hbm.at[layer, expert], w1f, sem.at[1]),
                pltpu.make_async_copy(w2_hbm.at[layer, expert], w2f, sem.at[2]))

    def token_tile(a):
        tok = jnp.minimum(a >> 2, N - 1)
        tile = xs[pl.ds(pl.multiple_of((tok >> 1) * PACK_PAIR, PACK_PAIR), PACK_PAIR), :]
        return tile, pltpu.roll(tile, PACK_ROWS, 0), tok & 1

    def gather_pair(blk, jj, dst_slot):
        t0, r0, half0 = token_tile(sa_ref[blk * MOE_BLOCK + 2 * jj])
        t1, r1, half1 = token_tile(sa_ref[blk * MOE_BLOCK + 2 * jj + 1])
        lower = jnp.where(half0 == 0, t0, r0)
        upper = jnp.where(half1 == 1, t1, r1)
        start = jj * PACK_PAIR if isinstance(jj, int) else pl.multiple_of(jj * PACK_PAIR, PACK_PAIR)
        xbuf[dst_slot, pl.ds(start, PACK_PAIR), :] = jnp.where(low_half, lower, upper)

    def out_copy(src_slot, j, a):
        start = j * ROW_TILE if isinstance(j, int) else pl.multiple_of(j * ROW_TILE, ROW_TILE)
        row = jnp.where(a >= PAD_BASE, a, (a & (TOP_K - 1)) * N + (a >> 2))
        dst = y_hbm.at[pl.ds(pl.multiple_of(row * ROW_TILE, ROW_TILE), ROW_TILE), :]
        return pltpu.make_async_copy(ybuf.at[src_slot, pl.ds(start, ROW_TILE), :], dst, osem.at[src_slot])

    def out_buffer_wait(src_slot):
        pltpu.make_async_copy(ybuf.at[src_slot], y_hbm.at[pl.ds(0, MOE_BLOCK * ROW_TILE), :],
                              osem.at[src_slot]).wait()

    @pl.when(i == 0)
    def _():
        xs_copy = pltpu.make_async_copy(x_hbm, xs, sem.at[0])
        xs_copy.start()
        for c in weight_copies(e):
            c.start()
        ybuf[...] = jnp.zeros_like(ybuf)

        def zero_rows(first_row):
            def body(j, carry):
                out_copy(0, j, first_row + j).start()
                return carry
            lax.fori_loop(0, MOE_BLOCK, body, 0, unroll=8)

        zero_rows(PAD_BASE)
        zero_rows(PAD_BASE + MOE_BLOCK)
        out_buffer_wait(0)
        out_buffer_wait(0)
        zero_rows(PLACEHOLDER_BASE + MOE_BLOCK)
        xs_copy.wait()

        def body(jj, carry):
            gather_pair(0, jj, 0)
            return carry
        lax.fori_loop(0, MOE_BLOCK // 2, body, 0, unroll=4)

    @pl.when((i < n_used) & ((i == 0) | (e != e_prev)))
    def _():
        for c in weight_copies(e):
            c.wait()
        for c in range(2 * D_FF // 256):
            wc = w1f[:, c * 256:(c + 1) * 256].astype(BF16)
            w1s[:, c * 256:(c + 1) * 256] = _dot(wc, perm_ref[...]).astype(BF16)
        w2s[...] = w2f[...].astype(BF16)
        e_next = nxt_ref[e]

        @pl.when(e_next < N_EXP)
        def _():
            for c in weight_copies(e_next):
                c.start()

    def block(prefetch_next):
        halves = [[], []]
        for q in range(PACK_ROWS):
            words = xbuf[slot, pl.ds(q, MOE_BLOCK, stride=PACK_ROWS), :]
            halves[0].append(lax.bitcast_convert_type(words & jnp.uint32(0xFFFF0000), F32).astype(BF16))
            halves[1].append(lax.bitcast_convert_type(words << 16, F32).astype(BF16))
        x = jnp.concatenate(halves[0] + halves[1], axis=1)
        if prefetch_next:
            for jj in range(MOE_BLOCK // 2):
                gather_pair(i + 1, jj, 1 - slot)
        prev = jnp.maximum(i - 1, 0) * MOE_BLOCK
        for j in range(MOE_BLOCK):
            out_copy(1 - slot, j, jnp.where(i == 0, PLACEHOLDER_BASE + j, sa_ref[prev + j])).start()
        hid = _dot(x, w1s[...])
        acts = []
        for c in range(D_FF // 128):
            h_glu = hid[:, c * 256:c * 256 + 128] + b1g_ref[:, c * 128:(c + 1) * 128]
            h_lin = hid[:, c * 256 + 128:(c + 1) * 256] + b1l_ref[:, c * 128:(c + 1) * 128]
            h_glu = jnp.minimum(h_glu, LIMIT)
            h_lin = jnp.clip(h_lin, -LIMIT, LIMIT)
            acts.append((h_glu * _sigmoid(ALPHA * h_glu) * (h_lin + 1.0)).astype(BF16))
        y = _dot(jnp.concatenate(acts, axis=1), w2s[...]) + b2_ref[...]
        out_buffer_wait(slot)
        for s in range(ROW_TILE):
            ybuf[slot, pl.ds(s, MOE_BLOCK, stride=ROW_TILE), :] = y[:, s * 128:(s + 1) * 128]
        if not prefetch_next:
            def send(j, carry):
                out_copy(slot, j, sa_ref[i * MOE_BLOCK + j]).start()
                return carry
            lax.fori_loop(0, MOE_BLOCK, send, 0, unroll=8)
            out_buffer_wait(1 - slot)
            out_buffer_wait(slot)

    @pl.when(i + 1 < n_used)
    def _():
        block(True)

    @pl.when(i + 1 == n_used)
    def _():
        block(False)


def _deinterleave_matrix():
    p = np.zeros((256, 256), np.float32)
    m = np.arange(128)
    p[2 * m, m] = 1.0
    p[2 * m + 1, 128 + m] = 1.0
    return jnp.asarray(p, BF16)


def _moe_experts(layer, fn_packed, slot_a, block_e, n_used, next_expert, w1, b1, w2, b2):
    b1g = b1[layer][:, 0::2].reshape(N_EXP, 1, D_FF)
    b1l = b1[layer][:, 1::2].reshape(N_EXP, 1, D_FF)
    ex = lambda i, be, nu, sa, nxt: (be[i], 0, 0)
    grid_spec = pltpu.PrefetchScalarGridSpec(
        num_scalar_prefetch=4,
        grid=(N_BLOCKS,),
        in_specs=[
            pl.BlockSpec(memory_space=pl.ANY),
            pl.BlockSpec(memory_space=pl.ANY),
            pl.BlockSpec((None, 1, D_FF), ex),
            pl.BlockSpec((None, 1, D_FF), ex),
            pl.BlockSpec(memory_space=pl.ANY),
            pl.BlockSpec((None, 1, D), ex),
            pl.BlockSpec((256, 256), lambda i, be, nu, sa, nxt: (0, 0)),
        ],
        out_specs=pl.BlockSpec(memory_space=pl.ANY),
        scratch_shapes=[
            pltpu.VMEM((N * PACK_ROWS, 128), jnp.uint32),
            pltpu.VMEM((2, MOE_BLOCK * PACK_ROWS, 128), jnp.uint32),
            pltpu.VMEM((D, 2 * D_FF), F32),
            pltpu.VMEM((D_FF, D), F32),
            pltpu.VMEM((D, 2 * D_FF), BF16),
            pltpu.VMEM((D_FF, D), BF16),
            pltpu.SemaphoreType.DMA((3,)),
            pltpu.VMEM((2, MOE_BLOCK * ROW_TILE, 128), F32),
            pltpu.SemaphoreType.DMA((2,)),
        ],
    )
    return pl.pallas_call(
        functools.partial(_moe_kernel, layer=layer),
        out_shape=jax.ShapeDtypeStruct((Y_ROWS * ROW_TILE, 128), F32),
        grid_spec=grid_spec,
        compiler_params=_cp(("arbitrary",), MOE_VMEM_LIMIT),
        name="moe_experts",
    )(block_e, n_used, slot_a, next_expert, fn_packed, w1, b1g, b1l, w2, b2[layer].reshape(N_EXP, 1, D),
      _deinterleave_matrix())


def _combine_kernel(y0_ref, y1_ref, y2_ref, y3_ref, g_ref, h_ref, m_ref, fg_ref, *outs, final):
    i = pl.program_id(0)
    g = g_ref[...]
    cols = []
    for s in range(ROW_TILE):
        y = y0_ref[pl.ds(s, TILE, stride=ROW_TILE), :] * g[:, 0:1]
        for k, yk_ref in ((1, y1_ref), (2, y2_ref), (3, y3_ref)):
            y = y + yk_ref[pl.ds(s, TILE, stride=ROW_TILE), :] * g[:, k:k + 1]
        cols.append(y)
    h_new = h_ref[...] + m_ref[5:6, :] * jnp.concatenate(cols, axis=1)
    if final:
        yp_ref, yl_ref = outs
        ms = jnp.mean(h_new * h_new, axis=-1, keepdims=True)
        y = h_new * lax.rsqrt(ms + EPS) * fg_ref[...]

        @pl.when(i < P_TILES)
        def _():
            yp_ref[...] = y

        @pl.when(i >= P_TILES)
        def _():
            yl_ref[...] = y
    else:
        outs[0][...] = h_new


def _moe_combine(y_rows, gates_nt, h, mod, final_g, final):
    if final:
        out_shape = (jax.ShapeDtypeStruct((N_P, D), F32), jax.ShapeDtypeStruct((N_S, D), F32))
        out_specs = [
            pl.BlockSpec((TILE, D), lambda i: (jnp.minimum(i, P_TILES - 1), 0)),
            pl.BlockSpec((TILE, D), lambda i: (jnp.maximum(i - P_TILES, 0), 0)),
        ]
    else:
        out_shape = jax.ShapeDtypeStruct((N, D), F32)
        out_specs = pl.BlockSpec((TILE, D), lambda i: (i, 0))
    return pl.pallas_call(
        functools.partial(_combine_kernel, final=final),
        out_shape=out_shape,
        grid=(N_TILES,),
        in_specs=[pl.BlockSpec((TILE * ROW_TILE, 128), lambda i, k=k: (k * N_TILES + i, 0)) for k in range(TOP_K)] + [
            pl.BlockSpec((TILE, TOP_K), lambda i: (i, 0)),
            pl.BlockSpec((TILE, D), lambda i: (i, 0)),
            pl.BlockSpec((None, 6, D), lambda i: (_tile_mod_row(i), 0, 0)),
            pl.BlockSpec((1, D), lambda i: (0, 0)),
        ],
        out_specs=out_specs,
        compiler_params=_cp(("arbitrary",), VMEM_LIMIT),
        name="moe_combine",
    )(y_rows, y_rows, y_rows, y_rows, gates_nt, h, mod, final_g.reshape(1, D))


def _moe(layer, fn_packed, top_e, gates, h, mod, w1, b1, w2, b2, final_g, final):
    slot_a, block_e, n_used, next_expert = _routing_tables(top_e)
    y_rows = _moe_experts(layer, fn_packed, slot_a, block_e, n_used, next_expert, w1, b1, w2, b2)
    return _moe_combine(y_rows, gates.T, h, mod, final_g, final)


def _gate_weights(w_r, w_i):
    per = LRU_BLK // LRU_HD
    eye = jnp.eye(per, dtype=F32)

    def blockdiag(w):
        w = w.reshape(2, LRU_HEADS // per, per, LRU_HD, LRU_HD)
        full = jnp.einsum("dgaij,ab->dgaibj", w, eye)
        return full.reshape(2, LRU_HEADS // per, LRU_BLK, LRU_BLK)

    return jnp.concatenate([blockdiag(w_r), blockdiag(w_i)], axis=-1).astype(BF16)


def kernel(x_prompt, x_sample, state_rglru, c, c_ctx, norm_mix_g, norm_ffn_g, w_mod, b_mod, w_in0, lru_conv_w, lru_conv_b, lru_w_r, lru_b_r, lru_w_i, lru_b_i, lru_lambda, w_out0, w_in1, sgu_ln_g, sgu_ln_b, sgu_w_s, sgu_b_s, conv_dw_w, conv_dw_b, conv_ln_g, conv_ln_b, w_out1, w_router, b_router, w1, b1, w2, b2, final_norm_g):
    h = (x_prompt.reshape(N_P, D), x_sample.reshape(N_S, D))
    cond8 = jnp.concatenate([c_ctx[None, :], c, jnp.zeros((N_MOD - 1 - N_SAMPLE_SEQ, D), F32)], axis=0)
    mod = _adaln(cond8, w_mod, b_mod)

    proj0 = _inproj(h, norm_mix_g[0], mod[0], w_in0[0].astype(BF16))
    st = state_rglru[:, 0].astype(F32)
    h0 = jnp.zeros((2, N_MOD, LRU_W), F32).at[:, 1:1 + N_SAMPLE_SEQ].set(jnp.swapaxes(st, 0, 1))
    hs, ctx_state = _lru(proj0, lru_conv_w[0], lru_conv_b[0], _gate_weights(lru_w_r[0], lru_w_i[0]),
              lru_b_r[0], lru_b_i[0], lru_lambda[0], h0)
    h, fn, top_e, gates = _post0(hs, proj0, _fourier_prompt(proj0), _fourier_sample(proj0), w_out0[0].astype(BF16), h, mod[0], norm_ffn_g[0],
                                 w_router[0].T, b_router[0])
    h = _moe(0, fn, top_e, gates, h, mod[0], w1, b1, w2, b2, final_norm_g, False)

    proj1 = _inproj(h, norm_mix_g[1], mod[1], w_in1[0].astype(BF16))
    bs_full = jnp.repeat(sgu_b_s[0].T, CHUNK, axis=1)
    h, fn, top_e, gates = _post1(proj1, sgu_ln_g[0], sgu_ln_b[0], sgu_w_s[0].astype(BF16), bs_full,
                                 conv_dw_w[0], conv_dw_b[0], conv_ln_g[0], conv_ln_b[0],
                                 w_out1[0].astype(BF16), h, mod[1], norm_ffn_g[1], w_router[1].T, b_router[1])
    y_p, y_l = _moe(1, fn, top_e, gates, h, mod[1], w1, b1, w2, b2, final_norm_g, True)

    y_prompt = y_p.reshape(N_PROMPT_SEQ, T_PROMPT, D)
    y_sample = y_l.reshape(N_SAMPLE_SEQ, T_SAMPLE, D)
    new_state = jnp.transpose(ctx_state, (1, 2, 0, 3))
    return (y_prompt, y_sample, new_state.astype(x_prompt.dtype))
```

```python
import functools

import numpy as np
import jax
import jax.numpy as jnp
from jax import lax
from jax.experimental import pallas as pl
from jax.experimental.pallas import tpu as pltpu

F32 = jnp.float32
BF16 = jnp.bfloat16

D = 1024
N_PROMPT_SEQ = 32
T_PROMPT = 256
N_SAMPLE_SEQ = 2
T_SAMPLE = 2048
N_P = N_PROMPT_SEQ * T_PROMPT
N_S = N_SAMPLE_SEQ * T_SAMPLE
N = N_P + N_S
EPS = 1e-6

TILE = 256
N_TILES = N // TILE
P_TILES = N_P // TILE
S_TILES = T_SAMPLE // TILE
TM_PROJ = 512
N_MOD = 8

LRU_W = 768
LRU_HEADS = 12
LRU_HD = 64
LRU_K = 4
LRU_LEFT = 2
LRU_C = 8.0
LRU_BLK = 256
FN_W = 256
FN_G = 4
FN_GD = 64
IN0 = 2 * LRU_W + FN_W

SGU_W = 512
SGU_G = 4
CHUNK = 128
CONV_W = 512
CONV_K = 31
CONV_PAD = 15
CONV_HALO = 16
IN1 = 2 * SGU_W + 2 * CONV_W

N_EXP = 32
TOP_K = 4
D_FF = 1024
ALPHA = 1.702
LIMIT = 7.0
MOE_BLOCK = 256
N_ASSIGN = N * TOP_K
N_BLOCKS = N_ASSIGN // MOE_BLOCK + N_EXP
N_SLOTS = N_BLOCKS * MOE_BLOCK
ROW_TILE = D // 128
PACK_ROWS = D // 2 // 128
PACK_PAIR = 2 * PACK_ROWS

VMEM_LIMIT = 56 * 1024 * 1024
MOE_VMEM_LIMIT = 60 * 1024 * 1024


def _cp(sem, vmem=None):
    return pltpu.CompilerParams(dimension_semantics=sem, vmem_limit_bytes=vmem)


def _dot(a, b):
    return jnp.dot(a, b, preferred_element_type=F32)


def _split(x):
    hi = x.astype(BF16)
    lo = (x - hi.astype(F32)).astype(BF16)
    return hi, lo


def _dot3(a, b):
    ah, al = _split(a)
    bh, bl = _split(b)
    return _dot(ah, bh) + _dot(al, bh) + _dot(ah, bl)


def _dot3_nt(a, b):
    dn = (((1,), (1,)), ((), ()))
    d = lambda x, y: lax.dot_general(x, y, dn, preferred_element_type=F32)
    ah, al = _split(a)
    bh, bl = _split(b)
    return d(ah, bh) + d(al, bh) + d(ah, bl)


def _sigmoid(x):
    return 1.0 / (1.0 + jnp.exp(-x))


def _gelu(x):
    return 0.5 * x * (1.0 + jnp.tanh(0.7978845608028654 * (x + 0.044715 * (x * x * x))))


def _rms_mod(x, g, shift, scale):
    ms = jnp.mean(x * x, axis=-1, keepdims=True)
    y = x * lax.rsqrt(ms + EPS) * g
    return y * (1.0 + scale) + shift


def _layernorm(x, g, b):
    xc = x - jnp.mean(x, axis=-1, keepdims=True)
    var = jnp.mean(xc * xc, axis=-1, keepdims=True)
    return xc * lax.rsqrt(var + EPS) * g + b


def _tile_mod_row(r):
    return jnp.where(r < P_TILES, 0, 1 + (r - P_TILES) // S_TILES)


def _tile_is_seq_start(r):
    return (r < P_TILES) | ((r - P_TILES) % S_TILES == 0)


def _tile_is_seq_end(r):
    return (r < P_TILES) | ((r - P_TILES) % S_TILES == S_TILES - 1)


MOD_TN = 512


def _adaln_kernel(cond_ref, w_ref, b_ref, o_ref):
    cond = cond_ref[...]
    s = cond * _sigmoid(cond)
    o_ref[...] = _dot3(s, w_ref[...]) + b_ref[...]


def _adaln(cond8, w_mod, b_mod):
    depth = w_mod.shape[0]
    out = pl.pallas_call(
        _adaln_kernel,
        out_shape=jax.ShapeDtypeStruct((depth, N_MOD, 6 * D), F32),
        grid=(depth, 6 * D // MOD_TN),
        in_specs=[
            pl.BlockSpec((N_MOD, D), lambda l, j: (0, 0)),
            pl.BlockSpec((None, D, MOD_TN), lambda l, j: (l, 0, j)),
            pl.BlockSpec((None, 1, MOD_TN), lambda l, j: (l, 0, j)),
        ],
        out_specs=pl.BlockSpec((None, N_MOD, MOD_TN), lambda l, j: (l, 0, j)),
        compiler_params=_cp(("arbitrary", "arbitrary")),
        name="adaln",
    )(cond8, w_mod, b_mod.reshape(depth, 1, 6 * D))
    return out.reshape(depth, N_MOD, 6, D)


PROJ_P_STEPS = N_P // TM_PROJ
PROJ_S_STEPS = T_SAMPLE // TM_PROJ


def _stream_specs(rows, p_steps, h):
    if isinstance(h, tuple):
        hp, hl = h
        first_latent = 0
    else:
        hp = hl = h
        first_latent = p_steps
    specs = [
        pl.BlockSpec((rows, D), lambda i, *_: (jnp.minimum(i, p_steps - 1), 0)),
        pl.BlockSpec((rows, D), lambda i, *_: (jnp.maximum(i - p_steps, 0) + first_latent, 0)),
    ]
    return specs, (hp, hl)


def _inproj_kernel(xp_ref, xl_ref, g_ref, m_ref, w_ref, o_ref):
    x = jnp.where(pl.program_id(0) < PROJ_P_STEPS, xp_ref[...], xl_ref[...])
    hn = _rms_mod(x, g_ref[...], m_ref[0:1, :], m_ref[1:2, :])
    o_ref[...] = _dot(hn.astype(BF16), w_ref[...])


def _inproj(h, g, mod, w_bf16):
    n_out = w_bf16.shape[1]

    def mod_row(i):
        return jnp.where(i < PROJ_P_STEPS, 0, 1 + (i - PROJ_P_STEPS) // PROJ_S_STEPS)

    h_specs, h_args = _stream_specs(TM_PROJ, PROJ_P_STEPS, h)
    return pl.pallas_call(
        _inproj_kernel,
        out_shape=jax.ShapeDtypeStruct((N, n_out), F32),
        grid=(N // TM_PROJ,),
        in_specs=h_specs + [
            pl.BlockSpec((1, D), lambda i: (0, 0)),
            pl.BlockSpec((None, 6, D), lambda i: (mod_row(i), 0, 0)),
            pl.BlockSpec((D, n_out), lambda i: (0, 0)),
        ],
        out_specs=pl.BlockSpec((TM_PROJ, n_out), lambda i: (i, 0)),
        compiler_params=_cp(("arbitrary",), VMEM_LIMIT),
        name="inproj",
    )(*h_args, g.reshape(1, D), mod, w_bf16)


LRU_HALO = 8


def _lru_tile(d, s):
    return jnp.where(d == 0, s, N_TILES - 1 - s)


def _lru_kernel(x_ref, prev_ref, next_ref, cw_ref, cb_ref, wg_ref, br_ref, bi_ref, lam_ref, h0_ref,
                o_ref, state_ref, ext_ref, a_ref, carry_ref):
    d = pl.program_id(0)
    r = _lru_tile(d, pl.program_id(1))
    start = _tile_is_seq_start(r)
    end = _tile_is_seq_end(r)

    ext_ref[0:LRU_HALO, :] = jnp.where(start, 0.0, prev_ref[...])
    ext_ref[LRU_HALO:LRU_HALO + TILE, :] = x_ref[...]
    ext_ref[LRU_HALO + TILE:, :] = jnp.where(end, 0.0, next_ref[...])
    xc = cb_ref[...] + jnp.zeros((TILE, LRU_W), F32)
    for k in range(LRU_K):
        off = LRU_HALO - LRU_LEFT + k
        xc = xc + ext_ref[off:off + TILE, :] * cw_ref[k:k + 1, :]

    xcb = xc.astype(BF16)
    pre_r, pre_i = [], []
    for blk in range(LRU_W // LRU_BLK):
        g = _dot(xcb[:, blk * LRU_BLK:(blk + 1) * LRU_BLK], wg_ref[blk])
        pre_r.append(g[:, :LRU_BLK])
        pre_i.append(g[:, LRU_BLK:])
    gate_r = _sigmoid(jnp.concatenate(pre_r, axis=1) + br_ref[...])
    gate_i = _sigmoid(jnp.concatenate(pre_i, axis=1) + bi_ref[...])
    neg_lam = -lam_ref[...]
    softplus = jnp.maximum(neg_lam, 0.0) + jnp.log1p(jnp.exp(-jnp.abs(neg_lam)))
    log_a = (-LRU_C) * gate_r * softplus
    a = jnp.exp(log_a)
    a_ref[...] = a
    o_ref[...] = jnp.sqrt(-jnp.tanh(log_a) * (a * a + 1.0)) * (gate_i * xc)

    fresh = jnp.where(d == 0, start, end)
    h_init = jnp.where(fresh, h0_ref[pl.ds(_tile_mod_row(r), 1), :], carry_ref[...])

    def step(t, h):
        tt = jnp.where(d == 0, t, TILE - 1 - t)
        h = a_ref[pl.ds(tt, 1), :] * h + o_ref[pl.ds(tt, 1), :]
        o_ref[pl.ds(tt, 1), :] = h
        return h

    h_last = lax.fori_loop(0, TILE, step, h_init, unroll=8)
    carry_ref[...] = h_last

    @pl.when(r < P_TILES)
    def _():
        state_ref[...] = h_last


def _lru(proj0, conv_w, conv_b, wg, b_r, b_i, lam, h0):
    n_halo_blocks = N // LRU_HALO
    per_tile = TILE // LRU_HALO
    tile = lambda d, s: _lru_tile(d, s)
    return pl.pallas_call(
        _lru_kernel,
        out_shape=(jax.ShapeDtypeStruct((2, N, LRU_W), F32),
                   jax.ShapeDtypeStruct((2, N_PROMPT_SEQ, 1, LRU_W), F32)),
        grid=(2, N_TILES),
        in_specs=[
            pl.BlockSpec((TILE, LRU_W), lambda d, s: (tile(d, s), 0)),
            pl.BlockSpec((LRU_HALO, LRU_W), lambda d, s: (jnp.maximum(tile(d, s) * per_tile - 1, 0), 0)),
            pl.BlockSpec((LRU_HALO, LRU_W),
                         lambda d, s: (jnp.minimum((tile(d, s) + 1) * per_tile, n_halo_blocks - 1), 0)),
            pl.BlockSpec((LRU_K, LRU_W), lambda d, s: (0, 0)),
            pl.BlockSpec((1, LRU_W), lambda d, s: (0, 0)),
            pl.BlockSpec((None, LRU_W // LRU_BLK, LRU_BLK, 2 * LRU_BLK), lambda d, s: (d, 0, 0, 0)),
            pl.BlockSpec((None, 1, LRU_W), lambda d, s: (d, 0, 0)),
            pl.BlockSpec((None, 1, LRU_W), lambda d, s: (d, 0, 0)),
            pl.BlockSpec((None, 1, LRU_W), lambda d, s: (d, 0, 0)),
            pl.BlockSpec((None, N_MOD, LRU_W), lambda d, s: (d, 0, 0)),
        ],
        out_specs=[
            pl.BlockSpec((None, TILE, LRU_W), lambda d, s: (d, tile(d, s), 0)),
            pl.BlockSpec((None, None, 1, LRU_W), lambda d, s: (d, jnp.minimum(tile(d, s), P_TILES - 1), 0, 0)),
        ],
        scratch_shapes=[
            pltpu.VMEM((TILE + 2 * LRU_HALO, LRU_W), F32),
            pltpu.VMEM((TILE, LRU_W), F32),
            pltpu.VMEM((1, LRU_W), F32),
        ],
        compiler_params=_cp(("arbitrary", "arbitrary"), VMEM_LIMIT),
        name="rglru_scan",
    )(proj0, proj0, proj0, conv_w, conv_b.reshape(1, LRU_W), wg,
      b_r.reshape(2, 1, LRU_W), b_i.reshape(2, 1, LRU_W), lam.reshape(2, 1, LRU_W), h0)


def _dft_tables(n, scale):
    k = np.arange(n, dtype=np.int64)
    ang = 2.0 * np.pi * ((k[:, None] * k[None, :]) % n).astype(np.float64) / n
    return np.cos(ang) * scale, np.sin(ang) * scale


def _channel_tables():
    c, s = _dft_tables(FN_GD, FN_GD ** -0.5)
    eye = np.eye(FN_G)
    return (jnp.asarray(np.kron(eye, c), BF16), jnp.asarray(np.kron(eye, s), BF16))


def _time_tables(t_len):
    c, s = _dft_tables(t_len, t_len ** -0.5)
    return jnp.asarray(c, BF16), jnp.asarray(s, BF16)


def _fourier_prompt_kernel(z_ref, cc_ref, sc_ref, ct_ref, st_ref, o_ref):
    z = z_ref[...].astype(BF16)
    zc = _dot(z, cc_ref[...]).astype(BF16)
    zs = _dot(z, sc_ref[...]).astype(BF16)
    o_ref[...] = _dot(ct_ref[...], zc) - _dot(st_ref[...], zs)


def _fourier_prompt(proj0):
    cc, sc = _channel_tables()
    ct, st = _time_tables(T_PROMPT)
    const = lambda b: (0, 0)
    return pl.pallas_call(
        _fourier_prompt_kernel,
        out_shape=jax.ShapeDtypeStruct((N_P, FN_W), F32),
        grid=(N_PROMPT_SEQ,),
        in_specs=[
            pl.BlockSpec((T_PROMPT, FN_W), lambda b: (b, 2 * LRU_W // FN_W)),
            pl.BlockSpec((FN_W, FN_W), const),
            pl.BlockSpec((FN_W, FN_W), const),
            pl.BlockSpec((T_PROMPT, T_PROMPT), const),
            pl.BlockSpec((T_PROMPT, T_PROMPT), const),
        ],
        out_specs=pl.BlockSpec((T_PROMPT, FN_W), lambda b: (b, 0)),
        compiler_params=_cp(("arbitrary",)),
        name="fourier_prompt",
    )(proj0, cc, sc, ct, st)


def _fourier_sample_kernel(z_ref, cc_ref, sc_ref, ct_ref, st_ref, o_ref, zc_ref, zs_ref):
    @pl.when(pl.program_id(1) == 0)
    def _():
        z = z_ref[...].astype(BF16)
        zc_ref[...] = _dot(z, cc_ref[...]).astype(BF16)
        zs_ref[...] = _dot(z, sc_ref[...]).astype(BF16)

    o_ref[...] = _dot(ct_ref[...], zc_ref[...]) - _dot(st_ref[...], zs_ref[...])


def _fourier_sample(proj0):
    cc, sc = _channel_tables()
    ct, st = _time_tables(T_SAMPLE)
    const = lambda b, i: (0, 0)
    first_seq_block = N_P // T_SAMPLE
    return pl.pallas_call(
        _fourier_sample_kernel,
        out_shape=jax.ShapeDtypeStruct((N_S, FN_W), F32),
        grid=(N_SAMPLE_SEQ, S_TILES),
        in_specs=[
            pl.BlockSpec((T_SAMPLE, FN_W), lambda b, i: (first_seq_block + b, 2 * LRU_W // FN_W)),
            pl.BlockSpec((FN_W, FN_W), const),
            pl.BlockSpec((FN_W, FN_W), const),
            pl.BlockSpec((TILE, T_SAMPLE), lambda b, i: (i, 0)),
            pl.BlockSpec((TILE, T_SAMPLE), lambda b, i: (i, 0)),
        ],
        out_specs=pl.BlockSpec((TILE, FN_W), lambda b, i: (b * S_TILES + i, 0)),
        scratch_shapes=[pltpu.VMEM((T_SAMPLE, FN_W), BF16), pltpu.VMEM((T_SAMPLE, FN_W), BF16)],
        compiler_params=_cp(("arbitrary", "arbitrary"), VMEM_LIMIT),
        name="fourier_sample",
    )(proj0, cc, sc, ct, st)


def _epilogue(mix, hp_ref, hl_ref, m_ref, gf_ref, wr_ref, brt_ref, hout_ref, fn_ref, tope_ref, gate_ref):
    h = jnp.where(pl.program_id(0) < P_TILES, hp_ref[...], hl_ref[...])
    h_new = h + m_ref[2:3, :] * mix
    hout_ref[...] = h_new
    fn = _rms_mod(h_new, gf_ref[...], m_ref[3:4, :], m_ref[4:5, :])
    hi = lax.bitcast_convert_type(fn[:, :D // 2].astype(BF16).astype(F32), jnp.uint32)
    lo = lax.bitcast_convert_type(fn[:, D // 2:].astype(BF16).astype(F32), jnp.uint32)
    packed = hi | (lo >> 16)
    for q in range(PACK_ROWS):
        fn_ref[pl.ds(q, TILE, stride=PACK_ROWS), :] = packed[:, q * 128:(q + 1) * 128]
    logits = _dot3_nt(wr_ref[...], fn) + brt_ref[...]
    iota = lax.broadcasted_iota(jnp.int32, logits.shape, 0)
    vals, idxs = [], []
    for _ in range(TOP_K):
        m = jnp.max(logits, axis=0, keepdims=True)
        idx = jnp.min(jnp.where(logits == m, iota, N_EXP), axis=0, keepdims=True)
        vals.append(m)
        idxs.append(idx)
        logits = jnp.where(iota == idx, -jnp.inf, logits)
    exps = [jnp.exp(v - vals[0]) for v in vals]
    denom = exps[0] + exps[1] + exps[2] + exps[3]
    for k in range(TOP_K):
        tope_ref[k:k + 1, :] = idxs[k]
        gate_ref[k:k + 1, :] = exps[k] / denom


_EPI_OUT_SHAPES = (
    jax.ShapeDtypeStruct((N, D), F32),
    jax.ShapeDtypeStruct((N * PACK_ROWS, 128), jnp.uint32),
    jax.ShapeDtypeStruct((TOP_K, N), jnp.int32),
    jax.ShapeDtypeStruct((TOP_K, N), F32),
)


def _epi_in_specs(h):
    h_specs, _ = _stream_specs(TILE, P_TILES, h)
    return h_specs + [
        pl.BlockSpec((None, 6, D), lambda r: (_tile_mod_row(r), 0, 0)),
        pl.BlockSpec((1, D), lambda r: (0, 0)),
        pl.BlockSpec((N_EXP, D), lambda r: (0, 0)),
        pl.BlockSpec((N_EXP, 1), lambda r: (0, 0)),
    ]


def _epi_out_specs():
    return [
        pl.BlockSpec((TILE, D), lambda r: (r, 0)),
        pl.BlockSpec((TILE * PACK_ROWS, 128), lambda r: (r, 0)),
        pl.BlockSpec((TOP_K, TILE), lambda r: (0, r)),
        pl.BlockSpec((TOP_K, TILE), lambda r: (0, r)),
    ]


def _post0_kernel(hs_ref, xg_ref, yfp_ref, yfl_ref, wo_ref, hp_ref, hl_ref, m_ref, gf_ref, wr_ref, brt_ref,
                  hout_ref, fn_ref, tope_ref, gate_ref):
    y_rec = (hs_ref[0] + hs_ref[1]) * _gelu(xg_ref[...])
    y_four = jnp.where(pl.program_id(0) < P_TILES, yfp_ref[...], yfl_ref[...])
    mix = (_dot(y_rec.astype(BF16), wo_ref[0:LRU_W, :])
           + _dot(y_four.astype(BF16), wo_ref[LRU_W:, :]))
    _epilogue(mix, hp_ref, hl_ref, m_ref, gf_ref, wr_ref, brt_ref, hout_ref, fn_ref, tope_ref, gate_ref)


def _post0(hs, proj0, yf_prompt, yf_latent, w_out_bf16, h, mod, g_ffn, w_router_t, b_router):
    return pl.pallas_call(
        _post0_kernel,
        out_shape=_EPI_OUT_SHAPES,
        grid=(N_TILES,),
        in_specs=[
            pl.BlockSpec((2, TILE, LRU_W), lambda r: (0, r, 0)),
            pl.BlockSpec((TILE, LRU_W), lambda r: (r, 1)),
            pl.BlockSpec((TILE, FN_W), lambda r: (jnp.minimum(r, P_TILES - 1), 0)),
            pl.BlockSpec((TILE, FN_W), lambda r: (jnp.maximum(r - P_TILES, 0), 0)),
            pl.BlockSpec((D, D), lambda r: (0, 0)),
        ] + _epi_in_specs(h),
        out_specs=_epi_out_specs(),
        compiler_params=_cp(("arbitrary",), VMEM_LIMIT),
        name="post_rglru_fourier",
    )(hs, proj0, yf_prompt, yf_latent, w_out_bf16, *_stream_specs(TILE, P_TILES, h)[1], mod, g_ffn.reshape(1, D), w_router_t,
      b_router.reshape(N_EXP, 1))


def _glu(x):
    return x[:, :CONV_W] * _sigmoid(x[:, CONV_W:])


def _post1_kernel(p_ref, prev_ref, next_ref, lng_ref, lnb_ref, ws_ref, bs_ref, dww_ref, dwb_ref,
                  clg_ref, clb_ref, wo_ref, hp_ref, hl_ref, m_ref, gf_ref, wr_ref, brt_ref,
                  hout_ref, fn_ref, tope_ref, gate_ref, ext_ref):
    r = pl.program_id(0)
    z = _gelu(p_ref[:, 0:2 * SGU_W])
    u = z[:, :SGU_W]
    v = _layernorm(z[:, SGU_W:], lng_ref[...], lnb_ref[...]).astype(BF16)
    rows = []
    for n in range(TILE // CHUNK):
        cols = []
        for g in range(SGU_G):
            vb = v[n * CHUNK:(n + 1) * CHUNK, g * CHUNK:(g + 1) * CHUNK]
            cols.append(_dot(ws_ref[g], vb))
        rows.append(jnp.concatenate(cols, axis=1) + bs_ref[...])
    y_sgu = u * jnp.concatenate(rows, axis=0)

    ext_ref[0:CONV_HALO, :] = jnp.where(_tile_is_seq_start(r), 0.0, _glu(prev_ref[...]))
    ext_ref[CONV_HALO:CONV_HALO + TILE, :] = _glu(p_ref[:, 2 * SGU_W:])
    ext_ref[CONV_HALO + TILE:, :] = jnp.where(_tile_is_seq_end(r), 0.0, _glu(next_ref[...]))
    dconv = dwb_ref[...] + jnp.zeros((TILE, CONV_W), F32)
    for k in range(CONV_K):
        off = CONV_HALO - CONV_PAD + k
        dconv = dconv + ext_ref[off:off + TILE, :] * dww_ref[k:k + 1, :]
    ln = _layernorm(dconv, clg_ref[...], clb_ref[...])
    y_conv = ln * _sigmoid(ln)

    mix = (_dot(y_sgu.astype(BF16), wo_ref[0:SGU_W, :])
           + _dot(y_conv.astype(BF16), wo_ref[SGU_W:, :]))
    _epilogue(mix, hp_ref, hl_ref, m_ref, gf_ref, wr_ref, brt_ref, hout_ref, fn_ref, tope_ref, gate_ref)


def _post1(proj1, ln_g, ln_b, ws_bf16, bs_full, dw_w, dw_b, cln_g, cln_b, w_out_bf16,
           h, mod, g_ffn, w_router_t, b_router):
    per_tile = TILE // CONV_HALO
    n_halo_blocks = N // CONV_HALO
    const2 = lambda r: (0, 0)
    row = lambda a: a.reshape(1, -1)
    return pl.pallas_call(
        _post1_kernel,
        out_shape=_EPI_OUT_SHAPES,
        grid=(N_TILES,),
        in_specs=[
            pl.BlockSpec((TILE, IN1), lambda r: (r, 0)),
            pl.BlockSpec((CONV_HALO, 2 * CONV_W), lambda r: (jnp.maximum(r * per_tile - 1, 0), 1)),
            pl.BlockSpec((CONV_HALO, 2 * CONV_W),
                         lambda r: (jnp.minimum((r + 1) * per_tile, n_halo_blocks - 1), 1)),
            pl.BlockSpec((1, SGU_W), const2),
            pl.BlockSpec((1, SGU_W), const2),
            pl.BlockSpec((SGU_G, CHUNK, CHUNK), lambda r: (0, 0, 0)),
            pl.BlockSpec((CHUNK, SGU_W), const2),
            pl.BlockSpec((CONV_K, CONV_W), const2),
            pl.BlockSpec((1, CONV_W), const2),
            pl.BlockSpec((1, CONV_W), const2),
            pl.BlockSpec((1, CONV_W), const2),
            pl.BlockSpec((D, D), const2),
        ] + _epi_in_specs(h),
        out_specs=_epi_out_specs(),
        scratch_shapes=[pltpu.VMEM((TILE + 2 * CONV_HALO, CONV_W), F32)],
        compiler_params=_cp(("arbitrary",), VMEM_LIMIT),
        name="post_sgu_conformer",
    )(proj1, proj1, proj1, row(ln_g), row(ln_b), ws_bf16, bs_full, dw_w, row(dw_b), row(cln_g), row(cln_b),
      w_out_bf16, *_stream_specs(TILE, P_TILES, h)[1], mod, g_ffn.reshape(1, D), w_router_t,
      b_router.reshape(N_EXP, 1))


RANK_TILE = 512
PAD_BASE = N_ASSIGN
PLACEHOLDER_BASE = PAD_BASE + 2 * MOE_BLOCK
Y_ROWS = PLACEHOLDER_BASE + 2 * MOE_BLOCK


def _rank_kernel(e_ref, tri_ref, rank_ref, cnt_ref, run_ref):
    @pl.when(pl.program_id(0) == 0)
    def _():
        run_ref[...] = jnp.zeros_like(run_ref)

    iota = lax.broadcasted_iota(jnp.int32, (N_EXP, RANK_TILE), 0)
    run = run_ref[...]
    for k in range(TOP_K):
        onehot = jnp.where(iota == e_ref[k:k + 1, :], 1.0, 0.0)
        incl = _dot(onehot.astype(BF16), tri_ref[...])
        rank = jnp.sum(onehot * (incl - 1.0 + run), axis=0, keepdims=True)
        rank_ref[k:k + 1, :] = rank.astype(jnp.int32)
        run = run + incl[:, RANK_TILE - 1:RANK_TILE]
    run_ref[...] = run
    cnt_ref[...] = run.astype(jnp.int32)


def _rank(top_e):
    tri = jnp.asarray(np.triu(np.ones((RANK_TILE, RANK_TILE), np.float32)), BF16)
    return pl.pallas_call(
        _rank_kernel,
        out_shape=(jax.ShapeDtypeStruct((TOP_K, N), jnp.int32), jax.ShapeDtypeStruct((N_EXP, 1), jnp.int32)),
        grid=(N // RANK_TILE,),
        in_specs=[
            pl.BlockSpec((TOP_K, RANK_TILE), lambda i: (0, i)),
            pl.BlockSpec((RANK_TILE, RANK_TILE), lambda i: (0, 0)),
        ],
        out_specs=[
            pl.BlockSpec((TOP_K, RANK_TILE), lambda i: (0, i)),
            pl.BlockSpec((N_EXP, 1), lambda i: (0, 0)),
        ],
        scratch_shapes=[pltpu.VMEM((N_EXP, 1), F32)],
        compiler_params=_cp(("arbitrary",)),
        name="moe_rank",
    )(top_e, tri)


def _inverse_kernel(pos_ref, pend_ref, out_ref):
    def fill_block(start):
        def fill(j, carry):
            s = start + j
            out_ref[s] = PAD_BASE + (s & (2 * MOE_BLOCK - 1))
            return carry
        lax.fori_loop(0, MOE_BLOCK, fill, 0, unroll=16)

    def per_expert(e, carry):
        fill_block(jnp.maximum(pend_ref[e] - MOE_BLOCK, 0))
        return carry

    def idle_block(b, carry):
        fill_block(b * MOE_BLOCK)
        return carry

    lax.fori_loop(0, N_EXP, per_expert, 0)
    lax.fori_loop(pend_ref[N_EXP - 1] // MOE_BLOCK, N_BLOCKS, idle_block, 0)

    def body(t, carry):
        for k in range(TOP_K):
            out_ref[pos_ref[k * N + t]] = t * TOP_K + k
        return carry

    lax.fori_loop(0, N, body, 0, unroll=8)


def _inverse(pos_flat, pad_end):
    smem = pl.BlockSpec(memory_space=pltpu.SMEM)
    return pl.pallas_call(
        _inverse_kernel,
        out_shape=jax.ShapeDtypeStruct((N_SLOTS,), jnp.int32),
        in_specs=[smem, smem],
        out_specs=smem,
        name="moe_inverse",
    )(pos_flat, pad_end)


def _routing_tables(top_e):
    rank, counts = _rank(top_e)
    counts = counts.reshape(N_EXP)
    padded = (counts + MOE_BLOCK - 1) // MOE_BLOCK * MOE_BLOCK
    pad_end = jnp.cumsum(padded).astype(jnp.int32)
    pad_start = pad_end - padded
    experts = jnp.arange(N_EXP, dtype=jnp.int32)[:, None, None]
    pos = rank + jnp.sum(jnp.where(top_e[None] == experts, pad_start[:, None, None], 0), axis=0)
    pos = pos.astype(jnp.int32).reshape(-1)
    n_used = pad_end[-1] // MOE_BLOCK
    blk = jnp.minimum(jnp.arange(N_BLOCKS, dtype=jnp.int32), n_used - 1)
    block_e = jnp.sum(pad_end[None, :] <= (blk * MOE_BLOCK)[:, None], axis=1).astype(jnp.int32)
    slot_a = _inverse(pos, pad_end)
    tok = jnp.minimum(slot_a >> 2, N - 1)
    gather_key = (tok >> 1) * (2 * PACK_PAIR) + (tok & 1)
    out_row = jnp.where(slot_a >= PAD_BASE, slot_a, (slot_a & (TOP_K - 1)) * N + (slot_a >> 2))
    placeholder_rows = PLACEHOLDER_BASE + jnp.arange(MOE_BLOCK, dtype=jnp.int32)
    out_row8 = jnp.concatenate([placeholder_rows, out_row]) * ROW_TILE
    ids = jnp.arange(N_EXP, dtype=jnp.int32)
    later_used = (ids[None, :] > ids[:, None]) & (counts[None, :] > 0)
    next_expert = jnp.min(jnp.where(later_used, ids[None, :], N_EXP), axis=1).astype(jnp.int32)
    return gather_key, out_row8, block_e, n_used.reshape(1), next_expert


def _moe_kernel(be_ref, nu_ref, gk_ref, row8_ref, nxt_ref, x_hbm, w1_hbm, b1g_ref, b1l_ref, w2_hbm, b2_ref, perm_ref,
                y_hbm, xs, xbuf, w1f, w2f, w1s, w2s, sem, ybuf, osem, *, layer):
    i = pl.program_id(0)
    n_used = nu_ref[0]
    slot = i % 2
    e = be_ref[i]
    e_prev = be_ref[jnp.maximum(i - 1, 0)]
    low_half = lax.broadcasted_iota(jnp.int32, (PACK_PAIR, 128), 0) < PACK_ROWS

    def weight_copies(expert):
        return (pltpu.make_async_copy(w1_hbm.at[layer, expert], w1f, sem.at[1]),
                pltpu.make_async_copy(w2_hbm.at[layer, expert], w2f, sem.at[2]))

    def token_tile(key):
        tile = xs[pl.ds(pl.multiple_of(key >> 1, PACK_PAIR), PACK_PAIR), :]
        return tile, pltpu.roll(tile, PACK_ROWS, 0), key & 1

    def gather_pair(blk, jj, dst_slot):
        t0, r0, half0 = token_tile(gk_ref[blk * MOE_BLOCK + 2 * jj])
        t1, r1, half1 = token_tile(gk_ref[blk * MOE_BLOCK + 2 * jj + 1])
        lower = jnp.where(half0 == 0, t0, r0)
        upper = jnp.where(half1 == 1, t1, r1)
        start = jj * PACK_PAIR if isinstance(jj, int) else pl.multiple_of(jj * PACK_PAIR, PACK_PAIR)
        xbuf[dst_slot, pl.ds(start, PACK_PAIR), :] = jnp.where(low_half, lower, upper)

    def out_copy(src_slot, j, row8):
        start = j * ROW_TILE if isinstance(j, int) else pl.multiple_of(j * ROW_TILE, ROW_TILE)
        dst = y_hbm.at[pl.ds(pl.multiple_of(row8, ROW_TILE), ROW_TILE), :]
        return pltpu.make_async_copy(ybuf.at[src_slot, pl.ds(start, ROW_TILE), :], dst, osem.at[src_slot])

    def out_buffer_wait(src_slot):
        pltpu.make_async_copy(ybuf.at[src_slot], y_hbm.at[pl.ds(0, MOE_BLOCK * ROW_TILE), :],
                              osem.at[src_slot]).wait()

    @pl.when(i == 0)
    def _():
        xs_copy = pltpu.make_async_copy(x_hbm, xs, sem.at[0])
        xs_copy.start()
        for c in weight_copies(e):
            c.start()
        ybuf[...] = jnp.zeros_like(ybuf)

        def zero_rows(first_row):
            def body(j, carry):
                out_copy(0, j, (first_row + j) * ROW_TILE).start()
                return carry
            lax.fori_loop(0, MOE_BLOCK, body, 0, unroll=8)

        zero_rows(PAD_BASE)
        zero_rows(PAD_BASE + MOE_BLOCK)
        out_buffer_wait(0)
        out_buffer_wait(0)
        zero_rows(PLACEHOLDER_BASE + MOE_BLOCK)
        xs_copy.wait()

        def body(jj, carry):
            gather_pair(0, jj, 0)
            return carry
        lax.fori_loop(0, MOE_BLOCK // 2, body, 0, unroll=4)

    @pl.when((i < n_used) & ((i == 0) | (e != e_prev)))
    def _():
        for c in weight_copies(e):
            c.wait()
        for c in range(2 * D_FF // 256):
            wc = w1f[:, c * 256:(c + 1) * 256].astype(BF16)
            w1s[:, c * 256:(c + 1) * 256] = _dot(wc, perm_ref[...]).astype(BF16)
        w2s[...] = w2f[...].astype(BF16)
        e_next = nxt_ref[e]

        @pl.when(e_next < N_EXP)
        def _():
            for c in weight_copies(e_next):
                c.start()

    def block(prefetch_next):
        for j in range(MOE_BLOCK):
            out_copy(1 - slot, j, row8_ref[i * MOE_BLOCK + j]).start()
        halves = [[], []]
        for q in range(PACK_ROWS):
            words = xbuf[slot, pl.ds(q, MOE_BLOCK, stride=PACK_ROWS), :]
            halves[0].append(lax.bitcast_convert_type(words & jnp.uint32(0xFFFF0000), F32).astype(BF16))
            halves[1].append(lax.bitcast_convert_type(words << 16, F32).astype(BF16))
        x = jnp.concatenate(halves[0] + halves[1], axis=1)
        if prefetch_next:
            for jj in range(MOE_BLOCK // 2):
                gather_pair(i + 1, jj, 1 - slot)
        hid = _dot(x, w1s[...])
        acts = []
        for c in range(D_FF // 128):
            h_glu = hid[:, c * 256:c * 256 + 128] + b1g_ref[:, c * 128:(c + 1) * 128]
            h_lin = hid[:, c * 256 + 128:(c + 1) * 256] + b1l_ref[:, c * 128:(c + 1) * 128]
            h_glu = jnp.minimum(h_glu, LIMIT)
            h_lin = jnp.clip(h_lin, -LIMIT, LIMIT)
            acts.append((h_glu * _sigmoid(ALPHA * h_glu) * (h_lin + 1.0)).astype(BF16))
        y = _dot(jnp.concatenate(acts, axis=1), w2s[...]) + b2_ref[...]
        out_buffer_wait(slot)
        for s in range(ROW_TILE):
            ybuf[slot, pl.ds(s, MOE_BLOCK, stride=ROW_TILE), :] = y[:, s * 128:(s + 1) * 128]
        if not prefetch_next:
            def send(j, carry):
                out_copy(slot, j, row8_ref[(i + 1) * MOE_BLOCK + j]).start()
                return carry
            lax.fori_loop(0, MOE_BLOCK, send, 0, unroll=8)
            out_buffer_wait(1 - slot)
            out_buffer_wait(slot)

    @pl.when(i + 1 < n_used)
    def _():
        block(True)

    @pl.when(i + 1 == n_used)
    def _():
        block(False)


def _deinterleave_matrix():
    p = np.zeros((256, 256), np.float32)
    m = np.arange(128)
    p[2 * m, m] = 1.0
    p[2 * m + 1, 128 + m] = 1.0
    return jnp.asarray(p, BF16)


def _moe_experts(layer, fn_packed, gather_key, out_row8, block_e, n_used, next_expert, w1, b1, w2, b2):
    b1g = b1[layer][:, 0::2].reshape(N_EXP, 1, D_FF)
    b1l = b1[layer][:, 1::2].reshape(N_EXP, 1, D_FF)
    ex = lambda i, be, *_: (be[i], 0, 0)
    grid_spec = pltpu.PrefetchScalarGridSpec(
        num_scalar_prefetch=5,
        grid=(N_BLOCKS,),
        in_specs=[
            pl.BlockSpec(memory_space=pl.ANY),
            pl.BlockSpec(memory_space=pl.ANY),
            pl.BlockSpec((None, 1, D_FF), ex),
            pl.BlockSpec((None, 1, D_FF), ex),
            pl.BlockSpec(memory_space=pl.ANY),
            pl.BlockSpec((None, 1, D), ex),
            pl.BlockSpec((256, 256), lambda i, *_: (0, 0)),
        ],
        out_specs=pl.BlockSpec(memory_space=pl.ANY),
        scratch_shapes=[
            pltpu.VMEM((N * PACK_ROWS, 128), jnp.uint32),
            pltpu.VMEM((2, MOE_BLOCK * PACK_ROWS, 128), jnp.uint32),
            pltpu.VMEM((D, 2 * D_FF), F32),
            pltpu.VMEM((D_FF, D), F32),
            pltpu.VMEM((D, 2 * D_FF), BF16),
            pltpu.VMEM((D_FF, D), BF16),
            pltpu.SemaphoreType.DMA((3,)),
            pltpu.VMEM((2, MOE_BLOCK * ROW_TILE, 128), F32),
            pltpu.SemaphoreType.DMA((2,)),
        ],
    )
    return pl.pallas_call(
        functools.partial(_moe_kernel, layer=layer),
        out_shape=jax.ShapeDtypeStruct((Y_ROWS * ROW_TILE, 128), F32),
        grid_spec=grid_spec,
        compiler_params=_cp(("arbitrary",), MOE_VMEM_LIMIT),
        name="moe_experts",
    )(block_e, n_used, gather_key, out_row8, next_expert, fn_packed, w1, b1g, b1l, w2, b2[layer].reshape(N_EXP, 1, D),
      _deinterleave_matrix())


def _combine_kernel(y0_ref, y1_ref, y2_ref, y3_ref, g_ref, h_ref, m_ref, fg_ref, *outs, final):
    i = pl.program_id(0)
    g = g_ref[...]
    cols = []
    for s in range(ROW_TILE):
        y = y0_ref[pl.ds(s, TILE, stride=ROW_TILE), :] * g[:, 0:1]
        for k, yk_ref in ((1, y1_ref), (2, y2_ref), (3, y3_ref)):
            y = y + yk_ref[pl.ds(s, TILE, stride=ROW_TILE), :] * g[:, k:k + 1]
        cols.append(y)
    h_new = h_ref[...] + m_ref[5:6, :] * jnp.concatenate(cols, axis=1)
    if final:
        yp_ref, yl_ref = outs
        ms = jnp.mean(h_new * h_new, axis=-1, keepdims=True)
        y = h_new * lax.rsqrt(ms + EPS) * fg_ref[...]

        @pl.when(i < P_TILES)
        def _():
            yp_ref[...] = y

        @pl.when(i >= P_TILES)
        def _():
            yl_ref[...] = y
    else:
        outs[0][...] = h_new


def _moe_combine(y_rows, gates_nt, h, mod, final_g, final):
    if final:
        out_shape = (jax.ShapeDtypeStruct((N_P, D), F32), jax.ShapeDtypeStruct((N_S, D), F32))
        out_specs = [
            pl.BlockSpec((TILE, D), lambda i: (jnp.minimum(i, P_TILES - 1), 0)),
            pl.BlockSpec((TILE, D), lambda i: (jnp.maximum(i - P_TILES, 0), 0)),
        ]
    else:
        out_shape = jax.ShapeDtypeStruct((N, D), F32)
        out_specs = pl.BlockSpec((TILE, D), lambda i: (i, 0))
    return pl.pallas_call(
        functools.partial(_combine_kernel, final=final),
        out_shape=out_shape,
        grid=(N_TILES,),
        in_specs=[pl.BlockSpec((TILE * ROW_TILE, 128), lambda i, k=k: (k * N_TILES + i, 0)) for k in range(TOP_K)] + [
            pl.BlockSpec((TILE, TOP_K), lambda i: (i, 0)),
            pl.BlockSpec((TILE, D), lambda i: (i, 0)),
            pl.BlockSpec((None, 6, D), lambda i: (_tile_mod_row(i), 0, 0)),
            pl.BlockSpec((1, D), lambda i: (0, 0)),
        ],
        out_specs=out_specs,
        compiler_params=_cp(("arbitrary",), VMEM_LIMIT),
        name="moe_combine",
    )(y_rows, y_rows, y_rows, y_rows, gates_nt, h, mod, final_g.reshape(1, D))


def _moe(layer, fn_packed, top_e, gates, h, mod, w1, b1, w2, b2, final_g, final):
    gather_key, out_row8, block_e, n_used, next_expert = _routing_tables(top_e)
    y_rows = _moe_experts(layer, fn_packed, gather_key, out_row8, block_e, n_used, next_expert, w1, b1, w2, b2)
    return _moe_combine(y_rows, gates.T, h, mod, final_g, final)


def _gate_weights(w_r, w_i):
    per = LRU_BLK // LRU_HD
    eye = jnp.eye(per, dtype=F32)

    def blockdiag(w):
        w = w.reshape(2, LRU_HEADS // per, per, LRU_HD, LRU_HD)
        full = jnp.einsum("dgaij,ab->dgaibj", w, eye)
        return full.reshape(2, LRU_HEADS // per, LRU_BLK, LRU_BLK)

    return jnp.concatenate([blockdiag(w_r), blockdiag(w_i)], axis=-1).astype(BF16)


def kernel(x_prompt, x_sample, state_rglru, c, c_ctx, norm_mix_g, norm_ffn_g, w_mod, b_mod, w_in0, lru_conv_w, lru_conv_b, lru_w_r, lru_b_r, lru_w_i, lru_b_i, lru_lambda, w_out0, w_in1, sgu_ln_g, sgu_ln_b, sgu_w_s, sgu_b_s, conv_dw_w, conv_dw_b, conv_ln_g, conv_ln_b, w_out1, w_router, b_router, w1, b1, w2, b2, final_norm_g):
    h = (x_prompt.reshape(N_P, D), x_sample.reshape(N_S, D))
    cond8 = jnp.concatenate([c_ctx[None, :], c, jnp.zeros((N_MOD - 1 - N_SAMPLE_SEQ, D), F32)], axis=0)
    mod = _adaln(cond8, w_mod, b_mod)

    proj0 = _inproj(h, norm_mix_g[0], mod[0], w_in0[0].astype(BF16))
    st = state_rglru[:, 0].astype(F32)
    h0 = jnp.zeros((2, N_MOD, LRU_W), F32).at[:, 1:1 + N_SAMPLE_SEQ].set(jnp.swapaxes(st, 0, 1))
    hs, ctx_state = _lru(proj0, lru_conv_w[0], lru_conv_b[0], _gate_weights(lru_w_r[0], lru_w_i[0]),
              lru_b_r[0], lru_b_i[0], lru_lambda[0], h0)
    h, fn, top_e, gates = _post0(hs, proj0, _fourier_prompt(proj0), _fourier_sample(proj0), w_out0[0].astype(BF16), h, mod[0], norm_ffn_g[0],
                                 w_router[0].T, b_router[0])
    h = _moe(0, fn, top_e, gates, h, mod[0], w1, b1, w2, b2, final_norm_g, False)

    proj1 = _inproj(h, norm_mix_g[1], mod[1], w_in1[0].astype(BF16))
    bs_full = jnp.repeat(sgu_b_s[0].T, CHUNK, axis=1)
    h, fn, top_e, gates = _post1(proj1, sgu_ln_g[0], sgu_ln_b[0], sgu_w_s[0].astype(BF16), bs_full,
                                 conv_dw_w[0], conv_dw_b[0], conv_ln_g[0], conv_ln_b[0],
                                 w_out1[0].astype(BF16), h, mod[1], norm_ffn_g[1], w_router[1].T, b_router[1])
    y_p, y_l = _moe(1, fn, top_e, gates, h, mod[1], w1, b1, w2, b2, final_norm_g, True)

    y_prompt = y_p.reshape(N_PROMPT_SEQ, T_PROMPT, D)
    y_sample = y_l.reshape(N_SAMPLE_SEQ, T_SAMPLE, D)
    new_state = jnp.transpose(ctx_state, (1, 2, 0, 3))
    return (y_prompt, y_sample, new_state.astype(x_prompt.dtype))
```

```python
import functools

import numpy as np
import jax
import jax.numpy as jnp
from jax import lax
from jax.experimental import pallas as pl
from jax.experimental.pallas import tpu as pltpu

F32 = jnp.float32
BF16 = jnp.bfloat16

D = 1024
N_PROMPT_SEQ = 32
T_PROMPT = 256
N_SAMPLE_SEQ = 2
T_SAMPLE = 2048
N_P = N_PROMPT_SEQ * T_PROMPT
N_S = N_SAMPLE_SEQ * T_SAMPLE
N = N_P + N_S
EPS = 1e-6

TILE = 256
N_TILES = N // TILE
P_TILES = N_P // TILE
S_TILES = T_SAMPLE // TILE
TM_PROJ = 1024
N_MOD = 8

LRU_W = 768
LRU_HEADS = 12
LRU_HD = 64
LRU_K = 4
LRU_LEFT = 2
LRU_C = 8.0
LRU_BLK = 256
FN_W = 256
FN_G = 4
FN_GD = 64
IN0 = 2 * LRU_W + FN_W

SGU_W = 512
SGU_G = 4
CHUNK = 128
CONV_W = 512
CONV_K = 31
CONV_PAD = 15
CONV_HALO = 16
CONV_ROWS = 32
IN1 = 2 * SGU_W + 2 * CONV_W

N_EXP = 32
TOP_K = 4
D_FF = 1024
ALPHA = 1.702
LIMIT = 7.0
MOE_BLOCK = 256
N_ASSIGN = N * TOP_K
N_BLOCKS = N_ASSIGN // MOE_BLOCK + N_EXP
N_SLOTS = N_BLOCKS * MOE_BLOCK
ROW_TILE = D // 128
PACK_ROWS = D // 2 // 128
PACK_PAIR = 2 * PACK_ROWS

VMEM_LIMIT = 56 * 1024 * 1024
MOE_VMEM_LIMIT = 60 * 1024 * 1024


def _cp(sem, vmem=None):
    return pltpu.CompilerParams(dimension_semantics=sem, vmem_limit_bytes=vmem)


def _dot(a, b):
    return jnp.dot(a, b, preferred_element_type=F32)


def _split(x):
    hi = x.astype(BF16)
    lo = (x - hi.astype(F32)).astype(BF16)
    return hi, lo


def _dot3(a, b):
    ah, al = _split(a)
    bh, bl = _split(b)
    return _dot(ah, bh) + _dot(al, bh) + _dot(ah, bl)


def _dot3_nt(a, b):
    dn = (((1,), (1,)), ((), ()))
    d = lambda x, y: lax.dot_general(x, y, dn, preferred_element_type=F32)
    ah, al = _split(a)
    bh, bl = _split(b)
    return d(ah, bh) + d(al, bh) + d(ah, bl)


def _sigmoid(x):
    return 0.5 * jnp.tanh(0.5 * x) + 0.5


def _gelu(x):
    return 0.5 * x * (1.0 + jnp.tanh(0.7978845608028654 * (x + 0.044715 * (x * x * x))))


def _rms_mod(x, g, shift, scale):
    ms = jnp.mean(x * x, axis=-1, keepdims=True)
    y = x * lax.rsqrt(ms + EPS) * g
    return y * (1.0 + scale) + shift


def _layernorm(x, g, b):
    xc = x - jnp.mean(x, axis=-1, keepdims=True)
    var = jnp.mean(xc * xc, axis=-1, keepdims=True)
    return xc * lax.rsqrt(var + EPS) * g + b


def _tile_mod_row(r):
    return jnp.where(r < P_TILES, 0, 1 + (r - P_TILES) // S_TILES)


def _tile_is_seq_start(r):
    return (r < P_TILES) | ((r - P_TILES) % S_TILES == 0)


def _tile_is_seq_end(r):
    return (r < P_TILES) | ((r - P_TILES) % S_TILES == S_TILES - 1)


MOD_TN = 512


def _adaln_kernel(cond_ref, w_ref, b_ref, o_ref):
    cond = cond_ref[...]
    s = cond * _sigmoid(cond)
    o_ref[...] = _dot3(s, w_ref[...]) + b_ref[...]


def _adaln(cond8, w_mod, b_mod):
    depth = w_mod.shape[0]
    out = pl.pallas_call(
        _adaln_kernel,
        out_shape=jax.ShapeDtypeStruct((depth, N_MOD, 6 * D), F32),
        grid=(depth, 6 * D // MOD_TN),
        in_specs=[
            pl.BlockSpec((N_MOD, D), lambda l, j: (0, 0)),
            pl.BlockSpec((None, D, MOD_TN), lambda l, j: (l, 0, j)),
            pl.BlockSpec((None, 1, MOD_TN), lambda l, j: (l, 0, j)),
        ],
        out_specs=pl.BlockSpec((None, N_MOD, MOD_TN), lambda l, j: (l, 0, j)),
        compiler_params=_cp(("arbitrary", "arbitrary")),
        name="adaln",
    )(cond8, w_mod, b_mod.reshape(depth, 1, 6 * D))
    return out.reshape(depth, N_MOD, 6, D)


PROJ_P_STEPS = N_P // TM_PROJ
PROJ_S_STEPS = T_SAMPLE // TM_PROJ


def _stream_specs(rows, p_steps, h):
    if isinstance(h, tuple):
        hp, hl = h
        first_latent = 0
    else:
        hp = hl = h
        first_latent = p_steps
    specs = [
        pl.BlockSpec((rows, D), lambda i, *_: (jnp.minimum(i, p_steps - 1), 0)),
        pl.BlockSpec((rows, D), lambda i, *_: (jnp.maximum(i - p_steps, 0) + first_latent, 0)),
    ]
    return specs, (hp, hl)


def _inproj_kernel(xp_ref, xl_ref, g_ref, m_ref, w_ref, o_ref):
    x = jnp.where(pl.program_id(0) < PROJ_P_STEPS, xp_ref[...], xl_ref[...])
    hn = _rms_mod(x, g_ref[...], m_ref[0:1, :], m_ref[1:2, :])
    o_ref[...] = _dot(hn.astype(BF16), w_ref[...])


def _inproj(h, g, mod, w_bf16):
    n_out = w_bf16.shape[1]

    def mod_row(i):
        return jnp.where(i < PROJ_P_STEPS, 0, 1 + (i - PROJ_P_STEPS) // PROJ_S_STEPS)

    h_specs, h_args = _stream_specs(TM_PROJ, PROJ_P_STEPS, h)
    return pl.pallas_call(
        _inproj_kernel,
        out_shape=jax.ShapeDtypeStruct((N, n_out), F32),
        grid=(N // TM_PROJ,),
        in_specs=h_specs + [
            pl.BlockSpec((1, D), lambda i: (0, 0)),
            pl.BlockSpec((None, 6, D), lambda i: (mod_row(i), 0, 0)),
            pl.BlockSpec((D, n_out), lambda i: (0, 0)),
        ],
        out_specs=pl.BlockSpec((TM_PROJ, n_out), lambda i: (i, 0)),
        compiler_params=_cp(("arbitrary",), VMEM_LIMIT),
        name="inproj",
    )(*h_args, g.reshape(1, D), mod, w_bf16)


LRU_HALO = 8
SCAN_ROWS = 8


def _lru_tile(d, s):
    return jnp.where(d == 0, s, N_TILES - 1 - s)


def _lru_kernel(x_ref, prev_ref, next_ref, cw_ref, cb_ref, wg_ref, br_ref, bi_ref, lam_ref, h0_ref,
                o_ref, state_ref, ext_ref, a_ref, carry_ref):
    d = pl.program_id(0)
    r = _lru_tile(d, pl.program_id(1))
    start = _tile_is_seq_start(r)
    end = _tile_is_seq_end(r)

    ext_ref[0:LRU_HALO, :] = jnp.where(start, 0.0, prev_ref[...])
    ext_ref[LRU_HALO:LRU_HALO + TILE, :] = x_ref[...]
    ext_ref[LRU_HALO + TILE:, :] = jnp.where(end, 0.0, next_ref[...])
    xc = cb_ref[...] + jnp.zeros((TILE, LRU_W), F32)
    for k in range(LRU_K):
        off = LRU_HALO - LRU_LEFT + k
        xc = xc + ext_ref[off:off + TILE, :] * cw_ref[k:k + 1, :]

    xcb = xc.astype(BF16)
    pre_r, pre_i = [], []
    for blk in range(LRU_W // LRU_BLK):
        g = _dot(xcb[:, blk * LRU_BLK:(blk + 1) * LRU_BLK], wg_ref[blk])
        pre_r.append(g[:, :LRU_BLK])
        pre_i.append(g[:, LRU_BLK:])
    gate_r = _sigmoid(jnp.concatenate(pre_r, axis=1) + br_ref[...])
    gate_i = _sigmoid(jnp.concatenate(pre_i, axis=1) + bi_ref[...])
    neg_lam = -lam_ref[...]
    softplus = jnp.maximum(neg_lam, 0.0) + jnp.log1p(jnp.exp(-jnp.abs(neg_lam)))
    log_a = (-LRU_C) * gate_r * softplus
    a = jnp.exp(log_a)
    a_ref[...] = a
    o_ref[...] = jnp.sqrt(-jnp.tanh(log_a) * (a * a + 1.0)) * (gate_i * xc)

    fresh = jnp.where(d == 0, start, end)
    h_init = jnp.where(fresh, h0_ref[pl.ds(_tile_mod_row(r), 1), :], carry_ref[...])

    row = lax.broadcasted_iota(jnp.int32, (SCAN_ROWS, LRU_W), 0)

    def scan_group(g, h, reverse):
        rows = pl.ds(pl.multiple_of(g * SCAN_ROWS, SCAN_ROWS), SCAN_ROWS)
        a = a_ref[rows, :]
        b = o_ref[rows, :]
        for s in (1, 2, 4):
            shift = SCAN_ROWS - s if reverse else s
            inside = (row < SCAN_ROWS - s) if reverse else (row >= s)
            b = jnp.where(inside, a * pltpu.roll(b, shift, 0) + b, b)
            a = jnp.where(inside, a * pltpu.roll(a, shift, 0), a)
        hs = a * h + b
        o_ref[rows, :] = hs
        return hs[0:1, :] if reverse else hs[SCAN_ROWS - 1:SCAN_ROWS, :]

    n_groups = TILE // SCAN_ROWS

    @pl.when(d == 0)
    def _():
        carry_ref[...] = lax.fori_loop(0, n_groups, lambda g, h: scan_group(g, h, False), h_init, unroll=4)

    @pl.when(d == 1)
    def _():
        carry_ref[...] = lax.fori_loop(0, n_groups, lambda g, h: scan_group(n_groups - 1 - g, h, True),
                                       h_init, unroll=4)

    h_last = carry_ref[...]

    @pl.when(r < P_TILES)
    def _():
        state_ref[...] = h_last


def _lru(proj0, conv_w, conv_b, wg, b_r, b_i, lam, h0):
    n_halo_blocks = N // LRU_HALO
    per_tile = TILE // LRU_HALO
    tile = lambda d, s: _lru_tile(d, s)
    return pl.pallas_call(
        _lru_kernel,
        out_shape=(jax.ShapeDtypeStruct((2, N, LRU_W), F32),
                   jax.ShapeDtypeStruct((2, N_PROMPT_SEQ, 1, LRU_W), F32)),
        grid=(2, N_TILES),
        in_specs=[
            pl.BlockSpec((TILE, LRU_W), lambda d, s: (tile(d, s), 0)),
            pl.BlockSpec((LRU_HALO, LRU_W), lambda d, s: (jnp.maximum(tile(d, s) * per_tile - 1, 0), 0)),
            pl.BlockSpec((LRU_HALO, LRU_W),
                         lambda d, s: (jnp.minimum((tile(d, s) + 1) * per_tile, n_halo_blocks - 1), 0)),
            pl.BlockSpec((LRU_K, LRU_W), lambda d, s: (0, 0)),
            pl.BlockSpec((1, LRU_W), lambda d, s: (0, 0)),
            pl.BlockSpec((None, LRU_W // LRU_BLK, LRU_BLK, 2 * LRU_BLK), lambda d, s: (d, 0, 0, 0)),
            pl.BlockSpec((None, 1, LRU_W), lambda d, s: (d, 0, 0)),
            pl.BlockSpec((None, 1, LRU_W), lambda d, s: (d, 0, 0)),
            pl.BlockSpec((None, 1, LRU_W), lambda d, s: (d, 0, 0)),
            pl.BlockSpec((None, N_MOD, LRU_W), lambda d, s: (d, 0, 0)),
        ],
        out_specs=[
            pl.BlockSpec((None, TILE, LRU_W), lambda d, s: (d, tile(d, s), 0)),
            pl.BlockSpec((None, None, 1, LRU_W), lambda d, s: (d, jnp.minimum(tile(d, s), P_TILES - 1), 0, 0)),
        ],
        scratch_shapes=[
            pltpu.VMEM((TILE + 2 * LRU_HALO, LRU_W), F32),
            pltpu.VMEM((TILE, LRU_W), F32),
            pltpu.VMEM((1, LRU_W), F32),
        ],
        compiler_params=_cp(("arbitrary", "arbitrary"), VMEM_LIMIT),
        name="rglru_scan",
    )(proj0, proj0, proj0, conv_w, conv_b.reshape(1, LRU_W), wg,
      b_r.reshape(2, 1, LRU_W), b_i.reshape(2, 1, LRU_W), lam.reshape(2, 1, LRU_W), h0)


def _dft_tables(n, scale):
    k = np.arange(n, dtype=np.int64)
    ang = 2.0 * np.pi * ((k[:, None] * k[None, :]) % n).astype(np.float64) / n
    return np.cos(ang) * scale, np.sin(ang) * scale


def _channel_tables():
    c, s = _dft_tables(FN_GD, FN_GD ** -0.5)
    eye = np.eye(FN_G)
    return (jnp.asarray(np.kron(eye, c), BF16), jnp.asarray(np.kron(eye, s), BF16))


def _time_tables(t_len):
    c, s = _dft_tables(t_len, t_len ** -0.5)
    return jnp.asarray(c, BF16), jnp.asarray(s, BF16)


def _fourier_prompt_kernel(z_ref, cc_ref, sc_ref, ct_ref, st_ref, o_ref):
    z = z_ref[...].astype(BF16)
    zc = _dot(z, cc_ref[...]).astype(BF16)
    zs = _dot(z, sc_ref[...]).astype(BF16)
    o_ref[...] = _dot(ct_ref[...], zc) - _dot(st_ref[...], zs)


def _fourier_prompt(proj0):
    cc, sc = _channel_tables()
    ct, st = _time_tables(T_PROMPT)
    const = lambda b: (0, 0)
    return pl.pallas_call(
        _fourier_prompt_kernel,
        out_shape=jax.ShapeDtypeStruct((N_P, FN_W), F32),
        grid=(N_PROMPT_SEQ,),
        in_specs=[
            pl.BlockSpec((T_PROMPT, FN_W), lambda b: (b, 2 * LRU_W // FN_W)),
            pl.BlockSpec((FN_W, FN_W), const),
            pl.BlockSpec((FN_W, FN_W), const),
            pl.BlockSpec((T_PROMPT, T_PROMPT), const),
            pl.BlockSpec((T_PROMPT, T_PROMPT), const),
        ],
        out_specs=pl.BlockSpec((T_PROMPT, FN_W), lambda b: (b, 0)),
        compiler_params=_cp(("arbitrary",)),
        name="fourier_prompt",
    )(proj0, cc, sc, ct, st)


def _fourier_sample_kernel(z_ref, cc_ref, sc_ref, ct_ref, st_ref, o_ref, zc_ref, zs_ref):
    @pl.when(pl.program_id(1) == 0)
    def _():
        z = z_ref[...].astype(BF16)
        zc_ref[...] = _dot(z, cc_ref[...]).astype(BF16)
        zs_ref[...] = _dot(z, sc_ref[...]).astype(BF16)

    o_ref[...] = _dot(ct_ref[...], zc_ref[...]) - _dot(st_ref[...], zs_ref[...])


def _fourier_sample(proj0):
    cc, sc = _channel_tables()
    ct, st = _time_tables(T_SAMPLE)
    const = lambda b, i: (0, 0)
    first_seq_block = N_P // T_SAMPLE
    return pl.pallas_call(
        _fourier_sample_kernel,
        out_shape=jax.ShapeDtypeStruct((N_S, FN_W), F32),
        grid=(N_SAMPLE_SEQ, S_TILES),
        in_specs=[
            pl.BlockSpec((T_SAMPLE, FN_W), lambda b, i: (first_seq_block + b, 2 * LRU_W // FN_W)),
            pl.BlockSpec((FN_W, FN_W), const),
            pl.BlockSpec((FN_W, FN_W), const),
            pl.BlockSpec((TILE, T_SAMPLE), lambda b, i: (i, 0)),
            pl.BlockSpec((TILE, T_SAMPLE), lambda b, i: (i, 0)),
        ],
        out_specs=pl.BlockSpec((TILE, FN_W), lambda b, i: (b * S_TILES + i, 0)),
        scratch_shapes=[pltpu.VMEM((T_SAMPLE, FN_W), BF16), pltpu.VMEM((T_SAMPLE, FN_W), BF16)],
        compiler_params=_cp(("arbitrary", "arbitrary"), VMEM_LIMIT),
        name="fourier_sample",
    )(proj0, cc, sc, ct, st)


def _epilogue(mix, hp_ref, hl_ref, m_ref, gf_ref, wr_ref, brt_ref, tri_ref,
              hout_ref, fn_ref, tope_ref, gate_ref, rank_ref, cnt_ref, run_ref):
    h = jnp.where(pl.program_id(0) < P_TILES, hp_ref[...], hl_ref[...])
    h_new = h + m_ref[2:3, :] * mix
    hout_ref[...] = h_new
    fn = _rms_mod(h_new, gf_ref[...], m_ref[3:4, :], m_ref[4:5, :])
    hi = lax.bitcast_convert_type(fn[:, :D // 2].astype(BF16).astype(F32), jnp.uint32)
    lo = lax.bitcast_convert_type(fn[:, D // 2:].astype(BF16).astype(F32), jnp.uint32)
    packed = hi | (lo >> 16)
    for q in range(PACK_ROWS):
        fn_ref[pl.ds(q, TILE, stride=PACK_ROWS), :] = packed[:, q * 128:(q + 1) * 128]
    logits = _dot3_nt(wr_ref[...], fn) + brt_ref[...]
    iota = lax.broadcasted_iota(jnp.int32, logits.shape, 0)
    vals, idxs = [], []
    for _ in range(TOP_K):
        m = jnp.max(logits, axis=0, keepdims=True)
        idx = jnp.min(jnp.where(logits == m, iota, N_EXP), axis=0, keepdims=True)
        vals.append(m)
        idxs.append(idx)
        logits = jnp.where(iota == idx, -jnp.inf, logits)
    exps = [jnp.exp(v - vals[0]) for v in vals]
    denom = exps[0] + exps[1] + exps[2] + exps[3]
    for k in range(TOP_K):
        tope_ref[k:k + 1, :] = idxs[k]
        gate_ref[k:k + 1, :] = exps[k] / denom

    @pl.when(pl.program_id(0) == 0)
    def _():
        run_ref[...] = jnp.zeros_like(run_ref)

    run = run_ref[...]
    for k in range(TOP_K):
        onehot = jnp.where(iota == idxs[k], 1.0, 0.0)
        incl = _dot(onehot.astype(BF16), tri_ref[...])
        rank = jnp.sum(onehot * (incl - 1.0 + run), axis=0, keepdims=True)
        rank_ref[k:k + 1, :] = rank.astype(jnp.int32)
        run = run + incl[:, TILE - 1:TILE]
    run_ref[...] = run
    cnt_ref[...] = run.astype(jnp.int32)


_EPI_OUT_SHAPES = (
    jax.ShapeDtypeStruct((N, D), F32),
    jax.ShapeDtypeStruct((N * PACK_ROWS, 128), jnp.uint32),
    jax.ShapeDtypeStruct((TOP_K, N), jnp.int32),
    jax.ShapeDtypeStruct((TOP_K, N), F32),
    jax.ShapeDtypeStruct((TOP_K, N), jnp.int32),
    jax.ShapeDtypeStruct((N_EXP, 1), jnp.int32),
)
_EPI_SCRATCH = [pltpu.VMEM((N_EXP, 1), F32)]


def _epi_operands(h, mod, g_ffn, w_router_t, b_router):
    tri = jnp.asarray(np.triu(np.ones((TILE, TILE), np.float32)), BF16)
    return (*_stream_specs(TILE, P_TILES, h)[1], mod, g_ffn.reshape(1, D), w_router_t,
            b_router.reshape(N_EXP, 1), tri)


def _epi_in_specs(h):
    h_specs, _ = _stream_specs(TILE, P_TILES, h)
    return h_specs + [
        pl.BlockSpec((None, 6, D), lambda r: (_tile_mod_row(r), 0, 0)),
        pl.BlockSpec((1, D), lambda r: (0, 0)),
        pl.BlockSpec((N_EXP, D), lambda r: (0, 0)),
        pl.BlockSpec((N_EXP, 1), lambda r: (0, 0)),
        pl.BlockSpec((TILE, TILE), lambda r: (0, 0)),
    ]


def _epi_out_specs():
    return [
        pl.BlockSpec((TILE, D), lambda r: (r, 0)),
        pl.BlockSpec((TILE * PACK_ROWS, 128), lambda r: (r, 0)),
        pl.BlockSpec((TOP_K, TILE), lambda r: (0, r)),
        pl.BlockSpec((TOP_K, TILE), lambda r: (0, r)),
        pl.BlockSpec((TOP_K, TILE), lambda r: (0, r)),
        pl.BlockSpec((N_EXP, 1), lambda r: (0, 0)),
    ]


def _post0_kernel(hs_ref, xg_ref, yfp_ref, yfl_ref, wo_ref, *epilogue_refs):
    y_rec = (hs_ref[0] + hs_ref[1]) * _gelu(xg_ref[...])
    y_four = jnp.where(pl.program_id(0) < P_TILES, yfp_ref[...], yfl_ref[...])
    mix = (_dot(y_rec.astype(BF16), wo_ref[0:LRU_W, :])
           + _dot(y_four.astype(BF16), wo_ref[LRU_W:, :]))
    _epilogue(mix, *epilogue_refs)


def _post0(hs, proj0, yf_prompt, yf_latent, w_out_bf16, h, mod, g_ffn, w_router_t, b_router):
    return pl.pallas_call(
        _post0_kernel,
        out_shape=_EPI_OUT_SHAPES,
        grid=(N_TILES,),
        in_specs=[
            pl.BlockSpec((2, TILE, LRU_W), lambda r: (0, r, 0)),
            pl.BlockSpec((TILE, LRU_W), lambda r: (r, 1)),
            pl.BlockSpec((TILE, FN_W), lambda r: (jnp.minimum(r, P_TILES - 1), 0)),
            pl.BlockSpec((TILE, FN_W), lambda r: (jnp.maximum(r - P_TILES, 0), 0)),
            pl.BlockSpec((D, D), lambda r: (0, 0)),
        ] + _epi_in_specs(h),
        out_specs=_epi_out_specs(),
        scratch_shapes=_EPI_SCRATCH,
        compiler_params=_cp(("arbitrary",), VMEM_LIMIT),
        name="post_rglru_fourier",
    )(hs, proj0, yf_prompt, yf_latent, w_out_bf16, *_epi_operands(h, mod, g_ffn, w_router_t, b_router))


def _glu(x):
    return x[:, :CONV_W] * _sigmoid(x[:, CONV_W:])


def _post1_kernel(p_ref, prev_ref, next_ref, lng_ref, lnb_ref, ws_ref, bs_ref, dww_ref, dwb_ref,
                  clg_ref, clb_ref, wo_ref, *rest):
    epilogue_refs, (ext_ref, shift_ref, dconv_ref) = rest[:-3], rest[-3:]
    r = pl.program_id(0)
    z = _gelu(p_ref[:, 0:2 * SGU_W])
    u = z[:, :SGU_W]
    v = _layernorm(z[:, SGU_W:], lng_ref[...], lnb_ref[...]).astype(BF16)
    rows = []
    for n in range(TILE // CHUNK):
        cols = []
        for g in range(SGU_G):
            vb = v[n * CHUNK:(n + 1) * CHUNK, g * CHUNK:(g + 1) * CHUNK]
            cols.append(_dot(ws_ref[g], vb))
        rows.append(jnp.concatenate(cols, axis=1) + bs_ref[...])
    y_sgu = u * jnp.concatenate(rows, axis=0)

    ext_ref[0:CONV_HALO, :] = jnp.where(_tile_is_seq_start(r), 0.0, _glu(prev_ref[...]))
    ext_ref[CONV_HALO:CONV_HALO + TILE, :] = _glu(p_ref[:, 2 * SGU_W:])
    ext_ref[CONV_HALO + TILE:, :] = jnp.where(_tile_is_seq_end(r), 0.0, _glu(next_ref[...]))
    n_shift_rows = TILE + 2 * CONV_HALO - 8
    for phase in range(8):
        shift_ref[phase] = ext_ref[phase:phase + n_shift_rows, :]
    for c in range(TILE // CONV_ROWS):
        acc = dwb_ref[...] + jnp.zeros((CONV_ROWS // 8, 8, CONV_W), F32)
        for k in range(CONV_K):
            off = CONV_HALO - CONV_PAD + k
            start = off // 8 * 8 + c * CONV_ROWS
            rows = shift_ref[off % 8, start:start + CONV_ROWS, :].reshape(CONV_ROWS // 8, 8, CONV_W)
            acc = acc + rows * dww_ref[k]
        dconv_ref[c * CONV_ROWS:(c + 1) * CONV_ROWS, :] = acc.reshape(CONV_ROWS, CONV_W)
    ln = _layernorm(dconv_ref[...], clg_ref[...], clb_ref[...])
    y_conv = ln * _sigmoid(ln)

    mix = (_dot(y_sgu.astype(BF16), wo_ref[0:SGU_W, :])
           + _dot(y_conv.astype(BF16), wo_ref[SGU_W:, :]))
    _epilogue(mix, *epilogue_refs)


def _post1(proj1, ln_g, ln_b, ws_bf16, bs_full, dw_w, dw_b, cln_g, cln_b, w_out_bf16,
           h, mod, g_ffn, w_router_t, b_router):
    per_tile = TILE // CONV_HALO
    n_halo_blocks = N // CONV_HALO
    const2 = lambda r: (0, 0)
    row = lambda a: a.reshape(1, -1)
    return pl.pallas_call(
        _post1_kernel,
        out_shape=_EPI_OUT_SHAPES,
        grid=(N_TILES,),
        in_specs=[
            pl.BlockSpec((TILE, IN1), lambda r: (r, 0)),
            pl.BlockSpec((CONV_HALO, 2 * CONV_W), lambda r: (jnp.maximum(r * per_tile - 1, 0), 1)),
            pl.BlockSpec((CONV_HALO, 2 * CONV_W),
                         lambda r: (jnp.minimum((r + 1) * per_tile, n_halo_blocks - 1), 1)),
            pl.BlockSpec((1, SGU_W), const2),
            pl.BlockSpec((1, SGU_W), const2),
            pl.BlockSpec((SGU_G, CHUNK, CHUNK), lambda r: (0, 0, 0)),
            pl.BlockSpec((CHUNK, SGU_W), const2),
            pl.BlockSpec((CONV_K, 8, CONV_W), lambda r: (0, 0, 0)),
            pl.BlockSpec((1, CONV_W), const2),
            pl.BlockSpec((1, CONV_W), const2),
            pl.BlockSpec((1, CONV_W), const2),
            pl.BlockSpec((D, D), const2),
        ] + _epi_in_specs(h),
        out_specs=_epi_out_specs(),
        scratch_shapes=_EPI_SCRATCH + [
            pltpu.VMEM((TILE + 2 * CONV_HALO, CONV_W), F32),
            pltpu.VMEM((8, TILE + 2 * CONV_HALO - 8, CONV_W), F32),
            pltpu.VMEM((TILE, CONV_W), F32),
        ],
        compiler_params=_cp(("arbitrary",), VMEM_LIMIT),
        name="post_sgu_conformer",
    )(proj1, proj1, proj1, row(ln_g), row(ln_b), ws_bf16, bs_full,
      jnp.broadcast_to(dw_w[:, None, :], (CONV_K, 8, CONV_W)), row(dw_b), row(cln_g), row(cln_b),
      w_out_bf16, *_epi_operands(h, mod, g_ffn, w_router_t, b_router))


PAD_BASE = N_ASSIGN
PLACEHOLDER_BASE = PAD_BASE + 2 * MOE_BLOCK
Y_ROWS = PLACEHOLDER_BASE + 2 * MOE_BLOCK


def _inverse_kernel(pos_ref, pend_ref, out_ref):
    def fill_block(start):
        def fill(j, carry):
            s = start + j
            out_ref[s] = PAD_BASE + (s & (2 * MOE_BLOCK - 1))
            return carry
        lax.fori_loop(0, MOE_BLOCK, fill, 0, unroll=16)

    def per_expert(e, carry):
        fill_block(jnp.maximum(pend_ref[e] - MOE_BLOCK, 0))
        return carry

    def idle_block(b, carry):
        fill_block(b * MOE_BLOCK)
        return carry

    lax.fori_loop(0, N_EXP, per_expert, 0)
    lax.fori_loop(pend_ref[N_EXP - 1] // MOE_BLOCK, N_BLOCKS, idle_block, 0)

    def body(t, carry):
        for k in range(TOP_K):
            out_ref[pos_ref[k * N + t]] = t * TOP_K + k
        return carry

    lax.fori_loop(0, N, body, 0, unroll=8)


def _inverse(pos_flat, pad_end):
    smem = pl.BlockSpec(memory_space=pltpu.SMEM)
    return pl.pallas_call(
        _inverse_kernel,
        out_shape=jax.ShapeDtypeStruct((N_SLOTS,), jnp.int32),
        in_specs=[smem, smem],
        out_specs=smem,
        name="moe_inverse",
    )(pos_flat, pad_end)


def _routing_tables(top_e, rank, counts):
    counts = counts.reshape(N_EXP)
    padded = (counts + MOE_BLOCK - 1) // MOE_BLOCK * MOE_BLOCK
    pad_end = jnp.cumsum(padded).astype(jnp.int32)
    pad_start = pad_end - padded
    experts = jnp.arange(N_EXP, dtype=jnp.int32)[:, None, None]
    pos = rank + jnp.sum(jnp.where(top_e[None] == experts, pad_start[:, None, None], 0), axis=0)
    pos = pos.astype(jnp.int32).reshape(-1)
    n_used = pad_end[-1] // MOE_BLOCK
    blk = jnp.minimum(jnp.arange(N_BLOCKS, dtype=jnp.int32), n_used - 1)
    block_e = jnp.sum(pad_end[None, :] <= (blk * MOE_BLOCK)[:, None], axis=1).astype(jnp.int32)
    slot_a = _inverse(pos, pad_end)
    tok = jnp.minimum(slot_a >> 2, N - 1)
    gather_key = (tok >> 1) * (2 * PACK_PAIR) + (tok & 1)
    out_row = jnp.where(slot_a >= PAD_BASE, slot_a, (slot_a & (TOP_K - 1)) * N + (slot_a >> 2))
    placeholder_rows = PLACEHOLDER_BASE + jnp.arange(MOE_BLOCK, dtype=jnp.int32)
    out_row8 = jnp.concatenate([placeholder_rows, out_row]) * ROW_TILE
    ids = jnp.arange(N_EXP, dtype=jnp.int32)
    later_used = (ids[None, :] > ids[:, None]) & (counts[None, :] > 0)
    next_expert = jnp.min(jnp.where(later_used, ids[None, :], N_EXP), axis=1).astype(jnp.int32)
    return gather_key, out_row8, block_e, n_used.reshape(1), next_expert


def _moe_kernel(be_ref, nu_ref, gk_ref, row8_ref, nxt_ref, x_hbm, w1_hbm, b1g_ref, b1l_ref, w2_hbm, b2_ref, perm_ref,
                y_hbm, xs, xbuf, w1f, w2f, w1s, w2s, sem, ybuf, osem, *, layer):
    i = pl.program_id(0)
    n_used = nu_ref[0]
    slot = i % 2
    e = be_ref[i]
    e_prev = be_ref[jnp.maximum(i - 1, 0)]
    low_half = lax.broadcasted_iota(jnp.int32, (PACK_PAIR, 128), 0) < PACK_ROWS

    def weight_copies(expert):
        return (pltpu.make_async_copy(w1_hbm.at[layer, expert], w1f, sem.at[1]),
                pltpu.make_async_copy(w2_hbm.at[layer, expert], w2f, sem.at[2]))

    def token_tile(key):
        tile = xs[pl.ds(pl.multiple_of(key >> 1, PACK_PAIR), PACK_PAIR), :]
        return tile, pltpu.roll(tile, PACK_ROWS, 0), key & 1

    def gather_pair(blk, jj, dst_slot):
        t0, r0, half0 = token_tile(gk_ref[blk * MOE_BLOCK + 2 * jj])
        t1, r1, half1 = token_tile(gk_ref[blk * MOE_BLOCK + 2 * jj + 1])
        lower = jnp.where(half0 == 0, t0, r0)
        upper = jnp.where(half1 == 1, t1, r1)
        start = jj * PACK_PAIR if isinstance(jj, int) else pl.multiple_of(jj * PACK_PAIR, PACK_PAIR)
        xbuf[dst_slot, pl.ds(start, PACK_PAIR), :] = jnp.where(low_half, lower, upper)

    def out_copy(src_slot, j, row8):
        start = j * ROW_TILE if isinstance(j, int) else pl.multiple_of(j * ROW_TILE, ROW_TILE)
        dst = y_hbm.at[pl.ds(pl.multiple_of(row8, ROW_TILE), ROW_TILE), :]
        return pltpu.make_async_copy(ybuf.at[src_slot, pl.ds(start, ROW_TILE), :], dst, osem.at[src_slot])

    def out_buffer_wait(src_slot):
        pltpu.make_async_copy(ybuf.at[src_slot], y_hbm.at[pl.ds(0, MOE_BLOCK * ROW_TILE), :],
                              osem.at[src_slot]).wait()

    @pl.when(i == 0)
    def _():
        xs_copy = pltpu.make_async_copy(x_hbm, xs, sem.at[0])
        xs_copy.start()
        for c in weight_copies(e):
            c.start()
        ybuf[...] = jnp.zeros_like(ybuf)

        def zero_rows(first_row):
            def body(j, carry):
                out_copy(0, j, (first_row + j) * ROW_TILE).start()
                return carry
            lax.fori_loop(0, MOE_BLOCK, body, 0, unroll=8)

        zero_rows(PAD_BASE)
        zero_rows(PAD_BASE + MOE_BLOCK)
        out_buffer_wait(0)
        out_buffer_wait(0)
        zero_rows(PLACEHOLDER_BASE + MOE_BLOCK)
        xs_copy.wait()

        def body(jj, carry):
            gather_pair(0, jj, 0)
            return carry
        lax.fori_loop(0, MOE_BLOCK // 2, body, 0, unroll=4)

    @pl.when((i < n_used) & ((i == 0) | (e != e_prev)))
    def _():
        for c in weight_copies(e):
            c.wait()
        for c in range(2 * D_FF // 256):
            wc = w1f[:, c * 256:(c + 1) * 256].astype(BF16)
            w1s[:, c * 256:(c + 1) * 256] = _dot(wc, perm_ref[...]).astype(BF16)
        w2s[...] = w2f[...].astype(BF16)
        e_next = nxt_ref[e]

        @pl.when(e_next < N_EXP)
        def _():
            for c in weight_copies(e_next):
                c.start()

    def block(prefetch_next):
        for j in range(MOE_BLOCK):
            out_copy(1 - slot, j, row8_ref[i * MOE_BLOCK + j]).start()
        halves = [[], []]
        for q in range(PACK_ROWS):
            words = xbuf[slot, pl.ds(q, MOE_BLOCK, stride=PACK_ROWS), :]
            halves[0].append(lax.bitcast_convert_type(words & jnp.uint32(0xFFFF0000), F32).astype(BF16))
            halves[1].append(lax.bitcast_convert_type(words << 16, F32).astype(BF16))
        x = jnp.concatenate(halves[0] + halves[1], axis=1)
        if prefetch_next:
            for jj in range(MOE_BLOCK // 2):
                gather_pair(i + 1, jj, 1 - slot)
        hid = _dot(x, w1s[...])
        acts = []
        for c in range(D_FF // 128):
            h_glu = hid[:, c * 256:c * 256 + 128] + b1g_ref[:, c * 128:(c + 1) * 128]
            h_lin = hid[:, c * 256 + 128:(c + 1) * 256] + b1l_ref[:, c * 128:(c + 1) * 128]
            h_glu = jnp.minimum(h_glu, LIMIT)
            h_lin = jnp.clip(h_lin, -LIMIT, LIMIT)
            acts.append((h_glu * _sigmoid(ALPHA * h_glu) * (h_lin + 1.0)).astype(BF16))
        y = _dot(jnp.concatenate(acts, axis=1), w2s[...]) + b2_ref[...]
        out_buffer_wait(slot)
        for s in range(ROW_TILE):
            ybuf[slot, pl.ds(s, MOE_BLOCK, stride=ROW_TILE), :] = y[:, s * 128:(s + 1) * 128]
        if not prefetch_next:
            def send(j, carry):
                out_copy(slot, j, row8_ref[(i + 1) * MOE_BLOCK + j]).start()
                return carry
            lax.fori_loop(0, MOE_BLOCK, send, 0, unroll=8)
            out_buffer_wait(1 - slot)
            out_buffer_wait(slot)

    @pl.when(i + 1 < n_used)
    def _():
        block(True)

    @pl.when(i + 1 == n_used)
    def _():
        block(False)


def _deinterleave_matrix():
    p = np.zeros((256, 256), np.float32)
    m = np.arange(128)
    p[2 * m, m] = 1.0
    p[2 * m + 1, 128 + m] = 1.0
    return jnp.asarray(p, BF16)


def _moe_experts(layer, fn_packed, gather_key, out_row8, block_e, n_used, next_expert, w1, b1, w2, b2):
    b1g = b1[layer][:, 0::2].reshape(N_EXP, 1, D_FF)
    b1l = b1[layer][:, 1::2].reshape(N_EXP, 1, D_FF)
    ex = lambda i, be, *_: (be[i], 0, 0)
    grid_spec = pltpu.PrefetchScalarGridSpec(
        num_scalar_prefetch=5,
        grid=(N_BLOCKS,),
        in_specs=[
            pl.BlockSpec(memory_space=pl.ANY),
            pl.BlockSpec(memory_space=pl.ANY),
            pl.BlockSpec((None, 1, D_FF), ex),
            pl.BlockSpec((None, 1, D_FF), ex),
            pl.BlockSpec(memory_space=pl.ANY),
            pl.BlockSpec((None, 1, D), ex),
            pl.BlockSpec((256, 256), lambda i, *_: (0, 0)),
        ],
        out_specs=pl.BlockSpec(memory_space=pl.ANY),
        scratch_shapes=[
            pltpu.VMEM((N * PACK_ROWS, 128), jnp.uint32),
            pltpu.VMEM((2, MOE_BLOCK * PACK_ROWS, 128), jnp.uint32),
            pltpu.VMEM((D, 2 * D_FF), F32),
            pltpu.VMEM((D_FF, D), F32),
            pltpu.VMEM((D, 2 * D_FF), BF16),
            pltpu.VMEM((D_FF, D), BF16),
            pltpu.SemaphoreType.DMA((3,)),
            pltpu.VMEM((2, MOE_BLOCK * ROW_TILE, 128), F32),
            pltpu.SemaphoreType.DMA((2,)),
        ],
    )
    return pl.pallas_call(
        functools.partial(_moe_kernel, layer=layer),
        out_shape=jax.ShapeDtypeStruct((Y_ROWS * ROW_TILE, 128), F32),
        grid_spec=grid_spec,
        compiler_params=_cp(("arbitrary",), MOE_VMEM_LIMIT),
        name="moe_experts",
    )(block_e, n_used, gather_key, out_row8, next_expert, fn_packed, w1, b1g, b1l, w2, b2[layer].reshape(N_EXP, 1, D),
      _deinterleave_matrix())


def _combine_kernel(y0_ref, y1_ref, y2_ref, y3_ref, g_ref, h_ref, m_ref, fg_ref, *outs, final):
    i = pl.program_id(0)
    g = g_ref[...]
    cols = []
    for s in range(ROW_TILE):
        y = y0_ref[pl.ds(s, TILE, stride=ROW_TILE), :] * g[:, 0:1]
        for k, yk_ref in ((1, y1_ref), (2, y2_ref), (3, y3_ref)):
            y = y + yk_ref[pl.ds(s, TILE, stride=ROW_TILE), :] * g[:, k:k + 1]
        cols.append(y)
    h_new = h_ref[...] + m_ref[5:6, :] * jnp.concatenate(cols, axis=1)
    if final:
        yp_ref, yl_ref = outs
        ms = jnp.mean(h_new * h_new, axis=-1, keepdims=True)
        y = h_new * lax.rsqrt(ms + EPS) * fg_ref[...]

        @pl.when(i < P_TILES)
        def _():
            yp_ref[...] = y

        @pl.when(i >= P_TILES)
        def _():
            yl_ref[...] = y
    else:
        outs[0][...] = h_new


def _moe_combine(y_rows, gates_nt, h, mod, final_g, final):
    if final:
        out_shape = (jax.ShapeDtypeStruct((N_P, D), F32), jax.ShapeDtypeStruct((N_S, D), F32))
        out_specs = [
            pl.BlockSpec((TILE, D), lambda i: (jnp.minimum(i, P_TILES - 1), 0)),
            pl.BlockSpec((TILE, D), lambda i: (jnp.maximum(i - P_TILES, 0), 0)),
        ]
    else:
        out_shape = jax.ShapeDtypeStruct((N, D), F32)
        out_specs = pl.BlockSpec((TILE, D), lambda i: (i, 0))
    return pl.pallas_call(
        functools.partial(_combine_kernel, final=final),
        out_shape=out_shape,
        grid=(N_TILES,),
        in_specs=[pl.BlockSpec((TILE * ROW_TILE, 128), lambda i, k=k: (k * N_TILES + i, 0)) for k in range(TOP_K)] + [
            pl.BlockSpec((TILE, TOP_K), lambda i: (i, 0)),
            pl.BlockSpec((TILE, D), lambda i: (i, 0)),
            pl.BlockSpec((None, 6, D), lambda i: (_tile_mod_row(i), 0, 0)),
            pl.BlockSpec((1, D), lambda i: (0, 0)),
        ],
        out_specs=out_specs,
        compiler_params=_cp(("arbitrary",), VMEM_LIMIT),
        name="moe_combine",
    )(y_rows, y_rows, y_rows, y_rows, gates_nt, h, mod, final_g.reshape(1, D))


def _moe(layer, routed, h, mod, w1, b1, w2, b2, final_g, final):
    fn_packed, top_e, gates, rank, counts = routed
    gather_key, out_row8, block_e, n_used, next_expert = _routing_tables(top_e, rank, counts)
    y_rows = _moe_experts(layer, fn_packed, gather_key, out_row8, block_e, n_used, next_expert, w1, b1, w2, b2)
    return _moe_combine(y_rows, gates.T, h, mod, final_g, final)


def _gate_weights(w_r, w_i):
    per = LRU_BLK // LRU_HD
    eye = jnp.eye(per, dtype=F32)

    def blockdiag(w):
        w = w.reshape(2, LRU_HEADS // per, per, LRU_HD, LRU_HD)
        full = jnp.einsum("dgaij,ab->dgaibj", w, eye)
        return full.reshape(2, LRU_HEADS // per, LRU_BLK, LRU_BLK)

    return jnp.concatenate([blockdiag(w_r), blockdiag(w_i)], axis=-1).astype(BF16)


def kernel(x_prompt, x_sample, state_rglru, c, c_ctx, norm_mix_g, norm_ffn_g, w_mod, b_mod, w_in0, lru_conv_w, lru_conv_b, lru_w_r, lru_b_r, lru_w_i, lru_b_i, lru_lambda, w_out0, w_in1, sgu_ln_g, sgu_ln_b, sgu_w_s, sgu_b_s, conv_dw_w, conv_dw_b, conv_ln_g, conv_ln_b, w_out1, w_router, b_router, w1, b1, w2, b2, final_norm_g):
    h = (x_prompt.reshape(N_P, D), x_sample.reshape(N_S, D))
    cond8 = jnp.concatenate([c_ctx[None, :], c, jnp.zeros((N_MOD - 1 - N_SAMPLE_SEQ, D), F32)], axis=0)
    mod = _adaln(cond8, w_mod, b_mod)

    proj0 = _inproj(h, norm_mix_g[0], mod[0], w_in0[0].astype(BF16))
    st = state_rglru[:, 0].astype(F32)
    h0 = jnp.zeros((2, N_MOD, LRU_W), F32).at[:, 1:1 + N_SAMPLE_SEQ].set(jnp.swapaxes(st, 0, 1))
    hs, ctx_state = _lru(proj0, lru_conv_w[0], lru_conv_b[0], _gate_weights(lru_w_r[0], lru_w_i[0]),
              lru_b_r[0], lru_b_i[0], lru_lambda[0], h0)
    h, *routed = _post0(hs, proj0, _fourier_prompt(proj0), _fourier_sample(proj0), w_out0[0].astype(BF16),
                        h, mod[0], norm_ffn_g[0], w_router[0].T, b_router[0])
    h = _moe(0, routed, h, mod[0], w1, b1, w2, b2, final_norm_g, False)

    proj1 = _inproj(h, norm_mix_g[1], mod[1], w_in1[0].astype(BF16))
    bs_full = jnp.repeat(sgu_b_s[0].T, CHUNK, axis=1)
    h, *routed = _post1(proj1, sgu_ln_g[0], sgu_ln_b[0], sgu_w_s[0].astype(BF16), bs_full,
                        conv_dw_w[0], conv_dw_b[0], conv_ln_g[0], conv_ln_b[0],
                        w_out1[0].astype(BF16), h, mod[1], norm_ffn_g[1], w_router[1].T, b_router[1])
    y_p, y_l = _moe(1, routed, h, mod[1], w1, b1, w2, b2, final_norm_g, True)

    y_prompt = y_p.reshape(N_PROMPT_SEQ, T_PROMPT, D)
    y_sample = y_l.reshape(N_SAMPLE_SEQ, T_SAMPLE, D)
    new_state = jnp.transpose(ctx_state, (1, 2, 0, 3))
    return (y_prompt, y_sample, new_state.astype(x_prompt.dtype))
```

```python
import functools

import numpy as np
import jax
import jax.numpy as jnp
from jax import lax
from jax.experimental import pallas as pl
from jax.experimental.pallas import tpu as pltpu

F32 = jnp.float32
BF16 = jnp.bfloat16

D = 1024
N_PROMPT_SEQ = 32
T_PROMPT = 256
N_SAMPLE_SEQ = 2
T_SAMPLE = 2048
N_P = N_PROMPT_SEQ * T_PROMPT
N_S = N_SAMPLE_SEQ * T_SAMPLE
N = N_P + N_S
EPS = 1e-6

TILE = 256
N_TILES = N // TILE
P_TILES = N_P // TILE
S_TILES = T_SAMPLE // TILE
TM_PROJ = 1024
N_MOD = 8

LRU_W = 768
LRU_HEADS = 12
LRU_HD = 64
LRU_K = 4
LRU_LEFT = 2
LRU_C = 8.0
LRU_BLK = 256
FN_W = 256
FN_G = 4
FN_GD = 64
IN0 = 2 * LRU_W + FN_W

SGU_W = 512
SGU_G = 4
CHUNK = 128
CONV_W = 512
CONV_K = 31
CONV_PAD = 15
CONV_HALO = 16
CONV_ROWS = 32
ROW_PIECE = 32
IN1 = 2 * SGU_W + 2 * CONV_W

N_EXP = 32
TOP_K = 4
D_FF = 1024
ALPHA = 1.702
LIMIT = 7.0
MOE_BLOCK = 256
N_ASSIGN = N * TOP_K
N_BLOCKS = N_ASSIGN // MOE_BLOCK + N_EXP
N_SLOTS = N_BLOCKS * MOE_BLOCK
ROW_TILE = D // 128
PACK_ROWS = D // 2 // 128
PACK_PAIR = 2 * PACK_ROWS

VMEM_LIMIT = 56 * 1024 * 1024
MOE_VMEM_LIMIT = 60 * 1024 * 1024


def _cp(sem, vmem=None):
    return pltpu.CompilerParams(dimension_semantics=sem, vmem_limit_bytes=vmem)


def _dot(a, b):
    return jnp.dot(a, b, preferred_element_type=F32)


def _split(x):
    hi = x.astype(BF16)
    lo = (x - hi.astype(F32)).astype(BF16)
    return hi, lo


def _dot3(a, b):
    ah, al = _split(a)
    bh, bl = _split(b)
    return _dot(ah, bh) + _dot(al, bh) + _dot(ah, bl)


def _dot3_nt(a, b):
    dn = (((1,), (1,)), ((), ()))
    d = lambda x, y: lax.dot_general(x, y, dn, preferred_element_type=F32)
    ah, al = _split(a)
    bh, bl = _split(b)
    return d(ah, bh) + d(al, bh) + d(ah, bl)


def _sigmoid(x):
    return 0.5 * jnp.tanh(0.5 * x) + 0.5


def _gelu(x):
    return 0.5 * x * (1.0 + jnp.tanh(0.7978845608028654 * (x + 0.044715 * (x * x * x))))


def _rms_mod(x, g, shift, scale):
    ms = jnp.mean(x * x, axis=-1, keepdims=True)
    y = x * lax.rsqrt(ms + EPS) * g
    return y * (1.0 + scale) + shift


def _layernorm(x, g, b):
    xc = x - jnp.mean(x, axis=-1, keepdims=True)
    var = jnp.mean(xc * xc, axis=-1, keepdims=True)
    return xc * lax.rsqrt(var + EPS) * g + b


def _tile_mod_row(r):
    return jnp.where(r < P_TILES, 0, 1 + (r - P_TILES) // S_TILES)


def _tile_is_seq_start(r):
    return (r < P_TILES) | ((r - P_TILES) % S_TILES == 0)


def _tile_is_seq_end(r):
    return (r < P_TILES) | ((r - P_TILES) % S_TILES == S_TILES - 1)


MOD_TN = 512


def _adaln_kernel(cond_ref, w_ref, b_ref, o_ref):
    cond = cond_ref[...]
    s = cond * _sigmoid(cond)
    o_ref[...] = _dot3(s, w_ref[...]) + b_ref[...]


def _adaln(cond8, w_mod, b_mod):
    depth = w_mod.shape[0]
    out = pl.pallas_call(
        _adaln_kernel,
        out_shape=jax.ShapeDtypeStruct((depth, N_MOD, 6 * D), F32),
        grid=(depth, 6 * D // MOD_TN),
        in_specs=[
            pl.BlockSpec((N_MOD, D), lambda l, j: (0, 0)),
            pl.BlockSpec((None, D, MOD_TN), lambda l, j: (l, 0, j)),
            pl.BlockSpec((None, 1, MOD_TN), lambda l, j: (l, 0, j)),
        ],
        out_specs=pl.BlockSpec((None, N_MOD, MOD_TN), lambda l, j: (l, 0, j)),
        compiler_params=_cp(("arbitrary", "arbitrary")),
        name="adaln",
    )(cond8, w_mod, b_mod.reshape(depth, 1, 6 * D))
    return out.reshape(depth, N_MOD, 6, D)


PROJ_P_STEPS = N_P // TM_PROJ
PROJ_S_STEPS = T_SAMPLE // TM_PROJ


def _stream_specs(rows, p_steps, h):
    if isinstance(h, tuple):
        hp, hl = h
        first_latent = 0
    else:
        hp = hl = h
        first_latent = p_steps
    specs = [
        pl.BlockSpec((rows, D), lambda i, *_: (jnp.minimum(i, p_steps - 1), 0)),
        pl.BlockSpec((rows, D), lambda i, *_: (jnp.maximum(i - p_steps, 0) + first_latent, 0)),
    ]
    return specs, (hp, hl)


def _inproj_kernel(xp_ref, xl_ref, g_ref, m_ref, w_ref, o_ref):
    x = jnp.where(pl.program_id(0) < PROJ_P_STEPS, xp_ref[...], xl_ref[...])
    hn = _rms_mod(x, g_ref[...], m_ref[0:1, :], m_ref[1:2, :])
    o_ref[...] = _dot(hn.astype(BF16), w_ref[...])


def _inproj(h, g, mod, w_bf16):
    n_out = w_bf16.shape[1]

    def mod_row(i):
        return jnp.where(i < PROJ_P_STEPS, 0, 1 + (i - PROJ_P_STEPS) // PROJ_S_STEPS)

    h_specs, h_args = _stream_specs(TM_PROJ, PROJ_P_STEPS, h)
    return pl.pallas_call(
        _inproj_kernel,
        out_shape=jax.ShapeDtypeStruct((N, n_out), F32),
        grid=(N // TM_PROJ,),
        in_specs=h_specs + [
            pl.BlockSpec((1, D), lambda i: (0, 0)),
            pl.BlockSpec((None, 6, D), lambda i: (mod_row(i), 0, 0)),
            pl.BlockSpec((D, n_out), lambda i: (0, 0)),
        ],
        out_specs=pl.BlockSpec((TM_PROJ, n_out), lambda i: (i, 0)),
        compiler_params=_cp(("arbitrary",), VMEM_LIMIT),
        name="inproj",
    )(*h_args, g.reshape(1, D), mod, w_bf16)


LRU_HALO = 8
SCAN_ROWS = 8


def _lru_tile(d, s):
    return jnp.where(d == 0, s, N_TILES - 1 - s)


def _lru_kernel(x_ref, prev_ref, next_ref, cw_ref, cb_ref, wg_ref, br_ref, bi_ref, lam_ref, h0_ref,
                o_ref, state_ref, ext_ref, a_ref, carry_ref):
    d = pl.program_id(0)
    r = _lru_tile(d, pl.program_id(1))
    start = _tile_is_seq_start(r)
    end = _tile_is_seq_end(r)

    ext_ref[0:LRU_HALO, :] = jnp.where(start, 0.0, prev_ref[...])
    ext_ref[LRU_HALO:LRU_HALO + TILE, :] = x_ref[...]
    ext_ref[LRU_HALO + TILE:, :] = jnp.where(end, 0.0, next_ref[...])
    xc = cb_ref[...] + jnp.zeros((TILE, LRU_W), F32)
    for k in range(LRU_K):
        off = LRU_HALO - LRU_LEFT + k
        xc = xc + ext_ref[off:off + TILE, :] * cw_ref[k:k + 1, :]

    xcb = xc.astype(BF16)
    pre_r, pre_i = [], []
    for blk in range(LRU_W // LRU_BLK):
        g = _dot(xcb[:, blk * LRU_BLK:(blk + 1) * LRU_BLK], wg_ref[blk])
        pre_r.append(g[:, :LRU_BLK])
        pre_i.append(g[:, LRU_BLK:])
    gate_r = _sigmoid(jnp.concatenate(pre_r, axis=1) + br_ref[...])
    gate_i = _sigmoid(jnp.concatenate(pre_i, axis=1) + bi_ref[...])
    neg_lam = -lam_ref[...]
    softplus = jnp.maximum(neg_lam, 0.0) + jnp.log1p(jnp.exp(-jnp.abs(neg_lam)))
    log_a = (-LRU_C) * gate_r * softplus
    a = jnp.exp(log_a)
    a_ref[...] = a
    o_ref[...] = jnp.sqrt(-jnp.tanh(log_a) * (a * a + 1.0)) * (gate_i * xc)

    fresh = jnp.where(d == 0, start, end)
    h_init = jnp.where(fresh, h0_ref[pl.ds(_tile_mod_row(r), 1), :], carry_ref[...])

    row = lax.broadcasted_iota(jnp.int32, (SCAN_ROWS, LRU_W), 0)

    def scan_group(g, h, reverse):
        rows = pl.ds(pl.multiple_of(g * SCAN_ROWS, SCAN_ROWS), SCAN_ROWS)
        a = a_ref[rows, :]
        b = o_ref[rows, :]
        for s in (1, 2, 4):
            shift = SCAN_ROWS - s if reverse else s
            inside = (row < SCAN_ROWS - s) if reverse else (row >= s)
            b = jnp.where(inside, a * pltpu.roll(b, shift, 0) + b, b)
            a = jnp.where(inside, a * pltpu.roll(a, shift, 0), a)
        hs = a * h + b
        o_ref[rows, :] = hs
        return hs[0:1, :] if reverse else hs[SCAN_ROWS - 1:SCAN_ROWS, :]

    n_groups = TILE // SCAN_ROWS

    @pl.when(d == 0)
    def _():
        carry_ref[...] = lax.fori_loop(0, n_groups, lambda g, h: scan_group(g, h, False), h_init, unroll=4)

    @pl.when(d == 1)
    def _():
        carry_ref[...] = lax.fori_loop(0, n_groups, lambda g, h: scan_group(n_groups - 1 - g, h, True),
                                       h_init, unroll=4)

    h_last = carry_ref[...]

    @pl.when(r < P_TILES)
    def _():
        state_ref[...] = h_last


def _lru(proj0, conv_w, conv_b, wg, b_r, b_i, lam, h0):
    n_halo_blocks = N // LRU_HALO
    per_tile = TILE // LRU_HALO
    tile = lambda d, s: _lru_tile(d, s)
    return pl.pallas_call(
        _lru_kernel,
        out_shape=(jax.ShapeDtypeStruct((2, N, LRU_W), F32),
                   jax.ShapeDtypeStruct((2, N_PROMPT_SEQ, 1, LRU_W), F32)),
        grid=(2, N_TILES),
        in_specs=[
            pl.BlockSpec((TILE, LRU_W), lambda d, s: (tile(d, s), 0)),
            pl.BlockSpec((LRU_HALO, LRU_W), lambda d, s: (jnp.maximum(tile(d, s) * per_tile - 1, 0), 0)),
            pl.BlockSpec((LRU_HALO, LRU_W),
                         lambda d, s: (jnp.minimum((tile(d, s) + 1) * per_tile, n_halo_blocks - 1), 0)),
            pl.BlockSpec((LRU_K, LRU_W), lambda d, s: (0, 0)),
            pl.BlockSpec((1, LRU_W), lambda d, s: (0, 0)),
            pl.BlockSpec((None, LRU_W // LRU_BLK, LRU_BLK, 2 * LRU_BLK), lambda d, s: (d, 0, 0, 0)),
            pl.BlockSpec((None, 1, LRU_W), lambda d, s: (d, 0, 0)),
            pl.BlockSpec((None, 1, LRU_W), lambda d, s: (d, 0, 0)),
            pl.BlockSpec((None, 1, LRU_W), lambda d, s: (d, 0, 0)),
            pl.BlockSpec((None, N_MOD, LRU_W), lambda d, s: (d, 0, 0)),
        ],
        out_specs=[
            pl.BlockSpec((None, TILE, LRU_W), lambda d, s: (d, tile(d, s), 0)),
            pl.BlockSpec((None, None, 1, LRU_W), lambda d, s: (d, jnp.minimum(tile(d, s), P_TILES - 1), 0, 0)),
        ],
        scratch_shapes=[
            pltpu.VMEM((TILE + 2 * LRU_HALO, LRU_W), F32),
            pltpu.VMEM((TILE, LRU_W), F32),
            pltpu.VMEM((1, LRU_W), F32),
        ],
        compiler_params=_cp(("arbitrary", "arbitrary"), VMEM_LIMIT),
        name="rglru_scan",
    )(proj0, proj0, proj0, conv_w, conv_b.reshape(1, LRU_W), wg,
      b_r.reshape(2, 1, LRU_W), b_i.reshape(2, 1, LRU_W), lam.reshape(2, 1, LRU_W), h0)


def _dft_tables(n, scale):
    k = np.arange(n, dtype=np.int64)
    ang = 2.0 * np.pi * ((k[:, None] * k[None, :]) % n).astype(np.float64) / n
    return np.cos(ang) * scale, np.sin(ang) * scale


def _channel_tables():
    c, s = _dft_tables(FN_GD, FN_GD ** -0.5)
    eye = np.eye(FN_G)
    return (jnp.asarray(np.kron(eye, c), BF16), jnp.asarray(np.kron(eye, s), BF16))


def _time_tables(t_len):
    c, s = _dft_tables(t_len, t_len ** -0.5)
    return jnp.asarray(c, BF16), jnp.asarray(s, BF16)


FN_SEQ_PER_STEP = 4


def _fourier_prompt_kernel(z_ref, cc_ref, sc_ref, ct_ref, st_ref, o_ref):
    z = z_ref[...].astype(BF16)
    zc = _dot(z, cc_ref[...]).astype(BF16)
    zs = _dot(z, sc_ref[...]).astype(BF16)
    for b in range(FN_SEQ_PER_STEP):
        rows = slice(b * T_PROMPT, (b + 1) * T_PROMPT)
        o_ref[rows, :] = _dot(ct_ref[...], zc[rows, :]) - _dot(st_ref[...], zs[rows, :])


def _fourier_prompt(proj0):
    cc, sc = _channel_tables()
    ct, st = _time_tables(T_PROMPT)
    const = lambda b: (0, 0)
    return pl.pallas_call(
        _fourier_prompt_kernel,
        out_shape=jax.ShapeDtypeStruct((N_P, FN_W), F32),
        grid=(N_PROMPT_SEQ // FN_SEQ_PER_STEP,),
        in_specs=[
            pl.BlockSpec((FN_SEQ_PER_STEP * T_PROMPT, FN_W), lambda b: (b, 2 * LRU_W // FN_W)),
            pl.BlockSpec((FN_W, FN_W), const),
            pl.BlockSpec((FN_W, FN_W), const),
            pl.BlockSpec((T_PROMPT, T_PROMPT), const),
            pl.BlockSpec((T_PROMPT, T_PROMPT), const),
        ],
        out_specs=pl.BlockSpec((FN_SEQ_PER_STEP * T_PROMPT, FN_W), lambda b: (b, 0)),
        compiler_params=_cp(("arbitrary",)),
        name="fourier_prompt",
    )(proj0, cc, sc, ct, st)


def _fourier_sample_kernel(z_ref, cc_ref, sc_ref, ct_ref, st_ref, o_ref, zc_ref, zs_ref):
    @pl.when(pl.program_id(1) == 0)
    def _():
        z = z_ref[...].astype(BF16)
        zc_ref[...] = _dot(z, cc_ref[...]).astype(BF16)
        zs_ref[...] = _dot(z, sc_ref[...]).astype(BF16)

    o_ref[...] = _dot(ct_ref[...], zc_ref[...]) - _dot(st_ref[...], zs_ref[...])


def _fourier_sample(proj0):
    cc, sc = _channel_tables()
    ct, st = _time_tables(T_SAMPLE)
    const = lambda b, i: (0, 0)
    first_seq_block = N_P // T_SAMPLE
    return pl.pallas_call(
        _fourier_sample_kernel,
        out_shape=jax.ShapeDtypeStruct((N_S, FN_W), F32),
        grid=(N_SAMPLE_SEQ, S_TILES),
        in_specs=[
            pl.BlockSpec((T_SAMPLE, FN_W), lambda b, i: (first_seq_block + b, 2 * LRU_W // FN_W)),
            pl.BlockSpec((FN_W, FN_W), const),
            pl.BlockSpec((FN_W, FN_W), const),
            pl.BlockSpec((TILE, T_SAMPLE), lambda b, i: (i, 0)),
            pl.BlockSpec((TILE, T_SAMPLE), lambda b, i: (i, 0)),
        ],
        out_specs=pl.BlockSpec((TILE, FN_W), lambda b, i: (b * S_TILES + i, 0)),
        scratch_shapes=[pltpu.VMEM((T_SAMPLE, FN_W), BF16), pltpu.VMEM((T_SAMPLE, FN_W), BF16)],
        compiler_params=_cp(("arbitrary", "arbitrary"), VMEM_LIMIT),
        name="fourier_sample",
    )(proj0, cc, sc, ct, st)


def _epilogue(mix, hp_ref, hl_ref, m_ref, gf_ref, wr_ref, brt_ref, tri_ref,
              hout_ref, fn_ref, tope_ref, gate_ref, rank_ref, cnt_ref, run_ref):
    h = jnp.where(pl.program_id(0) < P_TILES, hp_ref[...], hl_ref[...])
    h_new = h + m_ref[2:3, :] * mix
    hout_ref[...] = h_new
    fn = _rms_mod(h_new, gf_ref[...], m_ref[3:4, :], m_ref[4:5, :])
    hi = lax.bitcast_convert_type(fn[:, :D // 2].astype(BF16).astype(F32), jnp.uint32)
    lo = lax.bitcast_convert_type(fn[:, D // 2:].astype(BF16).astype(F32), jnp.uint32)
    packed = hi | (lo >> 16)
    for q in range(PACK_ROWS):
        fn_ref[pl.ds(q, TILE, stride=PACK_ROWS), :] = packed[:, q * 128:(q + 1) * 128]
    logits = _dot3_nt(wr_ref[...], fn) + brt_ref[...]
    iota = lax.broadcasted_iota(jnp.int32, logits.shape, 0)
    vals, idxs = [], []
    for _ in range(TOP_K):
        m = jnp.max(logits, axis=0, keepdims=True)
        idx = jnp.min(jnp.where(logits == m, iota, N_EXP), axis=0, keepdims=True)
        vals.append(m)
        idxs.append(idx)
        logits = jnp.where(iota == idx, -jnp.inf, logits)
    exps = [jnp.exp(v - vals[0]) for v in vals]
    denom = exps[0] + exps[1] + exps[2] + exps[3]
    for k in range(TOP_K):
        tope_ref[k:k + 1, :] = idxs[k]
        gate_ref[k:k + 1, :] = exps[k] / denom

    @pl.when(pl.program_id(0) == 0)
    def _():
        run_ref[...] = jnp.zeros_like(run_ref)

    run = run_ref[...]
    onehots = [jnp.where(iota == idxs[k], 1.0, 0.0) for k in range(TOP_K)]
    incl_all = _dot(jnp.concatenate(onehots, axis=0).astype(BF16), tri_ref[...])
    for k in range(TOP_K):
        incl = incl_all[k * N_EXP:(k + 1) * N_EXP, :]
        rank = jnp.sum(onehots[k] * (incl - 1.0 + run), axis=0, keepdims=True)
        rank_ref[k:k + 1, :] = rank.astype(jnp.int32)
        run = run + incl[:, TILE - 1:TILE]
    run_ref[...] = run
    cnt_ref[...] = run.astype(jnp.int32)


_EPI_OUT_SHAPES = (
    jax.ShapeDtypeStruct((N, D), F32),
    jax.ShapeDtypeStruct((N * PACK_ROWS, 128), jnp.uint32),
    jax.ShapeDtypeStruct((TOP_K, N), jnp.int32),
    jax.ShapeDtypeStruct((TOP_K, N), F32),
    jax.ShapeDtypeStruct((TOP_K, N), jnp.int32),
    jax.ShapeDtypeStruct((N_EXP, 1), jnp.int32),
)
_EPI_SCRATCH = [pltpu.VMEM((N_EXP, 1), F32)]


def _epi_operands(h, mod, g_ffn, w_router_t, b_router):
    tri = jnp.asarray(np.triu(np.ones((TILE, TILE), np.float32)), BF16)
    return (*_stream_specs(TILE, P_TILES, h)[1], mod, g_ffn.reshape(1, D), w_router_t,
            b_router.reshape(N_EXP, 1), tri)


def _epi_in_specs(h):
    h_specs, _ = _stream_specs(TILE, P_TILES, h)
    return h_specs + [
        pl.BlockSpec((None, 6, D), lambda r: (_tile_mod_row(r), 0, 0)),
        pl.BlockSpec((1, D), lambda r: (0, 0)),
        pl.BlockSpec((N_EXP, D), lambda r: (0, 0)),
        pl.BlockSpec((N_EXP, 1), lambda r: (0, 0)),
        pl.BlockSpec((TILE, TILE), lambda r: (0, 0)),
    ]


def _epi_out_specs():
    return [
        pl.BlockSpec((TILE, D), lambda r: (r, 0)),
        pl.BlockSpec((TILE * PACK_ROWS, 128), lambda r: (r, 0)),
        pl.BlockSpec((TOP_K, TILE), lambda r: (0, r)),
        pl.BlockSpec((TOP_K, TILE), lambda r: (0, r)),
        pl.BlockSpec((TOP_K, TILE), lambda r: (0, r)),
        pl.BlockSpec((N_EXP, 1), lambda r: (0, 0)),
    ]


def _post0_kernel(hs_ref, xg_ref, yfp_ref, yfl_ref, wo_ref, *epilogue_refs):
    y_rec = (hs_ref[0] + hs_ref[1]) * _gelu(xg_ref[...])
    y_four = jnp.where(pl.program_id(0) < P_TILES, yfp_ref[...], yfl_ref[...])
    mix = (_dot(y_rec.astype(BF16), wo_ref[0:LRU_W, :])
           + _dot(y_four.astype(BF16), wo_ref[LRU_W:, :]))
    _epilogue(mix, *epilogue_refs)


def _post0(hs, proj0, yf_prompt, yf_latent, w_out_bf16, h, mod, g_ffn, w_router_t, b_router):
    return pl.pallas_call(
        _post0_kernel,
        out_shape=_EPI_OUT_SHAPES,
        grid=(N_TILES,),
        in_specs=[
            pl.BlockSpec((2, TILE, LRU_W), lambda r: (0, r, 0)),
            pl.BlockSpec((TILE, LRU_W), lambda r: (r, 1)),
            pl.BlockSpec((TILE, FN_W), lambda r: (jnp.minimum(r, P_TILES - 1), 0)),
            pl.BlockSpec((TILE, FN_W), lambda r: (jnp.maximum(r - P_TILES, 0), 0)),
            pl.BlockSpec((D, D), lambda r: (0, 0)),
        ] + _epi_in_specs(h),
        out_specs=_epi_out_specs(),
        scratch_shapes=_EPI_SCRATCH,
        compiler_params=_cp(("arbitrary",), VMEM_LIMIT),
        name="post_rglru_fourier",
    )(hs, proj0, yf_prompt, yf_latent, w_out_bf16, *_epi_operands(h, mod, g_ffn, w_router_t, b_router))


def _glu(x):
    return x[:, :CONV_W] * _sigmoid(x[:, CONV_W:])


def _post1_kernel(p_ref, prev_ref, next_ref, lng_ref, lnb_ref, ws_ref, bs_ref, dww_ref, dwb_ref,
                  clg_ref, clb_ref, wo_ref, *rest):
    epilogue_refs, (ext_ref, shift_ref, u_ref, v_ref, mix_ref) = rest[:-5], rest[-5:]
    r = pl.program_id(0)

    for c in range(TILE // ROW_PIECE):
        rows = slice(c * ROW_PIECE, (c + 1) * ROW_PIECE)
        z = _gelu(p_ref[rows, 0:2 * SGU_W])
        u_ref[rows, :] = z[:, :SGU_W]
        v_ref[rows, :] = _layernorm(z[:, SGU_W:], lng_ref[...], lnb_ref[...]).astype(BF16)
        ext_ref[CONV_HALO + c * ROW_PIECE:CONV_HALO + (c + 1) * ROW_PIECE, :] = _glu(p_ref[rows, 2 * SGU_W:])
    ext_ref[0:CONV_HALO, :] = jnp.where(_tile_is_seq_start(r), 0.0, _glu(prev_ref[...]))
    ext_ref[CONV_HALO + TILE:, :] = jnp.where(_tile_is_seq_end(r), 0.0, _glu(next_ref[...]))

    for n in range(TILE // CHUNK):
        for g in range(SGU_G):
            rows, cols = slice(n * CHUNK, (n + 1) * CHUNK), slice(g * CHUNK, (g + 1) * CHUNK)
            s = _dot(ws_ref[g], v_ref[rows, cols]) + bs_ref[:, cols]
            mix_ref[rows, cols] = (u_ref[rows, cols] * s).astype(BF16)

    n_shift_rows = TILE + 2 * CONV_HALO - 8
    for phase in range(8):
        shift_ref[phase] = ext_ref[phase:phase + n_shift_rows, :]
    for c in range(TILE // CONV_ROWS):
        acc = dwb_ref[...] + jnp.zeros((CONV_ROWS // 8, 8, CONV_W), F32)
        for k in range(CONV_K):
            off = CONV_HALO - CONV_PAD + k
            start = off // 8 * 8 + c * CONV_ROWS
            rows = shift_ref[off % 8, start:start + CONV_ROWS, :].reshape(CONV_ROWS // 8, 8, CONV_W)
            acc = acc + rows * dww_ref[k]
        ln = _layernorm(acc.reshape(CONV_ROWS, CONV_W), clg_ref[...], clb_ref[...])
        mix_ref[c * CONV_ROWS:(c + 1) * CONV_ROWS, SGU_W:] = (ln * _sigmoid(ln)).astype(BF16)

    _epilogue(_dot(mix_ref[...], wo_ref[...]), *epilogue_refs)


def _post1(proj1, ln_g, ln_b, ws_bf16, bs_full, dw_w, dw_b, cln_g, cln_b, w_out_bf16,
           h, mod, g_ffn, w_router_t, b_router):
    per_tile = TILE // CONV_HALO
    n_halo_blocks = N // CONV_HALO
    const2 = lambda r: (0, 0)
    row = lambda a: a.reshape(1, -1)
    return pl.pallas_call(
        _post1_kernel,
        out_shape=_EPI_OUT_SHAPES,
        grid=(N_TILES,),
        in_specs=[
            pl.BlockSpec((TILE, IN1), lambda r: (r, 0)),
            pl.BlockSpec((CONV_HALO, 2 * CONV_W), lambda r: (jnp.maximum(r * per_tile - 1, 0), 1)),
            pl.BlockSpec((CONV_HALO, 2 * CONV_W),
                         lambda r: (jnp.minimum((r + 1) * per_tile, n_halo_blocks - 1), 1)),
            pl.BlockSpec((1, SGU_W), const2),
            pl.BlockSpec((1, SGU_W), const2),
            pl.BlockSpec((SGU_G, CHUNK, CHUNK), lambda r: (0, 0, 0)),
            pl.BlockSpec((CHUNK, SGU_W), const2),
            pl.BlockSpec((CONV_K, 8, CONV_W), lambda r: (0, 0, 0)),
            pl.BlockSpec((1, CONV_W), const2),
            pl.BlockSpec((1, CONV_W), const2),
            pl.BlockSpec((1, CONV_W), const2),
            pl.BlockSpec((D, D), const2),
        ] + _epi_in_specs(h),
        out_specs=_epi_out_specs(),
        scratch_shapes=_EPI_SCRATCH + [
            pltpu.VMEM((TILE + 2 * CONV_HALO, CONV_W), F32),
            pltpu.VMEM((8, TILE + 2 * CONV_HALO - 8, CONV_W), F32),
            pltpu.VMEM((TILE, SGU_W), F32),
            pltpu.VMEM((TILE, SGU_W), BF16),
            pltpu.VMEM((TILE, D), BF16),
        ],
        compiler_params=_cp(("arbitrary",), VMEM_LIMIT),
        name="post_sgu_conformer",
    )(proj1, proj1, proj1, row(ln_g), row(ln_b), ws_bf16, bs_full,
      jnp.broadcast_to(dw_w[:, None, :], (CONV_K, 8, CONV_W)), row(dw_b), row(cln_g), row(cln_b),
      w_out_bf16, *_epi_operands(h, mod, g_ffn, w_router_t, b_router))


PAD_BASE = N_ASSIGN
PLACEHOLDER_BASE = PAD_BASE + 2 * MOE_BLOCK
Y_ROWS = PLACEHOLDER_BASE + 2 * MOE_BLOCK


def _inverse_kernel(pos_ref, pend_ref, out_ref):
    def fill_block(start):
        def fill(j, carry):
            s = start + j
            out_ref[s] = PAD_BASE + (s & (2 * MOE_BLOCK - 1))
            return carry
        lax.fori_loop(0, MOE_BLOCK, fill, 0, unroll=16)

    def per_expert(e, carry):
        fill_block(jnp.maximum(pend_ref[e] - MOE_BLOCK, 0))
        return carry

    def idle_block(b, carry):
        fill_block(b * MOE_BLOCK)
        return carry

    lax.fori_loop(0, N_EXP, per_expert, 0)
    lax.fori_loop(pend_ref[N_EXP - 1] // MOE_BLOCK, N_BLOCKS, idle_block, 0)

    def body(t, carry):
        for k in range(TOP_K):
            out_ref[pos_ref[k * N + t]] = t * TOP_K + k
        return carry

    lax.fori_loop(0, N, body, 0, unroll=8)


def _inverse(pos_flat, pad_end):
    smem = pl.BlockSpec(memory_space=pltpu.SMEM)
    return pl.pallas_call(
        _inverse_kernel,
        out_shape=jax.ShapeDtypeStruct((N_SLOTS,), jnp.int32),
        in_specs=[smem, smem],
        out_specs=smem,
        name="moe_inverse",
    )(pos_flat, pad_end)


def _routing_tables(top_e, rank, counts):
    counts = counts.reshape(N_EXP)
    padded = (counts + MOE_BLOCK - 1) // MOE_BLOCK * MOE_BLOCK
    pad_end = jnp.cumsum(padded).astype(jnp.int32)
    pad_start = pad_end - padded
    experts = jnp.arange(N_EXP, dtype=jnp.int32)[:, None, None]
    pos = rank + jnp.sum(jnp.where(top_e[None] == experts, pad_start[:, None, None], 0), axis=0)
    pos = pos.astype(jnp.int32).reshape(-1)
    n_used = pad_end[-1] // MOE_BLOCK
    blk = jnp.minimum(jnp.arange(N_BLOCKS, dtype=jnp.int32), n_used - 1)
    block_e = jnp.sum(pad_end[None, :] <= (blk * MOE_BLOCK)[:, None], axis=1).astype(jnp.int32)
    slot_a = _inverse(pos, pad_end)
    tok = jnp.minimum(slot_a >> 2, N - 1)
    gather_key = (tok >> 1) * (2 * PACK_PAIR) + (tok & 1)
    out_row = jnp.where(slot_a >= PAD_BASE, slot_a, (slot_a & (TOP_K - 1)) * N + (slot_a >> 2))
    placeholder_rows = PLACEHOLDER_BASE + jnp.arange(MOE_BLOCK, dtype=jnp.int32)
    out_row8 = jnp.concatenate([placeholder_rows, out_row]) * ROW_TILE
    ids = jnp.arange(N_EXP, dtype=jnp.int32)
    later_used = (ids[None, :] > ids[:, None]) & (counts[None, :] > 0)
    next_expert = jnp.min(jnp.where(later_used, ids[None, :], N_EXP), axis=1).astype(jnp.int32)
    return gather_key, out_row8, block_e, n_used.reshape(1), next_expert


def _moe_kernel(be_ref, nu_ref, gk_ref, row8_ref, nxt_ref, x_hbm, w1_hbm, b1g_ref, b1l_ref, w2_hbm, b2_ref, perm_ref,
                y_hbm, xs, xbuf, w1f, w2f, w1s, w2s, sem, ybuf, osem, *, layer):
    i = pl.program_id(0)
    n_used = nu_ref[0]
    slot = i % 2
    e = be_ref[i]
    e_prev = be_ref[jnp.maximum(i - 1, 0)]
    low_half = lax.broadcasted_iota(jnp.int32, (PACK_PAIR, 128), 0) < PACK_ROWS

    def weight_copies(expert):
        return (pltpu.make_async_copy(w1_hbm.at[layer, expert], w1f, sem.at[1]),
                pltpu.make_async_copy(w2_hbm.at[layer, expert], w2f, sem.at[2]))

    def token_tile(key):
        tile = xs[pl.ds(pl.multiple_of(key >> 1, PACK_PAIR), PACK_PAIR), :]
        return tile, pltpu.roll(tile, PACK_ROWS, 0), key & 1

    def gather_pair(blk, jj, dst_slot):
        t0, r0, half0 = token_tile(gk_ref[blk * MOE_BLOCK + 2 * jj])
        t1, r1, half1 = token_tile(gk_ref[blk * MOE_BLOCK + 2 * jj + 1])
        lower = jnp.where(half0 == 0, t0, r0)
        upper = jnp.where(half1 == 1, t1, r1)
        start = jj * PACK_PAIR if isinstance(jj, int) else pl.multiple_of(jj * PACK_PAIR, PACK_PAIR)
        xbuf[dst_slot, pl.ds(start, PACK_PAIR), :] = jnp.where(low_half, lower, upper)

    def out_copy(src_slot, j, row8):
        start = j * ROW_TILE if isinstance(j, int) else pl.multiple_of(j * ROW_TILE, ROW_TILE)
        dst = y_hbm.at[pl.ds(pl.multiple_of(row8, ROW_TILE), ROW_TILE), :]
        return pltpu.make_async_copy(ybuf.at[src_slot, pl.ds(start, ROW_TILE), :], dst, osem.at[src_slot])

    def out_buffer_wait(src_slot):
        pltpu.make_async_copy(ybuf.at[src_slot], y_hbm.at[pl.ds(0, MOE_BLOCK * ROW_TILE), :],
                              osem.at[src_slot]).wait()

    @pl.when(i == 0)
    def _():
        xs_copy = pltpu.make_async_copy(x_hbm, xs, sem.at[0])
        xs_copy.start()
        for c in weight_copies(e):
            c.start()
        ybuf[...] = jnp.zeros_like(ybuf)

        def zero_rows(first_row):
            def body(j, carry):
                out_copy(0, j, (first_row + j) * ROW_TILE).start()
                return carry
            lax.fori_loop(0, MOE_BLOCK, body, 0, unroll=8)

        zero_rows(PAD_BASE)
        zero_rows(PAD_BASE + MOE_BLOCK)
        out_buffer_wait(0)
        out_buffer_wait(0)
        zero_rows(PLACEHOLDER_BASE + MOE_BLOCK)
        xs_copy.wait()

        def body(jj, carry):
            gather_pair(0, jj, 0)
            return carry
        lax.fori_loop(0, MOE_BLOCK // 2, body, 0, unroll=4)

    @pl.when((i < n_used) & ((i == 0) | (e != e_prev)))
    def _():
        for c in weight_copies(e):
            c.wait()
        for c in range(2 * D_FF // 256):
            wc = w1f[:, c * 256:(c + 1) * 256].astype(BF16)
            w1s[:, c * 256:(c + 1) * 256] = _dot(wc, perm_ref[...]).astype(BF16)
        w2s[...] = w2f[...].astype(BF16)
        e_next = nxt_ref[e]

        @pl.when(e_next < N_EXP)
        def _():
            for c in weight_copies(e_next):
                c.start()

    def block(prefetch_next):
        for j in range(MOE_BLOCK):
            out_copy(1 - slot, j, row8_ref[i * MOE_BLOCK + j]).start()
        halves = [[], []]
        for q in range(PACK_ROWS):
            words = xbuf[slot, pl.ds(q, MOE_BLOCK, stride=PACK_ROWS), :]
            halves[0].append(lax.bitcast_convert_type(words & jnp.uint32(0xFFFF0000), F32).astype(BF16))
            halves[1].append(lax.bitcast_convert_type(words << 16, F32).astype(BF16))
        x = jnp.concatenate(halves[0] + halves[1], axis=1)
        if prefetch_next:
            for jj in range(MOE_BLOCK // 2):
                gather_pair(i + 1, jj, 1 - slot)
        hid = _dot(x, w1s[...])
        acts = []
        for c in range(D_FF // 128):
            h_glu = hid[:, c * 256:c * 256 + 128] + b1g_ref[:, c * 128:(c + 1) * 128]
            h_lin = hid[:, c * 256 + 128:(c + 1) * 256] + b1l_ref[:, c * 128:(c + 1) * 128]
            h_glu = jnp.minimum(h_glu, LIMIT)
            h_lin = jnp.clip(h_lin, -LIMIT, LIMIT)
            acts.append((h_glu * _sigmoid(ALPHA * h_glu) * (h_lin + 1.0)).astype(BF16))
        y = _dot(jnp.concatenate(acts, axis=1), w2s[...]) + b2_ref[...]
        out_buffer_wait(slot)
        for s in range(ROW_TILE):
            ybuf[slot, pl.ds(s, MOE_BLOCK, stride=ROW_TILE), :] = y[:, s * 128:(s + 1) * 128]
        if not prefetch_next:
            def send(j, carry):
                out_copy(slot, j, row8_ref[(i + 1) * MOE_BLOCK + j]).start()
                return carry
            lax.fori_loop(0, MOE_BLOCK, send, 0, unroll=8)
            out_buffer_wait(1 - slot)
            out_buffer_wait(slot)

    @pl.when(i + 1 < n_used)
    def _():
        block(True)

    @pl.when(i + 1 == n_used)
    def _():
        block(False)


def _deinterleave_matrix():
    p = np.zeros((256, 256), np.float32)
    m = np.arange(128)
    p[2 * m, m] = 1.0
    p[2 * m + 1, 128 + m] = 1.0
    return jnp.asarray(p, BF16)


def _moe_experts(layer, fn_packed, gather_key, out_row8, block_e, n_used, next_expert, w1, b1, w2, b2):
    b1g = b1[0][layer].reshape(N_EXP, 1, D_FF)
    b1l = b1[1][layer].reshape(N_EXP, 1, D_FF)
    ex = lambda i, be, *_: (be[i], 0, 0)
    grid_spec = pltpu.PrefetchScalarGridSpec(
        num_scalar_prefetch=5,
        grid=(N_BLOCKS,),
        in_specs=[
            pl.BlockSpec(memory_space=pl.ANY),
            pl.BlockSpec(memory_space=pl.ANY),
            pl.BlockSpec((None, 1, D_FF), ex),
            pl.BlockSpec((None, 1, D_FF), ex),
            pl.BlockSpec(memory_space=pl.ANY),
            pl.BlockSpec((None, 1, D), ex),
            pl.BlockSpec((256, 256), lambda i, *_: (0, 0)),
        ],
        out_specs=pl.BlockSpec(memory_space=pl.ANY),
        scratch_shapes=[
            pltpu.VMEM((N * PACK_ROWS, 128), jnp.uint32),
            pltpu.VMEM((2, MOE_BLOCK * PACK_ROWS, 128), jnp.uint32),
            pltpu.VMEM((D, 2 * D_FF), F32),
            pltpu.VMEM((D_FF, D), F32),
            pltpu.VMEM((D, 2 * D_FF), BF16),
            pltpu.VMEM((D_FF, D), BF16),
            pltpu.SemaphoreType.DMA((3,)),
            pltpu.VMEM((2, MOE_BLOCK * ROW_TILE, 128), F32),
            pltpu.SemaphoreType.DMA((2,)),
        ],
    )
    return pl.pallas_call(
        functools.partial(_moe_kernel, layer=layer),
        out_shape=jax.ShapeDtypeStruct((Y_ROWS * ROW_TILE, 128), F32),
        grid_spec=grid_spec,
        compiler_params=_cp(("arbitrary",), MOE_VMEM_LIMIT),
        name="moe_experts",
    )(block_e, n_used, gather_key, out_row8, next_expert, fn_packed, w1, b1g, b1l, w2, b2[layer].reshape(N_EXP, 1, D),
      _deinterleave_matrix())


def _combine_kernel(y0_ref, y1_ref, y2_ref, y3_ref, g_ref, h_ref, m_ref, fg_ref, *outs, final):
    i = pl.program_id(0)
    g = g_ref[...]
    cols = []
    for s in range(ROW_TILE):
        y = y0_ref[pl.ds(s, TILE, stride=ROW_TILE), :] * g[:, 0:1]
        for k, yk_ref in ((1, y1_ref), (2, y2_ref), (3, y3_ref)):
            y = y + yk_ref[pl.ds(s, TILE, stride=ROW_TILE), :] * g[:, k:k + 1]
        cols.append(y)
    h_new = h_ref[...] + m_ref[5:6, :] * jnp.concatenate(cols, axis=1)
    if final:
        yp_ref, yl_ref = outs
        ms = jnp.mean(h_new * h_new, axis=-1, keepdims=True)
        y = h_new * lax.rsqrt(ms + EPS) * fg_ref[...]

        @pl.when(i < P_TILES)
        def _():
            yp_ref[...] = y

        @pl.when(i >= P_TILES)
        def _():
            yl_ref[...] = y
    else:
        outs[0][...] = h_new


def _moe_combine(y_rows, gates_nt, h, mod, final_g, final):
    if final:
        out_shape = (jax.ShapeDtypeStruct((N_P, D), F32), jax.ShapeDtypeStruct((N_S, D), F32))
        out_specs = [
            pl.BlockSpec((TILE, D), lambda i: (jnp.minimum(i, P_TILES - 1), 0)),
            pl.BlockSpec((TILE, D), lambda i: (jnp.maximum(i - P_TILES, 0), 0)),
        ]
    else:
        out_shape = jax.ShapeDtypeStruct((N, D), F32)
        out_specs = pl.BlockSpec((TILE, D), lambda i: (i, 0))
    return pl.pallas_call(
        functools.partial(_combine_kernel, final=final),
        out_shape=out_shape,
        grid=(N_TILES,),
        in_specs=[pl.BlockSpec((TILE * ROW_TILE, 128), lambda i, k=k: (k * N_TILES + i, 0)) for k in range(TOP_K)] + [
            pl.BlockSpec((TILE, TOP_K), lambda i: (i, 0)),
            pl.BlockSpec((TILE, D), lambda i: (i, 0)),
            pl.BlockSpec((None, 6, D), lambda i: (_tile_mod_row(i), 0, 0)),
            pl.BlockSpec((1, D), lambda i: (0, 0)),
        ],
        out_specs=out_specs,
        compiler_params=_cp(("arbitrary",), VMEM_LIMIT),
        name="moe_combine",
    )(y_rows, y_rows, y_rows, y_rows, gates_nt, h, mod, final_g.reshape(1, D))


def _moe(layer, routed, h, mod, w1, b1, w2, b2, final_g, final):
    fn_packed, top_e, gates, rank, counts = routed
    gather_key, out_row8, block_e, n_used, next_expert = _routing_tables(top_e, rank, counts)
    y_rows = _moe_experts(layer, fn_packed, gather_key, out_row8, block_e, n_used, next_expert, w1, b1, w2, b2)
    return _moe_combine(y_rows, gates.T, h, mod, final_g, final)


def _gate_weights(w_r, w_i):
    per = LRU_BLK // LRU_HD
    eye = jnp.eye(per, dtype=F32)

    def blockdiag(w):
        w = w.reshape(2, LRU_HEADS // per, per, LRU_HD, LRU_HD)
        full = jnp.einsum("dgaij,ab->dgaibj", w, eye)
        return full.reshape(2, LRU_HEADS // per, LRU_BLK, LRU_BLK)

    return jnp.concatenate([blockdiag(w_r), blockdiag(w_i)], axis=-1).astype(BF16)


def kernel(x_prompt, x_sample, state_rglru, c, c_ctx, norm_mix_g, norm_ffn_g, w_mod, b_mod, w_in0, lru_conv_w, lru_conv_b, lru_w_r, lru_b_r, lru_w_i, lru_b_i, lru_lambda, w_out0, w_in1, sgu_ln_g, sgu_ln_b, sgu_w_s, sgu_b_s, conv_dw_w, conv_dw_b, conv_ln_g, conv_ln_b, w_out1, w_router, b_router, w1, b1, w2, b2, final_norm_g):
    h = (x_prompt.reshape(N_P, D), x_sample.reshape(N_S, D))
    cond8 = jnp.concatenate([c_ctx[None, :], c, jnp.zeros((N_MOD - 1 - N_SAMPLE_SEQ, D), F32)], axis=0)
    b1 = (b1[:, :, 0::2], b1[:, :, 1::2])
    mod = _adaln(cond8, w_mod, b_mod)

    proj0 = _inproj(h, norm_mix_g[0], mod[0], w_in0[0].astype(BF16))
    st = state_rglru[:, 0].astype(F32)
    h0 = jnp.zeros((2, N_MOD, LRU_W), F32).at[:, 1:1 + N_SAMPLE_SEQ].set(jnp.swapaxes(st, 0, 1))
    hs, ctx_state = _lru(proj0, lru_conv_w[0], lru_conv_b[0], _gate_weights(lru_w_r[0], lru_w_i[0]),
              lru_b_r[0], lru_b_i[0], lru_lambda[0], h0)
    h, *routed = _post0(hs, proj0, _fourier_prompt(proj0), _fourier_sample(proj0), w_out0[0].astype(BF16),
                        h, mod[0], norm_ffn_g[0], w_router[0].T, b_router[0])
    h = _moe(0, routed, h, mod[0], w1, b1, w2, b2, final_norm_g, False)

    proj1 = _inproj(h, norm_mix_g[1], mod[1], w_in1[0].astype(BF16))
    bs_full = jnp.repeat(sgu_b_s[0].T, CHUNK, axis=1)
    h, *routed = _post1(proj1, sgu_ln_g[0], sgu_ln_b[0], sgu_w_s[0].astype(BF16), bs_full,
                        conv_dw_w[0], conv_dw_b[0], conv_ln_g[0], conv_ln_b[0],
                        w_out1[0].astype(BF16), h, mod[1], norm_ffn_g[1], w_router[1].T, b_router[1])
    y_p, y_l = _moe(1, routed, h, mod[1], w1, b1, w2, b2, final_norm_g, True)

    y_prompt = y_p.reshape(N_PROMPT_SEQ, T_PROMPT, D)
    y_sample = y_l.reshape(N_SAMPLE_SEQ, T_SAMPLE, D)
    new_state = jnp.transpose(ctx_state, (1, 2, 0, 3))
    return (y_prompt, y_sample, new_state.astype(x_prompt.dtype))
```

```python
import functools

import numpy as np
import jax
import jax.numpy as jnp
from jax import lax
from jax.experimental import pallas as pl
from jax.experimental.pallas import tpu as pltpu

F32 = jnp.float32
BF16 = jnp.bfloat16

D = 1024
N_PROMPT_SEQ = 32
T_PROMPT = 256
N_SAMPLE_SEQ = 2
T_SAMPLE = 2048
N_P = N_PROMPT_SEQ * T_PROMPT
N_S = N_SAMPLE_SEQ * T_SAMPLE
N = N_P + N_S
EPS = 1e-6

TILE = 256
N_TILES = N // TILE
P_TILES = N_P // TILE
S_TILES = T_SAMPLE // TILE
TM_PROJ = 1024
N_MOD = 8

LRU_W = 768
LRU_HEADS = 12
LRU_HD = 64
LRU_K = 4
LRU_LEFT = 2
LRU_C = 8.0
LRU_BLK = 256
FN_W = 256
FN_G = 4
FN_GD = 64
IN0 = 2 * LRU_W + FN_W

SGU_W = 512
SGU_G = 4
CHUNK = 128
CONV_W = 512
CONV_K = 31
CONV_PAD = 15
CONV_HALO = 16
CONV_ROWS = 32
ROW_PIECE = 32
IN1 = 2 * SGU_W + 2 * CONV_W

N_EXP = 32
TOP_K = 4
D_FF = 1024
ALPHA = 1.702
LIMIT = 7.0
MOE_BLOCK = 256
N_ASSIGN = N * TOP_K
N_BLOCKS = N_ASSIGN // MOE_BLOCK + N_EXP
N_SLOTS = N_BLOCKS * MOE_BLOCK
ROW_TILE = D // 128
PACK_ROWS = D // 2 // 128
PACK_PAIR = 2 * PACK_ROWS

VMEM_LIMIT = 56 * 1024 * 1024
MOE_VMEM_LIMIT = 60 * 1024 * 1024


def _cp(sem, vmem=None):
    return pltpu.CompilerParams(dimension_semantics=sem, vmem_limit_bytes=vmem)


def _dot(a, b):
    return jnp.dot(a, b, preferred_element_type=F32)


def _split(x):
    hi = x.astype(BF16)
    lo = (x - hi.astype(F32)).astype(BF16)
    return hi, lo


def _dot3(a, b):
    ah, al = _split(a)
    bh, bl = _split(b)
    return _dot(ah, bh) + _dot(al, bh) + _dot(ah, bl)


def _dot3_nt(a, b):
    dn = (((1,), (1,)), ((), ()))
    d = lambda x, y: lax.dot_general(x, y, dn, preferred_element_type=F32)
    ah, al = _split(a)
    bh, bl = _split(b)
    return d(ah, bh) + d(al, bh) + d(ah, bl)


def _sigmoid(x):
    return 0.5 * jnp.tanh(0.5 * x) + 0.5


def _gelu(x):
    return 0.5 * x * (1.0 + jnp.tanh(0.7978845608028654 * (x + 0.044715 * (x * x * x))))


def _rms_mod(x, g, shift, scale):
    ms = jnp.mean(x * x, axis=-1, keepdims=True)
    y = x * lax.rsqrt(ms + EPS) * g
    return y * (1.0 + scale) + shift


def _layernorm(x, g, b):
    xc = x - jnp.mean(x, axis=-1, keepdims=True)
    var = jnp.mean(xc * xc, axis=-1, keepdims=True)
    return xc * lax.rsqrt(var + EPS) * g + b


def _tile_mod_row(r):
    return jnp.where(r < P_TILES, 0, 1 + (r - P_TILES) // S_TILES)


def _tile_is_seq_start(r):
    return (r < P_TILES) | ((r - P_TILES) % S_TILES == 0)


def _tile_is_seq_end(r):
    return (r < P_TILES) | ((r - P_TILES) % S_TILES == S_TILES - 1)


MOD_TN = 512


def _adaln_kernel(cond_ref, w_ref, b_ref, o_ref):
    cond = cond_ref[...]
    s = cond * _sigmoid(cond)
    o_ref[...] = _dot3(s, w_ref[...]) + b_ref[...]


def _adaln(cond8, w_mod, b_mod):
    depth = w_mod.shape[0]
    out = pl.pallas_call(
        _adaln_kernel,
        out_shape=jax.ShapeDtypeStruct((depth, N_MOD, 6 * D), F32),
        grid=(depth, 6 * D // MOD_TN),
        in_specs=[
            pl.BlockSpec((N_MOD, D), lambda l, j: (0, 0)),
            pl.BlockSpec((None, D, MOD_TN), lambda l, j: (l, 0, j)),
            pl.BlockSpec((None, 1, MOD_TN), lambda l, j: (l, 0, j)),
        ],
        out_specs=pl.BlockSpec((None, N_MOD, MOD_TN), lambda l, j: (l, 0, j)),
        compiler_params=_cp(("arbitrary", "arbitrary")),
        name="adaln",
    )(cond8, w_mod, b_mod.reshape(depth, 1, 6 * D))
    return out.reshape(depth, N_MOD, 6, D)


PROJ_P_STEPS = N_P // TM_PROJ
PROJ_S_STEPS = T_SAMPLE // TM_PROJ


def _stream_specs(rows, p_steps, h):
    if isinstance(h, tuple):
        hp, hl = h
        first_latent = 0
    else:
        hp = hl = h
        first_latent = p_steps
    specs = [
        pl.BlockSpec((rows, D), lambda i, *_: (jnp.minimum(i, p_steps - 1), 0)),
        pl.BlockSpec((rows, D), lambda i, *_: (jnp.maximum(i - p_steps, 0) + first_latent, 0)),
    ]
    return specs, (hp, hl)


def _inproj_kernel(xp_ref, xl_ref, g_ref, m_ref, w_ref, o_ref):
    x = jnp.where(pl.program_id(0) < PROJ_P_STEPS, xp_ref[...], xl_ref[...])
    hn = _rms_mod(x, g_ref[...], m_ref[0:1, :], m_ref[1:2, :])
    o_ref[...] = _dot(hn.astype(BF16), w_ref[...])


def _inproj(h, g, mod, w_bf16):
    n_out = w_bf16.shape[1]

    def mod_row(i):
        return jnp.where(i < PROJ_P_STEPS, 0, 1 + (i - PROJ_P_STEPS) // PROJ_S_STEPS)

    h_specs, h_args = _stream_specs(TM_PROJ, PROJ_P_STEPS, h)
    return pl.pallas_call(
        _inproj_kernel,
        out_shape=jax.ShapeDtypeStruct((N, n_out), F32),
        grid=(N // TM_PROJ,),
        in_specs=h_specs + [
            pl.BlockSpec((1, D), lambda i: (0, 0)),
            pl.BlockSpec((None, 6, D), lambda i: (mod_row(i), 0, 0)),
            pl.BlockSpec((D, n_out), lambda i: (0, 0)),
        ],
        out_specs=pl.BlockSpec((TM_PROJ, n_out), lambda i: (i, 0)),
        compiler_params=_cp(("arbitrary",), VMEM_LIMIT),
        name="inproj",
    )(*h_args, g.reshape(1, D), mod, w_bf16)


LRU_HALO = 8
SCAN_ROWS = 8


def _lru_tile(d, s):
    return jnp.where(d == 0, s, N_TILES - 1 - s)


def _lru_kernel(x_ref, prev_ref, next_ref, cw_ref, cb_ref, wg_ref, br_ref, bi_ref, lam_ref, h0_ref,
                o_ref, state_ref, ext_ref, a_ref, carry_ref):
    d = pl.program_id(0)
    r = _lru_tile(d, pl.program_id(1))
    start = _tile_is_seq_start(r)
    end = _tile_is_seq_end(r)

    ext_ref[0:LRU_HALO, :] = jnp.where(start, 0.0, prev_ref[...])
    ext_ref[LRU_HALO:LRU_HALO + TILE, :] = x_ref[...]
    ext_ref[LRU_HALO + TILE:, :] = jnp.where(end, 0.0, next_ref[...])
    xc = cb_ref[...] + jnp.zeros((TILE, LRU_W), F32)
    for k in range(LRU_K):
        off = LRU_HALO - LRU_LEFT + k
        xc = xc + ext_ref[off:off + TILE, :] * cw_ref[k:k + 1, :]

    xcb = xc.astype(BF16)
    pre_r, pre_i = [], []
    for blk in range(LRU_W // LRU_BLK):
        g = _dot(xcb[:, blk * LRU_BLK:(blk + 1) * LRU_BLK], wg_ref[blk])
        pre_r.append(g[:, :LRU_BLK])
        pre_i.append(g[:, LRU_BLK:])
    gate_r = _sigmoid(jnp.concatenate(pre_r, axis=1) + br_ref[...])
    gate_i = _sigmoid(jnp.concatenate(pre_i, axis=1) + bi_ref[...])
    neg_lam = -lam_ref[...]
    softplus = jnp.maximum(neg_lam, 0.0) + jnp.log1p(jnp.exp(-jnp.abs(neg_lam)))
    log_a = (-LRU_C) * gate_r * softplus
    a = jnp.exp(log_a)
    a_ref[...] = a
    o_ref[...] = jnp.sqrt(-jnp.tanh(log_a) * (a * a + 1.0)) * (gate_i * xc)

    fresh = jnp.where(d == 0, start, end)
    h_init = jnp.where(fresh, h0_ref[pl.ds(_tile_mod_row(r), 1), :], carry_ref[...])

    row = lax.broadcasted_iota(jnp.int32, (SCAN_ROWS, LRU_W), 0)

    def scan_group(g, h, reverse):
        rows = pl.ds(pl.multiple_of(g * SCAN_ROWS, SCAN_ROWS), SCAN_ROWS)
        a = a_ref[rows, :]
        b = o_ref[rows, :]
        for s in (1, 2, 4):
            shift = SCAN_ROWS - s if reverse else s
            inside = (row < SCAN_ROWS - s) if reverse else (row >= s)
            b = jnp.where(inside, a * pltpu.roll(b, shift, 0) + b, b)
            a = jnp.where(inside, a * pltpu.roll(a, shift, 0), a)
        hs = a * h + b
        o_ref[rows, :] = hs
        return hs[0:1, :] if reverse else hs[SCAN_ROWS - 1:SCAN_ROWS, :]

    n_groups = TILE // SCAN_ROWS

    @pl.when(d == 0)
    def _():
        carry_ref[...] = lax.fori_loop(0, n_groups, lambda g, h: scan_group(g, h, False), h_init, unroll=4)

    @pl.when(d == 1)
    def _():
        carry_ref[...] = lax.fori_loop(0, n_groups, lambda g, h: scan_group(n_groups - 1 - g, h, True),
                                       h_init, unroll=4)

    h_last = carry_ref[...]

    @pl.when(r < P_TILES)
    def _():
        state_ref[...] = h_last


def _lru(proj0, conv_w, conv_b, wg, b_r, b_i, lam, h0):
    n_halo_blocks = N // LRU_HALO
    per_tile = TILE // LRU_HALO
    tile = lambda d, s: _lru_tile(d, s)
    return pl.pallas_call(
        _lru_kernel,
        out_shape=(jax.ShapeDtypeStruct((2, N, LRU_W), F32),
                   jax.ShapeDtypeStruct((2, N_PROMPT_SEQ, 1, LRU_W), F32)),
        grid=(2, N_TILES),
        in_specs=[
            pl.BlockSpec((TILE, LRU_W), lambda d, s: (tile(d, s), 0)),
            pl.BlockSpec((LRU_HALO, LRU_W), lambda d, s: (jnp.maximum(tile(d, s) * per_tile - 1, 0), 0)),
            pl.BlockSpec((LRU_HALO, LRU_W),
                         lambda d, s: (jnp.minimum((tile(d, s) + 1) * per_tile, n_halo_blocks - 1), 0)),
            pl.BlockSpec((LRU_K, LRU_W), lambda d, s: (0, 0)),
            pl.BlockSpec((1, LRU_W), lambda d, s: (0, 0)),
            pl.BlockSpec((None, LRU_W // LRU_BLK, LRU_BLK, 2 * LRU_BLK), lambda d, s: (d, 0, 0, 0)),
            pl.BlockSpec((None, 1, LRU_W), lambda d, s: (d, 0, 0)),
            pl.BlockSpec((None, 1, LRU_W), lambda d, s: (d, 0, 0)),
            pl.BlockSpec((None, 1, LRU_W), lambda d, s: (d, 0, 0)),
            pl.BlockSpec((None, N_MOD, LRU_W), lambda d, s: (d, 0, 0)),
        ],
        out_specs=[
            pl.BlockSpec((None, TILE, LRU_W), lambda d, s: (d, tile(d, s), 0)),
            pl.BlockSpec((None, None, 1, LRU_W), lambda d, s: (d, jnp.minimum(tile(d, s), P_TILES - 1), 0, 0)),
        ],
        scratch_shapes=[
            pltpu.VMEM((TILE + 2 * LRU_HALO, LRU_W), F32),
            pltpu.VMEM((TILE, LRU_W), F32),
            pltpu.VMEM((1, LRU_W), F32),
        ],
        compiler_params=_cp(("arbitrary", "arbitrary"), VMEM_LIMIT),
        name="rglru_scan",
    )(proj0, proj0, proj0, conv_w, conv_b.reshape(1, LRU_W), wg,
      b_r.reshape(2, 1, LRU_W), b_i.reshape(2, 1, LRU_W), lam.reshape(2, 1, LRU_W), h0)


def _dft_tables(n, scale):
    k = np.arange(n, dtype=np.int64)
    ang = 2.0 * np.pi * ((k[:, None] * k[None, :]) % n).astype(np.float64) / n
    return np.cos(ang) * scale, np.sin(ang) * scale


def _channel_tables():
    c, s = _dft_tables(FN_GD, FN_GD ** -0.5)
    eye = np.eye(FN_G)
    return (jnp.asarray(np.kron(eye, c), BF16), jnp.asarray(np.kron(eye, s), BF16))


def _time_tables(t_len):
    c, s = _dft_tables(t_len, t_len ** -0.5)
    return jnp.asarray(c, BF16), jnp.asarray(s, BF16)


FN_SEQ_PER_STEP = 4


def _fourier_prompt_kernel(z_ref, cc_ref, sc_ref, ct_ref, st_ref, o_ref):
    z = z_ref[...].astype(BF16)
    zc = _dot(z, cc_ref[...]).astype(BF16)
    zs = _dot(z, sc_ref[...]).astype(BF16)
    for b in range(FN_SEQ_PER_STEP):
        rows = slice(b * T_PROMPT, (b + 1) * T_PROMPT)
        o_ref[rows, :] = _dot(ct_ref[...], zc[rows, :]) - _dot(st_ref[...], zs[rows, :])


def _fourier_prompt(proj0):
    cc, sc = _channel_tables()
    ct, st = _time_tables(T_PROMPT)
    const = lambda b: (0, 0)
    return pl.pallas_call(
        _fourier_prompt_kernel,
        out_shape=jax.ShapeDtypeStruct((N_P, FN_W), F32),
        grid=(N_PROMPT_SEQ // FN_SEQ_PER_STEP,),
        in_specs=[
            pl.BlockSpec((FN_SEQ_PER_STEP * T_PROMPT, FN_W), lambda b: (b, 2 * LRU_W // FN_W)),
            pl.BlockSpec((FN_W, FN_W), const),
            pl.BlockSpec((FN_W, FN_W), const),
            pl.BlockSpec((T_PROMPT, T_PROMPT), const),
            pl.BlockSpec((T_PROMPT, T_PROMPT), const),
        ],
        out_specs=pl.BlockSpec((FN_SEQ_PER_STEP * T_PROMPT, FN_W), lambda b: (b, 0)),
        compiler_params=_cp(("arbitrary",)),
        name="fourier_prompt",
    )(proj0, cc, sc, ct, st)


def _fourier_sample_kernel(z_ref, cc_ref, sc_ref, ct_ref, st_ref, o_ref, zc_ref, zs_ref):
    @pl.when(pl.program_id(1) == 0)
    def _():
        z = z_ref[...].astype(BF16)
        zc_ref[...] = _dot(z, cc_ref[...]).astype(BF16)
        zs_ref[...] = _dot(z, sc_ref[...]).astype(BF16)

    o_ref[...] = _dot(ct_ref[...], zc_ref[...]) - _dot(st_ref[...], zs_ref[...])


def _fourier_sample(proj0):
    cc, sc = _channel_tables()
    ct, st = _time_tables(T_SAMPLE)
    const = lambda b, i: (0, 0)
    first_seq_block = N_P // T_SAMPLE
    return pl.pallas_call(
        _fourier_sample_kernel,
        out_shape=jax.ShapeDtypeStruct((N_S, FN_W), F32),
        grid=(N_SAMPLE_SEQ, S_TILES),
        in_specs=[
            pl.BlockSpec((T_SAMPLE, FN_W), lambda b, i: (first_seq_block + b, 2 * LRU_W // FN_W)),
            pl.BlockSpec((FN_W, FN_W), const),
            pl.BlockSpec((FN_W, FN_W), const),
            pl.BlockSpec((TILE, T_SAMPLE), lambda b, i: (i, 0)),
            pl.BlockSpec((TILE, T_SAMPLE), lambda b, i: (i, 0)),
        ],
        out_specs=pl.BlockSpec((TILE, FN_W), lambda b, i: (b * S_TILES + i, 0)),
        scratch_shapes=[pltpu.VMEM((T_SAMPLE, FN_W), BF16), pltpu.VMEM((T_SAMPLE, FN_W), BF16)],
        compiler_params=_cp(("arbitrary", "arbitrary"), VMEM_LIMIT),
        name="fourier_sample",
    )(proj0, cc, sc, ct, st)


def _epilogue(mix, hp_ref, hl_ref, m_ref, gf_ref, wr_ref, brt_ref, tri_ref,
              hout_ref, fn_ref, tope_ref, gate_ref, rank_ref, cnt_ref, run_ref):
    h = jnp.where(pl.program_id(0) < P_TILES, hp_ref[...], hl_ref[...])
    h_new = h + m_ref[2:3, :] * mix
    hout_ref[...] = h_new
    fn = _rms_mod(h_new, gf_ref[...], m_ref[3:4, :], m_ref[4:5, :])
    hi = lax.bitcast_convert_type(fn[:, :D // 2].astype(BF16).astype(F32), jnp.uint32)
    lo = lax.bitcast_convert_type(fn[:, D // 2:].astype(BF16).astype(F32), jnp.uint32)
    packed = hi | (lo >> 16)
    for q in range(PACK_ROWS):
        fn_ref[pl.ds(q, TILE, stride=PACK_ROWS), :] = packed[:, q * 128:(q + 1) * 128]
    logits = _dot3_nt(wr_ref[...], fn) + brt_ref[...]
    iota = lax.broadcasted_iota(jnp.int32, logits.shape, 0)
    vals, idxs = [], []
    for _ in range(TOP_K):
        m = jnp.max(logits, axis=0, keepdims=True)
        idx = jnp.min(jnp.where(logits == m, iota, N_EXP), axis=0, keepdims=True)
        vals.append(m)
        idxs.append(idx)
        logits = jnp.where(iota == idx, -jnp.inf, logits)
    exps = [jnp.exp(v - vals[0]) for v in vals]
    denom = exps[0] + exps[1] + exps[2] + exps[3]
    for k in range(TOP_K):
        tope_ref[k:k + 1, :] = idxs[k]
        gate_ref[k:k + 1, :] = exps[k] / denom

    @pl.when(pl.program_id(0) == 0)
    def _():
        run_ref[...] = jnp.zeros_like(run_ref)

    run = run_ref[...]
    onehots = [jnp.where(iota == idxs[k], 1.0, 0.0) for k in range(TOP_K)]
    incl_all = _dot(jnp.concatenate(onehots, axis=0).astype(BF16), tri_ref[...])
    for k in range(TOP_K):
        incl = incl_all[k * N_EXP:(k + 1) * N_EXP, :]
        rank = jnp.sum(onehots[k] * (incl - 1.0 + run), axis=0, keepdims=True)
        rank_ref[k:k + 1, :] = rank.astype(jnp.int32)
        run = run + incl[:, TILE - 1:TILE]
    run_ref[...] = run
    cnt_ref[...] = run.astype(jnp.int32)


_EPI_OUT_SHAPES = (
    jax.ShapeDtypeStruct((N, D), F32),
    jax.ShapeDtypeStruct((N * PACK_ROWS, 128), jnp.uint32),
    jax.ShapeDtypeStruct((TOP_K, N), jnp.int32),
    jax.ShapeDtypeStruct((TOP_K, N), F32),
    jax.ShapeDtypeStruct((TOP_K, N), jnp.int32),
    jax.ShapeDtypeStruct((N_EXP, 1), jnp.int32),
)
_EPI_SCRATCH = [pltpu.VMEM((N_EXP, 1), F32)]


def _epi_operands(h, mod, g_ffn, w_router_t, b_router):
    tri = jnp.asarray(np.triu(np.ones((TILE, TILE), np.float32)), BF16)
    return (*_stream_specs(TILE, P_TILES, h)[1], mod, g_ffn.reshape(1, D), w_router_t,
            b_router.reshape(N_EXP, 1), tri)


def _epi_in_specs(h):
    h_specs, _ = _stream_specs(TILE, P_TILES, h)
    return h_specs + [
        pl.BlockSpec((None, 6, D), lambda r: (_tile_mod_row(r), 0, 0)),
        pl.BlockSpec((1, D), lambda r: (0, 0)),
        pl.BlockSpec((N_EXP, D), lambda r: (0, 0)),
        pl.BlockSpec((N_EXP, 1), lambda r: (0, 0)),
        pl.BlockSpec((TILE, TILE), lambda r: (0, 0)),
    ]


def _epi_out_specs():
    return [
        pl.BlockSpec((TILE, D), lambda r: (r, 0)),
        pl.BlockSpec((TILE * PACK_ROWS, 128), lambda r: (r, 0)),
        pl.BlockSpec((TOP_K, TILE), lambda r: (0, r)),
        pl.BlockSpec((TOP_K, TILE), lambda r: (0, r)),
        pl.BlockSpec((TOP_K, TILE), lambda r: (0, r)),
        pl.BlockSpec((N_EXP, 1), lambda r: (0, 0)),
    ]


def _post0_kernel(hs_ref, xg_ref, yfp_ref, yfl_ref, wo_ref, *epilogue_refs):
    y_rec = (hs_ref[0] + hs_ref[1]) * _gelu(xg_ref[...])
    y_four = jnp.where(pl.program_id(0) < P_TILES, yfp_ref[...], yfl_ref[...])
    mix = (_dot(y_rec.astype(BF16), wo_ref[0:LRU_W, :])
           + _dot(y_four.astype(BF16), wo_ref[LRU_W:, :]))
    _epilogue(mix, *epilogue_refs)


def _post0(hs, proj0, yf_prompt, yf_latent, w_out_bf16, h, mod, g_ffn, w_router_t, b_router):
    return pl.pallas_call(
        _post0_kernel,
        out_shape=_EPI_OUT_SHAPES,
        grid=(N_TILES,),
        in_specs=[
            pl.BlockSpec((2, TILE, LRU_W), lambda r: (0, r, 0)),
            pl.BlockSpec((TILE, LRU_W), lambda r: (r, 1)),
            pl.BlockSpec((TILE, FN_W), lambda r: (jnp.minimum(r, P_TILES - 1), 0)),
            pl.BlockSpec((TILE, FN_W), lambda r: (jnp.maximum(r - P_TILES, 0), 0)),
            pl.BlockSpec((D, D), lambda r: (0, 0)),
        ] + _epi_in_specs(h),
        out_specs=_epi_out_specs(),
        scratch_shapes=_EPI_SCRATCH,
        compiler_params=_cp(("arbitrary",), VMEM_LIMIT),
        name="post_rglru_fourier",
    )(hs, proj0, yf_prompt, yf_latent, w_out_bf16, *_epi_operands(h, mod, g_ffn, w_router_t, b_router))


def _glu(x):
    return x[:, :CONV_W] * _sigmoid(x[:, CONV_W:])


def _post1_kernel(p_ref, prev_ref, next_ref, lng_ref, lnb_ref, ws_ref, bs_ref, dww_ref, dwb_ref,
                  clg_ref, clb_ref, wo_ref, *rest):
    epilogue_refs, (ext_ref, shift_ref, u_ref, v_ref, mix_ref) = rest[:-5], rest[-5:]
    r = pl.program_id(0)

    for c in range(TILE // ROW_PIECE):
        rows = slice(c * ROW_PIECE, (c + 1) * ROW_PIECE)
        z = _gelu(p_ref[rows, 0:2 * SGU_W])
        u_ref[rows, :] = z[:, :SGU_W]
        v_ref[rows, :] = _layernorm(z[:, SGU_W:], lng_ref[...], lnb_ref[...]).astype(BF16)
        ext_ref[CONV_HALO + c * ROW_PIECE:CONV_HALO + (c + 1) * ROW_PIECE, :] = _glu(p_ref[rows, 2 * SGU_W:])
    ext_ref[0:CONV_HALO, :] = jnp.where(_tile_is_seq_start(r), 0.0, _glu(prev_ref[...]))
    ext_ref[CONV_HALO + TILE:, :] = jnp.where(_tile_is_seq_end(r), 0.0, _glu(next_ref[...]))

    for n in range(TILE // CHUNK):
        for g in range(SGU_G):
            rows, cols = slice(n * CHUNK, (n + 1) * CHUNK), slice(g * CHUNK, (g + 1) * CHUNK)
            s = _dot(ws_ref[g], v_ref[rows, cols]) + bs_ref[:, cols]
            mix_ref[rows, cols] = (u_ref[rows, cols] * s).astype(BF16)

    n_shift_rows = TILE + 2 * CONV_HALO - 8
    for phase in range(8):
        shift_ref[phase] = ext_ref[phase:phase + n_shift_rows, :]
    for c in range(TILE // CONV_ROWS):
        acc = dwb_ref[...] + jnp.zeros((CONV_ROWS // 8, 8, CONV_W), F32)
        for k in range(CONV_K):
            off = CONV_HALO - CONV_PAD + k
            start = off // 8 * 8 + c * CONV_ROWS
            rows = shift_ref[off % 8, start:start + CONV_ROWS, :].reshape(CONV_ROWS // 8, 8, CONV_W)
            acc = acc + rows * dww_ref[k]
        ln = _layernorm(acc.reshape(CONV_ROWS, CONV_W), clg_ref[...], clb_ref[...])
        mix_ref[c * CONV_ROWS:(c + 1) * CONV_ROWS, SGU_W:] = (ln * _sigmoid(ln)).astype(BF16)

    _epilogue(_dot(mix_ref[...], wo_ref[...]), *epilogue_refs)


def _post1(proj1, ln_g, ln_b, ws_bf16, bs_full, dw_w, dw_b, cln_g, cln_b, w_out_bf16,
           h, mod, g_ffn, w_router_t, b_router):
    per_tile = TILE // CONV_HALO
    n_halo_blocks = N // CONV_HALO
    const2 = lambda r: (0, 0)
    row = lambda a: a.reshape(1, -1)
    return pl.pallas_call(
        _post1_kernel,
        out_shape=_EPI_OUT_SHAPES,
        grid=(N_TILES,),
        in_specs=[
            pl.BlockSpec((TILE, IN1), lambda r: (r, 0)),
            pl.BlockSpec((CONV_HALO, 2 * CONV_W), lambda r: (jnp.maximum(r * per_tile - 1, 0), 1)),
            pl.BlockSpec((CONV_HALO, 2 * CONV_W),
                         lambda r: (jnp.minimum((r + 1) * per_tile, n_halo_blocks - 1), 1)),
            pl.BlockSpec((1, SGU_W), const2),
            pl.BlockSpec((1, SGU_W), const2),
            pl.BlockSpec((SGU_G, CHUNK, CHUNK), lambda r: (0, 0, 0)),
            pl.BlockSpec((CHUNK, SGU_W), const2),
            pl.BlockSpec((CONV_K, 8, CONV_W), lambda r: (0, 0, 0)),
            pl.BlockSpec((1, CONV_W), const2),
            pl.BlockSpec((1, CONV_W), const2),
            pl.BlockSpec((1, CONV_W), const2),
            pl.BlockSpec((D, D), const2),
        ] + _epi_in_specs(h),
        out_specs=_epi_out_specs(),
        scratch_shapes=_EPI_SCRATCH + [
            pltpu.VMEM((TILE + 2 * CONV_HALO, CONV_W), F32),
            pltpu.VMEM((8, TILE + 2 * CONV_HALO - 8, CONV_W), F32),
            pltpu.VMEM((TILE, SGU_W), F32),
            pltpu.VMEM((TILE, SGU_W), BF16),
            pltpu.VMEM((TILE, D), BF16),
        ],
        compiler_params=_cp(("arbitrary",), VMEM_LIMIT),
        name="post_sgu_conformer",
    )(proj1, proj1, proj1, row(ln_g), row(ln_b), ws_bf16, bs_full,
      jnp.broadcast_to(dw_w[:, None, :], (CONV_K, 8, CONV_W)), row(dw_b), row(cln_g), row(cln_b),
      w_out_bf16, *_epi_operands(h, mod, g_ffn, w_router_t, b_router))


PAD_BASE = N_ASSIGN
PLACEHOLDER_BASE = PAD_BASE + 2 * MOE_BLOCK
Y_ROWS = PLACEHOLDER_BASE + 2 * MOE_BLOCK


def _inverse_kernel(pos_ref, pend_ref, out_ref):
    def fill_block(start):
        def fill(j, carry):
            s = start + j
            out_ref[s] = PAD_BASE + (s & (2 * MOE_BLOCK - 1))
            return carry
        lax.fori_loop(0, MOE_BLOCK, fill, 0, unroll=16)

    def per_expert(e, carry):
        fill_block(jnp.maximum(pend_ref[e] - MOE_BLOCK, 0))
        return carry

    def idle_block(b, carry):
        fill_block(b * MOE_BLOCK)
        return carry

    lax.fori_loop(0, N_EXP, per_expert, 0)
    lax.fori_loop(pend_ref[N_EXP - 1] // MOE_BLOCK, N_BLOCKS, idle_block, 0)

    def body(t, carry):
        for k in range(TOP_K):
            a = t * TOP_K + k
            out_ref[pos_ref[a]] = a
        return carry

    lax.fori_loop(0, N, body, 0, unroll=8)


def _inverse(pos_flat, pad_end):
    smem = pl.BlockSpec(memory_space=pltpu.SMEM)
    return pl.pallas_call(
        _inverse_kernel,
        out_shape=jax.ShapeDtypeStruct((N_SLOTS,), jnp.int32),
        in_specs=[smem, smem],
        out_specs=smem,
        name="moe_inverse",
    )(pos_flat, pad_end)


def _routing_tables(top_e, rank, counts):
    counts = counts.reshape(N_EXP)
    padded = (counts + MOE_BLOCK - 1) // MOE_BLOCK * MOE_BLOCK
    pad_end = jnp.cumsum(padded).astype(jnp.int32)
    pad_start = pad_end - padded
    experts = jnp.arange(N_EXP, dtype=jnp.int32)[:, None, None]
    pos = rank + jnp.sum(jnp.where(top_e[None] == experts, pad_start[:, None, None], 0), axis=0)
    pos = pos.astype(jnp.int32).T.reshape(-1)
    n_used = pad_end[-1] // MOE_BLOCK
    blk = jnp.minimum(jnp.arange(N_BLOCKS, dtype=jnp.int32), n_used - 1)
    block_e = jnp.sum(pad_end[None, :] <= (blk * MOE_BLOCK)[:, None], axis=1).astype(jnp.int32)
    slot_a = _inverse(pos, pad_end)
    tok = jnp.minimum(slot_a >> 2, N - 1)
    gather_key = (tok >> 1) * (2 * PACK_PAIR) + (tok & 1)
    out_row = jnp.where(slot_a >= PAD_BASE, slot_a, (slot_a & (TOP_K - 1)) * N + (slot_a >> 2))
    placeholder_rows = PLACEHOLDER_BASE + jnp.arange(MOE_BLOCK, dtype=jnp.int32)
    out_row8 = jnp.concatenate([placeholder_rows, out_row]) * ROW_TILE
    ids = jnp.arange(N_EXP, dtype=jnp.int32)
    later_used = (ids[None, :] > ids[:, None]) & (counts[None, :] > 0)
    next_expert = jnp.min(jnp.where(later_used, ids[None, :], N_EXP), axis=1).astype(jnp.int32)
    return gather_key, out_row8, block_e, n_used.reshape(1), next_expert


def _moe_kernel(be_ref, nu_ref, gk_ref, row8_ref, nxt_ref, x_hbm, w1_hbm, b1g_ref, b1l_ref, w2_hbm, b2_ref, perm_ref,
                y_hbm, xs, xbuf, w1f, w2f, w1s, w2s, sem, ybuf, osem, *, layer):
    i = pl.program_id(0)
    n_used = nu_ref[0]
    slot = i % 2
    e = be_ref[i]
    e_prev = be_ref[jnp.maximum(i - 1, 0)]
    low_half = lax.broadcasted_iota(jnp.int32, (PACK_PAIR, 128), 0) < PACK_ROWS

    def weight_copies(expert):
        return (pltpu.make_async_copy(w1_hbm.at[layer, expert], w1f, sem.at[1]),
                pltpu.make_async_copy(w2_hbm.at[layer, expert], w2f, sem.at[2]))

    def token_tile(key):
        tile = xs[pl.ds(pl.multiple_of(key >> 1, PACK_PAIR), PACK_PAIR), :]
        return tile, pltpu.roll(tile, PACK_ROWS, 0), key & 1

    def gather_pair(blk, jj, dst_slot):
        t0, r0, half0 = token_tile(gk_ref[blk * MOE_BLOCK + 2 * jj])
        t1, r1, half1 = token_tile(gk_ref[blk * MOE_BLOCK + 2 * jj + 1])
        lower = jnp.where(half0 == 0, t0, r0)
        upper = jnp.where(half1 == 1, t1, r1)
        start = jj * PACK_PAIR if isinstance(jj, int) else pl.multiple_of(jj * PACK_PAIR, PACK_PAIR)
        xbuf[dst_slot, pl.ds(start, PACK_PAIR), :] = jnp.where(low_half, lower, upper)

    def out_copy(src_slot, j, row8):
        start = j * ROW_TILE if isinstance(j, int) else pl.multiple_of(j * ROW_TILE, ROW_TILE)
        dst = y_hbm.at[pl.ds(pl.multiple_of(row8, ROW_TILE), ROW_TILE), :]
        return pltpu.make_async_copy(ybuf.at[src_slot, pl.ds(start, ROW_TILE), :], dst, osem.at[src_slot])

    def out_buffer_wait(src_slot):
        pltpu.make_async_copy(ybuf.at[src_slot], y_hbm.at[pl.ds(0, MOE_BLOCK * ROW_TILE), :],
                              osem.at[src_slot]).wait()

    @pl.when(i == 0)
    def _():
        xs_copy = pltpu.make_async_copy(x_hbm, xs, sem.at[0])
        xs_copy.start()
        for c in weight_copies(e):
            c.start()
        ybuf[...] = jnp.zeros_like(ybuf)

        def zero_rows(first_row):
            def body(j, carry):
                out_copy(0, j, (first_row + j) * ROW_TILE).start()
                return carry
            lax.fori_loop(0, MOE_BLOCK, body, 0, unroll=8)

        zero_rows(PAD_BASE)
        zero_rows(PAD_BASE + MOE_BLOCK)
        out_buffer_wait(0)
        out_buffer_wait(0)
        zero_rows(PLACEHOLDER_BASE + MOE_BLOCK)
        xs_copy.wait()

        def body(jj, carry):
            gather_pair(0, jj, 0)
            return carry
        lax.fori_loop(0, MOE_BLOCK // 2, body, 0, unroll=4)

    @pl.when((i < n_used) & ((i == 0) | (e != e_prev)))
    def _():
        for c in weight_copies(e):
            c.wait()
        for c in range(2 * D_FF // 256):
            wc = w1f[:, c * 256:(c + 1) * 256].astype(BF16)
            w1s[:, c * 256:(c + 1) * 256] = _dot(wc, perm_ref[...]).astype(BF16)
        w2s[...] = w2f[...].astype(BF16)
        e_next = nxt_ref[e]

        @pl.when(e_next < N_EXP)
        def _():
            for c in weight_copies(e_next):
                c.start()

    def block(prefetch_next):
        for j in range(MOE_BLOCK):
            out_copy(1 - slot, j, row8_ref[i * MOE_BLOCK + j]).start()
        halves = [[], []]
        for q in range(PACK_ROWS):
            words = xbuf[slot, pl.ds(q, MOE_BLOCK, stride=PACK_ROWS), :]
            halves[0].append(lax.bitcast_convert_type(words & jnp.uint32(0xFFFF0000), F32).astype(BF16))
            halves[1].append(lax.bitcast_convert_type(words << 16, F32).astype(BF16))
        x = jnp.concatenate(halves[0] + halves[1], axis=1)
        if prefetch_next:
            for jj in range(MOE_BLOCK // 2):
                gather_pair(i + 1, jj, 1 - slot)
        hid = _dot(x, w1s[...])
        acts = []
        for c in range(D_FF // 128):
            h_glu = hid[:, c * 256:c * 256 + 128] + b1g_ref[:, c * 128:(c + 1) * 128]
            h_lin = hid[:, c * 256 + 128:(c + 1) * 256] + b1l_ref[:, c * 128:(c + 1) * 128]
            h_glu = jnp.minimum(h_glu, LIMIT)
            h_lin = jnp.clip(h_lin, -LIMIT, LIMIT)
            acts.append((h_glu * _sigmoid(ALPHA * h_glu) * (h_lin + 1.0)).astype(BF16))
        y = _dot(jnp.concatenate(acts, axis=1), w2s[...]) + b2_ref[...]
        out_buffer_wait(slot)
        for s in range(ROW_TILE):
            ybuf[slot, pl.ds(s, MOE_BLOCK, stride=ROW_TILE), :] = y[:, s * 128:(s + 1) * 128]
        if not prefetch_next:
            def send(j, carry):
                out_copy(slot, j, row8_ref[(i + 1) * MOE_BLOCK + j]).start()
                return carry
            lax.fori_loop(0, MOE_BLOCK, send, 0, unroll=8)
            out_buffer_wait(1 - slot)
            out_buffer_wait(slot)

    @pl.when(i + 1 < n_used)
    def _():
        block(True)

    @pl.when(i + 1 == n_used)
    def _():
        block(False)


def _deinterleave_matrix():
    p = np.zeros((256, 256), np.float32)
    m = np.arange(128)
    p[2 * m, m] = 1.0
    p[2 * m + 1, 128 + m] = 1.0
    return jnp.asarray(p, BF16)


def _moe_experts(layer, fn_packed, gather_key, out_row8, block_e, n_used, next_expert, w1, b1, w2, b2):
    b1g = b1[0][layer].reshape(N_EXP, 1, D_FF)
    b1l = b1[1][layer].reshape(N_EXP, 1, D_FF)
    ex = lambda i, be, *_: (be[i], 0, 0)
    grid_spec = pltpu.PrefetchScalarGridSpec(
        num_scalar_prefetch=5,
        grid=(N_BLOCKS,),
        in_specs=[
            pl.BlockSpec(memory_space=pl.ANY),
            pl.BlockSpec(memory_space=pl.ANY),
            pl.BlockSpec((None, 1, D_FF), ex),
            pl.BlockSpec((None, 1, D_FF), ex),
            pl.BlockSpec(memory_space=pl.ANY),
            pl.BlockSpec((None, 1, D), ex),
            pl.BlockSpec((256, 256), lambda i, *_: (0, 0)),
        ],
        out_specs=pl.BlockSpec(memory_space=pl.ANY),
        scratch_shapes=[
            pltpu.VMEM((N * PACK_ROWS, 128), jnp.uint32),
            pltpu.VMEM((2, MOE_BLOCK * PACK_ROWS, 128), jnp.uint32),
            pltpu.VMEM((D, 2 * D_FF), F32),
            pltpu.VMEM((D_FF, D), F32),
            pltpu.VMEM((D, 2 * D_FF), BF16),
            pltpu.VMEM((D_FF, D), BF16),
            pltpu.SemaphoreType.DMA((3,)),
            pltpu.VMEM((2, MOE_BLOCK * ROW_TILE, 128), F32),
            pltpu.SemaphoreType.DMA((2,)),
        ],
    )
    return pl.pallas_call(
        functools.partial(_moe_kernel, layer=layer),
        out_shape=jax.ShapeDtypeStruct((Y_ROWS * ROW_TILE, 128), F32),
        grid_spec=grid_spec,
        compiler_params=_cp(("arbitrary",), MOE_VMEM_LIMIT),
        name="moe_experts",
    )(block_e, n_used, gather_key, out_row8, next_expert, fn_packed, w1, b1g, b1l, w2, b2[layer].reshape(N_EXP, 1, D),
      _deinterleave_matrix())


def _moe_residual(y_refs, g_ref, h_ref, m_ref, rows):
    g = g_ref[...]
    cols = []
    for s in range(ROW_TILE):
        y = y_refs[0][pl.ds(s, rows, stride=ROW_TILE), :] * g[:, 0:1]
        for k in range(1, TOP_K):
            y = y + y_refs[k][pl.ds(s, rows, stride=ROW_TILE), :] * g[:, k:k + 1]
        cols.append(y)
    return h_ref[...] + m_ref[5:6, :] * jnp.concatenate(cols, axis=1)


def _moe_stream_specs(rows, mod_row):
    steps = N // rows
    return [pl.BlockSpec((rows * ROW_TILE, 128), lambda i, k=k: (k * steps + i, 0)) for k in range(TOP_K)] + [
        pl.BlockSpec((rows, TOP_K), lambda i: (i, 0)),
        pl.BlockSpec((rows, D), lambda i: (i, 0)),
        pl.BlockSpec((None, 6, D), lambda i: (mod_row(i), 0, 0)),
    ]


def _combine_final_kernel(y0_ref, y1_ref, y2_ref, y3_ref, g_ref, h_ref, m_ref, fg_ref, yp_ref, yl_ref):
    i = pl.program_id(0)
    h_new = _moe_residual((y0_ref, y1_ref, y2_ref, y3_ref), g_ref, h_ref, m_ref, TILE)
    ms = jnp.mean(h_new * h_new, axis=-1, keepdims=True)
    y = h_new * lax.rsqrt(ms + EPS) * fg_ref[...]

    @pl.when(i < P_TILES)
    def _():
        yp_ref[...] = y

    @pl.when(i >= P_TILES)
    def _():
        yl_ref[...] = y


def _moe_combine_final(y_rows, gates_nt, h, mod, final_g):
    return pl.pallas_call(
        _combine_final_kernel,
        out_shape=(jax.ShapeDtypeStruct((N_P, D), F32), jax.ShapeDtypeStruct((N_S, D), F32)),
        grid=(N_TILES,),
        in_specs=_moe_stream_specs(TILE, _tile_mod_row) + [pl.BlockSpec((1, D), lambda i: (0, 0))],
        out_specs=[
            pl.BlockSpec((TILE, D), lambda i: (jnp.minimum(i, P_TILES - 1), 0)),
            pl.BlockSpec((TILE, D), lambda i: (jnp.maximum(i - P_TILES, 0), 0)),
        ],
        compiler_params=_cp(("arbitrary",), VMEM_LIMIT),
        name="moe_combine_final",
    )(y_rows, y_rows, y_rows, y_rows, gates_nt, h, mod, final_g.reshape(1, D))


COMBINE_ROWS = 512


def _combine_inproj_kernel(y0_ref, y1_ref, y2_ref, y3_ref, g_ref, h_ref, m_ref, gmix_ref, mnext_ref, w_ref,
                           hout_ref, proj_ref):
    h_new = _moe_residual((y0_ref, y1_ref, y2_ref, y3_ref), g_ref, h_ref, m_ref, COMBINE_ROWS)
    hout_ref[...] = h_new
    hn = _rms_mod(h_new, gmix_ref[...], mnext_ref[0:1, :], mnext_ref[1:2, :])
    proj_ref[...] = _dot(hn.astype(BF16), w_ref[...])


def _moe_combine_inproj(y_rows, gates_nt, h, mod, g_mix_next, mod_next, w_next_bf16):
    n_out = w_next_bf16.shape[1]
    p_steps, s_steps = N_P // COMBINE_ROWS, T_SAMPLE // COMBINE_ROWS

    def mod_row(i):
        return jnp.where(i < p_steps, 0, 1 + (i - p_steps) // s_steps)

    return pl.pallas_call(
        _combine_inproj_kernel,
        out_shape=(jax.ShapeDtypeStruct((N, D), F32), jax.ShapeDtypeStruct((N, n_out), F32)),
        grid=(N // COMBINE_ROWS,),
        in_specs=_moe_stream_specs(COMBINE_ROWS, mod_row) + [
            pl.BlockSpec((1, D), lambda i: (0, 0)),
            pl.BlockSpec((None, 6, D), lambda i: (mod_row(i), 0, 0)),
            pl.BlockSpec((D, n_out), lambda i: (0, 0)),
        ],
        out_specs=[
            pl.BlockSpec((COMBINE_ROWS, D), lambda i: (i, 0)),
            pl.BlockSpec((COMBINE_ROWS, n_out), lambda i: (i, 0)),
        ],
        compiler_params=_cp(("arbitrary",), VMEM_LIMIT),
        name="moe_combine_inproj",
    )(y_rows, y_rows, y_rows, y_rows, gates_nt, h, mod, g_mix_next.reshape(1, D), mod_next, w_next_bf16)


def _moe_expert_rows(layer, routed, w1, b1, w2, b2):
    fn_packed, top_e, gates, rank, counts = routed
    gather_key, out_row8, block_e, n_used, next_expert = _routing_tables(top_e, rank, counts)
    return _moe_experts(layer, fn_packed, gather_key, out_row8, block_e, n_used, next_expert, w1, b1, w2, b2)


def _gate_weights(w_r, w_i):
    per = LRU_BLK // LRU_HD
    eye = jnp.eye(per, dtype=F32)

    def blockdiag(w):
        w = w.reshape(2, LRU_HEADS // per, per, LRU_HD, LRU_HD)
        full = jnp.einsum("dgaij,ab->dgaibj", w, eye)
        return full.reshape(2, LRU_HEADS // per, LRU_BLK, LRU_BLK)

    return jnp.concatenate([blockdiag(w_r), blockdiag(w_i)], axis=-1).astype(BF16)


def kernel(x_prompt, x_sample, state_rglru, c, c_ctx, norm_mix_g, norm_ffn_g, w_mod, b_mod, w_in0, lru_conv_w, lru_conv_b, lru_w_r, lru_b_r, lru_w_i, lru_b_i, lru_lambda, w_out0, w_in1, sgu_ln_g, sgu_ln_b, sgu_w_s, sgu_b_s, conv_dw_w, conv_dw_b, conv_ln_g, conv_ln_b, w_out1, w_router, b_router, w1, b1, w2, b2, final_norm_g):
    h = (x_prompt.reshape(N_P, D), x_sample.reshape(N_S, D))
    cond8 = jnp.concatenate([c_ctx[None, :], c, jnp.zeros((N_MOD - 1 - N_SAMPLE_SEQ, D), F32)], axis=0)
    b1 = (b1[:, :, 0::2], b1[:, :, 1::2])
    mod = _adaln(cond8, w_mod, b_mod)

    proj0 = _inproj(h, norm_mix_g[0], mod[0], w_in0[0].astype(BF16))
    st = state_rglru[:, 0].astype(F32)
    h0 = jnp.zeros((2, N_MOD, LRU_W), F32).at[:, 1:1 + N_SAMPLE_SEQ].set(jnp.swapaxes(st, 0, 1))
    hs, ctx_state = _lru(proj0, lru_conv_w[0], lru_conv_b[0], _gate_weights(lru_w_r[0], lru_w_i[0]),
              lru_b_r[0], lru_b_i[0], lru_lambda[0], h0)
    h, *routed = _post0(hs, proj0, _fourier_prompt(proj0), _fourier_sample(proj0), w_out0[0].astype(BF16),
                        h, mod[0], norm_ffn_g[0], w_router[0].T, b_router[0])
    y_rows = _moe_expert_rows(0, routed, w1, b1, w2, b2)
    gates = routed[2]

    h, proj1 = _moe_combine_inproj(y_rows, gates.T, h, mod[0], norm_mix_g[1], mod[1], w_in1[0].astype(BF16))
    bs_full = jnp.repeat(sgu_b_s[0].T, CHUNK, axis=1)
    h, *routed = _post1(proj1, sgu_ln_g[0], sgu_ln_b[0], sgu_w_s[0].astype(BF16), bs_full,
                        conv_dw_w[0], conv_dw_b[0], conv_ln_g[0], conv_ln_b[0],
                        w_out1[0].astype(BF16), h, mod[1], norm_ffn_g[1], w_router[1].T, b_router[1])
    y_rows = _moe_expert_rows(1, routed, w1, b1, w2, b2)
    y_p, y_l = _moe_combine_final(y_rows, routed[2].T, h, mod[1], final_norm_g)

    y_prompt = y_p.reshape(N_PROMPT_SEQ, T_PROMPT, D)
    y_sample = y_l.reshape(N_SAMPLE_SEQ, T_SAMPLE, D)
    new_state = jnp.transpose(ctx_state, (1, 2, 0, 3))
    return (y_prompt, y_sample, new_state.astype(x_prompt.dtype))
```

```python
import functools

import numpy as np
import jax
import jax.numpy as jnp
from jax import lax
from jax.experimental import pallas as pl
from jax.experimental.pallas import tpu as pltpu

F32 = jnp.float32
BF16 = jnp.bfloat16

D = 1024
N_PROMPT_SEQ = 32
T_PROMPT = 256
N_SAMPLE_SEQ = 2
T_SAMPLE = 2048
N_P = N_PROMPT_SEQ * T_PROMPT
N_S = N_SAMPLE_SEQ * T_SAMPLE
N = N_P + N_S
EPS = 1e-6

TILE = 256
N_TILES = N // TILE
P_TILES = N_P // TILE
S_TILES = T_SAMPLE // TILE
TM_PROJ = 1024
N_MOD = 8

LRU_W = 768
LRU_HEADS = 12
LRU_HD = 64
LRU_K = 4
LRU_LEFT = 2
LRU_C = 8.0
LRU_BLK = 256
FN_W = 256
FN_G = 4
FN_GD = 64
IN0 = 2 * LRU_W + FN_W

SGU_W = 512
SGU_G = 4
CHUNK = 128
CONV_W = 512
CONV_K = 31
CONV_PAD = 15
CONV_HALO = 16
CONV_ROWS = 32
ROW_PIECE = 32
IN1 = 2 * SGU_W + 2 * CONV_W

N_EXP = 32
TOP_K = 4
D_FF = 1024
ALPHA = 1.702
LIMIT = 7.0
MOE_BLOCK = 256
N_ASSIGN = N * TOP_K
N_BLOCKS = N_ASSIGN // MOE_BLOCK + N_EXP
N_SLOTS = N_BLOCKS * MOE_BLOCK
ROW_TILE = D // 128
PACK_ROWS = D // 2 // 128
PACK_PAIR = 2 * PACK_ROWS

VMEM_LIMIT = 56 * 1024 * 1024
MOE_VMEM_LIMIT = 60 * 1024 * 1024


def _cp(sem, vmem=None):
    return pltpu.CompilerParams(dimension_semantics=sem, vmem_limit_bytes=vmem)


def _dot(a, b):
    return jnp.dot(a, b, preferred_element_type=F32)


def _split(x):
    hi = x.astype(BF16)
    lo = (x - hi.astype(F32)).astype(BF16)
    return hi, lo


def _dot3(a, b):
    ah, al = _split(a)
    bh, bl = _split(b)
    return _dot(ah, bh) + _dot(al, bh) + _dot(ah, bl)


def _dot3_nt(a, b):
    dn = (((1,), (1,)), ((), ()))
    d = lambda x, y: lax.dot_general(x, y, dn, preferred_element_type=F32)
    ah, al = _split(a)
    bh, bl = _split(b)
    return d(ah, bh) + d(al, bh) + d(ah, bl)


def _sigmoid(x):
    return 0.5 * jnp.tanh(0.5 * x) + 0.5


def _gelu(x):
    return 0.5 * x * (1.0 + jnp.tanh(0.7978845608028654 * (x + 0.044715 * (x * x * x))))


def _rms_mod(x, g, shift, scale):
    ms = jnp.mean(x * x, axis=-1, keepdims=True)
    y = x * lax.rsqrt(ms + EPS) * g
    return y * (1.0 + scale) + shift


def _layernorm(x, g, b):
    xc = x - jnp.mean(x, axis=-1, keepdims=True)
    var = jnp.mean(xc * xc, axis=-1, keepdims=True)
    return xc * lax.rsqrt(var + EPS) * g + b


def _tile_mod_row(r):
    return jnp.where(r < P_TILES, 0, 1 + (r - P_TILES) // S_TILES)


def _tile_is_seq_start(r):
    return (r < P_TILES) | ((r - P_TILES) % S_TILES == 0)


def _tile_is_seq_end(r):
    return (r < P_TILES) | ((r - P_TILES) % S_TILES == S_TILES - 1)


MOD_TN = 512


def _adaln_kernel(cond_ref, w_ref, b_ref, o_ref):
    cond = cond_ref[...]
    s = cond * _sigmoid(cond)
    o_ref[...] = _dot3(s, w_ref[...]) + b_ref[...]


def _adaln(cond8, w_mod, b_mod):
    depth = w_mod.shape[0]
    out = pl.pallas_call(
        _adaln_kernel,
        out_shape=jax.ShapeDtypeStruct((depth, N_MOD, 6 * D), F32),
        grid=(depth, 6 * D // MOD_TN),
        in_specs=[
            pl.BlockSpec((N_MOD, D), lambda l, j: (0, 0)),
            pl.BlockSpec((None, D, MOD_TN), lambda l, j: (l, 0, j)),
            pl.BlockSpec((None, 1, MOD_TN), lambda l, j: (l, 0, j)),
        ],
        out_specs=pl.BlockSpec((None, N_MOD, MOD_TN), lambda l, j: (l, 0, j)),
        compiler_params=_cp(("arbitrary", "arbitrary")),
        name="adaln",
    )(cond8, w_mod, b_mod.reshape(depth, 1, 6 * D))
    return out.reshape(depth, N_MOD, 6, D)


PROJ_P_STEPS = N_P // TM_PROJ
PROJ_S_STEPS = T_SAMPLE // TM_PROJ


def _stream_specs(rows, p_steps, h):
    if isinstance(h, tuple):
        hp, hl = h
        first_latent = 0
    else:
        hp = hl = h
        first_latent = p_steps
    specs = [
        pl.BlockSpec((rows, D), lambda i, *_: (jnp.minimum(i, p_steps - 1), 0)),
        pl.BlockSpec((rows, D), lambda i, *_: (jnp.maximum(i - p_steps, 0) + first_latent, 0)),
    ]
    return specs, (hp, hl)


def _inproj_kernel(xp_ref, xl_ref, g_ref, m_ref, w_ref, o_ref):
    x = jnp.where(pl.program_id(0) < PROJ_P_STEPS, xp_ref[...], xl_ref[...])
    hn = _rms_mod(x, g_ref[...], m_ref[0:1, :], m_ref[1:2, :])
    o_ref[...] = _dot(hn.astype(BF16), w_ref[...])


def _inproj(h, g, mod, w_bf16):
    n_out = w_bf16.shape[1]

    def mod_row(i):
        return jnp.where(i < PROJ_P_STEPS, 0, 1 + (i - PROJ_P_STEPS) // PROJ_S_STEPS)

    h_specs, h_args = _stream_specs(TM_PROJ, PROJ_P_STEPS, h)
    return pl.pallas_call(
        _inproj_kernel,
        out_shape=jax.ShapeDtypeStruct((N, n_out), F32),
        grid=(N // TM_PROJ,),
        in_specs=h_specs + [
            pl.BlockSpec((1, D), lambda i: (0, 0)),
            pl.BlockSpec((None, 6, D), lambda i: (mod_row(i), 0, 0)),
            pl.BlockSpec((D, n_out), lambda i: (0, 0)),
        ],
        out_specs=pl.BlockSpec((TM_PROJ, n_out), lambda i: (i, 0)),
        compiler_params=_cp(("arbitrary",), VMEM_LIMIT),
        name="inproj",
    )(*h_args, g.reshape(1, D), mod, w_bf16)


LRU_HALO = 8
SCAN_ROWS = 8


def _lru_tile(d, s):
    return jnp.where(d == 0, s, N_TILES - 1 - s)


def _lru_kernel(x_ref, prev_ref, next_ref, cw_ref, cb_ref, wg_ref, br_ref, bi_ref, lam_ref, h0_ref,
                o_ref, state_ref, ext_ref, a_ref, carry_ref):
    d = pl.program_id(0)
    r = _lru_tile(d, pl.program_id(1))
    start = _tile_is_seq_start(r)
    end = _tile_is_seq_end(r)

    ext_ref[0:LRU_HALO, :] = jnp.where(start, 0.0, prev_ref[...])
    ext_ref[LRU_HALO:LRU_HALO + TILE, :] = x_ref[...]
    ext_ref[LRU_HALO + TILE:, :] = jnp.where(end, 0.0, next_ref[...])
    xc = cb_ref[...] + jnp.zeros((TILE, LRU_W), F32)
    for k in range(LRU_K):
        off = LRU_HALO - LRU_LEFT + k
        xc = xc + ext_ref[off:off + TILE, :] * cw_ref[k:k + 1, :]

    xcb = xc.astype(BF16)
    pre_r, pre_i = [], []
    for blk in range(LRU_W // LRU_BLK):
        g = _dot(xcb[:, blk * LRU_BLK:(blk + 1) * LRU_BLK], wg_ref[blk])
        pre_r.append(g[:, :LRU_BLK])
        pre_i.append(g[:, LRU_BLK:])
    gate_r = _sigmoid(jnp.concatenate(pre_r, axis=1) + br_ref[...])
    gate_i = _sigmoid(jnp.concatenate(pre_i, axis=1) + bi_ref[...])
    neg_lam = -lam_ref[...]
    softplus = jnp.maximum(neg_lam, 0.0) + jnp.log1p(jnp.exp(-jnp.abs(neg_lam)))
    log_a = (-LRU_C) * gate_r * softplus
    a = jnp.exp(log_a)
    a_ref[...] = a
    o_ref[...] = jnp.sqrt(-jnp.tanh(log_a) * (a * a + 1.0)) * (gate_i * xc)

    fresh = jnp.where(d == 0, start, end)
    h_init = jnp.where(fresh, h0_ref[pl.ds(_tile_mod_row(r), 1), :], carry_ref[...])

    row = lax.broadcasted_iota(jnp.int32, (SCAN_ROWS, LRU_W), 0)

    def scan_group(g, h, reverse):
        rows = pl.ds(pl.multiple_of(g * SCAN_ROWS, SCAN_ROWS), SCAN_ROWS)
        a = a_ref[rows, :]
        b = o_ref[rows, :]
        for s in (1, 2, 4):
            shift = SCAN_ROWS - s if reverse else s
            inside = (row < SCAN_ROWS - s) if reverse else (row >= s)
            b = jnp.where(inside, a * pltpu.roll(b, shift, 0) + b, b)
            a = jnp.where(inside, a * pltpu.roll(a, shift, 0), a)
        hs = a * h + b
        o_ref[rows, :] = hs
        return hs[0:1, :] if reverse else hs[SCAN_ROWS - 1:SCAN_ROWS, :]

    n_groups = TILE // SCAN_ROWS

    @pl.when(d == 0)
    def _():
        carry_ref[...] = lax.fori_loop(0, n_groups, lambda g, h: scan_group(g, h, False), h_init, unroll=4)

    @pl.when(d == 1)
    def _():
        carry_ref[...] = lax.fori_loop(0, n_groups, lambda g, h: scan_group(n_groups - 1 - g, h, True),
                                       h_init, unroll=4)

    h_last = carry_ref[...]

    @pl.when(r < P_TILES)
    def _():
        state_ref[...] = h_last


def _lru(proj0, conv_w, conv_b, wg, b_r, b_i, lam, h0):
    n_halo_blocks = N // LRU_HALO
    per_tile = TILE // LRU_HALO
    tile = lambda d, s: _lru_tile(d, s)
    return pl.pallas_call(
        _lru_kernel,
        out_shape=(jax.ShapeDtypeStruct((2, N, LRU_W), F32),
                   jax.ShapeDtypeStruct((2, N_PROMPT_SEQ, 1, LRU_W), F32)),
        grid=(2, N_TILES),
        in_specs=[
            pl.BlockSpec((TILE, LRU_W), lambda d, s: (tile(d, s), 0)),
            pl.BlockSpec((LRU_HALO, LRU_W), lambda d, s: (jnp.maximum(tile(d, s) * per_tile - 1, 0), 0)),
            pl.BlockSpec((LRU_HALO, LRU_W),
                         lambda d, s: (jnp.minimum((tile(d, s) + 1) * per_tile, n_halo_blocks - 1), 0)),
            pl.BlockSpec((LRU_K, LRU_W), lambda d, s: (0, 0)),
            pl.BlockSpec((1, LRU_W), lambda d, s: (0, 0)),
            pl.BlockSpec((None, LRU_W // LRU_BLK, LRU_BLK, 2 * LRU_BLK), lambda d, s: (d, 0, 0, 0)),
            pl.BlockSpec((None, 1, LRU_W), lambda d, s: (d, 0, 0)),
            pl.BlockSpec((None, 1, LRU_W), lambda d, s: (d, 0, 0)),
            pl.BlockSpec((None, 1, LRU_W), lambda d, s: (d, 0, 0)),
            pl.BlockSpec((None, N_MOD, LRU_W), lambda d, s: (d, 0, 0)),
        ],
        out_specs=[
            pl.BlockSpec((None, TILE, LRU_W), lambda d, s: (d, tile(d, s), 0)),
            pl.BlockSpec((None, None, 1, LRU_W), lambda d, s: (d, jnp.minimum(tile(d, s), P_TILES - 1), 0, 0)),
        ],
        scratch_shapes=[
            pltpu.VMEM((TILE + 2 * LRU_HALO, LRU_W), F32),
            pltpu.VMEM((TILE, LRU_W), F32),
            pltpu.VMEM((1, LRU_W), F32),
        ],
        compiler_params=_cp(("arbitrary", "arbitrary"), VMEM_LIMIT),
        name="rglru_scan",
    )(proj0, proj0, proj0, conv_w, conv_b.reshape(1, LRU_W), wg,
      b_r.reshape(2, 1, LRU_W), b_i.reshape(2, 1, LRU_W), lam.reshape(2, 1, LRU_W), h0)


def _dft_tables(n, scale):
    k = np.arange(n, dtype=np.int64)
    ang = 2.0 * np.pi * ((k[:, None] * k[None, :]) % n).astype(np.float64) / n
    return np.cos(ang) * scale, np.sin(ang) * scale


def _channel_tables():
    c, s = _dft_tables(FN_GD, FN_GD ** -0.5)
    eye = np.eye(FN_G)
    return (jnp.asarray(np.kron(eye, c), BF16), jnp.asarray(np.kron(eye, s), BF16))


def _time_tables(t_len):
    c, s = _dft_tables(t_len, t_len ** -0.5)
    return jnp.asarray(c, BF16), jnp.asarray(s, BF16)


FN_SEQ_PER_STEP = 4


def _fourier_prompt_kernel(z_ref, cc_ref, sc_ref, ct_ref, st_ref, o_ref):
    z = z_ref[...].astype(BF16)
    zc = _dot(z, cc_ref[...]).astype(BF16)
    zs = _dot(z, sc_ref[...]).astype(BF16)
    for b in range(FN_SEQ_PER_STEP):
        rows = slice(b * T_PROMPT, (b + 1) * T_PROMPT)
        o_ref[rows, :] = _dot(ct_ref[...], zc[rows, :]) - _dot(st_ref[...], zs[rows, :])


def _fourier_prompt(proj0):
    cc, sc = _channel_tables()
    ct, st = _time_tables(T_PROMPT)
    const = lambda b: (0, 0)
    return pl.pallas_call(
        _fourier_prompt_kernel,
        out_shape=jax.ShapeDtypeStruct((N_P, FN_W), F32),
        grid=(N_PROMPT_SEQ // FN_SEQ_PER_STEP,),
        in_specs=[
            pl.BlockSpec((FN_SEQ_PER_STEP * T_PROMPT, FN_W), lambda b: (b, 2 * LRU_W // FN_W)),
            pl.BlockSpec((FN_W, FN_W), const),
            pl.BlockSpec((FN_W, FN_W), const),
            pl.BlockSpec((T_PROMPT, T_PROMPT), const),
            pl.BlockSpec((T_PROMPT, T_PROMPT), const),
        ],
        out_specs=pl.BlockSpec((FN_SEQ_PER_STEP * T_PROMPT, FN_W), lambda b: (b, 0)),
        compiler_params=_cp(("arbitrary",)),
        name="fourier_prompt",
    )(proj0, cc, sc, ct, st)


def _fourier_sample_kernel(z_ref, cc_ref, sc_ref, ct_ref, st_ref, o_ref, zc_ref, zs_ref):
    @pl.when(pl.program_id(1) == 0)
    def _():
        z = z_ref[...].astype(BF16)
        zc_ref[...] = _dot(z, cc_ref[...]).astype(BF16)
        zs_ref[...] = _dot(z, sc_ref[...]).astype(BF16)

    o_ref[...] = _dot(ct_ref[...], zc_ref[...]) - _dot(st_ref[...], zs_ref[...])


def _fourier_sample(proj0):
    cc, sc = _channel_tables()
    ct, st = _time_tables(T_SAMPLE)
    const = lambda b, i: (0, 0)
    first_seq_block = N_P // T_SAMPLE
    return pl.pallas_call(
        _fourier_sample_kernel,
        out_shape=jax.ShapeDtypeStruct((N_S, FN_W), F32),
        grid=(N_SAMPLE_SEQ, S_TILES),
        in_specs=[
            pl.BlockSpec((T_SAMPLE, FN_W), lambda b, i: (first_seq_block + b, 2 * LRU_W // FN_W)),
            pl.BlockSpec((FN_W, FN_W), const),
            pl.BlockSpec((FN_W, FN_W), const),
            pl.BlockSpec((TILE, T_SAMPLE), lambda b, i: (i, 0)),
            pl.BlockSpec((TILE, T_SAMPLE), lambda b, i: (i, 0)),
        ],
        out_specs=pl.BlockSpec((TILE, FN_W), lambda b, i: (b * S_TILES + i, 0)),
        scratch_shapes=[pltpu.VMEM((T_SAMPLE, FN_W), BF16), pltpu.VMEM((T_SAMPLE, FN_W), BF16)],
        compiler_params=_cp(("arbitrary", "arbitrary"), VMEM_LIMIT),
        name="fourier_sample",
    )(proj0, cc, sc, ct, st)


def _epilogue(mix, rows, hp_ref, hl_ref, m_ref, gf_ref, wr_ref, brt_ref, tri_ref,
              hout_ref, fn_ref, tope_ref, gate_ref, rank_ref, cnt_ref, run_ref):
    h = jnp.where(pl.program_id(0) < N_P // rows, hp_ref[...], hl_ref[...])
    h_new = h + m_ref[2:3, :] * mix
    hout_ref[...] = h_new
    fn = _rms_mod(h_new, gf_ref[...], m_ref[3:4, :], m_ref[4:5, :])
    hi = lax.bitcast_convert_type(fn[:, :D // 2].astype(BF16).astype(F32), jnp.uint32)
    lo = lax.bitcast_convert_type(fn[:, D // 2:].astype(BF16).astype(F32), jnp.uint32)
    packed = hi | (lo >> 16)
    for q in range(PACK_ROWS):
        fn_ref[pl.ds(q, rows, stride=PACK_ROWS), :] = packed[:, q * 128:(q + 1) * 128]
    logits = _dot3_nt(wr_ref[...], fn) + brt_ref[...]
    iota = lax.broadcasted_iota(jnp.int32, logits.shape, 0)
    vals, idxs = [], []
    for _ in range(TOP_K):
        m = jnp.max(logits, axis=0, keepdims=True)
        idx = jnp.min(jnp.where(logits == m, iota, N_EXP), axis=0, keepdims=True)
        vals.append(m)
        idxs.append(idx)
        logits = jnp.where(iota == idx, -jnp.inf, logits)
    exps = [jnp.exp(v - vals[0]) for v in vals]
    denom = exps[0] + exps[1] + exps[2] + exps[3]
    for k in range(TOP_K):
        tope_ref[k:k + 1, :] = idxs[k]
        gate_ref[k:k + 1, :] = exps[k] / denom

    @pl.when(pl.program_id(0) == 0)
    def _():
        run_ref[...] = jnp.zeros_like(run_ref)

    run = run_ref[...]
    onehots = [jnp.where(iota == idxs[k], 1.0, 0.0) for k in range(TOP_K)]
    incl_all = _dot(jnp.concatenate(onehots, axis=0).astype(BF16), tri_ref[...])
    for k in range(TOP_K):
        incl = incl_all[k * N_EXP:(k + 1) * N_EXP, :]
        rank = jnp.sum(onehots[k] * (incl - 1.0 + run), axis=0, keepdims=True)
        rank_ref[k:k + 1, :] = rank.astype(jnp.int32)
        run = run + incl[:, rows - 1:rows]
    run_ref[...] = run
    cnt_ref[...] = run.astype(jnp.int32)


_EPI_OUT_SHAPES = (
    jax.ShapeDtypeStruct((N, D), F32),
    jax.ShapeDtypeStruct((N * PACK_ROWS, 128), jnp.uint32),
    jax.ShapeDtypeStruct((TOP_K, N), jnp.int32),
    jax.ShapeDtypeStruct((TOP_K, N), F32),
    jax.ShapeDtypeStruct((TOP_K, N), jnp.int32),
    jax.ShapeDtypeStruct((N_EXP, 1), jnp.int32),
)
_EPI_SCRATCH = [pltpu.VMEM((N_EXP, 1), F32)]


def _epi_operands(h, mod, g_ffn, w_router_t, b_router, rows=TILE):
    tri = jnp.asarray(np.triu(np.ones((rows, rows), np.float32)), BF16)
    return (*_stream_specs(rows, N_P // rows, h)[1], mod, g_ffn.reshape(1, D), w_router_t,
            b_router.reshape(N_EXP, 1), tri)


def _epi_in_specs(h, rows=TILE):
    p_steps, s_steps = N_P // rows, T_SAMPLE // rows
    h_specs, _ = _stream_specs(rows, p_steps, h)
    mod_row = lambda r: jnp.where(r < p_steps, 0, 1 + (r - p_steps) // s_steps)
    return h_specs + [
        pl.BlockSpec((None, 6, D), lambda r: (mod_row(r), 0, 0)),
        pl.BlockSpec((1, D), lambda r: (0, 0)),
        pl.BlockSpec((N_EXP, D), lambda r: (0, 0)),
        pl.BlockSpec((N_EXP, 1), lambda r: (0, 0)),
        pl.BlockSpec((rows, rows), lambda r: (0, 0)),
    ]


def _epi_out_specs(rows=TILE):
    return [
        pl.BlockSpec((rows, D), lambda r: (r, 0)),
        pl.BlockSpec((rows * PACK_ROWS, 128), lambda r: (r, 0)),
        pl.BlockSpec((TOP_K, rows), lambda r: (0, r)),
        pl.BlockSpec((TOP_K, rows), lambda r: (0, r)),
        pl.BlockSpec((TOP_K, rows), lambda r: (0, r)),
        pl.BlockSpec((N_EXP, 1), lambda r: (0, 0)),
    ]


POST0_ROWS = 512
POST0_P_STEPS = N_P // POST0_ROWS


def _post0_kernel(hs_ref, xg_ref, yfp_ref, yfl_ref, wo_ref, *epilogue_refs):
    y_rec = (hs_ref[0] + hs_ref[1]) * _gelu(xg_ref[...])
    y_four = jnp.where(pl.program_id(0) < POST0_P_STEPS, yfp_ref[...], yfl_ref[...])
    mix = (_dot(y_rec.astype(BF16), wo_ref[0:LRU_W, :])
           + _dot(y_four.astype(BF16), wo_ref[LRU_W:, :]))
    _epilogue(mix, POST0_ROWS, *epilogue_refs)


def _post0(hs, proj0, yf_prompt, yf_latent, w_out_bf16, h, mod, g_ffn, w_router_t, b_router):
    return pl.pallas_call(
        _post0_kernel,
        out_shape=_EPI_OUT_SHAPES,
        grid=(N // POST0_ROWS,),
        in_specs=[
            pl.BlockSpec((2, POST0_ROWS, LRU_W), lambda r: (0, r, 0)),
            pl.BlockSpec((POST0_ROWS, LRU_W), lambda r: (r, 1)),
            pl.BlockSpec((POST0_ROWS, FN_W), lambda r: (jnp.minimum(r, POST0_P_STEPS - 1), 0)),
            pl.BlockSpec((POST0_ROWS, FN_W), lambda r: (jnp.maximum(r - POST0_P_STEPS, 0), 0)),
            pl.BlockSpec((D, D), lambda r: (0, 0)),
        ] + _epi_in_specs(h, POST0_ROWS),
        out_specs=_epi_out_specs(POST0_ROWS),
        scratch_shapes=_EPI_SCRATCH,
        compiler_params=_cp(("arbitrary",), VMEM_LIMIT),
        name="post_rglru_fourier",
    )(hs, proj0, yf_prompt, yf_latent, w_out_bf16,
      *_epi_operands(h, mod, g_ffn, w_router_t, b_router, POST0_ROWS))


def _glu(x):
    return x[:, :CONV_W] * _sigmoid(x[:, CONV_W:])


def _post1_kernel(p_ref, prev_ref, next_ref, lng_ref, lnb_ref, ws_ref, bs_ref, dww_ref, dwb_ref,
                  clg_ref, clb_ref, wo_ref, *rest):
    epilogue_refs, (ext_ref, shift_ref, u_ref, v_ref, mix_ref) = rest[:-5], rest[-5:]
    r = pl.program_id(0)

    for c in range(TILE // ROW_PIECE):
        rows = slice(c * ROW_PIECE, (c + 1) * ROW_PIECE)
        z = _gelu(p_ref[rows, 0:2 * SGU_W])
        u_ref[rows, :] = z[:, :SGU_W]
        v_ref[rows, :] = _layernorm(z[:, SGU_W:], lng_ref[...], lnb_ref[...]).astype(BF16)
        ext_ref[CONV_HALO + c * ROW_PIECE:CONV_HALO + (c + 1) * ROW_PIECE, :] = _glu(p_ref[rows, 2 * SGU_W:])
    ext_ref[0:CONV_HALO, :] = jnp.where(_tile_is_seq_start(r), 0.0, _glu(prev_ref[...]))
    ext_ref[CONV_HALO + TILE:, :] = jnp.where(_tile_is_seq_end(r), 0.0, _glu(next_ref[...]))

    for n in range(TILE // CHUNK):
        for g in range(SGU_G):
            rows, cols = slice(n * CHUNK, (n + 1) * CHUNK), slice(g * CHUNK, (g + 1) * CHUNK)
            s = _dot(ws_ref[g], v_ref[rows, cols]) + bs_ref[:, cols]
            mix_ref[rows, cols] = (u_ref[rows, cols] * s).astype(BF16)

    n_shift_rows = TILE + 2 * CONV_HALO - 8
    for phase in range(8):
        shift_ref[phase] = ext_ref[phase:phase + n_shift_rows, :]
    for c in range(TILE // CONV_ROWS):
        acc = dwb_ref[...] + jnp.zeros((CONV_ROWS // 8, 8, CONV_W), F32)
        for k in range(CONV_K):
            off = CONV_HALO - CONV_PAD + k
            start = off // 8 * 8 + c * CONV_ROWS
            rows = shift_ref[off % 8, start:start + CONV_ROWS, :].reshape(CONV_ROWS // 8, 8, CONV_W)
            acc = acc + rows * dww_ref[k]
        ln = _layernorm(acc.reshape(CONV_ROWS, CONV_W), clg_ref[...], clb_ref[...])
        mix_ref[c * CONV_ROWS:(c + 1) * CONV_ROWS, SGU_W:] = (ln * _sigmoid(ln)).astype(BF16)

    _epilogue(_dot(mix_ref[...], wo_ref[...]), TILE, *epilogue_refs)


def _post1(proj1, ln_g, ln_b, ws_bf16, bs_full, dw_w, dw_b, cln_g, cln_b, w_out_bf16,
           h, mod, g_ffn, w_router_t, b_router):
    per_tile = TILE // CONV_HALO
    n_halo_blocks = N // CONV_HALO
    const2 = lambda r: (0, 0)
    row = lambda a: a.reshape(1, -1)
    return pl.pallas_call(
        _post1_kernel,
        out_shape=_EPI_OUT_SHAPES,
        grid=(N_TILES,),
        in_specs=[
            pl.BlockSpec((TILE, IN1), lambda r: (r, 0)),
            pl.BlockSpec((CONV_HALO, 2 * CONV_W), lambda r: (jnp.maximum(r * per_tile - 1, 0), 1)),
            pl.BlockSpec((CONV_HALO, 2 * CONV_W),
                         lambda r: (jnp.minimum((r + 1) * per_tile, n_halo_blocks - 1), 1)),
            pl.BlockSpec((1, SGU_W), const2),
            pl.BlockSpec((1, SGU_W), const2),
            pl.BlockSpec((SGU_G, CHUNK, CHUNK), lambda r: (0, 0, 0)),
            pl.BlockSpec((CHUNK, SGU_W), const2),
            pl.BlockSpec((CONV_K, 8, CONV_W), lambda r: (0, 0, 0)),
            pl.BlockSpec((1, CONV_W), const2),
            pl.BlockSpec((1, CONV_W), const2),
            pl.BlockSpec((1, CONV_W), const2),
            pl.BlockSpec((D, D), const2),
        ] + _epi_in_specs(h),
        out_specs=_epi_out_specs(),
        scratch_shapes=_EPI_SCRATCH + [
            pltpu.VMEM((TILE + 2 * CONV_HALO, CONV_W), F32),
            pltpu.VMEM((8, TILE + 2 * CONV_HALO - 8, CONV_W), F32),
            pltpu.VMEM((TILE, SGU_W), F32),
            pltpu.VMEM((TILE, SGU_W), BF16),
            pltpu.VMEM((TILE, D), BF16),
        ],
        compiler_params=_cp(("arbitrary",), VMEM_LIMIT),
        name="post_sgu_conformer",
    )(proj1, proj1, proj1, row(ln_g), row(ln_b), ws_bf16, bs_full,
      jnp.broadcast_to(dw_w[:, None, :], (CONV_K, 8, CONV_W)), row(dw_b), row(cln_g), row(cln_b),
      w_out_bf16, *_epi_operands(h, mod, g_ffn, w_router_t, b_router))


PAD_BASE = N_ASSIGN
PLACEHOLDER_BASE = PAD_BASE + 2 * MOE_BLOCK
Y_ROWS = PLACEHOLDER_BASE + 2 * MOE_BLOCK


def _inverse_kernel(pos_ref, pend_ref, out_ref):
    def fill_block(start):
        def fill(j, carry):
            s = start + j
            out_ref[s] = PAD_BASE + (s & (2 * MOE_BLOCK - 1))
            return carry
        lax.fori_loop(0, MOE_BLOCK, fill, 0, unroll=16)

    def per_expert(e, carry):
        fill_block(jnp.maximum(pend_ref[e] - MOE_BLOCK, 0))
        return carry

    def idle_block(b, carry):
        fill_block(b * MOE_BLOCK)
        return carry

    lax.fori_loop(0, N_EXP, per_expert, 0)
    lax.fori_loop(pend_ref[N_EXP - 1] // MOE_BLOCK, N_BLOCKS, idle_block, 0)

    def body(t, carry):
        for k in range(TOP_K):
            a = t * TOP_K + k
            out_ref[pos_ref[a]] = a
        return carry

    lax.fori_loop(0, N, body, 0, unroll=8)


def _inverse(pos_flat, pad_end):
    smem = pl.BlockSpec(memory_space=pltpu.SMEM)
    return pl.pallas_call(
        _inverse_kernel,
        out_shape=jax.ShapeDtypeStruct((N_SLOTS,), jnp.int32),
        in_specs=[smem, smem],
        out_specs=smem,
        name="moe_inverse",
    )(pos_flat, pad_end)


def _routing_tables(top_e, rank, counts):
    counts = counts.reshape(N_EXP)
    padded = (counts + MOE_BLOCK - 1) // MOE_BLOCK * MOE_BLOCK
    pad_end = jnp.cumsum(padded).astype(jnp.int32)
    pad_start = pad_end - padded
    experts = jnp.arange(N_EXP, dtype=jnp.int32)[:, None, None]
    pos = rank + jnp.sum(jnp.where(top_e[None] == experts, pad_start[:, None, None], 0), axis=0)
    pos = pos.astype(jnp.int32).T.reshape(-1)
    n_used = pad_end[-1] // MOE_BLOCK
    blk = jnp.minimum(jnp.arange(N_BLOCKS, dtype=jnp.int32), n_used - 1)
    block_e = jnp.sum(pad_end[None, :] <= (blk * MOE_BLOCK)[:, None], axis=1).astype(jnp.int32)
    slot_a = _inverse(pos, pad_end)
    tok = jnp.minimum(slot_a >> 2, N - 1)
    gather_key = (tok >> 1) * (2 * PACK_PAIR) + (tok & 1)
    out_row = jnp.where(slot_a >= PAD_BASE, slot_a, (slot_a & (TOP_K - 1)) * N + (slot_a >> 2))
    placeholder_rows = PLACEHOLDER_BASE + jnp.arange(MOE_BLOCK, dtype=jnp.int32)
    out_row8 = jnp.concatenate([placeholder_rows, out_row]) * ROW_TILE
    ids = jnp.arange(N_EXP, dtype=jnp.int32)
    later_used = (ids[None, :] > ids[:, None]) & (counts[None, :] > 0)
    next_expert = jnp.min(jnp.where(later_used, ids[None, :], N_EXP), axis=1).astype(jnp.int32)
    return gather_key, out_row8, block_e, n_used.reshape(1), next_expert


def _moe_kernel(be_ref, nu_ref, gk_ref, row8_ref, nxt_ref, x_hbm, w1_hbm, b1g_ref, b1l_ref, w2_hbm, b2_ref, perm_ref,
                y_hbm, xs, xbuf, w1f, w2f, w1s, w2s, sem, ybuf, osem, *, layer):
    i = pl.program_id(0)
    n_used = nu_ref[0]
    slot = i % 2
    e = be_ref[i]
    e_prev = be_ref[jnp.maximum(i - 1, 0)]
    low_half = lax.broadcasted_iota(jnp.int32, (PACK_PAIR, 128), 0) < PACK_ROWS

    def weight_copies(expert):
        return (pltpu.make_async_copy(w1_hbm.at[layer, expert], w1f, sem.at[1]),
                pltpu.make_async_copy(w2_hbm.at[layer, expert], w2f, sem.at[2]))

    def token_tile(key):
        tile = xs[pl.ds(pl.multiple_of(key >> 1, PACK_PAIR), PACK_PAIR), :]
        return tile, pltpu.roll(tile, PACK_ROWS, 0), key & 1

    def gather_pair(blk, jj, dst_slot):
        t0, r0, half0 = token_tile(gk_ref[blk * MOE_BLOCK + 2 * jj])
        t1, r1, half1 = token_tile(gk_ref[blk * MOE_BLOCK + 2 * jj + 1])
        lower = jnp.where(half0 == 0, t0, r0)
        upper = jnp.where(half1 == 1, t1, r1)
        start = jj * PACK_PAIR if isinstance(jj, int) else pl.multiple_of(jj * PACK_PAIR, PACK_PAIR)
        xbuf[dst_slot, pl.ds(start, PACK_PAIR), :] = jnp.where(low_half, lower, upper)

    def out_copy(src_slot, j, row8):
        start = j * ROW_TILE if isinstance(j, int) else pl.multiple_of(j * ROW_TILE, ROW_TILE)
        dst = y_hbm.at[pl.ds(pl.multiple_of(row8, ROW_TILE), ROW_TILE), :]
        return pltpu.make_async_copy(ybuf.at[src_slot, pl.ds(start, ROW_TILE), :], dst, osem.at[src_slot])

    def out_buffer_wait(src_slot):
        pltpu.make_async_copy(ybuf.at[src_slot], y_hbm.at[pl.ds(0, MOE_BLOCK * ROW_TILE), :],
                              osem.at[src_slot]).wait()

    @pl.when(i == 0)
    def _():
        xs_copy = pltpu.make_async_copy(x_hbm, xs, sem.at[0])
        xs_copy.start()
        for c in weight_copies(e):
            c.start()
        ybuf[...] = jnp.zeros_like(ybuf)

        def zero_rows(first_row):
            def body(j, carry):
                out_copy(0, j, (first_row + j) * ROW_TILE).start()
                return carry
            lax.fori_loop(0, MOE_BLOCK, body, 0, unroll=8)

        zero_rows(PAD_BASE)
        zero_rows(PAD_BASE + MOE_BLOCK)
        out_buffer_wait(0)
        out_buffer_wait(0)
        zero_rows(PLACEHOLDER_BASE + MOE_BLOCK)
        xs_copy.wait()

        def body(jj, carry):
            gather_pair(0, jj, 0)
            return carry
        lax.fori_loop(0, MOE_BLOCK // 2, body, 0, unroll=4)

    @pl.when((i < n_used) & ((i == 0) | (e != e_prev)))
    def _():
        for c in weight_copies(e):
            c.wait()
        for c in range(2 * D_FF // 256):
            wc = w1f[:, c * 256:(c + 1) * 256].astype(BF16)
            w1s[:, c * 256:(c + 1) * 256] = _dot(wc, perm_ref[...]).astype(BF16)
        w2s[...] = w2f[...].astype(BF16)
        e_next = nxt_ref[e]

        @pl.when(e_next < N_EXP)
        def _():
            for c in weight_copies(e_next):
                c.start()

    def block(prefetch_next):
        for j in range(MOE_BLOCK):
            out_copy(1 - slot, j, row8_ref[i * MOE_BLOCK + j]).start()
        halves = [[], []]
        for q in range(PACK_ROWS):
            words = xbuf[slot, pl.ds(q, MOE_BLOCK, stride=PACK_ROWS), :]
            halves[0].append(lax.bitcast_convert_type(words & jnp.uint32(0xFFFF0000), F32).astype(BF16))
            halves[1].append(lax.bitcast_convert_type(words << 16, F32).astype(BF16))
        x = jnp.concatenate(halves[0] + halves[1], axis=1)
        if prefetch_next:
            for jj in range(MOE_BLOCK // 2):
                gather_pair(i + 1, jj, 1 - slot)
        hid = _dot(x, w1s[...])
        acts = []
        for c in range(D_FF // 128):
            h_glu = hid[:, c * 256:c * 256 + 128] + b1g_ref[:, c * 128:(c + 1) * 128]
            h_lin = hid[:, c * 256 + 128:(c + 1) * 256] + b1l_ref[:, c * 128:(c + 1) * 128]
            h_glu = jnp.minimum(h_glu, LIMIT)
            h_lin = jnp.clip(h_lin, -LIMIT, LIMIT)
            acts.append((h_glu * _sigmoid(ALPHA * h_glu) * (h_lin + 1.0)).astype(BF16))
        y = _dot(jnp.concatenate(acts, axis=1), w2s[...]) + b2_ref[...]
        out_buffer_wait(slot)
        for s in range(ROW_TILE):
            ybuf[slot, pl.ds(s, MOE_BLOCK, stride=ROW_TILE), :] = y[:, s * 128:(s + 1) * 128]
        if not prefetch_next:
            def send(j, carry):
                out_copy(slot, j, row8_ref[(i + 1) * MOE_BLOCK + j]).start()
                return carry
            lax.fori_loop(0, MOE_BLOCK, send, 0, unroll=8)
            out_buffer_wait(1 - slot)
            out_buffer_wait(slot)

    @pl.when(i + 1 < n_used)
    def _():
        block(True)

    @pl.when(i + 1 == n_used)
    def _():
        block(False)


def _deinterleave_matrix():
    p = np.zeros((256, 256), np.float32)
    m = np.arange(128)
    p[2 * m, m] = 1.0
    p[2 * m + 1, 128 + m] = 1.0
    return jnp.asarray(p, BF16)


def _moe_experts(layer, fn_packed, gather_key, out_row8, block_e, n_used, next_expert, w1, b1, w2, b2):
    b1g = b1[0][layer].reshape(N_EXP, 1, D_FF)
    b1l = b1[1][layer].reshape(N_EXP, 1, D_FF)
    ex = lambda i, be, *_: (be[i], 0, 0)
    grid_spec = pltpu.PrefetchScalarGridSpec(
        num_scalar_prefetch=5,
        grid=(N_BLOCKS,),
        in_specs=[
            pl.BlockSpec(memory_space=pl.ANY),
            pl.BlockSpec(memory_space=pl.ANY),
            pl.BlockSpec((None, 1, D_FF), ex),
            pl.BlockSpec((None, 1, D_FF), ex),
            pl.BlockSpec(memory_space=pl.ANY),
            pl.BlockSpec((None, 1, D), ex),
            pl.BlockSpec((256, 256), lambda i, *_: (0, 0)),
        ],
        out_specs=pl.BlockSpec(memory_space=pl.ANY),
        scratch_shapes=[
            pltpu.VMEM((N * PACK_ROWS, 128), jnp.uint32),
            pltpu.VMEM((2, MOE_BLOCK * PACK_ROWS, 128), jnp.uint32),
            pltpu.VMEM((D, 2 * D_FF), F32),
            pltpu.VMEM((D_FF, D), F32),
            pltpu.VMEM((D, 2 * D_FF), BF16),
            pltpu.VMEM((D_FF, D), BF16),
            pltpu.SemaphoreType.DMA((3,)),
            pltpu.VMEM((2, MOE_BLOCK * ROW_TILE, 128), F32),
            pltpu.SemaphoreType.DMA((2,)),
        ],
    )
    return pl.pallas_call(
        functools.partial(_moe_kernel, layer=layer),
        out_shape=jax.ShapeDtypeStruct((Y_ROWS * ROW_TILE, 128), F32),
        grid_spec=grid_spec,
        compiler_params=_cp(("arbitrary",), MOE_VMEM_LIMIT),
        name="moe_experts",
    )(block_e, n_used, gather_key, out_row8, next_expert, fn_packed, w1, b1g, b1l, w2, b2[layer].reshape(N_EXP, 1, D),
      _deinterleave_matrix())


def _moe_residual(y_refs, g_ref, h_ref, m_ref, rows):
    g = g_ref[...]
    cols = []
    for s in range(ROW_TILE):
        y = y_refs[0][pl.ds(s, rows, stride=ROW_TILE), :] * g[:, 0:1]
        for k in range(1, TOP_K):
            y = y + y_refs[k][pl.ds(s, rows, stride=ROW_TILE), :] * g[:, k:k + 1]
        cols.append(y)
    return h_ref[...] + m_ref[5:6, :] * jnp.concatenate(cols, axis=1)


def _moe_stream_specs(rows, mod_row):
    steps = N // rows
    return [pl.BlockSpec((rows * ROW_TILE, 128), lambda i, k=k: (k * steps + i, 0)) for k in range(TOP_K)] + [
        pl.BlockSpec((rows, TOP_K), lambda i: (i, 0)),
        pl.BlockSpec((rows, D), lambda i: (i, 0)),
        pl.BlockSpec((None, 6, D), lambda i: (mod_row(i), 0, 0)),
    ]


def _combine_final_kernel(y0_ref, y1_ref, y2_ref, y3_ref, g_ref, h_ref, m_ref, fg_ref, yp_ref, yl_ref):
    i = pl.program_id(0)
    h_new = _moe_residual((y0_ref, y1_ref, y2_ref, y3_ref), g_ref, h_ref, m_ref, TILE)
    ms = jnp.mean(h_new * h_new, axis=-1, keepdims=True)
    y = h_new * lax.rsqrt(ms + EPS) * fg_ref[...]

    @pl.when(i < P_TILES)
    def _():
        yp_ref[...] = y

    @pl.when(i >= P_TILES)
    def _():
        yl_ref[...] = y


def _moe_combine_final(y_rows, gates_nt, h, mod, final_g):
    return pl.pallas_call(
        _combine_final_kernel,
        out_shape=(jax.ShapeDtypeStruct((N_P, D), F32), jax.ShapeDtypeStruct((N_S, D), F32)),
        grid=(N_TILES,),
        in_specs=_moe_stream_specs(TILE, _tile_mod_row) + [pl.BlockSpec((1, D), lambda i: (0, 0))],
        out_specs=[
            pl.BlockSpec((TILE, D), lambda i: (jnp.minimum(i, P_TILES - 1), 0)),
            pl.BlockSpec((TILE, D), lambda i: (jnp.maximum(i - P_TILES, 0), 0)),
        ],
        compiler_params=_cp(("arbitrary",), VMEM_LIMIT),
        name="moe_combine_final",
    )(y_rows, y_rows, y_rows, y_rows, gates_nt, h, mod, final_g.reshape(1, D))


COMBINE_ROWS = 512


def _combine_inproj_kernel(y0_ref, y1_ref, y2_ref, y3_ref, g_ref, h_ref, m_ref, gmix_ref, mnext_ref, w_ref,
                           hout_ref, proj_ref):
    h_new = _moe_residual((y0_ref, y1_ref, y2_ref, y3_ref), g_ref, h_ref, m_ref, COMBINE_ROWS)
    hout_ref[...] = h_new
    hn = _rms_mod(h_new, gmix_ref[...], mnext_ref[0:1, :], mnext_ref[1:2, :])
    proj_ref[...] = _dot(hn.astype(BF16), w_ref[...])


def _moe_combine_inproj(y_rows, gates_nt, h, mod, g_mix_next, mod_next, w_next_bf16):
    n_out = w_next_bf16.shape[1]
    p_steps, s_steps = N_P // COMBINE_ROWS, T_SAMPLE // COMBINE_ROWS

    def mod_row(i):
        return jnp.where(i < p_steps, 0, 1 + (i - p_steps) // s_steps)

    return pl.pallas_call(
        _combine_inproj_kernel,
        out_shape=(jax.ShapeDtypeStruct((N, D), F32), jax.ShapeDtypeStruct((N, n_out), F32)),
        grid=(N // COMBINE_ROWS,),
        in_specs=_moe_stream_specs(COMBINE_ROWS, mod_row) + [
            pl.BlockSpec((1, D), lambda i: (0, 0)),
            pl.BlockSpec((None, 6, D), lambda i: (mod_row(i), 0, 0)),
            pl.BlockSpec((D, n_out), lambda i: (0, 0)),
        ],
        out_specs=[
            pl.BlockSpec((COMBINE_ROWS, D), lambda i: (i, 0)),
            pl.BlockSpec((COMBINE_ROWS, n_out), lambda i: (i, 0)),
        ],
        compiler_params=_cp(("arbitrary",), VMEM_LIMIT),
        name="moe_combine_inproj",
    )(y_rows, y_rows, y_rows, y_rows, gates_nt, h, mod, g_mix_next.reshape(1, D), mod_next, w_next_bf16)


def _moe_expert_rows(layer, routed, w1, b1, w2, b2):
    fn_packed, top_e, gates, rank, counts = routed
    gather_key, out_row8, block_e, n_used, next_expert = _routing_tables(top_e, rank, counts)
    return _moe_experts(layer, fn_packed, gather_key, out_row8, block_e, n_used, next_expert, w1, b1, w2, b2)


def _gate_weights(w_r, w_i):
    per = LRU_BLK // LRU_HD
    eye = jnp.eye(per, dtype=F32)

    def blockdiag(w):
        w = w.reshape(2, LRU_HEADS // per, per, LRU_HD, LRU_HD)
        full = jnp.einsum("dgaij,ab->dgaibj", w, eye)
        return full.reshape(2, LRU_HEADS // per, LRU_BLK, LRU_BLK)

    return jnp.concatenate([blockdiag(w_r), blockdiag(w_i)], axis=-1).astype(BF16)


def kernel(x_prompt, x_sample, state_rglru, c, c_ctx, norm_mix_g, norm_ffn_g, w_mod, b_mod, w_in0, lru_conv_w, lru_conv_b, lru_w_r, lru_b_r, lru_w_i, lru_b_i, lru_lambda, w_out0, w_in1, sgu_ln_g, sgu_ln_b, sgu_w_s, sgu_b_s, conv_dw_w, conv_dw_b, conv_ln_g, conv_ln_b, w_out1, w_router, b_router, w1, b1, w2, b2, final_norm_g):
    h = (x_prompt.reshape(N_P, D), x_sample.reshape(N_S, D))
    cond8 = jnp.concatenate([c_ctx[None, :], c, jnp.zeros((N_MOD - 1 - N_SAMPLE_SEQ, D), F32)], axis=0)
    b1 = (b1[:, :, 0::2], b1[:, :, 1::2])
    mod = _adaln(cond8, w_mod, b_mod)

    proj0 = _inproj(h, norm_mix_g[0], mod[0], w_in0[0].astype(BF16))
    st = state_rglru[:, 0].astype(F32)
    h0 = jnp.zeros((2, N_MOD, LRU_W), F32).at[:, 1:1 + N_SAMPLE_SEQ].set(jnp.swapaxes(st, 0, 1))
    hs, ctx_state = _lru(proj0, lru_conv_w[0], lru_conv_b[0], _gate_weights(lru_w_r[0], lru_w_i[0]),
              lru_b_r[0], lru_b_i[0], lru_lambda[0], h0)
    h, *routed = _post0(hs, proj0, _fourier_prompt(proj0), _fourier_sample(proj0), w_out0[0].astype(BF16),
                        h, mod[0], norm_ffn_g[0], w_router[0].T, b_router[0])
    y_rows = _moe_expert_rows(0, routed, w1, b1, w2, b2)
    gates = routed[2]

    h, proj1 = _moe_combine_inproj(y_rows, gates.T, h, mod[0], norm_mix_g[1], mod[1], w_in1[0].astype(BF16))
    bs_full = jnp.repeat(sgu_b_s[0].T, CHUNK, axis=1)
    h, *routed = _post1(proj1, sgu_ln_g[0], sgu_ln_b[0], sgu_w_s[0].astype(BF16), bs_full,
                        conv_dw_w[0], conv_dw_b[0], conv_ln_g[0], conv_ln_b[0],
                        w_out1[0].astype(BF16), h, mod[1], norm_ffn_g[1], w_router[1].T, b_router[1])
    y_rows = _moe_expert_rows(1, routed, w1, b1, w2, b2)
    y_p, y_l = _moe_combine_final(y_rows, routed[2].T, h, mod[1], final_norm_g)

    y_prompt = y_p.reshape(N_PROMPT_SEQ, T_PROMPT, D)
    y_sample = y_l.reshape(N_SAMPLE_SEQ, T_SAMPLE, D)
    new_state = jnp.transpose(ctx_state, (1, 2, 0, 3))
    return (y_prompt, y_sample, new_state.astype(x_prompt.dtype))
```

```python
import functools

import numpy as np
import jax
import jax.numpy as jnp
from jax import lax
from jax.experimental import pallas as pl
from jax.experimental.pallas import tpu as pltpu

F32 = jnp.float32
BF16 = jnp.bfloat16

D = 1024
N_PROMPT_SEQ = 32
T_PROMPT = 256
N_SAMPLE_SEQ = 2
T_SAMPLE = 2048
N_P = N_PROMPT_SEQ * T_PROMPT
N_S = N_SAMPLE_SEQ * T_SAMPLE
N = N_P + N_S
EPS = 1e-6

TILE = 256
N_TILES = N // TILE
P_TILES = N_P // TILE
S_TILES = T_SAMPLE // TILE
TM_PROJ = 1024
N_MOD = 8

LRU_W = 768
LRU_HEADS = 12
LRU_HD = 64
LRU_K = 4
LRU_LEFT = 2
LRU_C = 8.0
LRU_BLK = 256
FN_W = 256
FN_G = 4
FN_GD = 64
IN0 = 2 * LRU_W + FN_W

SGU_W = 512
SGU_G = 4
CHUNK = 128
CONV_W = 512
CONV_K = 31
CONV_PAD = 15
CONV_HALO = 16
CONV_ROWS = 32
ROW_PIECE = 32
IN1 = 2 * SGU_W + 2 * CONV_W

N_EXP = 32
TOP_K = 4
D_FF = 1024
ALPHA = 1.702
LIMIT = 7.0
MOE_BLOCK = 256
N_ASSIGN = N * TOP_K
N_BLOCKS = N_ASSIGN // MOE_BLOCK + N_EXP
N_SLOTS = N_BLOCKS * MOE_BLOCK
ROW_TILE = D // 128
PACK_ROWS = D // 2 // 128
PACK_PAIR = 2 * PACK_ROWS

VMEM_LIMIT = 56 * 1024 * 1024
MOE_VMEM_LIMIT = 60 * 1024 * 1024


def _cp(sem, vmem=None):
    return pltpu.CompilerParams(dimension_semantics=sem, vmem_limit_bytes=vmem)


def _dot(a, b):
    return jnp.dot(a, b, preferred_element_type=F32)


def _split(x):
    hi = x.astype(BF16)
    lo = (x - hi.astype(F32)).astype(BF16)
    return hi, lo


def _dot3(a, b):
    ah, al = _split(a)
    bh, bl = _split(b)
    return _dot(ah, bh) + _dot(al, bh) + _dot(ah, bl)


def _dot3_nt(a, b):
    dn = (((1,), (1,)), ((), ()))
    d = lambda x, y: lax.dot_general(x, y, dn, preferred_element_type=F32)
    ah, al = _split(a)
    bh, bl = _split(b)
    return d(ah, bh) + d(al, bh) + d(ah, bl)


def _sigmoid(x):
    return 0.5 * jnp.tanh(0.5 * x) + 0.5


def _gelu(x):
    return 0.5 * x * (1.0 + jnp.tanh(0.7978845608028654 * (x + 0.044715 * (x * x * x))))


def _rms_mod(x, g, shift, scale):
    ms = jnp.mean(x * x, axis=-1, keepdims=True)
    y = x * lax.rsqrt(ms + EPS) * g
    return y * (1.0 + scale) + shift


def _layernorm(x, g, b):
    xc = x - jnp.mean(x, axis=-1, keepdims=True)
    var = jnp.mean(xc * xc, axis=-1, keepdims=True)
    return xc * lax.rsqrt(var + EPS) * g + b


def _tile_mod_row(r):
    return jnp.where(r < P_TILES, 0, 1 + (r - P_TILES) // S_TILES)


def _tile_is_seq_start(r):
    return (r < P_TILES) | ((r - P_TILES) % S_TILES == 0)


def _tile_is_seq_end(r):
    return (r < P_TILES) | ((r - P_TILES) % S_TILES == S_TILES - 1)


MOD_TN = 512


def _adaln_kernel(cond_ref, w_ref, b_ref, o_ref):
    cond = cond_ref[...]
    s = cond * _sigmoid(cond)
    o_ref[...] = _dot3(s, w_ref[...]) + b_ref[...]


def _adaln(cond8, w_mod, b_mod):
    depth = w_mod.shape[0]
    out = pl.pallas_call(
        _adaln_kernel,
        out_shape=jax.ShapeDtypeStruct((depth, N_MOD, 6 * D), F32),
        grid=(depth, 6 * D // MOD_TN),
        in_specs=[
            pl.BlockSpec((N_MOD, D), lambda l, j: (0, 0)),
            pl.BlockSpec((None, D, MOD_TN), lambda l, j: (l, 0, j)),
            pl.BlockSpec((None, 1, MOD_TN), lambda l, j: (l, 0, j)),
        ],
        out_specs=pl.BlockSpec((None, N_MOD, MOD_TN), lambda l, j: (l, 0, j)),
        compiler_params=_cp(("arbitrary", "arbitrary")),
        name="adaln",
    )(cond8, w_mod, b_mod.reshape(depth, 1, 6 * D))
    return out.reshape(depth, N_MOD, 6, D)


PROJ_P_STEPS = N_P // TM_PROJ
PROJ_S_STEPS = T_SAMPLE // TM_PROJ


def _stream_specs(rows, p_steps, h):
    if isinstance(h, tuple):
        hp, hl = h
        first_latent = 0
    else:
        hp = hl = h
        first_latent = p_steps
    specs = [
        pl.BlockSpec((rows, D), lambda i, *_: (jnp.minimum(i, p_steps - 1), 0)),
        pl.BlockSpec((rows, D), lambda i, *_: (jnp.maximum(i - p_steps, 0) + first_latent, 0)),
    ]
    return specs, (hp, hl)


def _inproj_kernel(xp_ref, xl_ref, g_ref, m_ref, w_ref, o_ref):
    x = jnp.where(pl.program_id(0) < PROJ_P_STEPS, xp_ref[...], xl_ref[...])
    hn = _rms_mod(x, g_ref[...], m_ref[0:1, :], m_ref[1:2, :])
    o_ref[...] = _dot(hn.astype(BF16), w_ref[...])


def _inproj(h, g, mod, w_bf16):
    n_out = w_bf16.shape[1]

    def mod_row(i):
        return jnp.where(i < PROJ_P_STEPS, 0, 1 + (i - PROJ_P_STEPS) // PROJ_S_STEPS)

    h_specs, h_args = _stream_specs(TM_PROJ, PROJ_P_STEPS, h)
    return pl.pallas_call(
        _inproj_kernel,
        out_shape=jax.ShapeDtypeStruct((N, n_out), F32),
        grid=(N // TM_PROJ,),
        in_specs=h_specs + [
            pl.BlockSpec((1, D), lambda i: (0, 0)),
            pl.BlockSpec((None, 6, D), lambda i: (mod_row(i), 0, 0)),
            pl.BlockSpec((D, n_out), lambda i: (0, 0)),
        ],
        out_specs=pl.BlockSpec((TM_PROJ, n_out), lambda i: (i, 0)),
        compiler_params=_cp(("arbitrary",), VMEM_LIMIT),
        name="inproj",
    )(*h_args, g.reshape(1, D), mod, w_bf16)


LRU_HALO = 8
SCAN_ROWS = 8


def _lru_tile(d, s):
    return jnp.where(d == 0, s, N_TILES - 1 - s)


def _lru_kernel(x_ref, prev_ref, next_ref, cw_ref, cb_ref, wg_ref, br_ref, bi_ref, lam_ref, h0_ref,
                o_ref, state_ref, ext_ref, a_ref, carry_ref):
    d = pl.program_id(0)
    r = _lru_tile(d, pl.program_id(1))
    start = _tile_is_seq_start(r)
    end = _tile_is_seq_end(r)

    ext_ref[0:LRU_HALO, :] = jnp.where(start, 0.0, prev_ref[...])
    ext_ref[LRU_HALO:LRU_HALO + TILE, :] = x_ref[...]
    ext_ref[LRU_HALO + TILE:, :] = jnp.where(end, 0.0, next_ref[...])
    xc = cb_ref[...] + jnp.zeros((TILE, LRU_W), F32)
    for k in range(LRU_K):
        off = LRU_HALO - LRU_LEFT + k
        xc = xc + ext_ref[off:off + TILE, :] * cw_ref[k:k + 1, :]

    xcb = xc.astype(BF16)
    pre_r, pre_i = [], []
    for blk in range(LRU_W // LRU_BLK):
        g = _dot(xcb[:, blk * LRU_BLK:(blk + 1) * LRU_BLK], wg_ref[blk])
        pre_r.append(g[:, :LRU_BLK])
        pre_i.append(g[:, LRU_BLK:])
    gate_r = _sigmoid(jnp.concatenate(pre_r, axis=1) + br_ref[...])
    gate_i = _sigmoid(jnp.concatenate(pre_i, axis=1) + bi_ref[...])
    neg_lam = -lam_ref[...]
    softplus = jnp.maximum(neg_lam, 0.0) + jnp.log1p(jnp.exp(-jnp.abs(neg_lam)))
    log_a = (-LRU_C) * gate_r * softplus
    a = jnp.exp(log_a)
    a_ref[...] = a
    o_ref[...] = jnp.sqrt(-jnp.tanh(log_a) * (a * a + 1.0)) * (gate_i * xc)

    fresh = jnp.where(d == 0, start, end)
    h_init = jnp.where(fresh, h0_ref[pl.ds(_tile_mod_row(r), 1), :], carry_ref[...])

    row = lax.broadcasted_iota(jnp.int32, (SCAN_ROWS, LRU_W), 0)

    def scan_group(g, h, reverse):
        rows = pl.ds(pl.multiple_of(g * SCAN_ROWS, SCAN_ROWS), SCAN_ROWS)
        a = a_ref[rows, :]
        b = o_ref[rows, :]
        for s in (1, 2, 4):
            shift = SCAN_ROWS - s if reverse else s
            inside = (row < SCAN_ROWS - s) if reverse else (row >= s)
            b = jnp.where(inside, a * pltpu.roll(b, shift, 0) + b, b)
            a = jnp.where(inside, a * pltpu.roll(a, shift, 0), a)
        hs = a * h + b
        o_ref[rows, :] = hs
        return hs[0:1, :] if reverse else hs[SCAN_ROWS - 1:SCAN_ROWS, :]

    n_groups = TILE // SCAN_ROWS

    @pl.when(d == 0)
    def _():
        carry_ref[...] = lax.fori_loop(0, n_groups, lambda g, h: scan_group(g, h, False), h_init, unroll=4)

    @pl.when(d == 1)
    def _():
        carry_ref[...] = lax.fori_loop(0, n_groups, lambda g, h: scan_group(n_groups - 1 - g, h, True),
                                       h_init, unroll=4)

    h_last = carry_ref[...]

    @pl.when(r < P_TILES)
    def _():
        state_ref[...] = h_last


def _lru(proj0, conv_w, conv_b, wg, b_r, b_i, lam, h0):
    n_halo_blocks = N // LRU_HALO
    per_tile = TILE // LRU_HALO
    tile = lambda d, s: _lru_tile(d, s)
    return pl.pallas_call(
        _lru_kernel,
        out_shape=(jax.ShapeDtypeStruct((2, N, LRU_W), F32),
                   jax.ShapeDtypeStruct((2, N_PROMPT_SEQ, 1, LRU_W), F32)),
        grid=(2, N_TILES),
        in_specs=[
            pl.BlockSpec((TILE, LRU_W), lambda d, s: (tile(d, s), 0)),
            pl.BlockSpec((LRU_HALO, LRU_W), lambda d, s: (jnp.maximum(tile(d, s) * per_tile - 1, 0), 0)),
            pl.BlockSpec((LRU_HALO, LRU_W),
                         lambda d, s: (jnp.minimum((tile(d, s) + 1) * per_tile, n_halo_blocks - 1), 0)),
            pl.BlockSpec((LRU_K, LRU_W), lambda d, s: (0, 0)),
            pl.BlockSpec((1, LRU_W), lambda d, s: (0, 0)),
            pl.BlockSpec((None, LRU_W // LRU_BLK, LRU_BLK, 2 * LRU_BLK), lambda d, s: (d, 0, 0, 0)),
            pl.BlockSpec((None, 1, LRU_W), lambda d, s: (d, 0, 0)),
            pl.BlockSpec((None, 1, LRU_W), lambda d, s: (d, 0, 0)),
            pl.BlockSpec((None, 1, LRU_W), lambda d, s: (d, 0, 0)),
            pl.BlockSpec((None, N_MOD, LRU_W), lambda d, s: (d, 0, 0)),
        ],
        out_specs=[
            pl.BlockSpec((None, TILE, LRU_W), lambda d, s: (d, tile(d, s), 0)),
            pl.BlockSpec((None, None, 1, LRU_W), lambda d, s: (d, jnp.minimum(tile(d, s), P_TILES - 1), 0, 0)),
        ],
        scratch_shapes=[
            pltpu.VMEM((TILE + 2 * LRU_HALO, LRU_W), F32),
            pltpu.VMEM((TILE, LRU_W), F32),
            pltpu.VMEM((1, LRU_W), F32),
        ],
        compiler_params=_cp(("arbitrary", "arbitrary"), VMEM_LIMIT),
        name="rglru_scan",
    )(proj0, proj0, proj0, conv_w, conv_b.reshape(1, LRU_W), wg,
      b_r.reshape(2, 1, LRU_W), b_i.reshape(2, 1, LRU_W), lam.reshape(2, 1, LRU_W), h0)


def _dft_tables(n, scale):
    k = np.arange(n, dtype=np.int64)
    ang = 2.0 * np.pi * ((k[:, None] * k[None, :]) % n).astype(np.float64) / n
    return np.cos(ang) * scale, np.sin(ang) * scale


def _channel_tables():
    c, s = _dft_tables(FN_GD, FN_GD ** -0.5)
    eye = np.eye(FN_G)
    return (jnp.asarray(np.kron(eye, c), BF16), jnp.asarray(np.kron(eye, s), BF16))


def _time_tables(t_len):
    c, s = _dft_tables(t_len, t_len ** -0.5)
    return jnp.asarray(c, BF16), jnp.asarray(s, BF16)


FN_SEQ_PER_STEP = 4


def _fourier_prompt_kernel(z_ref, cc_ref, sc_ref, ct_ref, st_ref, o_ref):
    z = z_ref[...].astype(BF16)
    zc = _dot(z, cc_ref[...]).astype(BF16)
    zs = _dot(z, sc_ref[...]).astype(BF16)
    for b in range(FN_SEQ_PER_STEP):
        rows = slice(b * T_PROMPT, (b + 1) * T_PROMPT)
        o_ref[rows, :] = _dot(ct_ref[...], zc[rows, :]) - _dot(st_ref[...], zs[rows, :])


def _fourier_prompt(proj0):
    cc, sc = _channel_tables()
    ct, st = _time_tables(T_PROMPT)
    const = lambda b: (0, 0)
    return pl.pallas_call(
        _fourier_prompt_kernel,
        out_shape=jax.ShapeDtypeStruct((N_P, FN_W), F32),
        grid=(N_PROMPT_SEQ // FN_SEQ_PER_STEP,),
        in_specs=[
            pl.BlockSpec((FN_SEQ_PER_STEP * T_PROMPT, FN_W), lambda b: (b, 2 * LRU_W // FN_W)),
            pl.BlockSpec((FN_W, FN_W), const),
            pl.BlockSpec((FN_W, FN_W), const),
            pl.BlockSpec((T_PROMPT, T_PROMPT), const),
            pl.BlockSpec((T_PROMPT, T_PROMPT), const),
        ],
        out_specs=pl.BlockSpec((FN_SEQ_PER_STEP * T_PROMPT, FN_W), lambda b: (b, 0)),
        compiler_params=_cp(("arbitrary",)),
        name="fourier_prompt",
    )(proj0, cc, sc, ct, st)


def _fourier_sample_kernel(z_ref, cc_ref, sc_ref, ct_ref, st_ref, o_ref, zc_ref, zs_ref):
    @pl.when(pl.program_id(1) == 0)
    def _():
        z = z_ref[...].astype(BF16)
        zc_ref[...] = _dot(z, cc_ref[...]).astype(BF16)
        zs_ref[...] = _dot(z, sc_ref[...]).astype(BF16)

    o_ref[...] = _dot(ct_ref[...], zc_ref[...]) - _dot(st_ref[...], zs_ref[...])


def _fourier_sample(proj0):
    cc, sc = _channel_tables()
    ct, st = _time_tables(T_SAMPLE)
    const = lambda b, i: (0, 0)
    first_seq_block = N_P // T_SAMPLE
    return pl.pallas_call(
        _fourier_sample_kernel,
        out_shape=jax.ShapeDtypeStruct((N_S, FN_W), F32),
        grid=(N_SAMPLE_SEQ, S_TILES),
        in_specs=[
            pl.BlockSpec((T_SAMPLE, FN_W), lambda b, i: (first_seq_block + b, 2 * LRU_W // FN_W)),
            pl.BlockSpec((FN_W, FN_W), const),
            pl.BlockSpec((FN_W, FN_W), const),
            pl.BlockSpec((TILE, T_SAMPLE), lambda b, i: (i, 0)),
            pl.BlockSpec((TILE, T_SAMPLE), lambda b, i: (i, 0)),
        ],
        out_specs=pl.BlockSpec((TILE, FN_W), lambda b, i: (b * S_TILES + i, 0)),
        scratch_shapes=[pltpu.VMEM((T_SAMPLE, FN_W), BF16), pltpu.VMEM((T_SAMPLE, FN_W), BF16)],
        compiler_params=_cp(("arbitrary", "arbitrary"), VMEM_LIMIT),
        name="fourier_sample",
    )(proj0, cc, sc, ct, st)


def _epilogue(mix, rows, hp_ref, hl_ref, m_ref, gf_ref, wr_ref, brt_ref, tri_ref,
              hout_ref, fn_ref, tope_ref, gate_ref, rank_ref, cnt_ref, run_ref):
    h = jnp.where(pl.program_id(0) < N_P // rows, hp_ref[...], hl_ref[...])
    h_new = h + m_ref[2:3, :] * mix
    hout_ref[...] = h_new
    fn = _rms_mod(h_new, gf_ref[...], m_ref[3:4, :], m_ref[4:5, :])
    hi = lax.bitcast_convert_type(fn[:, :D // 2].astype(BF16).astype(F32), jnp.uint32)
    lo = lax.bitcast_convert_type(fn[:, D // 2:].astype(BF16).astype(F32), jnp.uint32)
    packed = hi | (lo >> 16)
    for q in range(PACK_ROWS):
        fn_ref[pl.ds(q, rows, stride=PACK_ROWS), :] = packed[:, q * 128:(q + 1) * 128]
    logits = _dot3_nt(wr_ref[...], fn) + brt_ref[...]
    iota = lax.broadcasted_iota(jnp.int32, logits.shape, 0)
    vals, idxs = [], []
    for _ in range(TOP_K):
        m = jnp.max(logits, axis=0, keepdims=True)
        idx = jnp.min(jnp.where(logits == m, iota, N_EXP), axis=0, keepdims=True)
        vals.append(m)
        idxs.append(idx)
        logits = jnp.where(iota == idx, -jnp.inf, logits)
    exps = [jnp.exp(v - vals[0]) for v in vals]
    denom = exps[0] + exps[1] + exps[2] + exps[3]
    for k in range(TOP_K):
        tope_ref[k:k + 1, :] = idxs[k]
        gate_ref[k:k + 1, :] = exps[k] / denom

    @pl.when(pl.program_id(0) == 0)
    def _():
        run_ref[...] = jnp.zeros_like(run_ref)

    run = run_ref[...]
    onehots = [jnp.where(iota == idxs[k], 1.0, 0.0) for k in range(TOP_K)]
    incl_all = _dot(jnp.concatenate(onehots, axis=0).astype(BF16), tri_ref[...])
    for k in range(TOP_K):
        incl = incl_all[k * N_EXP:(k + 1) * N_EXP, :]
        rank = jnp.sum(onehots[k] * (incl - 1.0 + run), axis=0, keepdims=True)
        rank_ref[k:k + 1, :] = rank.astype(jnp.int32)
        run = run + incl[:, rows - 1:rows]
    run_ref[...] = run
    cnt_ref[...] = run.astype(jnp.int32)


_EPI_OUT_SHAPES = (
    jax.ShapeDtypeStruct((N, D), F32),
    jax.ShapeDtypeStruct((N * PACK_ROWS, 128), jnp.uint32),
    jax.ShapeDtypeStruct((TOP_K, N), jnp.int32),
    jax.ShapeDtypeStruct((TOP_K, N), F32),
    jax.ShapeDtypeStruct((TOP_K, N), jnp.int32),
    jax.ShapeDtypeStruct((N_EXP, 1), jnp.int32),
)
_EPI_SCRATCH = [pltpu.VMEM((N_EXP, 1), F32)]


def _epi_operands(h, mod, g_ffn, w_router_t, b_router, rows=TILE):
    tri = jnp.asarray(np.triu(np.ones((rows, rows), np.float32)), BF16)
    return (*_stream_specs(rows, N_P // rows, h)[1], mod, g_ffn.reshape(1, D), w_router_t,
            b_router.reshape(N_EXP, 1), tri)


def _epi_in_specs(h, rows=TILE):
    p_steps, s_steps = N_P // rows, T_SAMPLE // rows
    h_specs, _ = _stream_specs(rows, p_steps, h)
    mod_row = lambda r: jnp.where(r < p_steps, 0, 1 + (r - p_steps) // s_steps)
    return h_specs + [
        pl.BlockSpec((None, 6, D), lambda r: (mod_row(r), 0, 0)),
        pl.BlockSpec((1, D), lambda r: (0, 0)),
        pl.BlockSpec((N_EXP, D), lambda r: (0, 0)),
        pl.BlockSpec((N_EXP, 1), lambda r: (0, 0)),
        pl.BlockSpec((rows, rows), lambda r: (0, 0)),
    ]


def _epi_out_specs(rows=TILE):
    return [
        pl.BlockSpec((rows, D), lambda r: (r, 0)),
        pl.BlockSpec((rows * PACK_ROWS, 128), lambda r: (r, 0)),
        pl.BlockSpec((TOP_K, rows), lambda r: (0, r)),
        pl.BlockSpec((TOP_K, rows), lambda r: (0, r)),
        pl.BlockSpec((TOP_K, rows), lambda r: (0, r)),
        pl.BlockSpec((N_EXP, 1), lambda r: (0, 0)),
    ]


POST0_ROWS = 512
POST0_P_STEPS = N_P // POST0_ROWS


def _post0_kernel(hs_ref, xg_ref, yfp_ref, yfl_ref, wo_ref, *epilogue_refs):
    y_rec = (hs_ref[0] + hs_ref[1]) * _gelu(xg_ref[...])
    y_four = jnp.where(pl.program_id(0) < POST0_P_STEPS, yfp_ref[...], yfl_ref[...])
    mix = (_dot(y_rec.astype(BF16), wo_ref[0:LRU_W, :])
           + _dot(y_four.astype(BF16), wo_ref[LRU_W:, :]))
    _epilogue(mix, POST0_ROWS, *epilogue_refs)


def _post0(hs, proj0, yf_prompt, yf_latent, w_out_bf16, h, mod, g_ffn, w_router_t, b_router):
    return pl.pallas_call(
        _post0_kernel,
        out_shape=_EPI_OUT_SHAPES,
        grid=(N // POST0_ROWS,),
        in_specs=[
            pl.BlockSpec((2, POST0_ROWS, LRU_W), lambda r: (0, r, 0)),
            pl.BlockSpec((POST0_ROWS, LRU_W), lambda r: (r, 1)),
            pl.BlockSpec((POST0_ROWS, FN_W), lambda r: (jnp.minimum(r, POST0_P_STEPS - 1), 0)),
            pl.BlockSpec((POST0_ROWS, FN_W), lambda r: (jnp.maximum(r - POST0_P_STEPS, 0), 0)),
            pl.BlockSpec((D, D), lambda r: (0, 0)),
        ] + _epi_in_specs(h, POST0_ROWS),
        out_specs=_epi_out_specs(POST0_ROWS),
        scratch_shapes=_EPI_SCRATCH,
        compiler_params=_cp(("arbitrary",), VMEM_LIMIT),
        name="post_rglru_fourier",
    )(hs, proj0, yf_prompt, yf_latent, w_out_bf16,
      *_epi_operands(h, mod, g_ffn, w_router_t, b_router, POST0_ROWS))


POST1_ROWS = 2 * TILE
POST1_P_STEPS = N_P // POST1_ROWS


def _glu(x):
    return x[:, :CONV_W] * _sigmoid(x[:, CONV_W:])


def _post1_kernel(p_ref, prev_ref, next_ref, lng_ref, lnb_ref, ws_ref, bs_ref, dww_ref, dwb_ref,
                  clg_ref, clb_ref, wo_ref, *rest):
    epilogue_refs, (ext_ref, shift_ref, u_ref, v_ref, mix_ref) = rest[:-5], rest[-5:]
    j = pl.program_id(0)
    is_prompt = j < POST1_P_STEPS
    pos_in_seq = (j - POST1_P_STEPS) % (T_SAMPLE // POST1_ROWS)
    first_starts_seq = is_prompt | (pos_in_seq == 0)
    second_ends_seq = is_prompt | (pos_in_seq == T_SAMPLE // POST1_ROWS - 1)

    for c in range(POST1_ROWS // ROW_PIECE):
        rows = slice(c * ROW_PIECE, (c + 1) * ROW_PIECE)
        z = _gelu(p_ref[rows, 0:2 * SGU_W])
        u_ref[rows, :] = z[:, :SGU_W]
        v_ref[rows, :] = _layernorm(z[:, SGU_W:], lng_ref[...], lnb_ref[...]).astype(BF16)
        half, start = divmod(c * ROW_PIECE, TILE)
        ext_ref[half, CONV_HALO + start:CONV_HALO + start + ROW_PIECE, :] = _glu(p_ref[rows, 2 * SGU_W:])
    ext_ref[0, 0:CONV_HALO, :] = jnp.where(first_starts_seq, 0.0, _glu(prev_ref[...]))
    ext_ref[0, CONV_HALO + TILE:, :] = jnp.where(is_prompt, 0.0, ext_ref[1, CONV_HALO:2 * CONV_HALO, :])
    ext_ref[1, 0:CONV_HALO, :] = jnp.where(is_prompt, 0.0, ext_ref[0, TILE:TILE + CONV_HALO, :])
    ext_ref[1, CONV_HALO + TILE:, :] = jnp.where(second_ends_seq, 0.0, _glu(next_ref[...]))

    for n in range(POST1_ROWS // CHUNK):
        for g in range(SGU_G):
            rows, cols = slice(n * CHUNK, (n + 1) * CHUNK), slice(g * CHUNK, (g + 1) * CHUNK)
            s = _dot(ws_ref[g], v_ref[rows, cols]) + bs_ref[:, cols]
            mix_ref[rows, cols] = (u_ref[rows, cols] * s).astype(BF16)

    n_shift_rows = TILE + 2 * CONV_HALO - 8
    for half in range(POST1_ROWS // TILE):
        for phase in range(8):
            shift_ref[half, phase] = ext_ref[half, phase:phase + n_shift_rows, :]
        for c in range(TILE // CONV_ROWS):
            acc = dwb_ref[...] + jnp.zeros((CONV_ROWS // 8, 8, CONV_W), F32)
            for k in range(CONV_K):
                off = CONV_HALO - CONV_PAD + k
                start = off // 8 * 8 + c * CONV_ROWS
                rows = shift_ref[half, off % 8, start:start + CONV_ROWS, :].reshape(CONV_ROWS // 8, 8, CONV_W)
                acc = acc + rows * dww_ref[k]
            ln = _layernorm(acc.reshape(CONV_ROWS, CONV_W), clg_ref[...], clb_ref[...])
            out_rows = slice(half * TILE + c * CONV_ROWS, half * TILE + (c + 1) * CONV_ROWS)
            mix_ref[out_rows, SGU_W:] = (ln * _sigmoid(ln)).astype(BF16)

    _epilogue(_dot(mix_ref[...], wo_ref[...]), POST1_ROWS, *epilogue_refs)


def _post1(proj1, ln_g, ln_b, ws_bf16, bs_full, dw_w, dw_b, cln_g, cln_b, w_out_bf16,
           h, mod, g_ffn, w_router_t, b_router):
    per_step = POST1_ROWS // CONV_HALO
    n_halo_blocks = N // CONV_HALO
    const2 = lambda r: (0, 0)
    row = lambda a: a.reshape(1, -1)
    return pl.pallas_call(
        _post1_kernel,
        out_shape=_EPI_OUT_SHAPES,
        grid=(N // POST1_ROWS,),
        in_specs=[
            pl.BlockSpec((POST1_ROWS, IN1), lambda r: (r, 0)),
            pl.BlockSpec((CONV_HALO, 2 * CONV_W), lambda r: (jnp.maximum(r * per_step - 1, 0), 1)),
            pl.BlockSpec((CONV_HALO, 2 * CONV_W),
                         lambda r: (jnp.minimum((r + 1) * per_step, n_halo_blocks - 1), 1)),
            pl.BlockSpec((1, SGU_W), const2),
            pl.BlockSpec((1, SGU_W), const2),
            pl.BlockSpec((SGU_G, CHUNK, CHUNK), lambda r: (0, 0, 0)),
            pl.BlockSpec((CHUNK, SGU_W), const2),
            pl.BlockSpec((CONV_K, 8, CONV_W), lambda r: (0, 0, 0)),
            pl.BlockSpec((1, CONV_W), const2),
            pl.BlockSpec((1, CONV_W), const2),
            pl.BlockSpec((1, CONV_W), const2),
            pl.BlockSpec((D, D), const2),
        ] + _epi_in_specs(h, POST1_ROWS),
        out_specs=_epi_out_specs(POST1_ROWS),
        scratch_shapes=_EPI_SCRATCH + [
            pltpu.VMEM((POST1_ROWS // TILE, TILE + 2 * CONV_HALO, CONV_W), F32),
            pltpu.VMEM((POST1_ROWS // TILE, 8, TILE + 2 * CONV_HALO - 8, CONV_W), F32),
            pltpu.VMEM((POST1_ROWS, SGU_W), F32),
            pltpu.VMEM((POST1_ROWS, SGU_W), BF16),
            pltpu.VMEM((POST1_ROWS, D), BF16),
        ],
        compiler_params=_cp(("arbitrary",), VMEM_LIMIT),
        name="post_sgu_conformer",
    )(proj1, proj1, proj1, row(ln_g), row(ln_b), ws_bf16, bs_full,
      jnp.broadcast_to(dw_w[:, None, :], (CONV_K, 8, CONV_W)), row(dw_b), row(cln_g), row(cln_b),
      w_out_bf16, *_epi_operands(h, mod, g_ffn, w_router_t, b_router, POST1_ROWS))


PAD_BASE = N_ASSIGN
PLACEHOLDER_BASE = PAD_BASE + 2 * MOE_BLOCK
Y_ROWS = PLACEHOLDER_BASE + 2 * MOE_BLOCK


def _inverse_kernel(pos_ref, pend_ref, out_ref):
    def fill_block(start):
        def fill(j, carry):
            s = start + j
            out_ref[s] = PAD_BASE + (s & (2 * MOE_BLOCK - 1))
            return carry
        lax.fori_loop(0, MOE_BLOCK, fill, 0, unroll=16)

    def per_expert(e, carry):
        fill_block(jnp.maximum(pend_ref[e] - MOE_BLOCK, 0))
        return carry

    def idle_block(b, carry):
        fill_block(b * MOE_BLOCK)
        return carry

    lax.fori_loop(0, N_EXP, per_expert, 0)
    lax.fori_loop(pend_ref[N_EXP - 1] // MOE_BLOCK, N_BLOCKS, idle_block, 0)

    def body(t, carry):
        for k in range(TOP_K):
            a = t * TOP_K + k
            out_ref[pos_ref[a]] = a
        return carry

    lax.fori_loop(0, N, body, 0, unroll=8)


def _inverse(pos_flat, pad_end):
    smem = pl.BlockSpec(memory_space=pltpu.SMEM)
    return pl.pallas_call(
        _inverse_kernel,
        out_shape=jax.ShapeDtypeStruct((N_SLOTS,), jnp.int32),
        in_specs=[smem, smem],
        out_specs=smem,
        name="moe_inverse",
    )(pos_flat, pad_end)


def _routing_tables(top_e, rank, counts):
    counts = counts.reshape(N_EXP)
    padded = (counts + MOE_BLOCK - 1) // MOE_BLOCK * MOE_BLOCK
    pad_end = jnp.cumsum(padded).astype(jnp.int32)
    pad_start = pad_end - padded
    experts = jnp.arange(N_EXP, dtype=jnp.int32)[:, None, None]
    pos = rank + jnp.sum(jnp.where(top_e[None] == experts, pad_start[:, None, None], 0), axis=0)
    pos = pos.astype(jnp.int32).T.reshape(-1)
    n_used = pad_end[-1] // MOE_BLOCK
    blk = jnp.minimum(jnp.arange(N_BLOCKS, dtype=jnp.int32), n_used - 1)
    block_e = jnp.sum(pad_end[None, :] <= (blk * MOE_BLOCK)[:, None], axis=1).astype(jnp.int32)
    slot_a = _inverse(pos, pad_end)
    tok = jnp.minimum(slot_a >> 2, N - 1)
    gather_key = (tok >> 1) * (2 * PACK_PAIR) + (tok & 1)
    out_row = jnp.where(slot_a >= PAD_BASE, slot_a, (slot_a & (TOP_K - 1)) * N + (slot_a >> 2))
    placeholder_rows = PLACEHOLDER_BASE + jnp.arange(MOE_BLOCK, dtype=jnp.int32)
    out_row8 = jnp.concatenate([placeholder_rows, out_row]) * ROW_TILE
    ids = jnp.arange(N_EXP, dtype=jnp.int32)
    later_used = (ids[None, :] > ids[:, None]) & (counts[None, :] > 0)
    next_expert = jnp.min(jnp.where(later_used, ids[None, :], N_EXP), axis=1).astype(jnp.int32)
    return gather_key, out_row8, block_e, n_used.reshape(1), next_expert


def _moe_kernel(be_ref, nu_ref, gk_ref, row8_ref, nxt_ref, x_hbm, w1_hbm, b1g_ref, b1l_ref, w2_hbm, b2_ref, perm_ref,
                y_hbm, xs, xbuf, w1f, w2f, w1s, w2s, sem, ybuf, osem, *, layer):
    i = pl.program_id(0)
    n_used = nu_ref[0]
    slot = i % 2
    e = be_ref[i]
    e_prev = be_ref[jnp.maximum(i - 1, 0)]
    low_half = lax.broadcasted_iota(jnp.int32, (PACK_PAIR, 128), 0) < PACK_ROWS

    def weight_copies(expert):
        return (pltpu.make_async_copy(w1_hbm.at[layer, expert], w1f, sem.at[1]),
                pltpu.make_async_copy(w2_hbm.at[layer, expert], w2f, sem.at[2]))

    def token_tile(key):
        tile = xs[pl.ds(pl.multiple_of(key >> 1, PACK_PAIR), PACK_PAIR), :]
        return tile, pltpu.roll(tile, PACK_ROWS, 0), key & 1

    def gather_pair(blk, jj, dst_slot):
        t0, r0, half0 = token_tile(gk_ref[blk * MOE_BLOCK + 2 * jj])
        t1, r1, half1 = token_tile(gk_ref[blk * MOE_BLOCK + 2 * jj + 1])
        lower = jnp.where(half0 == 0, t0, r0)
        upper = jnp.where(half1 == 1, t1, r1)
        start = jj * PACK_PAIR if isinstance(jj, int) else pl.multiple_of(jj * PACK_PAIR, PACK_PAIR)
        xbuf[dst_slot, pl.ds(start, PACK_PAIR), :] = jnp.where(low_half, lower, upper)

    def out_copy(src_slot, j, row8):
        start = j * ROW_TILE if isinstance(j, int) else pl.multiple_of(j * ROW_TILE, ROW_TILE)
        dst = y_hbm.at[pl.ds(pl.multiple_of(row8, ROW_TILE), ROW_TILE), :]
        return pltpu.make_async_copy(ybuf.at[src_slot, pl.ds(start, ROW_TILE), :], dst, osem.at[src_slot])

    def out_buffer_wait(src_slot):
        pltpu.make_async_copy(ybuf.at[src_slot], y_hbm.at[pl.ds(0, MOE_BLOCK * ROW_TILE), :],
                              osem.at[src_slot]).wait()

    @pl.when(i == 0)
    def _():
        xs_copy = pltpu.make_async_copy(x_hbm, xs, sem.at[0])
        xs_copy.start()
        for c in weight_copies(e):
            c.start()
        ybuf[...] = jnp.zeros_like(ybuf)

        def zero_rows(first_row):
            def body(j, carry):
                out_copy(0, j, (first_row + j) * ROW_TILE).start()
                return carry
            lax.fori_loop(0, MOE_BLOCK, body, 0, unroll=8)

        zero_rows(PAD_BASE)
        zero_rows(PAD_BASE + MOE_BLOCK)
        out_buffer_wait(0)
        out_buffer_wait(0)
        zero_rows(PLACEHOLDER_BASE + MOE_BLOCK)
        xs_copy.wait()

        def body(jj, carry):
            gather_pair(0, jj, 0)
            return carry
        lax.fori_loop(0, MOE_BLOCK // 2, body, 0, unroll=4)

    @pl.when((i < n_used) & ((i == 0) | (e != e_prev)))
    def _():
        for c in weight_copies(e):
            c.wait()
        for c in range(2 * D_FF // 256):
            wc = w1f[:, c * 256:(c + 1) * 256].astype(BF16)
            w1s[:, c * 256:(c + 1) * 256] = _dot(wc, perm_ref[...]).astype(BF16)
        w2s[...] = w2f[...].astype(BF16)
        e_next = nxt_ref[e]

        @pl.when(e_next < N_EXP)
        def _():
            for c in weight_copies(e_next):
                c.start()

    def block(prefetch_next):
        for j in range(MOE_BLOCK):
            out_copy(1 - slot, j, row8_ref[i * MOE_BLOCK + j]).start()
        halves = [[], []]
        for q in range(PACK_ROWS):
            words = xbuf[slot, pl.ds(q, MOE_BLOCK, stride=PACK_ROWS), :]
            halves[0].append(lax.bitcast_convert_type(words & jnp.uint32(0xFFFF0000), F32).astype(BF16))
            halves[1].append(lax.bitcast_convert_type(words << 16, F32).astype(BF16))
        x = jnp.concatenate(halves[0] + halves[1], axis=1)
        if prefetch_next:
            for jj in range(MOE_BLOCK // 2):
                gather_pair(i + 1, jj, 1 - slot)
        hid = _dot(x, w1s[...])
        acts = []
        for c in range(D_FF // 128):
            h_glu = hid[:, c * 256:c * 256 + 128] + b1g_ref[:, c * 128:(c + 1) * 128]
            h_lin = hid[:, c * 256 + 128:(c + 1) * 256] + b1l_ref[:, c * 128:(c + 1) * 128]
            h_glu = jnp.minimum(h_glu, LIMIT)
            h_lin = jnp.clip(h_lin, -LIMIT, LIMIT)
            acts.append((h_glu * _sigmoid(ALPHA * h_glu) * (h_lin + 1.0)).astype(BF16))
        y = _dot(jnp.concatenate(acts, axis=1), w2s[...]) + b2_ref[...]
        out_buffer_wait(slot)
        for s in range(ROW_TILE):
            ybuf[slot, pl.ds(s, MOE_BLOCK, stride=ROW_TILE), :] = y[:, s * 128:(s + 1) * 128]
        if not prefetch_next:
            def send(j, carry):
                out_copy(slot, j, row8_ref[(i + 1) * MOE_BLOCK + j]).start()
                return carry
            lax.fori_loop(0, MOE_BLOCK, send, 0, unroll=8)
            out_buffer_wait(1 - slot)
            out_buffer_wait(slot)

    @pl.when(i + 1 < n_used)
    def _():
        block(True)

    @pl.when(i + 1 == n_used)
    def _():
        block(False)


def _deinterleave_matrix():
    p = np.zeros((256, 256), np.float32)
    m = np.arange(128)
    p[2 * m, m] = 1.0
    p[2 * m + 1, 128 + m] = 1.0
    return jnp.asarray(p, BF16)


def _moe_experts(layer, fn_packed, gather_key, out_row8, block_e, n_used, next_expert, w1, b1, w2, b2):
    b1g = b1[0][layer].reshape(N_EXP, 1, D_FF)
    b1l = b1[1][layer].reshape(N_EXP, 1, D_FF)
    ex = lambda i, be, *_: (be[i], 0, 0)
    grid_spec = pltpu.PrefetchScalarGridSpec(
        num_scalar_prefetch=5,
        grid=(N_BLOCKS,),
        in_specs=[
            pl.BlockSpec(memory_space=pl.ANY),
            pl.BlockSpec(memory_space=pl.ANY),
            pl.BlockSpec((None, 1, D_FF), ex),
            pl.BlockSpec((None, 1, D_FF), ex),
            pl.BlockSpec(memory_space=pl.ANY),
            pl.BlockSpec((None, 1, D), ex),
            pl.BlockSpec((256, 256), lambda i, *_: (0, 0)),
        ],
        out_specs=pl.BlockSpec(memory_space=pl.ANY),
        scratch_shapes=[
            pltpu.VMEM((N * PACK_ROWS, 128), jnp.uint32),
            pltpu.VMEM((2, MOE_BLOCK * PACK_ROWS, 128), jnp.uint32),
            pltpu.VMEM((D, 2 * D_FF), F32),
            pltpu.VMEM((D_FF, D), F32),
            pltpu.VMEM((D, 2 * D_FF), BF16),
            pltpu.VMEM((D_FF, D), BF16),
            pltpu.SemaphoreType.DMA((3,)),
            pltpu.VMEM((2, MOE_BLOCK * ROW_TILE, 128), F32),
            pltpu.SemaphoreType.DMA((2,)),
        ],
    )
    return pl.pallas_call(
        functools.partial(_moe_kernel, layer=layer),
        out_shape=jax.ShapeDtypeStruct((Y_ROWS * ROW_TILE, 128), F32),
        grid_spec=grid_spec,
        compiler_params=_cp(("arbitrary",), MOE_VMEM_LIMIT),
        name="moe_experts",
    )(block_e, n_used, gather_key, out_row8, next_expert, fn_packed, w1, b1g, b1l, w2, b2[layer].reshape(N_EXP, 1, D),
      _deinterleave_matrix())


def _moe_residual(y_refs, g_ref, h_ref, m_ref, rows):
    g = g_ref[...]
    cols = []
    for s in range(ROW_TILE):
        y = y_refs[0][pl.ds(s, rows, stride=ROW_TILE), :] * g[:, 0:1]
        for k in range(1, TOP_K):
            y = y + y_refs[k][pl.ds(s, rows, stride=ROW_TILE), :] * g[:, k:k + 1]
        cols.append(y)
    return h_ref[...] + m_ref[5:6, :] * jnp.concatenate(cols, axis=1)


def _moe_stream_specs(rows, mod_row):
    steps = N // rows
    return [pl.BlockSpec((rows * ROW_TILE, 128), lambda i, k=k: (k * steps + i, 0)) for k in range(TOP_K)] + [
        pl.BlockSpec((rows, TOP_K), lambda i: (i, 0)),
        pl.BlockSpec((rows, D), lambda i: (i, 0)),
        pl.BlockSpec((None, 6, D), lambda i: (mod_row(i), 0, 0)),
    ]


def _combine_final_kernel(y0_ref, y1_ref, y2_ref, y3_ref, g_ref, h_ref, m_ref, fg_ref, yp_ref, yl_ref):
    i = pl.program_id(0)
    h_new = _moe_residual((y0_ref, y1_ref, y2_ref, y3_ref), g_ref, h_ref, m_ref, TILE)
    ms = jnp.mean(h_new * h_new, axis=-1, keepdims=True)
    y = h_new * lax.rsqrt(ms + EPS) * fg_ref[...]

    @pl.when(i < P_TILES)
    def _():
        yp_ref[...] = y

    @pl.when(i >= P_TILES)
    def _():
        yl_ref[...] = y


def _moe_combine_final(y_rows, gates_nt, h, mod, final_g):
    return pl.pallas_call(
        _combine_final_kernel,
        out_shape=(jax.ShapeDtypeStruct((N_P, D), F32), jax.ShapeDtypeStruct((N_S, D), F32)),
        grid=(N_TILES,),
        in_specs=_moe_stream_specs(TILE, _tile_mod_row) + [pl.BlockSpec((1, D), lambda i: (0, 0))],
        out_specs=[
            pl.BlockSpec((TILE, D), lambda i: (jnp.minimum(i, P_TILES - 1), 0)),
            pl.BlockSpec((TILE, D), lambda i: (jnp.maximum(i - P_TILES, 0), 0)),
        ],
        compiler_params=_cp(("arbitrary",), VMEM_LIMIT),
        name="moe_combine_final",
    )(y_rows, y_rows, y_rows, y_rows, gates_nt, h, mod, final_g.reshape(1, D))


COMBINE_ROWS = 512


def _combine_inproj_kernel(y0_ref, y1_ref, y2_ref, y3_ref, g_ref, h_ref, m_ref, gmix_ref, mnext_ref, w_ref,
                           hout_ref, proj_ref):
    h_new = _moe_residual((y0_ref, y1_ref, y2_ref, y3_ref), g_ref, h_ref, m_ref, COMBINE_ROWS)
    hout_ref[...] = h_new
    hn = _rms_mod(h_new, gmix_ref[...], mnext_ref[0:1, :], mnext_ref[1:2, :])
    proj_ref[...] = _dot(hn.astype(BF16), w_ref[...])


def _moe_combine_inproj(y_rows, gates_nt, h, mod, g_mix_next, mod_next, w_next_bf16):
    n_out = w_next_bf16.shape[1]
    p_steps, s_steps = N_P // COMBINE_ROWS, T_SAMPLE // COMBINE_ROWS

    def mod_row(i):
        return jnp.where(i < p_steps, 0, 1 + (i - p_steps) // s_steps)

    return pl.pallas_call(
        _combine_inproj_kernel,
        out_shape=(jax.ShapeDtypeStruct((N, D), F32), jax.ShapeDtypeStruct((N, n_out), F32)),
        grid=(N // COMBINE_ROWS,),
        in_specs=_moe_stream_specs(COMBINE_ROWS, mod_row) + [
            pl.BlockSpec((1, D), lambda i: (0, 0)),
            pl.BlockSpec((None, 6, D), lambda i: (mod_row(i), 0, 0)),
            pl.BlockSpec((D, n_out), lambda i: (0, 0)),
        ],
        out_specs=[
            pl.BlockSpec((COMBINE_ROWS, D), lambda i: (i, 0)),
            pl.BlockSpec((COMBINE_ROWS, n_out), lambda i: (i, 0)),
        ],
        compiler_params=_cp(("arbitrary",), VMEM_LIMIT),
        name="moe_combine_inproj",
    )(y_rows, y_rows, y_rows, y_rows, gates_nt, h, mod, g_mix_next.reshape(1, D), mod_next, w_next_bf16)


def _moe_expert_rows(layer, routed, w1, b1, w2, b2):
    fn_packed, top_e, gates, rank, counts = routed
    gather_key, out_row8, block_e, n_used, next_expert = _routing_tables(top_e, rank, counts)
    return _moe_experts(layer, fn_packed, gather_key, out_row8, block_e, n_used, next_expert, w1, b1, w2, b2)


def _gate_weights(w_r, w_i):
    per = LRU_BLK // LRU_HD
    eye = jnp.eye(per, dtype=F32)

    def blockdiag(w):
        w = w.reshape(2, LRU_HEADS // per, per, LRU_HD, LRU_HD)
        full = jnp.einsum("dgaij,ab->dgaibj", w, eye)
        return full.reshape(2, LRU_HEADS // per, LRU_BLK, LRU_BLK)

    return jnp.concatenate([blockdiag(w_r), blockdiag(w_i)], axis=-1).astype(BF16)


def kernel(x_prompt, x_sample, state_rglru, c, c_ctx, norm_mix_g, norm_ffn_g, w_mod, b_mod, w_in0, lru_conv_w, lru_conv_b, lru_w_r, lru_b_r, lru_w_i, lru_b_i, lru_lambda, w_out0, w_in1, sgu_ln_g, sgu_ln_b, sgu_w_s, sgu_b_s, conv_dw_w, conv_dw_b, conv_ln_g, conv_ln_b, w_out1, w_router, b_router, w1, b1, w2, b2, final_norm_g):
    h = (x_prompt.reshape(N_P, D), x_sample.reshape(N_S, D))
    cond8 = jnp.concatenate([c_ctx[None, :], c, jnp.zeros((N_MOD - 1 - N_SAMPLE_SEQ, D), F32)], axis=0)
    b1 = (b1[:, :, 0::2], b1[:, :, 1::2])
    mod = _adaln(cond8, w_mod, b_mod)

    proj0 = _inproj(h, norm_mix_g[0], mod[0], w_in0[0].astype(BF16))
    st = state_rglru[:, 0].astype(F32)
    h0 = jnp.zeros((2, N_MOD, LRU_W), F32).at[:, 1:1 + N_SAMPLE_SEQ].set(jnp.swapaxes(st, 0, 1))
    hs, ctx_state = _lru(proj0, lru_conv_w[0], lru_conv_b[0], _gate_weights(lru_w_r[0], lru_w_i[0]),
              lru_b_r[0], lru_b_i[0], lru_lambda[0], h0)
    h, *routed = _post0(hs, proj0, _fourier_prompt(proj0), _fourier_sample(proj0), w_out0[0].astype(BF16),
                        h, mod[0], norm_ffn_g[0], w_router[0].T, b_router[0])
    y_rows = _moe_expert_rows(0, routed, w1, b1, w2, b2)
    gates = routed[2]

    h, proj1 = _moe_combine_inproj(y_rows, gates.T, h, mod[0], norm_mix_g[1], mod[1], w_in1[0].astype(BF16))
    bs_full = jnp.repeat(sgu_b_s[0].T, CHUNK, axis=1)
    h, *routed = _post1(proj1, sgu_ln_g[0], sgu_ln_b[0], sgu_w_s[0].astype(BF16), bs_full,
                        conv_dw_w[0], conv_dw_b[0], conv_ln_g[0], conv_ln_b[0],
                        w_out1[0].astype(BF16), h, mod[1], norm_ffn_g[1], w_router[1].T, b_router[1])
    y_rows = _moe_expert_rows(1, routed, w1, b1, w2, b2)
    y_p, y_l = _moe_combine_final(y_rows, routed[2].T, h, mod[1], final_norm_g)

    y_prompt = y_p.reshape(N_PROMPT_SEQ, T_PROMPT, D)
    y_sample = y_l.reshape(N_SAMPLE_SEQ, T_SAMPLE, D)
    new_state = jnp.transpose(ctx_state, (1, 2, 0, 3))
    return (y_prompt, y_sample, new_state.astype(x_prompt.dtype))
```

```python
import functools

import numpy as np
import jax
import jax.numpy as jnp
from jax import lax
from jax.experimental import pallas as pl
from jax.experimental.pallas import tpu as pltpu

F32 = jnp.float32
BF16 = jnp.bfloat16

D = 1024
N_PROMPT_SEQ = 32
T_PROMPT = 256
N_SAMPLE_SEQ = 2
T_SAMPLE = 2048
N_P = N_PROMPT_SEQ * T_PROMPT
N_S = N_SAMPLE_SEQ * T_SAMPLE
N = N_P + N_S
EPS = 1e-6

TILE = 256
N_TILES = N // TILE
P_TILES = N_P // TILE
S_TILES = T_SAMPLE // TILE
TM_PROJ = 1024
N_MOD = 8

LRU_W = 768
LRU_HEADS = 12
LRU_HD = 64
LRU_K = 4
LRU_LEFT = 2
LRU_C = 8.0
LRU_BLK = 256
FN_W = 256
FN_G = 4
FN_GD = 64
IN0 = 2 * LRU_W + FN_W

SGU_W = 512
SGU_G = 4
CHUNK = 128
CONV_W = 512
CONV_K = 31
CONV_PAD = 15
CONV_HALO = 16
CONV_ROWS = 32
ROW_PIECE = 32
IN1 = 2 * SGU_W + 2 * CONV_W

N_EXP = 32
TOP_K = 4
D_FF = 1024
ALPHA = 1.702
LIMIT = 7.0
MOE_BLOCK = 256
N_ASSIGN = N * TOP_K
N_BLOCKS = N_ASSIGN // MOE_BLOCK + N_EXP
N_SLOTS = N_BLOCKS * MOE_BLOCK
ROW_TILE = D // 128
PACK_ROWS = D // 2 // 128
PACK_PAIR = 2 * PACK_ROWS

VMEM_LIMIT = 56 * 1024 * 1024
MOE_VMEM_LIMIT = 60 * 1024 * 1024


def _cp(sem, vmem=None):
    return pltpu.CompilerParams(dimension_semantics=sem, vmem_limit_bytes=vmem)


def _dot(a, b):
    return jnp.dot(a, b, preferred_element_type=F32)


def _split(x):
    hi = x.astype(BF16)
    lo = (x - hi.astype(F32)).astype(BF16)
    return hi, lo


def _dot3(a, b):
    ah, al = _split(a)
    bh, bl = _split(b)
    return _dot(ah, bh) + _dot(al, bh) + _dot(ah, bl)


def _dot3_nt(a, b):
    dn = (((1,), (1,)), ((), ()))
    d = lambda x, y: lax.dot_general(x, y, dn, preferred_element_type=F32)
    ah, al = _split(a)
    bh, bl = _split(b)
    return d(ah, bh) + d(al, bh) + d(ah, bl)


def _pack_bf16_pairs(x):
    hi = lax.bitcast_convert_type(x[:, :D // 2].astype(BF16).astype(F32), jnp.uint32)
    lo = lax.bitcast_convert_type(x[:, D // 2:].astype(BF16).astype(F32), jnp.uint32)
    return hi | (lo >> 16)


def _unpack_bf16_pairs(words):
    hi = lax.bitcast_convert_type(words & jnp.uint32(0xFFFF0000), F32)
    lo = lax.bitcast_convert_type(words << 16, F32)
    return hi, lo


def _sigmoid(x):
    return 0.5 * jnp.tanh(0.5 * x) + 0.5


def _gelu(x):
    return 0.5 * x * (1.0 + jnp.tanh(0.7978845608028654 * (x + 0.044715 * (x * x * x))))


def _rms_mod(x, g, shift, scale):
    ms = jnp.mean(x * x, axis=-1, keepdims=True)
    y = x * lax.rsqrt(ms + EPS) * g
    return y * (1.0 + scale) + shift


def _layernorm(x, g, b):
    xc = x - jnp.mean(x, axis=-1, keepdims=True)
    var = jnp.mean(xc * xc, axis=-1, keepdims=True)
    return xc * lax.rsqrt(var + EPS) * g + b


def _tile_mod_row(r):
    return jnp.where(r < P_TILES, 0, 1 + (r - P_TILES) // S_TILES)


def _tile_is_seq_start(r):
    return (r < P_TILES) | ((r - P_TILES) % S_TILES == 0)


def _tile_is_seq_end(r):
    return (r < P_TILES) | ((r - P_TILES) % S_TILES == S_TILES - 1)


MOD_TN = 512


def _adaln_kernel(cond_ref, w_ref, b_ref, o_ref):
    cond = cond_ref[...]
    s = cond * _sigmoid(cond)
    o_ref[...] = _dot3(s, w_ref[...]) + b_ref[...]


def _adaln(cond8, w_mod, b_mod):
    depth = w_mod.shape[0]
    out = pl.pallas_call(
        _adaln_kernel,
        out_shape=jax.ShapeDtypeStruct((depth, N_MOD, 6 * D), F32),
        grid=(depth, 6 * D // MOD_TN),
        in_specs=[
            pl.BlockSpec((N_MOD, D), lambda l, j: (0, 0)),
            pl.BlockSpec((None, D, MOD_TN), lambda l, j: (l, 0, j)),
            pl.BlockSpec((None, 1, MOD_TN), lambda l, j: (l, 0, j)),
        ],
        out_specs=pl.BlockSpec((None, N_MOD, MOD_TN), lambda l, j: (l, 0, j)),
        compiler_params=_cp(("arbitrary", "arbitrary")),
        name="adaln",
    )(cond8, w_mod, b_mod.reshape(depth, 1, 6 * D))
    return out.reshape(depth, N_MOD, 6, D)


PROJ_P_STEPS = N_P // TM_PROJ
PROJ_S_STEPS = T_SAMPLE // TM_PROJ


def _stream_specs(rows, p_steps, h):
    if isinstance(h, tuple):
        hp, hl = h
        first_latent = 0
    else:
        hp = hl = h
        first_latent = p_steps
    specs = [
        pl.BlockSpec((rows, D), lambda i, *_: (jnp.minimum(i, p_steps - 1), 0)),
        pl.BlockSpec((rows, D), lambda i, *_: (jnp.maximum(i - p_steps, 0) + first_latent, 0)),
    ]
    return specs, (hp, hl)


def _inproj_kernel(xp_ref, xl_ref, g_ref, m_ref, w_ref, o_ref):
    x = jnp.where(pl.program_id(0) < PROJ_P_STEPS, xp_ref[...], xl_ref[...])
    hn = _rms_mod(x, g_ref[...], m_ref[0:1, :], m_ref[1:2, :])
    o_ref[...] = _dot(hn.astype(BF16), w_ref[...])


def _inproj(h, g, mod, w_bf16):
    n_out = w_bf16.shape[1]

    def mod_row(i):
        return jnp.where(i < PROJ_P_STEPS, 0, 1 + (i - PROJ_P_STEPS) // PROJ_S_STEPS)

    h_specs, h_args = _stream_specs(TM_PROJ, PROJ_P_STEPS, h)
    return pl.pallas_call(
        _inproj_kernel,
        out_shape=jax.ShapeDtypeStruct((N, n_out), F32),
        grid=(N // TM_PROJ,),
        in_specs=h_specs + [
            pl.BlockSpec((1, D), lambda i: (0, 0)),
            pl.BlockSpec((None, 6, D), lambda i: (mod_row(i), 0, 0)),
            pl.BlockSpec((D, n_out), lambda i: (0, 0)),
        ],
        out_specs=pl.BlockSpec((TM_PROJ, n_out), lambda i: (i, 0)),
        compiler_params=_cp(("arbitrary",), VMEM_LIMIT),
        name="inproj",
    )(*h_args, g.reshape(1, D), mod, w_bf16)


LRU_HALO = 8
SCAN_ROWS = 8


def _lru_tile(d, s):
    return jnp.where(d == 0, s, N_TILES - 1 - s)


def _lru_kernel(x_ref, prev_ref, next_ref, cw_ref, cb_ref, wg_ref, br_ref, bi_ref, lam_ref, h0_ref,
                o_ref, state_ref, ext_ref, a_ref, carry_ref):
    d = pl.program_id(0)
    r = _lru_tile(d, pl.program_id(1))
    start = _tile_is_seq_start(r)
    end = _tile_is_seq_end(r)

    ext_ref[0:LRU_HALO, :] = jnp.where(start, 0.0, prev_ref[...])
    ext_ref[LRU_HALO:LRU_HALO + TILE, :] = x_ref[...]
    ext_ref[LRU_HALO + TILE:, :] = jnp.where(end, 0.0, next_ref[...])
    xc = cb_ref[...] + jnp.zeros((TILE, LRU_W), F32)
    for k in range(LRU_K):
        off = LRU_HALO - LRU_LEFT + k
        xc = xc + ext_ref[off:off + TILE, :] * cw_ref[k:k + 1, :]

    xcb = xc.astype(BF16)
    pre_r, pre_i = [], []
    for blk in range(LRU_W // LRU_BLK):
        g = _dot(xcb[:, blk * LRU_BLK:(blk + 1) * LRU_BLK], wg_ref[blk])
        pre_r.append(g[:, :LRU_BLK])
        pre_i.append(g[:, LRU_BLK:])
    gate_r = _sigmoid(jnp.concatenate(pre_r, axis=1) + br_ref[...])
    gate_i = _sigmoid(jnp.concatenate(pre_i, axis=1) + bi_ref[...])
    neg_lam = -lam_ref[...]
    softplus = jnp.maximum(neg_lam, 0.0) + jnp.log1p(jnp.exp(-jnp.abs(neg_lam)))
    log_a = (-LRU_C) * gate_r * softplus
    a = jnp.exp(log_a)
    a_ref[...] = a
    o_ref[...] = jnp.sqrt(-jnp.tanh(log_a) * (a * a + 1.0)) * (gate_i * xc)

    fresh = jnp.where(d == 0, start, end)
    h_init = jnp.where(fresh, h0_ref[pl.ds(_tile_mod_row(r), 1), :], carry_ref[...])

    row = lax.broadcasted_iota(jnp.int32, (SCAN_ROWS, LRU_W), 0)

    def scan_group(g, h, reverse):
        rows = pl.ds(pl.multiple_of(g * SCAN_ROWS, SCAN_ROWS), SCAN_ROWS)
        a = a_ref[rows, :]
        b = o_ref[rows, :]
        for s in (1, 2, 4):
            shift = SCAN_ROWS - s if reverse else s
            inside = (row < SCAN_ROWS - s) if reverse else (row >= s)
            b = jnp.where(inside, a * pltpu.roll(b, shift, 0) + b, b)
            a = jnp.where(inside, a * pltpu.roll(a, shift, 0), a)
        hs = a * h + b
        o_ref[rows, :] = hs
        return hs[0:1, :] if reverse else hs[SCAN_ROWS - 1:SCAN_ROWS, :]

    n_groups = TILE // SCAN_ROWS

    @pl.when(d == 0)
    def _():
        carry_ref[...] = lax.fori_loop(0, n_groups, lambda g, h: scan_group(g, h, False), h_init, unroll=4)

    @pl.when(d == 1)
    def _():
        carry_ref[...] = lax.fori_loop(0, n_groups, lambda g, h: scan_group(n_groups - 1 - g, h, True),
                                       h_init, unroll=4)

    h_last = carry_ref[...]

    @pl.when(r < P_TILES)
    def _():
        state_ref[...] = h_last


def _lru(proj0, conv_w, conv_b, wg, b_r, b_i, lam, h0):
    n_halo_blocks = N // LRU_HALO
    per_tile = TILE // LRU_HALO
    tile = lambda d, s: _lru_tile(d, s)
    return pl.pallas_call(
        _lru_kernel,
        out_shape=(jax.ShapeDtypeStruct((2, N, LRU_W), F32),
                   jax.ShapeDtypeStruct((2, N_PROMPT_SEQ, 1, LRU_W), F32)),
        grid=(2, N_TILES),
        in_specs=[
            pl.BlockSpec((TILE, LRU_W), lambda d, s: (tile(d, s), 0)),
            pl.BlockSpec((LRU_HALO, LRU_W), lambda d, s: (jnp.maximum(tile(d, s) * per_tile - 1, 0), 0)),
            pl.BlockSpec((LRU_HALO, LRU_W),
                         lambda d, s: (jnp.minimum((tile(d, s) + 1) * per_tile, n_halo_blocks - 1), 0)),
            pl.BlockSpec((LRU_K, LRU_W), lambda d, s: (0, 0)),
            pl.BlockSpec((1, LRU_W), lambda d, s: (0, 0)),
            pl.BlockSpec((None, LRU_W // LRU_BLK, LRU_BLK, 2 * LRU_BLK), lambda d, s: (d, 0, 0, 0)),
            pl.BlockSpec((None, 1, LRU_W), lambda d, s: (d, 0, 0)),
            pl.BlockSpec((None, 1, LRU_W), lambda d, s: (d, 0, 0)),
            pl.BlockSpec((None, 1, LRU_W), lambda d, s: (d, 0, 0)),
            pl.BlockSpec((None, N_MOD, LRU_W), lambda d, s: (d, 0, 0)),
        ],
        out_specs=[
            pl.BlockSpec((None, TILE, LRU_W), lambda d, s: (d, tile(d, s), 0)),
            pl.BlockSpec((None, None, 1, LRU_W), lambda d, s: (d, jnp.minimum(tile(d, s), P_TILES - 1), 0, 0)),
        ],
        scratch_shapes=[
            pltpu.VMEM((TILE + 2 * LRU_HALO, LRU_W), F32),
            pltpu.VMEM((TILE, LRU_W), F32),
            pltpu.VMEM((1, LRU_W), F32),
        ],
        compiler_params=_cp(("arbitrary", "arbitrary"), VMEM_LIMIT),
        name="rglru_scan",
    )(proj0, proj0, proj0, conv_w, conv_b.reshape(1, LRU_W), wg,
      b_r.reshape(2, 1, LRU_W), b_i.reshape(2, 1, LRU_W), lam.reshape(2, 1, LRU_W), h0)


def _dft_tables(n, scale):
    k = np.arange(n, dtype=np.int64)
    ang = 2.0 * np.pi * ((k[:, None] * k[None, :]) % n).astype(np.float64) / n
    return np.cos(ang) * scale, np.sin(ang) * scale


def _channel_tables():
    c, s = _dft_tables(FN_GD, FN_GD ** -0.5)
    eye = np.eye(FN_G)
    return (jnp.asarray(np.kron(eye, c), BF16), jnp.asarray(np.kron(eye, s), BF16))


def _time_tables(t_len):
    c, s = _dft_tables(t_len, t_len ** -0.5)
    return jnp.asarray(c, BF16), jnp.asarray(s, BF16)


FN_SEQ_PER_STEP = 4


def _fourier_prompt_kernel(z_ref, cc_ref, sc_ref, ct_ref, st_ref, o_ref):
    z = z_ref[...].astype(BF16)
    zc = _dot(z, cc_ref[...]).astype(BF16)
    zs = _dot(z, sc_ref[...]).astype(BF16)
    for b in range(FN_SEQ_PER_STEP):
        rows = slice(b * T_PROMPT, (b + 1) * T_PROMPT)
        o_ref[rows, :] = _dot(ct_ref[...], zc[rows, :]) - _dot(st_ref[...], zs[rows, :])


def _fourier_prompt(proj0):
    cc, sc = _channel_tables()
    ct, st = _time_tables(T_PROMPT)
    const = lambda b: (0, 0)
    return pl.pallas_call(
        _fourier_prompt_kernel,
        out_shape=jax.ShapeDtypeStruct((N_P, FN_W), F32),
        grid=(N_PROMPT_SEQ // FN_SEQ_PER_STEP,),
        in_specs=[
            pl.BlockSpec((FN_SEQ_PER_STEP * T_PROMPT, FN_W), lambda b: (b, 2 * LRU_W // FN_W)),
            pl.BlockSpec((FN_W, FN_W), const),
            pl.BlockSpec((FN_W, FN_W), const),
            pl.BlockSpec((T_PROMPT, T_PROMPT), const),
            pl.BlockSpec((T_PROMPT, T_PROMPT), const),
        ],
        out_specs=pl.BlockSpec((FN_SEQ_PER_STEP * T_PROMPT, FN_W), lambda b: (b, 0)),
        compiler_params=_cp(("arbitrary",)),
        name="fourier_prompt",
    )(proj0, cc, sc, ct, st)


def _fourier_sample_kernel(z_ref, cc_ref, sc_ref, ct_ref, st_ref, o_ref, zc_ref, zs_ref):
    @pl.when(pl.program_id(1) == 0)
    def _():
        z = z_ref[...].astype(BF16)
        zc_ref[...] = _dot(z, cc_ref[...]).astype(BF16)
        zs_ref[...] = _dot(z, sc_ref[...]).astype(BF16)

    o_ref[...] = _dot(ct_ref[...], zc_ref[...]) - _dot(st_ref[...], zs_ref[...])


def _fourier_sample(proj0):
    cc, sc = _channel_tables()
    ct, st = _time_tables(T_SAMPLE)
    const = lambda b, i: (0, 0)
    first_seq_block = N_P // T_SAMPLE
    return pl.pallas_call(
        _fourier_sample_kernel,
        out_shape=jax.ShapeDtypeStruct((N_S, FN_W), F32),
        grid=(N_SAMPLE_SEQ, S_TILES),
        in_specs=[
            pl.BlockSpec((T_SAMPLE, FN_W), lambda b, i: (first_seq_block + b, 2 * LRU_W // FN_W)),
            pl.BlockSpec((FN_W, FN_W), const),
            pl.BlockSpec((FN_W, FN_W), const),
            pl.BlockSpec((TILE, T_SAMPLE), lambda b, i: (i, 0)),
            pl.BlockSpec((TILE, T_SAMPLE), lambda b, i: (i, 0)),
        ],
        out_specs=pl.BlockSpec((TILE, FN_W), lambda b, i: (b * S_TILES + i, 0)),
        scratch_shapes=[pltpu.VMEM((T_SAMPLE, FN_W), BF16), pltpu.VMEM((T_SAMPLE, FN_W), BF16)],
        compiler_params=_cp(("arbitrary", "arbitrary"), VMEM_LIMIT),
        name="fourier_sample",
    )(proj0, cc, sc, ct, st)


def _epilogue(mix, rows, hp_ref, hl_ref, m_ref, gf_ref, wr_ref, brt_ref, tri_ref,
              hout_ref, fn_ref, tope_ref, gate_ref, rank_ref, cnt_ref, run_ref):
    h = jnp.where(pl.program_id(0) < N_P // rows, hp_ref[...], hl_ref[...])
    h_new = h + m_ref[2:3, :] * mix
    hout_ref[...] = h_new
    fn = _rms_mod(h_new, gf_ref[...], m_ref[3:4, :], m_ref[4:5, :])
    packed = _pack_bf16_pairs(fn)
    for q in range(PACK_ROWS):
        fn_ref[pl.ds(q, rows, stride=PACK_ROWS), :] = packed[:, q * 128:(q + 1) * 128]
    logits = _dot3_nt(wr_ref[...], fn) + brt_ref[...]
    iota = lax.broadcasted_iota(jnp.int32, logits.shape, 0)
    vals, idxs = [], []
    for _ in range(TOP_K):
        m = jnp.max(logits, axis=0, keepdims=True)
        idx = jnp.min(jnp.where(logits == m, iota, N_EXP), axis=0, keepdims=True)
        vals.append(m)
        idxs.append(idx)
        logits = jnp.where(iota == idx, -jnp.inf, logits)
    exps = [jnp.exp(v - vals[0]) for v in vals]
    denom = exps[0] + exps[1] + exps[2] + exps[3]
    for k in range(TOP_K):
        tope_ref[k:k + 1, :] = idxs[k]
        gate_ref[k:k + 1, :] = exps[k] / denom

    @pl.when(pl.program_id(0) == 0)
    def _():
        run_ref[...] = jnp.zeros_like(run_ref)

    run = run_ref[...]
    onehots = [jnp.where(iota == idxs[k], 1.0, 0.0) for k in range(TOP_K)]
    incl_all = _dot(jnp.concatenate(onehots, axis=0).astype(BF16), tri_ref[...])
    for k in range(TOP_K):
        incl = incl_all[k * N_EXP:(k + 1) * N_EXP, :]
        rank = jnp.sum(onehots[k] * (incl - 1.0 + run), axis=0, keepdims=True)
        rank_ref[k:k + 1, :] = rank.astype(jnp.int32)
        run = run + incl[:, rows - 1:rows]
    run_ref[...] = run
    cnt_ref[...] = run.astype(jnp.int32)


_EPI_OUT_SHAPES = (
    jax.ShapeDtypeStruct((N, D), F32),
    jax.ShapeDtypeStruct((N * PACK_ROWS, 128), jnp.uint32),
    jax.ShapeDtypeStruct((TOP_K, N), jnp.int32),
    jax.ShapeDtypeStruct((TOP_K, N), F32),
    jax.ShapeDtypeStruct((TOP_K, N), jnp.int32),
    jax.ShapeDtypeStruct((N_EXP, 1), jnp.int32),
)
_EPI_SCRATCH = [pltpu.VMEM((N_EXP, 1), F32)]


def _epi_operands(h, mod, g_ffn, w_router_t, b_router, rows=TILE):
    tri = jnp.asarray(np.triu(np.ones((rows, rows), np.float32)), BF16)
    return (*_stream_specs(rows, N_P // rows, h)[1], mod, g_ffn.reshape(1, D), w_router_t,
            b_router.reshape(N_EXP, 1), tri)


def _epi_in_specs(h, rows=TILE):
    p_steps, s_steps = N_P // rows, T_SAMPLE // rows
    h_specs, _ = _stream_specs(rows, p_steps, h)
    mod_row = lambda r: jnp.where(r < p_steps, 0, 1 + (r - p_steps) // s_steps)
    return h_specs + [
        pl.BlockSpec((None, 6, D), lambda r: (mod_row(r), 0, 0)),
        pl.BlockSpec((1, D), lambda r: (0, 0)),
        pl.BlockSpec((N_EXP, D), lambda r: (0, 0)),
        pl.BlockSpec((N_EXP, 1), lambda r: (0, 0)),
        pl.BlockSpec((rows, rows), lambda r: (0, 0)),
    ]


def _epi_out_specs(rows=TILE):
    return [
        pl.BlockSpec((rows, D), lambda r: (r, 0)),
        pl.BlockSpec((rows * PACK_ROWS, 128), lambda r: (r, 0)),
        pl.BlockSpec((TOP_K, rows), lambda r: (0, r)),
        pl.BlockSpec((TOP_K, rows), lambda r: (0, r)),
        pl.BlockSpec((TOP_K, rows), lambda r: (0, r)),
        pl.BlockSpec((N_EXP, 1), lambda r: (0, 0)),
    ]


POST0_ROWS = 512
POST0_P_STEPS = N_P // POST0_ROWS


def _post0_kernel(hs_ref, xg_ref, yfp_ref, yfl_ref, wo_ref, *epilogue_refs):
    y_rec = (hs_ref[0] + hs_ref[1]) * _gelu(xg_ref[...])
    y_four = jnp.where(pl.program_id(0) < POST0_P_STEPS, yfp_ref[...], yfl_ref[...])
    mix = (_dot(y_rec.astype(BF16), wo_ref[0:LRU_W, :])
           + _dot(y_four.astype(BF16), wo_ref[LRU_W:, :]))
    _epilogue(mix, POST0_ROWS, *epilogue_refs)


def _post0(hs, proj0, yf_prompt, yf_latent, w_out_bf16, h, mod, g_ffn, w_router_t, b_router):
    return pl.pallas_call(
        _post0_kernel,
        out_shape=_EPI_OUT_SHAPES,
        grid=(N // POST0_ROWS,),
        in_specs=[
            pl.BlockSpec((2, POST0_ROWS, LRU_W), lambda r: (0, r, 0)),
            pl.BlockSpec((POST0_ROWS, LRU_W), lambda r: (r, 1)),
            pl.BlockSpec((POST0_ROWS, FN_W), lambda r: (jnp.minimum(r, POST0_P_STEPS - 1), 0)),
            pl.BlockSpec((POST0_ROWS, FN_W), lambda r: (jnp.maximum(r - POST0_P_STEPS, 0), 0)),
            pl.BlockSpec((D, D), lambda r: (0, 0)),
        ] + _epi_in_specs(h, POST0_ROWS),
        out_specs=_epi_out_specs(POST0_ROWS),
        scratch_shapes=_EPI_SCRATCH,
        compiler_params=_cp(("arbitrary",), VMEM_LIMIT),
        name="post_rglru_fourier",
    )(hs, proj0, yf_prompt, yf_latent, w_out_bf16,
      *_epi_operands(h, mod, g_ffn, w_router_t, b_router, POST0_ROWS))


POST1_ROWS = 2 * TILE
POST1_P_STEPS = N_P // POST1_ROWS


def _glu(x):
    return x[:, :CONV_W] * _sigmoid(x[:, CONV_W:])


def _post1_kernel(p_ref, prev_ref, next_ref, lng_ref, lnb_ref, ws_ref, bs_ref, dww_ref, dwb_ref,
                  clg_ref, clb_ref, wo_ref, *rest):
    epilogue_refs, (ext_ref, shift_ref, u_ref, v_ref, mix_ref) = rest[:-5], rest[-5:]
    j = pl.program_id(0)
    is_prompt = j < POST1_P_STEPS
    pos_in_seq = (j - POST1_P_STEPS) % (T_SAMPLE // POST1_ROWS)
    first_starts_seq = is_prompt | (pos_in_seq == 0)
    second_ends_seq = is_prompt | (pos_in_seq == T_SAMPLE // POST1_ROWS - 1)

    for c in range(POST1_ROWS // ROW_PIECE):
        rows = slice(c * ROW_PIECE, (c + 1) * ROW_PIECE)
        z = _gelu(p_ref[rows, 0:2 * SGU_W])
        u_ref[rows, :] = z[:, :SGU_W]
        v_ref[rows, :] = _layernorm(z[:, SGU_W:], lng_ref[...], lnb_ref[...]).astype(BF16)
        half, start = divmod(c * ROW_PIECE, TILE)
        ext_ref[half, CONV_HALO + start:CONV_HALO + start + ROW_PIECE, :] = _glu(p_ref[rows, 2 * SGU_W:])
    ext_ref[0, 0:CONV_HALO, :] = jnp.where(first_starts_seq, 0.0, _glu(prev_ref[...]))
    ext_ref[0, CONV_HALO + TILE:, :] = jnp.where(is_prompt, 0.0, ext_ref[1, CONV_HALO:2 * CONV_HALO, :])
    ext_ref[1, 0:CONV_HALO, :] = jnp.where(is_prompt, 0.0, ext_ref[0, TILE:TILE + CONV_HALO, :])
    ext_ref[1, CONV_HALO + TILE:, :] = jnp.where(second_ends_seq, 0.0, _glu(next_ref[...]))

    for n in range(POST1_ROWS // CHUNK):
        for g in range(SGU_G):
            rows, cols = slice(n * CHUNK, (n + 1) * CHUNK), slice(g * CHUNK, (g + 1) * CHUNK)
            s = _dot(ws_ref[g], v_ref[rows, cols]) + bs_ref[:, cols]
            mix_ref[rows, cols] = (u_ref[rows, cols] * s).astype(BF16)

    n_shift_rows = TILE + 2 * CONV_HALO - 8
    for half in range(POST1_ROWS // TILE):
        for phase in range(8):
            shift_ref[half, phase] = ext_ref[half, phase:phase + n_shift_rows, :]
        for c in range(TILE // CONV_ROWS):
            acc = dwb_ref[...] + jnp.zeros((CONV_ROWS // 8, 8, CONV_W), F32)
            for k in range(CONV_K):
                off = CONV_HALO - CONV_PAD + k
                start = off // 8 * 8 + c * CONV_ROWS
                rows = shift_ref[half, off % 8, start:start + CONV_ROWS, :].reshape(CONV_ROWS // 8, 8, CONV_W)
                acc = acc + rows * dww_ref[k]
            ln = _layernorm(acc.reshape(CONV_ROWS, CONV_W), clg_ref[...], clb_ref[...])
            out_rows = slice(half * TILE + c * CONV_ROWS, half * TILE + (c + 1) * CONV_ROWS)
            mix_ref[out_rows, SGU_W:] = (ln * _sigmoid(ln)).astype(BF16)

    _epilogue(_dot(mix_ref[...], wo_ref[...]), POST1_ROWS, *epilogue_refs)


def _post1(proj1, ln_g, ln_b, ws_bf16, bs_full, dw_w, dw_b, cln_g, cln_b, w_out_bf16,
           h, mod, g_ffn, w_router_t, b_router):
    per_step = POST1_ROWS // CONV_HALO
    n_halo_blocks = N // CONV_HALO
    const2 = lambda r: (0, 0)
    row = lambda a: a.reshape(1, -1)
    return pl.pallas_call(
        _post1_kernel,
        out_shape=_EPI_OUT_SHAPES,
        grid=(N // POST1_ROWS,),
        in_specs=[
            pl.BlockSpec((POST1_ROWS, IN1), lambda r: (r, 0)),
            pl.BlockSpec((CONV_HALO, 2 * CONV_W), lambda r: (jnp.maximum(r * per_step - 1, 0), 1)),
            pl.BlockSpec((CONV_HALO, 2 * CONV_W),
                         lambda r: (jnp.minimum((r + 1) * per_step, n_halo_blocks - 1), 1)),
            pl.BlockSpec((1, SGU_W), const2),
            pl.BlockSpec((1, SGU_W), const2),
            pl.BlockSpec((SGU_G, CHUNK, CHUNK), lambda r: (0, 0, 0)),
            pl.BlockSpec((CHUNK, SGU_W), const2),
            pl.BlockSpec((CONV_K, 8, CONV_W), lambda r: (0, 0, 0)),
            pl.BlockSpec((1, CONV_W), const2),
            pl.BlockSpec((1, CONV_W), const2),
            pl.BlockSpec((1, CONV_W), const2),
            pl.BlockSpec((D, D), const2),
        ] + _epi_in_specs(h, POST1_ROWS),
        out_specs=_epi_out_specs(POST1_ROWS),
        scratch_shapes=_EPI_SCRATCH + [
            pltpu.VMEM((POST1_ROWS // TILE, TILE + 2 * CONV_HALO, CONV_W), F32),
            pltpu.VMEM((POST1_ROWS // TILE, 8, TILE + 2 * CONV_HALO - 8, CONV_W), F32),
            pltpu.VMEM((POST1_ROWS, SGU_W), F32),
            pltpu.VMEM((POST1_ROWS, SGU_W), BF16),
            pltpu.VMEM((POST1_ROWS, D), BF16),
        ],
        compiler_params=_cp(("arbitrary",), VMEM_LIMIT),
        name="post_sgu_conformer",
    )(proj1, proj1, proj1, row(ln_g), row(ln_b), ws_bf16, bs_full,
      jnp.broadcast_to(dw_w[:, None, :], (CONV_K, 8, CONV_W)), row(dw_b), row(cln_g), row(cln_b),
      w_out_bf16, *_epi_operands(h, mod, g_ffn, w_router_t, b_router, POST1_ROWS))


PAD_BASE = N_ASSIGN
PLACEHOLDER_BASE = PAD_BASE + 2 * MOE_BLOCK
Y_ROWS = PLACEHOLDER_BASE + 2 * MOE_BLOCK


def _inverse_kernel(pos_ref, pend_ref, out_ref):
    def fill_block(start):
        def fill(j, carry):
            s = start + j
            out_ref[s] = PAD_BASE + (s & (2 * MOE_BLOCK - 1))
            return carry
        lax.fori_loop(0, MOE_BLOCK, fill, 0, unroll=16)

    def per_expert(e, carry):
        fill_block(jnp.maximum(pend_ref[e] - MOE_BLOCK, 0))
        return carry

    def idle_block(b, carry):
        fill_block(b * MOE_BLOCK)
        return carry

    lax.fori_loop(0, N_EXP, per_expert, 0)
    lax.fori_loop(pend_ref[N_EXP - 1] // MOE_BLOCK, N_BLOCKS, idle_block, 0)

    def body(t, carry):
        for k in range(TOP_K):
            a = t * TOP_K + k
            out_ref[pos_ref[a]] = a
        return carry

    lax.fori_loop(0, N, body, 0, unroll=8)


def _inverse(pos_flat, pad_end):
    smem = pl.BlockSpec(memory_space=pltpu.SMEM)
    return pl.pallas_call(
        _inverse_kernel,
        out_shape=jax.ShapeDtypeStruct((N_SLOTS,), jnp.int32),
        in_specs=[smem, smem],
        out_specs=smem,
        name="moe_inverse",
    )(pos_flat, pad_end)


def _routing_tables(top_e, rank, counts):
    counts = counts.reshape(N_EXP)
    padded = (counts + MOE_BLOCK - 1) // MOE_BLOCK * MOE_BLOCK
    pad_end = jnp.cumsum(padded).astype(jnp.int32)
    pad_start = pad_end - padded
    experts = jnp.arange(N_EXP, dtype=jnp.int32)[:, None, None]
    pos = rank + jnp.sum(jnp.where(top_e[None] == experts, pad_start[:, None, None], 0), axis=0)
    pos = pos.astype(jnp.int32).T.reshape(-1)
    n_used = pad_end[-1] // MOE_BLOCK
    blk = jnp.minimum(jnp.arange(N_BLOCKS, dtype=jnp.int32), n_used - 1)
    block_e = jnp.sum(pad_end[None, :] <= (blk * MOE_BLOCK)[:, None], axis=1).astype(jnp.int32)
    slot_a = _inverse(pos, pad_end)
    tok = jnp.minimum(slot_a >> 2, N - 1)
    gather_key = (tok >> 1) * (2 * PACK_PAIR) + (tok & 1)
    out_row = jnp.where(slot_a >= PAD_BASE, slot_a, (slot_a & (TOP_K - 1)) * N + (slot_a >> 2))
    placeholder_rows = PLACEHOLDER_BASE + jnp.arange(MOE_BLOCK, dtype=jnp.int32)
    out_sub_row = jnp.concatenate([placeholder_rows, out_row]) * PACK_ROWS
    ids = jnp.arange(N_EXP, dtype=jnp.int32)
    later_used = (ids[None, :] > ids[:, None]) & (counts[None, :] > 0)
    next_expert = jnp.min(jnp.where(later_used, ids[None, :], N_EXP), axis=1).astype(jnp.int32)
    return gather_key, out_sub_row, block_e, n_used.reshape(1), next_expert


def _moe_kernel(be_ref, nu_ref, gk_ref, sub_row_ref, nxt_ref, x_hbm, w1_hbm, b1g_ref, b1l_ref, w2_hbm, b2_ref, perm_ref,
                y_hbm, xs, xbuf, w1f, w2f, w1s, w2s, sem, ybuf, osem, *, layer):
    i = pl.program_id(0)
    n_used = nu_ref[0]
    slot = i % 2
    e = be_ref[i]
    e_prev = be_ref[jnp.maximum(i - 1, 0)]
    low_half = lax.broadcasted_iota(jnp.int32, (PACK_PAIR, 128), 0) < PACK_ROWS

    def weight_copies(expert):
        return (pltpu.make_async_copy(w1_hbm.at[layer, expert], w1f, sem.at[1]),
                pltpu.make_async_copy(w2_hbm.at[layer, expert], w2f, sem.at[2]))

    def token_tile(key):
        tile = xs[pl.ds(pl.multiple_of(key >> 1, PACK_PAIR), PACK_PAIR), :]
        return tile, pltpu.roll(tile, PACK_ROWS, 0), key & 1

    def gather_pair(blk, jj, dst_slot):
        t0, r0, half0 = token_tile(gk_ref[blk * MOE_BLOCK + 2 * jj])
        t1, r1, half1 = token_tile(gk_ref[blk * MOE_BLOCK + 2 * jj + 1])
        lower = jnp.where(half0 == 0, t0, r0)
        upper = jnp.where(half1 == 1, t1, r1)
        start = jj * PACK_PAIR if isinstance(jj, int) else pl.multiple_of(jj * PACK_PAIR, PACK_PAIR)
        xbuf[dst_slot, pl.ds(start, PACK_PAIR), :] = jnp.where(low_half, lower, upper)

    def out_copy(src_slot, j, sub_row):
        start = j * PACK_ROWS if isinstance(j, int) else pl.multiple_of(j * PACK_ROWS, PACK_ROWS)
        dst = y_hbm.at[pl.ds(pl.multiple_of(sub_row, PACK_ROWS), PACK_ROWS), :]
        return pltpu.make_async_copy(ybuf.at[src_slot, pl.ds(start, PACK_ROWS), :], dst, osem.at[src_slot])

    def out_buffer_wait(src_slot):
        pltpu.make_async_copy(ybuf.at[src_slot], y_hbm.at[pl.ds(0, MOE_BLOCK * PACK_ROWS), :],
                              osem.at[src_slot]).wait()

    @pl.when(i == 0)
    def _():
        xs_copy = pltpu.make_async_copy(x_hbm, xs, sem.at[0])
        xs_copy.start()
        for c in weight_copies(e):
            c.start()
        ybuf[...] = jnp.zeros_like(ybuf)

        def zero_rows(first_row):
            def body(j, carry):
                out_copy(0, j, (first_row + j) * PACK_ROWS).start()
                return carry
            lax.fori_loop(0, MOE_BLOCK, body, 0, unroll=8)

        zero_rows(PAD_BASE)
        zero_rows(PAD_BASE + MOE_BLOCK)
        out_buffer_wait(0)
        out_buffer_wait(0)
        zero_rows(PLACEHOLDER_BASE + MOE_BLOCK)
        xs_copy.wait()

        def body(jj, carry):
            gather_pair(0, jj, 0)
            return carry
        lax.fori_loop(0, MOE_BLOCK // 2, body, 0, unroll=4)

    @pl.when((i < n_used) & ((i == 0) | (e != e_prev)))
    def _():
        for c in weight_copies(e):
            c.wait()
        for c in range(2 * D_FF // 256):
            wc = w1f[:, c * 256:(c + 1) * 256].astype(BF16)
            w1s[:, c * 256:(c + 1) * 256] = _dot(wc, perm_ref[...]).astype(BF16)
        w2s[...] = w2f[...].astype(BF16)
        e_next = nxt_ref[e]

        @pl.when(e_next < N_EXP)
        def _():
            for c in weight_copies(e_next):
                c.start()

    def block(prefetch_next):
        for j in range(MOE_BLOCK):
            out_copy(1 - slot, j, sub_row_ref[i * MOE_BLOCK + j]).start()
        halves = [[], []]
        for q in range(PACK_ROWS):
            hi, lo = _unpack_bf16_pairs(xbuf[slot, pl.ds(q, MOE_BLOCK, stride=PACK_ROWS), :])
            halves[0].append(hi.astype(BF16))
            halves[1].append(lo.astype(BF16))
        x = jnp.concatenate(halves[0] + halves[1], axis=1)
        if prefetch_next:
            for jj in range(MOE_BLOCK // 2):
                gather_pair(i + 1, jj, 1 - slot)
        hid = _dot(x, w1s[...])
        acts = []
        for c in range(D_FF // 128):
            h_glu = hid[:, c * 256:c * 256 + 128] + b1g_ref[:, c * 128:(c + 1) * 128]
            h_lin = hid[:, c * 256 + 128:(c + 1) * 256] + b1l_ref[:, c * 128:(c + 1) * 128]
            h_glu = jnp.minimum(h_glu, LIMIT)
            h_lin = jnp.clip(h_lin, -LIMIT, LIMIT)
            acts.append((h_glu * _sigmoid(ALPHA * h_glu) * (h_lin + 1.0)).astype(BF16))
        y = _dot(jnp.concatenate(acts, axis=1), w2s[...]) + b2_ref[...]
        out_buffer_wait(slot)
        packed = _pack_bf16_pairs(y)
        for q in range(PACK_ROWS):
            ybuf[slot, pl.ds(q, MOE_BLOCK, stride=PACK_ROWS), :] = packed[:, q * 128:(q + 1) * 128]
        if not prefetch_next:
            def send(j, carry):
                out_copy(slot, j, sub_row_ref[(i + 1) * MOE_BLOCK + j]).start()
                return carry
            lax.fori_loop(0, MOE_BLOCK, send, 0, unroll=8)
            out_buffer_wait(1 - slot)
            out_buffer_wait(slot)

    @pl.when(i + 1 < n_used)
    def _():
        block(True)

    @pl.when(i + 1 == n_used)
    def _():
        block(False)


def _deinterleave_matrix():
    p = np.zeros((256, 256), np.float32)
    m = np.arange(128)
    p[2 * m, m] = 1.0
    p[2 * m + 1, 128 + m] = 1.0
    return jnp.asarray(p, BF16)


def _moe_experts(layer, fn_packed, gather_key, out_sub_row, block_e, n_used, next_expert, w1, b1, w2, b2):
    b1g = b1[0][layer].reshape(N_EXP, 1, D_FF)
    b1l = b1[1][layer].reshape(N_EXP, 1, D_FF)
    ex = lambda i, be, *_: (be[i], 0, 0)
    grid_spec = pltpu.PrefetchScalarGridSpec(
        num_scalar_prefetch=5,
        grid=(N_BLOCKS,),
        in_specs=[
            pl.BlockSpec(memory_space=pl.ANY),
            pl.BlockSpec(memory_space=pl.ANY),
            pl.BlockSpec((None, 1, D_FF), ex),
            pl.BlockSpec((None, 1, D_FF), ex),
            pl.BlockSpec(memory_space=pl.ANY),
            pl.BlockSpec((None, 1, D), ex),
            pl.BlockSpec((256, 256), lambda i, *_: (0, 0)),
        ],
        out_specs=pl.BlockSpec(memory_space=pl.ANY),
        scratch_shapes=[
            pltpu.VMEM((N * PACK_ROWS, 128), jnp.uint32),
            pltpu.VMEM((2, MOE_BLOCK * PACK_ROWS, 128), jnp.uint32),
            pltpu.VMEM((D, 2 * D_FF), F32),
            pltpu.VMEM((D_FF, D), F32),
            pltpu.VMEM((D, 2 * D_FF), BF16),
            pltpu.VMEM((D_FF, D), BF16),
            pltpu.SemaphoreType.DMA((3,)),
            pltpu.VMEM((2, MOE_BLOCK * PACK_ROWS, 128), jnp.uint32),
            pltpu.SemaphoreType.DMA((2,)),
        ],
    )
    return pl.pallas_call(
        functools.partial(_moe_kernel, layer=layer),
        out_shape=jax.ShapeDtypeStruct((Y_ROWS * PACK_ROWS, 128), jnp.uint32),
        grid_spec=grid_spec,
        compiler_params=_cp(("arbitrary",), MOE_VMEM_LIMIT),
        name="moe_experts",
    )(block_e, n_used, gather_key, out_sub_row, next_expert, fn_packed, w1, b1g, b1l, w2, b2[layer].reshape(N_EXP, 1, D),
      _deinterleave_matrix())


def _moe_residual(y_refs, g_ref, h_ref, m_ref, rows):
    g = g_ref[...]
    halves = [[], []]
    for q in range(PACK_ROWS):
        acc_hi = acc_lo = None
        for k in range(TOP_K):
            hi, lo = _unpack_bf16_pairs(y_refs[k][pl.ds(q, rows, stride=PACK_ROWS), :])
            gk = g[:, k:k + 1]
            acc_hi = hi * gk if acc_hi is None else acc_hi + hi * gk
            acc_lo = lo * gk if acc_lo is None else acc_lo + lo * gk
        halves[0].append(acc_hi)
        halves[1].append(acc_lo)
    return h_ref[...] + m_ref[5:6, :] * jnp.concatenate(halves[0] + halves[1], axis=1)


def _moe_stream_specs(rows, mod_row):
    steps = N // rows
    return [pl.BlockSpec((rows * PACK_ROWS, 128), lambda i, k=k: (k * steps + i, 0)) for k in range(TOP_K)] + [
        pl.BlockSpec((rows, TOP_K), lambda i: (i, 0)),
        pl.BlockSpec((rows, D), lambda i: (i, 0)),
        pl.BlockSpec((None, 6, D), lambda i: (mod_row(i), 0, 0)),
    ]


def _combine_final_kernel(y0_ref, y1_ref, y2_ref, y3_ref, g_ref, h_ref, m_ref, fg_ref, yp_ref, yl_ref):
    i = pl.program_id(0)
    h_new = _moe_residual((y0_ref, y1_ref, y2_ref, y3_ref), g_ref, h_ref, m_ref, TILE)
    ms = jnp.mean(h_new * h_new, axis=-1, keepdims=True)
    y = h_new * lax.rsqrt(ms + EPS) * fg_ref[...]

    @pl.when(i < P_TILES)
    def _():
        yp_ref[...] = y

    @pl.when(i >= P_TILES)
    def _():
        yl_ref[...] = y


def _moe_combine_final(y_rows, gates_nt, h, mod, final_g):
    return pl.pallas_call(
        _combine_final_kernel,
        out_shape=(jax.ShapeDtypeStruct((N_P, D), F32), jax.ShapeDtypeStruct((N_S, D), F32)),
        grid=(N_TILES,),
        in_specs=_moe_stream_specs(TILE, _tile_mod_row) + [pl.BlockSpec((1, D), lambda i: (0, 0))],
        out_specs=[
            pl.BlockSpec((TILE, D), lambda i: (jnp.minimum(i, P_TILES - 1), 0)),
            pl.BlockSpec((TILE, D), lambda i: (jnp.maximum(i - P_TILES, 0), 0)),
        ],
        compiler_params=_cp(("arbitrary",), VMEM_LIMIT),
        name="moe_combine_final",
    )(y_rows, y_rows, y_rows, y_rows, gates_nt, h, mod, final_g.reshape(1, D))


COMBINE_ROWS = 512


def _combine_inproj_kernel(y0_ref, y1_ref, y2_ref, y3_ref, g_ref, h_ref, m_ref, gmix_ref, mnext_ref, w_ref,
                           hout_ref, proj_ref):
    h_new = _moe_residual((y0_ref, y1_ref, y2_ref, y3_ref), g_ref, h_ref, m_ref, COMBINE_ROWS)
    hout_ref[...] = h_new
    hn = _rms_mod(h_new, gmix_ref[...], mnext_ref[0:1, :], mnext_ref[1:2, :])
    proj_ref[...] = _dot(hn.astype(BF16), w_ref[...])


def _moe_combine_inproj(y_rows, gates_nt, h, mod, g_mix_next, mod_next, w_next_bf16):
    n_out = w_next_bf16.shape[1]
    p_steps, s_steps = N_P // COMBINE_ROWS, T_SAMPLE // COMBINE_ROWS

    def mod_row(i):
        return jnp.where(i < p_steps, 0, 1 + (i - p_steps) // s_steps)

    return pl.pallas_call(
        _combine_inproj_kernel,
        out_shape=(jax.ShapeDtypeStruct((N, D), F32), jax.ShapeDtypeStruct((N, n_out), F32)),
        grid=(N // COMBINE_ROWS,),
        in_specs=_moe_stream_specs(COMBINE_ROWS, mod_row) + [
            pl.BlockSpec((1, D), lambda i: (0, 0)),
            pl.BlockSpec((None, 6, D), lambda i: (mod_row(i), 0, 0)),
            pl.BlockSpec((D, n_out), lambda i: (0, 0)),
        ],
        out_specs=[
            pl.BlockSpec((COMBINE_ROWS, D), lambda i: (i, 0)),
            pl.BlockSpec((COMBINE_ROWS, n_out), lambda i: (i, 0)),
        ],
        compiler_params=_cp(("arbitrary",), VMEM_LIMIT),
        name="moe_combine_inproj",
    )(y_rows, y_rows, y_rows, y_rows, gates_nt, h, mod, g_mix_next.reshape(1, D), mod_next, w_next_bf16)


def _moe_expert_rows(layer, routed, w1, b1, w2, b2):
    fn_packed, top_e, gates, rank, counts = routed
    gather_key, out_sub_row, block_e, n_used, next_expert = _routing_tables(top_e, rank, counts)
    return _moe_experts(layer, fn_packed, gather_key, out_sub_row, block_e, n_used, next_expert, w1, b1, w2, b2)


def _gate_weights(w_r, w_i):
    per = LRU_BLK // LRU_HD
    eye = jnp.eye(per, dtype=F32)

    def blockdiag(w):
        w = w.reshape(2, LRU_HEADS // per, per, LRU_HD, LRU_HD)
        full = jnp.einsum("dgaij,ab->dgaibj", w, eye)
        return full.reshape(2, LRU_HEADS // per, LRU_BLK, LRU_BLK)

    return jnp.concatenate([blockdiag(w_r), blockdiag(w_i)], axis=-1).astype(BF16)


def kernel(x_prompt, x_sample, state_rglru, c, c_ctx, norm_mix_g, norm_ffn_g, w_mod, b_mod, w_in0, lru_conv_w, lru_conv_b, lru_w_r, lru_b_r, lru_w_i, lru_b_i, lru_lambda, w_out0, w_in1, sgu_ln_g, sgu_ln_b, sgu_w_s, sgu_b_s, conv_dw_w, conv_dw_b, conv_ln_g, conv_ln_b, w_out1, w_router, b_router, w1, b1, w2, b2, final_norm_g):
    h = (x_prompt.reshape(N_P, D), x_sample.reshape(N_S, D))
    cond8 = jnp.concatenate([c_ctx[None, :], c, jnp.zeros((N_MOD - 1 - N_SAMPLE_SEQ, D), F32)], axis=0)
    b1 = (b1[:, :, 0::2], b1[:, :, 1::2])
    mod = _adaln(cond8, w_mod, b_mod)

    proj0 = _inproj(h, norm_mix_g[0], mod[0], w_in0[0].astype(BF16))
    st = state_rglru[:, 0].astype(F32)
    h0 = jnp.zeros((2, N_MOD, LRU_W), F32).at[:, 1:1 + N_SAMPLE_SEQ].set(jnp.swapaxes(st, 0, 1))
    hs, ctx_state = _lru(proj0, lru_conv_w[0], lru_conv_b[0], _gate_weights(lru_w_r[0], lru_w_i[0]),
              lru_b_r[0], lru_b_i[0], lru_lambda[0], h0)
    h, *routed = _post0(hs, proj0, _fourier_prompt(proj0), _fourier_sample(proj0), w_out0[0].astype(BF16),
                        h, mod[0], norm_ffn_g[0], w_router[0].T, b_router[0])
    y_rows = _moe_expert_rows(0, routed, w1, b1, w2, b2)
    gates = routed[2]

    h, proj1 = _moe_combine_inproj(y_rows, gates.T, h, mod[0], norm_mix_g[1], mod[1], w_in1[0].astype(BF16))
    bs_full = jnp.repeat(sgu_b_s[0].T, CHUNK, axis=1)
    h, *routed = _post1(proj1, sgu_ln_g[0], sgu_ln_b[0], sgu_w_s[0].astype(BF16), bs_full,
                        conv_dw_w[0], conv_dw_b[0], conv_ln_g[0], conv_ln_b[0],
                        w_out1[0].astype(BF16), h, mod[1], norm_ffn_g[1], w_router[1].T, b_router[1])
    y_rows = _moe_expert_rows(1, routed, w1, b1, w2, b2)
    y_p, y_l = _moe_combine_final(y_rows, routed[2].T, h, mod[1], final_norm_g)

    y_prompt = y_p.reshape(N_PROMPT_SEQ, T_PROMPT, D)
    y_sample = y_l.reshape(N_SAMPLE_SEQ, T_SAMPLE, D)
    new_state = jnp.transpose(ctx_state, (1, 2, 0, 3))
    return (y_prompt, y_sample, new_state.astype(x_prompt.dtype))
```

```python
import functools

import numpy as np
import jax
import jax.numpy as jnp
from jax import lax
from jax.experimental import pallas as pl
from jax.experimental.pallas import tpu as pltpu

F32 = jnp.float32
BF16 = jnp.bfloat16

D = 1024
N_PROMPT_SEQ = 32
T_PROMPT = 256
N_SAMPLE_SEQ = 2
T_SAMPLE = 2048
N_P = N_PROMPT_SEQ * T_PROMPT
N_S = N_SAMPLE_SEQ * T_SAMPLE
N = N_P + N_S
EPS = 1e-6

TILE = 256
N_TILES = N // TILE
P_TILES = N_P // TILE
S_TILES = T_SAMPLE // TILE
TM_PROJ = 1024
N_MOD = 8

LRU_W = 768
LRU_HEADS = 12
LRU_HD = 64
LRU_K = 4
LRU_LEFT = 2
LRU_C = 8.0
LRU_BLK = 256
FN_W = 256
FN_G = 4
FN_GD = 64

SGU_W = 512
SGU_G = 4
CHUNK = 128
CONV_W = 512
CONV_K = 31
CONV_PAD = 15
CONV_HALO = 16
CONV_ROWS = 32
ROW_PIECE = 32
IN1 = 2 * SGU_W + 2 * CONV_W

N_EXP = 32
TOP_K = 4
D_FF = 1024
ALPHA = 1.702
LIMIT = 7.0
MOE_BLOCK = 256
N_ASSIGN = N * TOP_K
N_BLOCKS = N_ASSIGN // MOE_BLOCK + N_EXP
N_SLOTS = N_BLOCKS * MOE_BLOCK
PACK_ROWS = D // 2 // 128
PACK_PAIR = 2 * PACK_ROWS

VMEM_LIMIT = 56 * 1024 * 1024
MOE_VMEM_LIMIT = 60 * 1024 * 1024


def _cp(sem, vmem=None):
    return pltpu.CompilerParams(dimension_semantics=sem, vmem_limit_bytes=vmem)


def _dot(a, b):
    return jnp.dot(a, b, preferred_element_type=F32)


def _split(x):
    hi = x.astype(BF16)
    lo = (x - hi.astype(F32)).astype(BF16)
    return hi, lo


def _dot3(a, b):
    ah, al = _split(a)
    bh, bl = _split(b)
    return _dot(ah, bh) + _dot(al, bh) + _dot(ah, bl)


def _dot3_nt(a, b):
    dn = (((1,), (1,)), ((), ()))
    d = lambda x, y: lax.dot_general(x, y, dn, preferred_element_type=F32)
    ah, al = _split(a)
    bh, bl = _split(b)
    return d(ah, bh) + d(al, bh) + d(ah, bl)


def _pack_bf16_pairs(x):
    hi = lax.bitcast_convert_type(x[:, :D // 2].astype(BF16).astype(F32), jnp.uint32)
    lo = lax.bitcast_convert_type(x[:, D // 2:].astype(BF16).astype(F32), jnp.uint32)
    return hi | (lo >> 16)


def _unpack_bf16_pairs(words):
    hi = lax.bitcast_convert_type(words & jnp.uint32(0xFFFF0000), F32)
    lo = lax.bitcast_convert_type(words << 16, F32)
    return hi, lo


def _sigmoid(x):
    return 0.5 * jnp.tanh(0.5 * x) + 0.5


def _gelu(x):
    return 0.5 * x * (1.0 + jnp.tanh(0.7978845608028654 * (x + 0.044715 * (x * x * x))))


def _rms_mod(x, g, shift, scale):
    ms = jnp.mean(x * x, axis=-1, keepdims=True)
    y = x * lax.rsqrt(ms + EPS) * g
    return y * (1.0 + scale) + shift


def _layernorm(x, g, b):
    xc = x - jnp.mean(x, axis=-1, keepdims=True)
    var = jnp.mean(xc * xc, axis=-1, keepdims=True)
    return xc * lax.rsqrt(var + EPS) * g + b


def _tile_mod_row(r):
    return jnp.where(r < P_TILES, 0, 1 + (r - P_TILES) // S_TILES)


def _tile_is_seq_start(r):
    return (r < P_TILES) | ((r - P_TILES) % S_TILES == 0)


def _tile_is_seq_end(r):
    return (r < P_TILES) | ((r - P_TILES) % S_TILES == S_TILES - 1)


MOD_TN = 512


def _adaln_kernel(cond_ref, w_ref, b_ref, o_ref):
    cond = cond_ref[...]
    s = cond * _sigmoid(cond)
    o_ref[...] = _dot3(s, w_ref[...]) + b_ref[...]


def _adaln(cond8, w_mod, b_mod):
    depth = w_mod.shape[0]
    out = pl.pallas_call(
        _adaln_kernel,
        out_shape=jax.ShapeDtypeStruct((depth, N_MOD, 6 * D), F32),
        grid=(depth, 6 * D // MOD_TN),
        in_specs=[
            pl.BlockSpec((N_MOD, D), lambda l, j: (0, 0)),
            pl.BlockSpec((None, D, MOD_TN), lambda l, j: (l, 0, j)),
            pl.BlockSpec((None, 1, MOD_TN), lambda l, j: (l, 0, j)),
        ],
        out_specs=pl.BlockSpec((None, N_MOD, MOD_TN), lambda l, j: (l, 0, j)),
        compiler_params=_cp(("arbitrary", "arbitrary")),
        name="adaln",
    )(cond8, w_mod, b_mod.reshape(depth, 1, 6 * D))
    return out.reshape(depth, N_MOD, 6, D)


PROJ_P_STEPS = N_P // TM_PROJ
PROJ_S_STEPS = T_SAMPLE // TM_PROJ


def _stream_specs(rows, p_steps, h):
    if isinstance(h, tuple):
        hp, hl = h
        first_latent = 0
    else:
        hp = hl = h
        first_latent = p_steps
    specs = [
        pl.BlockSpec((rows, D), lambda i, *_: (jnp.minimum(i, p_steps - 1), 0)),
        pl.BlockSpec((rows, D), lambda i, *_: (jnp.maximum(i - p_steps, 0) + first_latent, 0)),
    ]
    return specs, (hp, hl)


def _inproj_kernel(xp_ref, xl_ref, g_ref, m_ref, w_ref, o_ref):
    x = jnp.where(pl.program_id(0) < PROJ_P_STEPS, xp_ref[...], xl_ref[...])
    hn = _rms_mod(x, g_ref[...], m_ref[0:1, :], m_ref[1:2, :])
    o_ref[...] = _dot(hn.astype(BF16), w_ref[...])


def _inproj(h, g, mod, w_bf16):
    n_out = w_bf16.shape[1]

    def mod_row(i):
        return jnp.where(i < PROJ_P_STEPS, 0, 1 + (i - PROJ_P_STEPS) // PROJ_S_STEPS)

    h_specs, h_args = _stream_specs(TM_PROJ, PROJ_P_STEPS, h)
    return pl.pallas_call(
        _inproj_kernel,
        out_shape=jax.ShapeDtypeStruct((N, n_out), F32),
        grid=(N // TM_PROJ,),
        in_specs=h_specs + [
            pl.BlockSpec((1, D), lambda i: (0, 0)),
            pl.BlockSpec((None, 6, D), lambda i: (mod_row(i), 0, 0)),
            pl.BlockSpec((D, n_out), lambda i: (0, 0)),
        ],
        out_specs=pl.BlockSpec((TM_PROJ, n_out), lambda i: (i, 0)),
        compiler_params=_cp(("arbitrary",), VMEM_LIMIT),
        name="inproj",
    )(*h_args, g.reshape(1, D), mod, w_bf16)


LRU_HALO = 8
SCAN_ROWS = 8


def _lru_tile(d, s):
    return jnp.where(d == 0, s, N_TILES - 1 - s)


def _lru_kernel(x_ref, prev_ref, next_ref, cw_ref, cb_ref, wg_ref, br_ref, bi_ref, lam_ref, h0_ref,
                o_ref, state_ref, ext_ref, a_ref, carry_ref):
    d = pl.program_id(0)
    r = _lru_tile(d, pl.program_id(1))
    start = _tile_is_seq_start(r)
    end = _tile_is_seq_end(r)

    ext_ref[0:LRU_HALO, :] = jnp.where(start, 0.0, prev_ref[...])
    ext_ref[LRU_HALO:LRU_HALO + TILE, :] = x_ref[...]
    ext_ref[LRU_HALO + TILE:, :] = jnp.where(end, 0.0, next_ref[...])
    xc = cb_ref[...] + jnp.zeros((TILE, LRU_W), F32)
    for k in range(LRU_K):
        off = LRU_HALO - LRU_LEFT + k
        xc = xc + ext_ref[off:off + TILE, :] * cw_ref[k:k + 1, :]

    xcb = xc.astype(BF16)
    pre_r, pre_i = [], []
    for blk in range(LRU_W // LRU_BLK):
        g = _dot(xcb[:, blk * LRU_BLK:(blk + 1) * LRU_BLK], wg_ref[blk])
        pre_r.append(g[:, :LRU_BLK])
        pre_i.append(g[:, LRU_BLK:])
    gate_r = _sigmoid(jnp.concatenate(pre_r, axis=1) + br_ref[...])
    gate_i = _sigmoid(jnp.concatenate(pre_i, axis=1) + bi_ref[...])
    neg_lam = -lam_ref[...]
    softplus = jnp.maximum(neg_lam, 0.0) + jnp.log1p(jnp.exp(-jnp.abs(neg_lam)))
    log_a = (-LRU_C) * gate_r * softplus
    a = jnp.exp(log_a)
    a_ref[...] = a
    o_ref[...] = jnp.sqrt(-jnp.tanh(log_a) * (a * a + 1.0)) * (gate_i * xc)

    fresh = jnp.where(d == 0, start, end)
    h_init = jnp.where(fresh, h0_ref[pl.ds(_tile_mod_row(r), 1), :], carry_ref[...])

    row = lax.broadcasted_iota(jnp.int32, (SCAN_ROWS, LRU_W), 0)

    def scan_group(g, h, reverse):
        rows = pl.ds(pl.multiple_of(g * SCAN_ROWS, SCAN_ROWS), SCAN_ROWS)
        a = a_ref[rows, :]
        b = o_ref[rows, :]
        for s in (1, 2, 4):
            shift = SCAN_ROWS - s if reverse else s
            inside = (row < SCAN_ROWS - s) if reverse else (row >= s)
            b = jnp.where(inside, a * pltpu.roll(b, shift, 0) + b, b)
            a = jnp.where(inside, a * pltpu.roll(a, shift, 0), a)
        hs = a * h + b
        o_ref[rows, :] = hs
        return hs[0:1, :] if reverse else hs[SCAN_ROWS - 1:SCAN_ROWS, :]

    n_groups = TILE // SCAN_ROWS

    @pl.when(d == 0)
    def _():
        carry_ref[...] = lax.fori_loop(0, n_groups, lambda g, h: scan_group(g, h, False), h_init, unroll=4)

    @pl.when(d == 1)
    def _():
        carry_ref[...] = lax.fori_loop(0, n_groups, lambda g, h: scan_group(n_groups - 1 - g, h, True),
                                       h_init, unroll=4)

    h_last = carry_ref[...]

    @pl.when(r < P_TILES)
    def _():
        state_ref[...] = h_last


def _lru(proj0, conv_w, conv_b, wg, b_r, b_i, lam, h0):
    n_halo_blocks = N // LRU_HALO
    per_tile = TILE // LRU_HALO
    tile = lambda d, s: _lru_tile(d, s)
    return pl.pallas_call(
        _lru_kernel,
        out_shape=(jax.ShapeDtypeStruct((2, N, LRU_W), F32),
                   jax.ShapeDtypeStruct((2, N_PROMPT_SEQ, 1, LRU_W), F32)),
        grid=(2, N_TILES),
        in_specs=[
            pl.BlockSpec((TILE, LRU_W), lambda d, s: (tile(d, s), 0)),
            pl.BlockSpec((LRU_HALO, LRU_W), lambda d, s: (jnp.maximum(tile(d, s) * per_tile - 1, 0), 0)),
            pl.BlockSpec((LRU_HALO, LRU_W),
                         lambda d, s: (jnp.minimum((tile(d, s) + 1) * per_tile, n_halo_blocks - 1), 0)),
            pl.BlockSpec((LRU_K, LRU_W), lambda d, s: (0, 0)),
            pl.BlockSpec((1, LRU_W), lambda d, s: (0, 0)),
            pl.BlockSpec((None, LRU_W // LRU_BLK, LRU_BLK, 2 * LRU_BLK), lambda d, s: (d, 0, 0, 0)),
            pl.BlockSpec((None, 1, LRU_W), lambda d, s: (d, 0, 0)),
            pl.BlockSpec((None, 1, LRU_W), lambda d, s: (d, 0, 0)),
            pl.BlockSpec((None, 1, LRU_W), lambda d, s: (d, 0, 0)),
            pl.BlockSpec((None, N_MOD, LRU_W), lambda d, s: (d, 0, 0)),
        ],
        out_specs=[
            pl.BlockSpec((None, TILE, LRU_W), lambda d, s: (d, tile(d, s), 0)),
            pl.BlockSpec((None, None, 1, LRU_W), lambda d, s: (d, jnp.minimum(tile(d, s), P_TILES - 1), 0, 0)),
        ],
        scratch_shapes=[
            pltpu.VMEM((TILE + 2 * LRU_HALO, LRU_W), F32),
            pltpu.VMEM((TILE, LRU_W), F32),
            pltpu.VMEM((1, LRU_W), F32),
        ],
        compiler_params=_cp(("arbitrary", "arbitrary"), VMEM_LIMIT),
        name="rglru_scan",
    )(proj0, proj0, proj0, conv_w, conv_b.reshape(1, LRU_W), wg,
      b_r.reshape(2, 1, LRU_W), b_i.reshape(2, 1, LRU_W), lam.reshape(2, 1, LRU_W), h0)


def _dft_tables(n, scale):
    k = np.arange(n, dtype=np.int64)
    ang = 2.0 * np.pi * ((k[:, None] * k[None, :]) % n).astype(np.float64) / n
    return np.cos(ang) * scale, np.sin(ang) * scale


def _channel_tables():
    c, s = _dft_tables(FN_GD, FN_GD ** -0.5)
    eye = np.eye(FN_G)
    return (jnp.asarray(np.kron(eye, c), BF16), jnp.asarray(np.kron(eye, s), BF16))


def _time_tables(t_len):
    c, s = _dft_tables(t_len, t_len ** -0.5)
    return jnp.asarray(c, BF16), jnp.asarray(s, BF16)


FN_SEQ_PER_STEP = 4


def _fourier_prompt_kernel(z_ref, cc_ref, sc_ref, ct_ref, st_ref, o_ref):
    z = z_ref[...].astype(BF16)
    zc = _dot(z, cc_ref[...]).astype(BF16)
    zs = _dot(z, sc_ref[...]).astype(BF16)
    for b in range(FN_SEQ_PER_STEP):
        rows = slice(b * T_PROMPT, (b + 1) * T_PROMPT)
        o_ref[rows, :] = _dot(ct_ref[...], zc[rows, :]) - _dot(st_ref[...], zs[rows, :])


def _fourier_prompt(proj0):
    cc, sc = _channel_tables()
    ct, st = _time_tables(T_PROMPT)
    const = lambda b: (0, 0)
    return pl.pallas_call(
        _fourier_prompt_kernel,
        out_shape=jax.ShapeDtypeStruct((N_P, FN_W), F32),
        grid=(N_PROMPT_SEQ // FN_SEQ_PER_STEP,),
        in_specs=[
            pl.BlockSpec((FN_SEQ_PER_STEP * T_PROMPT, FN_W), lambda b: (b, 2 * LRU_W // FN_W)),
            pl.BlockSpec((FN_W, FN_W), const),
            pl.BlockSpec((FN_W, FN_W), const),
            pl.BlockSpec((T_PROMPT, T_PROMPT), const),
            pl.BlockSpec((T_PROMPT, T_PROMPT), const),
        ],
        out_specs=pl.BlockSpec((FN_SEQ_PER_STEP * T_PROMPT, FN_W), lambda b: (b, 0)),
        compiler_params=_cp(("arbitrary",)),
        name="fourier_prompt",
    )(proj0, cc, sc, ct, st)


FN_SAMPLE_ROWS = 512


def _fourier_sample_kernel(z_ref, cc_ref, sc_ref, ct_ref, st_ref, o_ref, zc_ref, zs_ref):
    @pl.when(pl.program_id(1) == 0)
    def _():
        z = z_ref[...].astype(BF16)
        zc_ref[...] = _dot(z, cc_ref[...]).astype(BF16)
        zs_ref[...] = _dot(z, sc_ref[...]).astype(BF16)

    o_ref[...] = _dot(ct_ref[...], zc_ref[...]) - _dot(st_ref[...], zs_ref[...])


def _fourier_sample(proj0):
    cc, sc = _channel_tables()
    ct, st = _time_tables(T_SAMPLE)
    const = lambda b, i: (0, 0)
    first_seq_block = N_P // T_SAMPLE
    return pl.pallas_call(
        _fourier_sample_kernel,
        out_shape=jax.ShapeDtypeStruct((N_S, FN_W), F32),
        grid=(N_SAMPLE_SEQ, T_SAMPLE // FN_SAMPLE_ROWS),
        in_specs=[
            pl.BlockSpec((T_SAMPLE, FN_W), lambda b, i: (first_seq_block + b, 2 * LRU_W // FN_W)),
            pl.BlockSpec((FN_W, FN_W), const),
            pl.BlockSpec((FN_W, FN_W), const),
            pl.BlockSpec((FN_SAMPLE_ROWS, T_SAMPLE), lambda b, i: (i, 0)),
            pl.BlockSpec((FN_SAMPLE_ROWS, T_SAMPLE), lambda b, i: (i, 0)),
        ],
        out_specs=pl.BlockSpec((FN_SAMPLE_ROWS, FN_W), lambda b, i: (b * (T_SAMPLE // FN_SAMPLE_ROWS) + i, 0)),
        scratch_shapes=[pltpu.VMEM((T_SAMPLE, FN_W), BF16), pltpu.VMEM((T_SAMPLE, FN_W), BF16)],
        compiler_params=_cp(("arbitrary", "arbitrary"), VMEM_LIMIT),
        name="fourier_sample",
    )(proj0, cc, sc, ct, st)


def _epilogue(mix, rows, hp_ref, hl_ref, m_ref, gf_ref, wr_ref, brt_ref, tri_ref,
              hout_ref, fn_ref, tope_ref, gate_ref, rank_ref, cnt_ref, run_ref):
    h = jnp.where(pl.program_id(0) < N_P // rows, hp_ref[...], hl_ref[...])
    h_new = h + m_ref[2:3, :] * mix
    hout_ref[...] = h_new
    fn = _rms_mod(h_new, gf_ref[...], m_ref[3:4, :], m_ref[4:5, :])
    packed = _pack_bf16_pairs(fn)
    for q in range(PACK_ROWS):
        fn_ref[pl.ds(q, rows, stride=PACK_ROWS), :] = packed[:, q * 128:(q + 1) * 128]
    logits = _dot3_nt(wr_ref[...], fn) + brt_ref[...]
    iota = lax.broadcasted_iota(jnp.int32, logits.shape, 0)
    vals, idxs = [], []
    for _ in range(TOP_K):
        m = jnp.max(logits, axis=0, keepdims=True)
        idx = jnp.min(jnp.where(logits == m, iota, N_EXP), axis=0, keepdims=True)
        vals.append(m)
        idxs.append(idx)
        logits = jnp.where(iota == idx, -jnp.inf, logits)
    exps = [jnp.exp(v - vals[0]) for v in vals]
    denom = exps[0] + exps[1] + exps[2] + exps[3]
    for k in range(TOP_K):
        tope_ref[k:k + 1, :] = idxs[k]
        gate_ref[k:k + 1, :] = exps[k] / denom

    @pl.when(pl.program_id(0) == 0)
    def _():
        run_ref[...] = jnp.zeros_like(run_ref)

    run = run_ref[...]
    onehots = [jnp.where(iota == idxs[k], 1.0, 0.0) for k in range(TOP_K)]
    incl_all = _dot(jnp.concatenate(onehots, axis=0).astype(BF16), tri_ref[...])
    for k in range(TOP_K):
        incl = incl_all[k * N_EXP:(k + 1) * N_EXP, :]
        rank = jnp.sum(onehots[k] * (incl - 1.0 + run), axis=0, keepdims=True)
        rank_ref[k:k + 1, :] = rank.astype(jnp.int32)
        run = run + incl[:, rows - 1:rows]
    run_ref[...] = run
    cnt_ref[...] = run.astype(jnp.int32)


_EPI_OUT_SHAPES = (
    jax.ShapeDtypeStruct((N, D), F32),
    jax.ShapeDtypeStruct((N * PACK_ROWS, 128), jnp.uint32),
    jax.ShapeDtypeStruct((TOP_K, N), jnp.int32),
    jax.ShapeDtypeStruct((TOP_K, N), F32),
    jax.ShapeDtypeStruct((TOP_K, N), jnp.int32),
    jax.ShapeDtypeStruct((N_EXP, 1), jnp.int32),
)
_EPI_SCRATCH = [pltpu.VMEM((N_EXP, 1), F32)]


def _epi_operands(h, mod, g_ffn, w_router_t, b_router, rows=TILE):
    tri = jnp.asarray(np.triu(np.ones((rows, rows), np.float32)), BF16)
    return (*_stream_specs(rows, N_P // rows, h)[1], mod, g_ffn.reshape(1, D), w_router_t,
            b_router.reshape(N_EXP, 1), tri)


def _epi_in_specs(h, rows=TILE):
    p_steps, s_steps = N_P // rows, T_SAMPLE // rows
    h_specs, _ = _stream_specs(rows, p_steps, h)
    mod_row = lambda r: jnp.where(r < p_steps, 0, 1 + (r - p_steps) // s_steps)
    return h_specs + [
        pl.BlockSpec((None, 6, D), lambda r: (mod_row(r), 0, 0)),
        pl.BlockSpec((1, D), lambda r: (0, 0)),
        pl.BlockSpec((N_EXP, D), lambda r: (0, 0)),
        pl.BlockSpec((N_EXP, 1), lambda r: (0, 0)),
        pl.BlockSpec((rows, rows), lambda r: (0, 0)),
    ]


def _epi_out_specs(rows=TILE):
    return [
        pl.BlockSpec((rows, D), lambda r: (r, 0)),
        pl.BlockSpec((rows * PACK_ROWS, 128), lambda r: (r, 0)),
        pl.BlockSpec((TOP_K, rows), lambda r: (0, r)),
        pl.BlockSpec((TOP_K, rows), lambda r: (0, r)),
        pl.BlockSpec((TOP_K, rows), lambda r: (0, r)),
        pl.BlockSpec((N_EXP, 1), lambda r: (0, 0)),
    ]


POST0_ROWS = 512
POST0_P_STEPS = N_P // POST0_ROWS


def _post0_kernel(hs_ref, xg_ref, yfp_ref, yfl_ref, wo_ref, *epilogue_refs):
    y_rec = (hs_ref[0] + hs_ref[1]) * _gelu(xg_ref[...])
    y_four = jnp.where(pl.program_id(0) < POST0_P_STEPS, yfp_ref[...], yfl_ref[...])
    mix = (_dot(y_rec.astype(BF16), wo_ref[0:LRU_W, :])
           + _dot(y_four.astype(BF16), wo_ref[LRU_W:, :]))
    _epilogue(mix, POST0_ROWS, *epilogue_refs)


def _post0(hs, proj0, yf_prompt, yf_latent, w_out_bf16, h, mod, g_ffn, w_router_t, b_router):
    return pl.pallas_call(
        _post0_kernel,
        out_shape=_EPI_OUT_SHAPES,
        grid=(N // POST0_ROWS,),
        in_specs=[
            pl.BlockSpec((2, POST0_ROWS, LRU_W), lambda r: (0, r, 0)),
            pl.BlockSpec((POST0_ROWS, LRU_W), lambda r: (r, 1)),
            pl.BlockSpec((POST0_ROWS, FN_W), lambda r: (jnp.minimum(r, POST0_P_STEPS - 1), 0)),
            pl.BlockSpec((POST0_ROWS, FN_W), lambda r: (jnp.maximum(r - POST0_P_STEPS, 0), 0)),
            pl.BlockSpec((D, D), lambda r: (0, 0)),
        ] + _epi_in_specs(h, POST0_ROWS),
        out_specs=_epi_out_specs(POST0_ROWS),
        scratch_shapes=_EPI_SCRATCH,
        compiler_params=_cp(("arbitrary",), VMEM_LIMIT),
        name="post_rglru_fourier",
    )(hs, proj0, yf_prompt, yf_latent, w_out_bf16,
      *_epi_operands(h, mod, g_ffn, w_router_t, b_router, POST0_ROWS))


POST1_ROWS = 2 * TILE
POST1_P_STEPS = N_P // POST1_ROWS


def _glu(x):
    return x[:, :CONV_W] * _sigmoid(x[:, CONV_W:])


def _post1_kernel(p_ref, prev_ref, next_ref, lng_ref, lnb_ref, ws_ref, bs_ref, dww_ref, dwb_ref,
                  clg_ref, clb_ref, wo_ref, *rest):
    epilogue_refs, (ext_ref, shift_ref, u_ref, v_ref, mix_ref) = rest[:-5], rest[-5:]
    j = pl.program_id(0)
    is_prompt = j < POST1_P_STEPS
    pos_in_seq = (j - POST1_P_STEPS) % (T_SAMPLE // POST1_ROWS)
    first_starts_seq = is_prompt | (pos_in_seq == 0)
    second_ends_seq = is_prompt | (pos_in_seq == T_SAMPLE // POST1_ROWS - 1)

    for c in range(POST1_ROWS // ROW_PIECE):
        rows = slice(c * ROW_PIECE, (c + 1) * ROW_PIECE)
        z = _gelu(p_ref[rows, 0:2 * SGU_W])
        u_ref[rows, :] = z[:, :SGU_W]
        v_ref[rows, :] = _layernorm(z[:, SGU_W:], lng_ref[...], lnb_ref[...]).astype(BF16)
        half, start = divmod(c * ROW_PIECE, TILE)
        ext_ref[half, CONV_HALO + start:CONV_HALO + start + ROW_PIECE, :] = _glu(p_ref[rows, 2 * SGU_W:])
    ext_ref[0, 0:CONV_HALO, :] = jnp.where(first_starts_seq, 0.0, _glu(prev_ref[...]))
    ext_ref[0, CONV_HALO + TILE:, :] = jnp.where(is_prompt, 0.0, ext_ref[1, CONV_HALO:2 * CONV_HALO, :])
    ext_ref[1, 0:CONV_HALO, :] = jnp.where(is_prompt, 0.0, ext_ref[0, TILE:TILE + CONV_HALO, :])
    ext_ref[1, CONV_HALO + TILE:, :] = jnp.where(second_ends_seq, 0.0, _glu(next_ref[...]))

    for n in range(POST1_ROWS // CHUNK):
        for g in range(SGU_G):
            rows, cols = slice(n * CHUNK, (n + 1) * CHUNK), slice(g * CHUNK, (g + 1) * CHUNK)
            s = _dot(ws_ref[g], v_ref[rows, cols]) + bs_ref[:, cols]
            mix_ref[rows, cols] = (u_ref[rows, cols] * s).astype(BF16)

    n_shift_rows = TILE + 2 * CONV_HALO - 8
    for half in range(POST1_ROWS // TILE):
        for phase in range(8):
            shift_ref[half, phase] = ext_ref[half, phase:phase + n_shift_rows, :]
        for c in range(TILE // CONV_ROWS):
            acc = dwb_ref[...] + jnp.zeros((CONV_ROWS // 8, 8, CONV_W), F32)
            for k in range(CONV_K):
                off = CONV_HALO - CONV_PAD + k
                start = off // 8 * 8 + c * CONV_ROWS
                rows = shift_ref[half, off % 8, start:start + CONV_ROWS, :].reshape(CONV_ROWS // 8, 8, CONV_W)
                acc = acc + rows * dww_ref[k]
            ln = _layernorm(acc.reshape(CONV_ROWS, CONV_W), clg_ref[...], clb_ref[...])
            out_rows = slice(half * TILE + c * CONV_ROWS, half * TILE + (c + 1) * CONV_ROWS)
            mix_ref[out_rows, SGU_W:] = (ln * _sigmoid(ln)).astype(BF16)

    _epilogue(_dot(mix_ref[...], wo_ref[...]), POST1_ROWS, *epilogue_refs)


def _post1(proj1, ln_g, ln_b, ws_bf16, bs_full, dw_w, dw_b, cln_g, cln_b, w_out_bf16,
           h, mod, g_ffn, w_router_t, b_router):
    per_step = POST1_ROWS // CONV_HALO
    n_halo_blocks = N // CONV_HALO
    const2 = lambda r: (0, 0)
    row = lambda a: a.reshape(1, -1)
    return pl.pallas_call(
        _post1_kernel,
        out_shape=_EPI_OUT_SHAPES,
        grid=(N // POST1_ROWS,),
        in_specs=[
            pl.BlockSpec((POST1_ROWS, IN1), lambda r: (r, 0)),
            pl.BlockSpec((CONV_HALO, 2 * CONV_W), lambda r: (jnp.maximum(r * per_step - 1, 0), 1)),
            pl.BlockSpec((CONV_HALO, 2 * CONV_W),
                         lambda r: (jnp.minimum((r + 1) * per_step, n_halo_blocks - 1), 1)),
            pl.BlockSpec((1, SGU_W), const2),
            pl.BlockSpec((1, SGU_W), const2),
            pl.BlockSpec((SGU_G, CHUNK, CHUNK), lambda r: (0, 0, 0)),
            pl.BlockSpec((CHUNK, SGU_W), const2),
            pl.BlockSpec((CONV_K, 8, CONV_W), lambda r: (0, 0, 0)),
            pl.BlockSpec((1, CONV_W), const2),
            pl.BlockSpec((1, CONV_W), const2),
            pl.BlockSpec((1, CONV_W), const2),
            pl.BlockSpec((D, D), const2),
        ] + _epi_in_specs(h, POST1_ROWS),
        out_specs=_epi_out_specs(POST1_ROWS),
        scratch_shapes=_EPI_SCRATCH + [
            pltpu.VMEM((POST1_ROWS // TILE, TILE + 2 * CONV_HALO, CONV_W), F32),
            pltpu.VMEM((POST1_ROWS // TILE, 8, TILE + 2 * CONV_HALO - 8, CONV_W), F32),
            pltpu.VMEM((POST1_ROWS, SGU_W), F32),
            pltpu.VMEM((POST1_ROWS, SGU_W), BF16),
            pltpu.VMEM((POST1_ROWS, D), BF16),
        ],
        compiler_params=_cp(("arbitrary",), VMEM_LIMIT),
        name="post_sgu_conformer",
    )(proj1, proj1, proj1, row(ln_g), row(ln_b), ws_bf16, bs_full,
      jnp.broadcast_to(dw_w[:, None, :], (CONV_K, 8, CONV_W)), row(dw_b), row(cln_g), row(cln_b),
      w_out_bf16, *_epi_operands(h, mod, g_ffn, w_router_t, b_router, POST1_ROWS))


PAD_BASE = N_ASSIGN
PLACEHOLDER_BASE = PAD_BASE + 2 * MOE_BLOCK
Y_ROWS = PLACEHOLDER_BASE + 2 * MOE_BLOCK


def _inverse_kernel(pos_ref, pend_ref, out_ref):
    def fill_block(start):
        def fill(j, carry):
            s = start + j
            out_ref[s] = PAD_BASE + (s & (2 * MOE_BLOCK - 1))
            return carry
        lax.fori_loop(0, MOE_BLOCK, fill, 0, unroll=16)

    def per_expert(e, carry):
        fill_block(jnp.maximum(pend_ref[e] - MOE_BLOCK, 0))
        return carry

    def idle_block(b, carry):
        fill_block(b * MOE_BLOCK)
        return carry

    lax.fori_loop(0, N_EXP, per_expert, 0)
    lax.fori_loop(pend_ref[N_EXP - 1] // MOE_BLOCK, N_BLOCKS, idle_block, 0)

    def body(t, carry):
        for k in range(TOP_K):
            a = t * TOP_K + k
            out_ref[pos_ref[a]] = a
        return carry

    lax.fori_loop(0, N, body, 0, unroll=8)


def _inverse(pos_flat, pad_end):
    smem = pl.BlockSpec(memory_space=pltpu.SMEM)
    return pl.pallas_call(
        _inverse_kernel,
        out_shape=jax.ShapeDtypeStruct((N_SLOTS,), jnp.int32),
        in_specs=[smem, smem],
        out_specs=smem,
        name="moe_inverse",
    )(pos_flat, pad_end)


def _routing_tables(top_e, rank, counts):
    counts = counts.reshape(N_EXP)
    padded = (counts + MOE_BLOCK - 1) // MOE_BLOCK * MOE_BLOCK
    pad_end = jnp.cumsum(padded).astype(jnp.int32)
    pad_start = pad_end - padded
    experts = jnp.arange(N_EXP, dtype=jnp.int32)[:, None, None]
    pos = rank + jnp.sum(jnp.where(top_e[None] == experts, pad_start[:, None, None], 0), axis=0)
    pos = pos.astype(jnp.int32).T.reshape(-1)
    n_used = pad_end[-1] // MOE_BLOCK
    blk = jnp.minimum(jnp.arange(N_BLOCKS, dtype=jnp.int32), n_used - 1)
    block_e = jnp.sum(pad_end[None, :] <= (blk * MOE_BLOCK)[:, None], axis=1).astype(jnp.int32)
    slot_a = _inverse(pos, pad_end)
    tok = jnp.minimum(slot_a >> 2, N - 1)
    gather_key = (tok >> 1) * (2 * PACK_PAIR) + (tok & 1)
    out_row = jnp.where(slot_a >= PAD_BASE, slot_a, (slot_a & (TOP_K - 1)) * N + (slot_a >> 2))
    placeholder_rows = PLACEHOLDER_BASE + jnp.arange(MOE_BLOCK, dtype=jnp.int32)
    out_sub_row = jnp.concatenate([placeholder_rows, out_row]) * PACK_ROWS
    ids = jnp.arange(N_EXP, dtype=jnp.int32)
    later_used = (ids[None, :] > ids[:, None]) & (counts[None, :] > 0)
    next_expert = jnp.min(jnp.where(later_used, ids[None, :], N_EXP), axis=1).astype(jnp.int32)
    return gather_key, out_sub_row, block_e, n_used.reshape(1), next_expert


def _moe_kernel(be_ref, nu_ref, gk_ref, sub_row_ref, nxt_ref, x_hbm, w1_hbm, b1g_ref, b1l_ref, w2_hbm, b2_ref, perm_ref,
                y_hbm, xs, xbuf, w1f, w2f, w1s, w2s, sem, ybuf, osem, *, layer):
    i = pl.program_id(0)
    n_used = nu_ref[0]
    slot = i % 2
    e = be_ref[i]
    e_prev = be_ref[jnp.maximum(i - 1, 0)]
    low_half = lax.broadcasted_iota(jnp.int32, (PACK_PAIR, 128), 0) < PACK_ROWS

    def weight_copies(expert):
        return (pltpu.make_async_copy(w1_hbm.at[layer, expert], w1f, sem.at[1]),
                pltpu.make_async_copy(w2_hbm.at[layer, expert], w2f, sem.at[2]))

    def token_tile(key):
        tile = xs[pl.ds(pl.multiple_of(key >> 1, PACK_PAIR), PACK_PAIR), :]
        return tile, pltpu.roll(tile, PACK_ROWS, 0), key & 1

    def gather_pair(blk, jj, dst_slot):
        t0, r0, half0 = token_tile(gk_ref[blk * MOE_BLOCK + 2 * jj])
        t1, r1, half1 = token_tile(gk_ref[blk * MOE_BLOCK + 2 * jj + 1])
        lower = jnp.where(half0 == 0, t0, r0)
        upper = jnp.where(half1 == 1, t1, r1)
        start = jj * PACK_PAIR if isinstance(jj, int) else pl.multiple_of(jj * PACK_PAIR, PACK_PAIR)
        xbuf[dst_slot, pl.ds(start, PACK_PAIR), :] = jnp.where(low_half, lower, upper)

    def out_copy(src_slot, j, sub_row):
        start = j * PACK_ROWS if isinstance(j, int) else pl.multiple_of(j * PACK_ROWS, PACK_ROWS)
        dst = y_hbm.at[pl.ds(pl.multiple_of(sub_row, PACK_ROWS), PACK_ROWS), :]
        return pltpu.make_async_copy(ybuf.at[src_slot, pl.ds(start, PACK_ROWS), :], dst, osem.at[src_slot])

    def out_buffer_wait(src_slot):
        pltpu.make_async_copy(ybuf.at[src_slot], y_hbm.at[pl.ds(0, MOE_BLOCK * PACK_ROWS), :],
                              osem.at[src_slot]).wait()

    @pl.when(i == 0)
    def _():
        xs_copy = pltpu.make_async_copy(x_hbm, xs, sem.at[0])
        xs_copy.start()
        for c in weight_copies(e):
            c.start()
        ybuf[...] = jnp.zeros_like(ybuf)

        def zero_rows(first_row):
            def body(j, carry):
                out_copy(0, j, (first_row + j) * PACK_ROWS).start()
                return carry
            lax.fori_loop(0, MOE_BLOCK, body, 0, unroll=8)

        zero_rows(PAD_BASE)
        zero_rows(PAD_BASE + MOE_BLOCK)
        out_buffer_wait(0)
        out_buffer_wait(0)
        zero_rows(PLACEHOLDER_BASE + MOE_BLOCK)
        xs_copy.wait()

        def body(jj, carry):
            gather_pair(0, jj, 0)
            return carry
        lax.fori_loop(0, MOE_BLOCK // 2, body, 0, unroll=4)

    @pl.when((i < n_used) & ((i == 0) | (e != e_prev)))
    def _():
        for c in weight_copies(e):
            c.wait()
        for c in range(2 * D_FF // 256):
            wc = w1f[:, c * 256:(c + 1) * 256].astype(BF16)
            w1s[:, c * 256:(c + 1) * 256] = _dot(wc, perm_ref[...]).astype(BF16)
        w2s[...] = w2f[...].astype(BF16)
        e_next = nxt_ref[e]

        @pl.when(e_next < N_EXP)
        def _():
            for c in weight_copies(e_next):
                c.start()

    def block(prefetch_next):
        for j in range(MOE_BLOCK):
            out_copy(1 - slot, j, sub_row_ref[i * MOE_BLOCK + j]).start(priority=j % 2)
        halves = [[], []]
        for q in range(PACK_ROWS):
            hi, lo = _unpack_bf16_pairs(xbuf[slot, pl.ds(q, MOE_BLOCK, stride=PACK_ROWS), :])
            halves[0].append(hi.astype(BF16))
            halves[1].append(lo.astype(BF16))
        x = jnp.concatenate(halves[0] + halves[1], axis=1)
        if prefetch_next:
            for jj in range(MOE_BLOCK // 2):
                gather_pair(i + 1, jj, 1 - slot)
        hid = _dot(x, w1s[...])
        acts = []
        for c in range(D_FF // 128):
            h_glu = hid[:, c * 256:c * 256 + 128] + b1g_ref[:, c * 128:(c + 1) * 128]
            h_lin = hid[:, c * 256 + 128:(c + 1) * 256] + b1l_ref[:, c * 128:(c + 1) * 128]
            h_glu = jnp.minimum(h_glu, LIMIT)
            h_lin = jnp.clip(h_lin, -LIMIT, LIMIT)
            acts.append((h_glu * _sigmoid(ALPHA * h_glu) * (h_lin + 1.0)).astype(BF16))
        y = _dot(jnp.concatenate(acts, axis=1), w2s[...]) + b2_ref[...]
        out_buffer_wait(slot)
        packed = _pack_bf16_pairs(y)
        for q in range(PACK_ROWS):
            ybuf[slot, pl.ds(q, MOE_BLOCK, stride=PACK_ROWS), :] = packed[:, q * 128:(q + 1) * 128]
        if not prefetch_next:
            def send(j, carry):
                out_copy(slot, j, sub_row_ref[(i + 1) * MOE_BLOCK + j]).start()
                return carry
            lax.fori_loop(0, MOE_BLOCK, send, 0, unroll=8)
            out_buffer_wait(1 - slot)
            out_buffer_wait(slot)

    @pl.when(i + 1 < n_used)
    def _():
        block(True)

    @pl.when(i + 1 == n_used)
    def _():
        block(False)


def _deinterleave_matrix():
    p = np.zeros((256, 256), np.float32)
    m = np.arange(128)
    p[2 * m, m] = 1.0
    p[2 * m + 1, 128 + m] = 1.0
    return jnp.asarray(p, BF16)


def _moe_experts(layer, fn_packed, gather_key, out_sub_row, block_e, n_used, next_expert, w1, b1, w2, b2):
    b1g = b1[0][layer].reshape(N_EXP, 1, D_FF)
    b1l = b1[1][layer].reshape(N_EXP, 1, D_FF)
    ex = lambda i, be, *_: (be[i], 0, 0)
    grid_spec = pltpu.PrefetchScalarGridSpec(
        num_scalar_prefetch=5,
        grid=(N_BLOCKS,),
        in_specs=[
            pl.BlockSpec(memory_space=pl.ANY),
            pl.BlockSpec(memory_space=pl.ANY),
            pl.BlockSpec((None, 1, D_FF), ex),
            pl.BlockSpec((None, 1, D_FF), ex),
            pl.BlockSpec(memory_space=pl.ANY),
            pl.BlockSpec((None, 1, D), ex),
            pl.BlockSpec((256, 256), lambda i, *_: (0, 0)),
        ],
        out_specs=pl.BlockSpec(memory_space=pl.ANY),
        scratch_shapes=[
            pltpu.VMEM((N * PACK_ROWS, 128), jnp.uint32),
            pltpu.VMEM((2, MOE_BLOCK * PACK_ROWS, 128), jnp.uint32),
            pltpu.VMEM((D, 2 * D_FF), F32),
            pltpu.VMEM((D_FF, D), F32),
            pltpu.VMEM((D, 2 * D_FF), BF16),
            pltpu.VMEM((D_FF, D), BF16),
            pltpu.SemaphoreType.DMA((3,)),
            pltpu.VMEM((2, MOE_BLOCK * PACK_ROWS, 128), jnp.uint32),
            pltpu.SemaphoreType.DMA((2,)),
        ],
    )
    return pl.pallas_call(
        functools.partial(_moe_kernel, layer=layer),
        out_shape=jax.ShapeDtypeStruct((Y_ROWS * PACK_ROWS, 128), jnp.uint32),
        grid_spec=grid_spec,
        compiler_params=_cp(("arbitrary",), MOE_VMEM_LIMIT),
        name="moe_experts",
    )(block_e, n_used, gather_key, out_sub_row, next_expert, fn_packed, w1, b1g, b1l, w2, b2[layer].reshape(N_EXP, 1, D),
      _deinterleave_matrix())


def _moe_residual(y_refs, g_ref, h_ref, m_ref, rows):
    g = g_ref[...]
    halves = [[], []]
    for q in range(PACK_ROWS):
        acc_hi = acc_lo = None
        for k in range(TOP_K):
            hi, lo = _unpack_bf16_pairs(y_refs[k][pl.ds(q, rows, stride=PACK_ROWS), :])
            gk = g[:, k:k + 1]
            acc_hi = hi * gk if acc_hi is None else acc_hi + hi * gk
            acc_lo = lo * gk if acc_lo is None else acc_lo + lo * gk
        halves[0].append(acc_hi)
        halves[1].append(acc_lo)
    return h_ref[...] + m_ref[5:6, :] * jnp.concatenate(halves[0] + halves[1], axis=1)


def _moe_stream_specs(rows, mod_row):
    steps = N // rows
    return [pl.BlockSpec((rows * PACK_ROWS, 128), lambda i, k=k: (k * steps + i, 0)) for k in range(TOP_K)] + [
        pl.BlockSpec((rows, TOP_K), lambda i: (i, 0)),
        pl.BlockSpec((rows, D), lambda i: (i, 0)),
        pl.BlockSpec((None, 6, D), lambda i: (mod_row(i), 0, 0)),
    ]


def _combine_final_kernel(y0_ref, y1_ref, y2_ref, y3_ref, g_ref, h_ref, m_ref, fg_ref, yp_ref, yl_ref):
    i = pl.program_id(0)
    h_new = _moe_residual((y0_ref, y1_ref, y2_ref, y3_ref), g_ref, h_ref, m_ref, TILE)
    ms = jnp.mean(h_new * h_new, axis=-1, keepdims=True)
    y = h_new * lax.rsqrt(ms + EPS) * fg_ref[...]

    @pl.when(i < P_TILES)
    def _():
        yp_ref[...] = y

    @pl.when(i >= P_TILES)
    def _():
        yl_ref[...] = y


def _moe_combine_final(y_rows, gates_nt, h, mod, final_g):
    return pl.pallas_call(
        _combine_final_kernel,
        out_shape=(jax.ShapeDtypeStruct((N_P, D), F32), jax.ShapeDtypeStruct((N_S, D), F32)),
        grid=(N_TILES,),
        in_specs=_moe_stream_specs(TILE, _tile_mod_row) + [pl.BlockSpec((1, D), lambda i: (0, 0))],
        out_specs=[
            pl.BlockSpec((TILE, D), lambda i: (jnp.minimum(i, P_TILES - 1), 0)),
            pl.BlockSpec((TILE, D), lambda i: (jnp.maximum(i - P_TILES, 0), 0)),
        ],
        compiler_params=_cp(("arbitrary",), VMEM_LIMIT),
        name="moe_combine_final",
    )(y_rows, y_rows, y_rows, y_rows, gates_nt, h, mod, final_g.reshape(1, D))


COMBINE_ROWS = 512


def _combine_inproj_kernel(y0_ref, y1_ref, y2_ref, y3_ref, g_ref, h_ref, m_ref, gmix_ref, mnext_ref, w_ref,
                           hout_ref, proj_ref):
    h_new = _moe_residual((y0_ref, y1_ref, y2_ref, y3_ref), g_ref, h_ref, m_ref, COMBINE_ROWS)
    hout_ref[...] = h_new
    hn = _rms_mod(h_new, gmix_ref[...], mnext_ref[0:1, :], mnext_ref[1:2, :])
    proj_ref[...] = _dot(hn.astype(BF16), w_ref[...])


def _moe_combine_inproj(y_rows, gates_nt, h, mod, g_mix_next, mod_next, w_next_bf16):
    n_out = w_next_bf16.shape[1]
    p_steps, s_steps = N_P // COMBINE_ROWS, T_SAMPLE // COMBINE_ROWS

    def mod_row(i):
        return jnp.where(i < p_steps, 0, 1 + (i - p_steps) // s_steps)

    return pl.pallas_call(
        _combine_inproj_kernel,
        out_shape=(jax.ShapeDtypeStruct((N, D), F32), jax.ShapeDtypeStruct((N, n_out), F32)),
        grid=(N // COMBINE_ROWS,),
        in_specs=_moe_stream_specs(COMBINE_ROWS, mod_row) + [
            pl.BlockSpec((1, D), lambda i: (0, 0)),
            pl.BlockSpec((None, 6, D), lambda i: (mod_row(i), 0, 0)),
            pl.BlockSpec((D, n_out), lambda i: (0, 0)),
        ],
        out_specs=[
            pl.BlockSpec((COMBINE_ROWS, D), lambda i: (i, 0)),
            pl.BlockSpec((COMBINE_ROWS, n_out), lambda i: (i, 0)),
        ],
        compiler_params=_cp(("arbitrary",), VMEM_LIMIT),
        name="moe_combine_inproj",
    )(y_rows, y_rows, y_rows, y_rows, gates_nt, h, mod, g_mix_next.reshape(1, D), mod_next, w_next_bf16)


def _moe_expert_rows(layer, routed, w1, b1, w2, b2):
    fn_packed, top_e, gates, rank, counts = routed
    gather_key, out_sub_row, block_e, n_used, next_expert = _routing_tables(top_e, rank, counts)
    return _moe_experts(layer, fn_packed, gather_key, out_sub_row, block_e, n_used, next_expert, w1, b1, w2, b2)


def _gate_weights(w_r, w_i):
    per = LRU_BLK // LRU_HD
    eye = jnp.eye(per, dtype=F32)

    def blockdiag(w):
        w = w.reshape(2, LRU_HEADS // per, per, LRU_HD, LRU_HD)
        full = jnp.einsum("dgaij,ab->dgaibj", w, eye)
        return full.reshape(2, LRU_HEADS // per, LRU_BLK, LRU_BLK)

    return jnp.concatenate([blockdiag(w_r), blockdiag(w_i)], axis=-1).astype(BF16)


def kernel(x_prompt, x_sample, state_rglru, c, c_ctx, norm_mix_g, norm_ffn_g, w_mod, b_mod, w_in0, lru_conv_w, lru_conv_b, lru_w_r, lru_b_r, lru_w_i, lru_b_i, lru_lambda, w_out0, w_in1, sgu_ln_g, sgu_ln_b, sgu_w_s, sgu_b_s, conv_dw_w, conv_dw_b, conv_ln_g, conv_ln_b, w_out1, w_router, b_router, w1, b1, w2, b2, final_norm_g):
    h = (x_prompt.reshape(N_P, D), x_sample.reshape(N_S, D))
    cond8 = jnp.concatenate([c_ctx[None, :], c, jnp.zeros((N_MOD - 1 - N_SAMPLE_SEQ, D), F32)], axis=0)
    b1 = (b1[:, :, 0::2], b1[:, :, 1::2])
    mod = _adaln(cond8, w_mod, b_mod)

    proj0 = _inproj(h, norm_mix_g[0], mod[0], w_in0[0].astype(BF16))
    st = state_rglru[:, 0].astype(F32)
    h0 = jnp.zeros((2, N_MOD, LRU_W), F32).at[:, 1:1 + N_SAMPLE_SEQ].set(jnp.swapaxes(st, 0, 1))
    hs, ctx_state = _lru(proj0, lru_conv_w[0], lru_conv_b[0], _gate_weights(lru_w_r[0], lru_w_i[0]),
              lru_b_r[0], lru_b_i[0], lru_lambda[0], h0)
    h, *routed = _post0(hs, proj0, _fourier_prompt(proj0), _fourier_sample(proj0), w_out0[0].astype(BF16),
                        h, mod[0], norm_ffn_g[0], w_router[0].T, b_router[0])
    y_rows = _moe_expert_rows(0, routed, w1, b1, w2, b2)
    gates = routed[2]

    h, proj1 = _moe_combine_inproj(y_rows, gates.T, h, mod[0], norm_mix_g[1], mod[1], w_in1[0].astype(BF16))
    bs_full = jnp.repeat(sgu_b_s[0].T, CHUNK, axis=1)
    h, *routed = _post1(proj1, sgu_ln_g[0], sgu_ln_b[0], sgu_w_s[0].astype(BF16), bs_full,
                        conv_dw_w[0], conv_dw_b[0], conv_ln_g[0], conv_ln_b[0],
                        w_out1[0].astype(BF16), h, mod[1], norm_ffn_g[1], w_router[1].T, b_router[1])
    y_rows = _moe_expert_rows(1, routed, w1, b1, w2, b2)
    y_p, y_l = _moe_combine_final(y_rows, routed[2].T, h, mod[1], final_norm_g)

    y_prompt = y_p.reshape(N_PROMPT_SEQ, T_PROMPT, D)
    y_sample = y_l.reshape(N_SAMPLE_SEQ, T_SAMPLE, D)
    new_state = jnp.transpose(ctx_state, (1, 2, 0, 3))
    return (y_prompt, y_sample, new_state.astype(x_prompt.dtype))
```

```python
import functools

import numpy as np
import jax
import jax.numpy as jnp
from jax import lax
from jax.experimental import pallas as pl
from jax.experimental.pallas import tpu as pltpu

F32 = jnp.float32
BF16 = jnp.bfloat16

D = 1024
N_PROMPT_SEQ = 32
T_PROMPT = 256
N_SAMPLE_SEQ = 2
T_SAMPLE = 2048
N_P = N_PROMPT_SEQ * T_PROMPT
N_S = N_SAMPLE_SEQ * T_SAMPLE
N = N_P + N_S
EPS = 1e-6

TILE = 256
N_TILES = N // TILE
P_TILES = N_P // TILE
S_TILES = T_SAMPLE // TILE
TM_PROJ = 1024
N_MOD = 8

LRU_W = 768
LRU_HEADS = 12
LRU_HD = 64
LRU_K = 4
LRU_LEFT = 2
LRU_C = 8.0
LRU_BLK = 256
FN_W = 256
FN_G = 4
FN_GD = 64

SGU_W = 512
SGU_G = 4
CHUNK = 128
CONV_W = 512
CONV_K = 31
CONV_PAD = 15
CONV_HALO = 16
CONV_ROWS = 32
ROW_PIECE = 32
IN1 = 2 * SGU_W + 2 * CONV_W

N_EXP = 32
TOP_K = 4
D_FF = 1024
ALPHA = 1.702
LIMIT = 7.0
MOE_BLOCK = 256
N_ASSIGN = N * TOP_K
N_BLOCKS = N_ASSIGN // MOE_BLOCK + N_EXP
N_SLOTS = N_BLOCKS * MOE_BLOCK
PACK_ROWS = D // 2 // 128
PACK_PAIR = 2 * PACK_ROWS

VMEM_LIMIT = 56 * 1024 * 1024
MOE_VMEM_LIMIT = 60 * 1024 * 1024


def _cp(sem, vmem=None):
    return pltpu.CompilerParams(dimension_semantics=sem, vmem_limit_bytes=vmem)


def _dot(a, b):
    return jnp.dot(a, b, preferred_element_type=F32)


def _split(x):
    hi = x.astype(BF16)
    lo = (x - hi.astype(F32)).astype(BF16)
    return hi, lo


def _dot3(a, b):
    ah, al = _split(a)
    bh, bl = _split(b)
    return _dot(ah, bh) + _dot(al, bh) + _dot(ah, bl)


def _dot3_nt(a, b):
    dn = (((1,), (1,)), ((), ()))
    d = lambda x, y: lax.dot_general(x, y, dn, preferred_element_type=F32)
    ah, al = _split(a)
    bh, bl = _split(b)
    return d(ah, bh) + d(al, bh) + d(ah, bl)


def _pack_bf16_pairs(x):
    hi = lax.bitcast_convert_type(x[:, :D // 2].astype(BF16).astype(F32), jnp.uint32)
    lo = lax.bitcast_convert_type(x[:, D // 2:].astype(BF16).astype(F32), jnp.uint32)
    return hi | (lo >> 16)


def _unpack_bf16_pairs(words):
    hi = lax.bitcast_convert_type(words & jnp.uint32(0xFFFF0000), F32)
    lo = lax.bitcast_convert_type(words << 16, F32)
    return hi, lo


def _sigmoid(x):
    return 0.5 * jnp.tanh(0.5 * x) + 0.5


def _gelu(x):
    return 0.5 * x * (1.0 + jnp.tanh(0.7978845608028654 * (x + 0.044715 * (x * x * x))))


def _rms_mod(x, g, shift, scale):
    ms = jnp.mean(x * x, axis=-1, keepdims=True)
    y = x * lax.rsqrt(ms + EPS) * g
    return y * (1.0 + scale) + shift


def _layernorm(x, g, b):
    xc = x - jnp.mean(x, axis=-1, keepdims=True)
    var = jnp.mean(xc * xc, axis=-1, keepdims=True)
    return xc * lax.rsqrt(var + EPS) * g + b


def _tile_mod_row(r):
    return jnp.where(r < P_TILES, 0, 1 + (r - P_TILES) // S_TILES)


def _tile_is_seq_start(r):
    return (r < P_TILES) | ((r - P_TILES) % S_TILES == 0)


def _tile_is_seq_end(r):
    return (r < P_TILES) | ((r - P_TILES) % S_TILES == S_TILES - 1)


MOD_TN = 512


def _adaln_kernel(cond_ref, w_ref, b_ref, o_ref):
    cond = cond_ref[...]
    s = cond * _sigmoid(cond)
    o_ref[...] = _dot3(s, w_ref[...]) + b_ref[...]


def _adaln(cond8, w_mod, b_mod):
    depth = w_mod.shape[0]
    out = pl.pallas_call(
        _adaln_kernel,
        out_shape=jax.ShapeDtypeStruct((depth, N_MOD, 6 * D), F32),
        grid=(depth, 6 * D // MOD_TN),
        in_specs=[
            pl.BlockSpec((N_MOD, D), lambda l, j: (0, 0)),
            pl.BlockSpec((None, D, MOD_TN), lambda l, j: (l, 0, j)),
            pl.BlockSpec((None, 1, MOD_TN), lambda l, j: (l, 0, j)),
        ],
        out_specs=pl.BlockSpec((None, N_MOD, MOD_TN), lambda l, j: (l, 0, j)),
        compiler_params=_cp(("arbitrary", "arbitrary")),
        name="adaln",
    )(cond8, w_mod, b_mod.reshape(depth, 1, 6 * D))
    return out.reshape(depth, N_MOD, 6, D)


PROJ_P_STEPS = N_P // TM_PROJ
PROJ_S_STEPS = T_SAMPLE // TM_PROJ


def _stream_specs(rows, p_steps, h):
    if isinstance(h, tuple):
        hp, hl = h
        first_latent = 0
    else:
        hp = hl = h
        first_latent = p_steps
    specs = [
        pl.BlockSpec((rows, D), lambda i, *_: (jnp.minimum(i, p_steps - 1), 0)),
        pl.BlockSpec((rows, D), lambda i, *_: (jnp.maximum(i - p_steps, 0) + first_latent, 0)),
    ]
    return specs, (hp, hl)


def _inproj_kernel(xp_ref, xl_ref, g_ref, m_ref, w_ref, o_ref):
    x = jnp.where(pl.program_id(0) < PROJ_P_STEPS, xp_ref[...], xl_ref[...])
    hn = _rms_mod(x, g_ref[...], m_ref[0:1, :], m_ref[1:2, :])
    o_ref[...] = _dot(hn.astype(BF16), w_ref[...])


def _inproj(h, g, mod, w_bf16):
    n_out = w_bf16.shape[1]

    def mod_row(i):
        return jnp.where(i < PROJ_P_STEPS, 0, 1 + (i - PROJ_P_STEPS) // PROJ_S_STEPS)

    h_specs, h_args = _stream_specs(TM_PROJ, PROJ_P_STEPS, h)
    return pl.pallas_call(
        _inproj_kernel,
        out_shape=jax.ShapeDtypeStruct((N, n_out), F32),
        grid=(N // TM_PROJ,),
        in_specs=h_specs + [
            pl.BlockSpec((1, D), lambda i: (0, 0)),
            pl.BlockSpec((None, 6, D), lambda i: (mod_row(i), 0, 0)),
            pl.BlockSpec((D, n_out), lambda i: (0, 0)),
        ],
        out_specs=pl.BlockSpec((TM_PROJ, n_out), lambda i: (i, 0)),
        compiler_params=_cp(("arbitrary",), VMEM_LIMIT),
        name="inproj",
    )(*h_args, g.reshape(1, D), mod, w_bf16)


LRU_HALO = 8
SCAN_ROWS = 8


def _lru_tile(d, s):
    return jnp.where(d == 0, s, N_TILES - 1 - s)


def _lru_kernel(x_ref, prev_ref, next_ref, cw_ref, cb_ref, wg_ref, br_ref, bi_ref, lam_ref, h0_ref,
                o_ref, state_ref, ext_ref, a_ref, carry_ref):
    d = pl.program_id(0)
    r = _lru_tile(d, pl.program_id(1))
    start = _tile_is_seq_start(r)
    end = _tile_is_seq_end(r)

    ext_ref[0:LRU_HALO, :] = jnp.where(start, 0.0, prev_ref[...])
    ext_ref[LRU_HALO:LRU_HALO + TILE, :] = x_ref[...]
    ext_ref[LRU_HALO + TILE:, :] = jnp.where(end, 0.0, next_ref[...])
    xc = cb_ref[...] + jnp.zeros((TILE, LRU_W), F32)
    for k in range(LRU_K):
        off = LRU_HALO - LRU_LEFT + k
        xc = xc + ext_ref[off:off + TILE, :] * cw_ref[k:k + 1, :]

    xcb = xc.astype(BF16)
    pre_r, pre_i = [], []
    for blk in range(LRU_W // LRU_BLK):
        g = _dot(xcb[:, blk * LRU_BLK:(blk + 1) * LRU_BLK], wg_ref[blk])
        pre_r.append(g[:, :LRU_BLK])
        pre_i.append(g[:, LRU_BLK:])
    gate_r = _sigmoid(jnp.concatenate(pre_r, axis=1) + br_ref[...])
    gate_i = _sigmoid(jnp.concatenate(pre_i, axis=1) + bi_ref[...])
    neg_lam = -lam_ref[...]
    softplus = jnp.maximum(neg_lam, 0.0) + jnp.log1p(jnp.exp(-jnp.abs(neg_lam)))
    log_a = (-LRU_C) * gate_r * softplus
    a = jnp.exp(log_a)
    a_ref[...] = a
    o_ref[...] = jnp.sqrt(-jnp.tanh(log_a) * (a * a + 1.0)) * (gate_i * xc)

    fresh = jnp.where(d == 0, start, end)
    h_init = jnp.where(fresh, h0_ref[pl.ds(_tile_mod_row(r), 1), :], carry_ref[...])

    row = lax.broadcasted_iota(jnp.int32, (SCAN_ROWS, LRU_W), 0)

    def scan_group(g, h, reverse):
        rows = pl.ds(pl.multiple_of(g * SCAN_ROWS, SCAN_ROWS), SCAN_ROWS)
        a = a_ref[rows, :]
        b = o_ref[rows, :]
        for s in (1, 2, 4):
            shift = SCAN_ROWS - s if reverse else s
            inside = (row < SCAN_ROWS - s) if reverse else (row >= s)
            b = jnp.where(inside, a * pltpu.roll(b, shift, 0) + b, b)
            a = jnp.where(inside, a * pltpu.roll(a, shift, 0), a)
        hs = a * h + b
        o_ref[rows, :] = hs
        return hs[0:1, :] if reverse else hs[SCAN_ROWS - 1:SCAN_ROWS, :]

    n_groups = TILE // SCAN_ROWS

    @pl.when(d == 0)
    def _():
        carry_ref[...] = lax.fori_loop(0, n_groups, lambda g, h: scan_group(g, h, False), h_init, unroll=4)

    @pl.when(d == 1)
    def _():
        carry_ref[...] = lax.fori_loop(0, n_groups, lambda g, h: scan_group(n_groups - 1 - g, h, True),
                                       h_init, unroll=4)

    h_last = carry_ref[...]

    @pl.when(r < P_TILES)
    def _():
        state_ref[...] = h_last


def _lru(proj0, conv_w, conv_b, wg, b_r, b_i, lam, h0):
    n_halo_blocks = N // LRU_HALO
    per_tile = TILE // LRU_HALO
    tile = lambda d, s: _lru_tile(d, s)
    return pl.pallas_call(
        _lru_kernel,
        out_shape=(jax.ShapeDtypeStruct((2, N, LRU_W), F32),
                   jax.ShapeDtypeStruct((2, N_PROMPT_SEQ, 1, LRU_W), F32)),
        grid=(2, N_TILES),
        in_specs=[
            pl.BlockSpec((TILE, LRU_W), lambda d, s: (tile(d, s), 0)),
            pl.BlockSpec((LRU_HALO, LRU_W), lambda d, s: (jnp.maximum(tile(d, s) * per_tile - 1, 0), 0)),
            pl.BlockSpec((LRU_HALO, LRU_W),
                         lambda d, s: (jnp.minimum((tile(d, s) + 1) * per_tile, n_halo_blocks - 1), 0)),
            pl.BlockSpec((LRU_K, LRU_W), lambda d, s: (0, 0)),
            pl.BlockSpec((1, LRU_W), lambda d, s: (0, 0)),
            pl.BlockSpec((None, LRU_W // LRU_BLK, LRU_BLK, 2 * LRU_BLK), lambda d, s: (d, 0, 0, 0)),
            pl.BlockSpec((None, 1, LRU_W), lambda d, s: (d, 0, 0)),
            pl.BlockSpec((None, 1, LRU_W), lambda d, s: (d, 0, 0)),
            pl.BlockSpec((None, 1, LRU_W), lambda d, s: (d, 0, 0)),
            pl.BlockSpec((None, N_MOD, LRU_W), lambda d, s: (d, 0, 0)),
        ],
        out_specs=[
            pl.BlockSpec((None, TILE, LRU_W), lambda d, s: (d, tile(d, s), 0)),
            pl.BlockSpec((None, None, 1, LRU_W), lambda d, s: (d, jnp.minimum(tile(d, s), P_TILES - 1), 0, 0)),
        ],
        scratch_shapes=[
            pltpu.VMEM((TILE + 2 * LRU_HALO, LRU_W), F32),
            pltpu.VMEM((TILE, LRU_W), F32),
            pltpu.VMEM((1, LRU_W), F32),
        ],
        compiler_params=_cp(("arbitrary", "arbitrary"), VMEM_LIMIT),
        name="rglru_scan",
    )(proj0, proj0, proj0, conv_w, conv_b.reshape(1, LRU_W), wg,
      b_r.reshape(2, 1, LRU_W), b_i.reshape(2, 1, LRU_W), lam.reshape(2, 1, LRU_W), h0)


def _dft_tables(n, scale):
    k = np.arange(n, dtype=np.int64)
    ang = 2.0 * np.pi * ((k[:, None] * k[None, :]) % n).astype(np.float64) / n
    return np.cos(ang) * scale, np.sin(ang) * scale


def _channel_tables():
    c, s = _dft_tables(FN_GD, FN_GD ** -0.5)
    eye = np.eye(FN_G)
    return (jnp.asarray(np.kron(eye, c), BF16), jnp.asarray(np.kron(eye, s), BF16))


def _time_tables(t_len):
    c, s = _dft_tables(t_len, t_len ** -0.5)
    return jnp.asarray(c, BF16), jnp.asarray(s, BF16)


FN_SEQ_PER_STEP = 4


def _fourier_prompt_kernel(z_ref, cc_ref, sc_ref, ct_ref, st_ref, o_ref):
    z = z_ref[...].astype(BF16)
    zc = _dot(z, cc_ref[...]).astype(BF16)
    zs = _dot(z, sc_ref[...]).astype(BF16)
    for b in range(FN_SEQ_PER_STEP):
        rows = slice(b * T_PROMPT, (b + 1) * T_PROMPT)
        o_ref[rows, :] = _dot(ct_ref[...], zc[rows, :]) - _dot(st_ref[...], zs[rows, :])


def _fourier_prompt(proj0):
    cc, sc = _channel_tables()
    ct, st = _time_tables(T_PROMPT)
    const = lambda b: (0, 0)
    return pl.pallas_call(
        _fourier_prompt_kernel,
        out_shape=jax.ShapeDtypeStruct((N_P, FN_W), F32),
        grid=(N_PROMPT_SEQ // FN_SEQ_PER_STEP,),
        in_specs=[
            pl.BlockSpec((FN_SEQ_PER_STEP * T_PROMPT, FN_W), lambda b: (b, 2 * LRU_W // FN_W)),
            pl.BlockSpec((FN_W, FN_W), const),
            pl.BlockSpec((FN_W, FN_W), const),
            pl.BlockSpec((T_PROMPT, T_PROMPT), const),
            pl.BlockSpec((T_PROMPT, T_PROMPT), const),
        ],
        out_specs=pl.BlockSpec((FN_SEQ_PER_STEP * T_PROMPT, FN_W), lambda b: (b, 0)),
        compiler_params=_cp(("arbitrary",)),
        name="fourier_prompt",
    )(proj0, cc, sc, ct, st)


FN_SAMPLE_ROWS = 512


def _fourier_sample_kernel(z_ref, cc_ref, sc_ref, ct_ref, st_ref, o_ref, zc_ref, zs_ref):
    @pl.when(pl.program_id(1) == 0)
    def _():
        z = z_ref[...].astype(BF16)
        zc_ref[...] = _dot(z, cc_ref[...]).astype(BF16)
        zs_ref[...] = _dot(z, sc_ref[...]).astype(BF16)

    o_ref[...] = _dot(ct_ref[...], zc_ref[...]) - _dot(st_ref[...], zs_ref[...])


def _fourier_sample(proj0):
    cc, sc = _channel_tables()
    ct, st = _time_tables(T_SAMPLE)
    const = lambda b, i: (0, 0)
    first_seq_block = N_P // T_SAMPLE
    return pl.pallas_call(
        _fourier_sample_kernel,
        out_shape=jax.ShapeDtypeStruct((N_S, FN_W), F32),
        grid=(N_SAMPLE_SEQ, T_SAMPLE // FN_SAMPLE_ROWS),
        in_specs=[
            pl.BlockSpec((T_SAMPLE, FN_W), lambda b, i: (first_seq_block + b, 2 * LRU_W // FN_W)),
            pl.BlockSpec((FN_W, FN_W), const),
            pl.BlockSpec((FN_W, FN_W), const),
            pl.BlockSpec((FN_SAMPLE_ROWS, T_SAMPLE), lambda b, i: (i, 0)),
            pl.BlockSpec((FN_SAMPLE_ROWS, T_SAMPLE), lambda b, i: (i, 0)),
        ],
        out_specs=pl.BlockSpec((FN_SAMPLE_ROWS, FN_W), lambda b, i: (b * (T_SAMPLE // FN_SAMPLE_ROWS) + i, 0)),
        scratch_shapes=[pltpu.VMEM((T_SAMPLE, FN_W), BF16), pltpu.VMEM((T_SAMPLE, FN_W), BF16)],
        compiler_params=_cp(("arbitrary", "arbitrary"), VMEM_LIMIT),
        name="fourier_sample",
    )(proj0, cc, sc, ct, st)


def _epilogue(mix, rows, hp_ref, hl_ref, m_ref, gf_ref, wr_ref, brt_ref, tri_ref,
              hout_ref, fn_ref, tope_ref, gate_ref, rank_ref, cnt_ref, run_ref):
    h = jnp.where(pl.program_id(0) < N_P // rows, hp_ref[...], hl_ref[...])
    h_new = h + m_ref[2:3, :] * mix
    hout_ref[...] = h_new
    fn = _rms_mod(h_new, gf_ref[...], m_ref[3:4, :], m_ref[4:5, :])
    packed = _pack_bf16_pairs(fn)
    for q in range(PACK_ROWS):
        fn_ref[pl.ds(q, rows, stride=PACK_ROWS), :] = packed[:, q * 128:(q + 1) * 128]
    logits = _dot3_nt(wr_ref[...], fn) + brt_ref[...]
    iota = lax.broadcasted_iota(jnp.int32, logits.shape, 0)
    vals, idxs = [], []
    for _ in range(TOP_K):
        m = jnp.max(logits, axis=0, keepdims=True)
        idx = jnp.min(jnp.where(logits == m, iota, N_EXP), axis=0, keepdims=True)
        vals.append(m)
        idxs.append(idx)
        logits = jnp.where(iota == idx, -jnp.inf, logits)
    exps = [jnp.exp(v - vals[0]) for v in vals]
    denom = exps[0] + exps[1] + exps[2] + exps[3]
    for k in range(TOP_K):
        tope_ref[k:k + 1, :] = idxs[k]
        gate_ref[k:k + 1, :] = exps[k] / denom

    @pl.when(pl.program_id(0) == 0)
    def _():
        run_ref[...] = jnp.zeros_like(run_ref)

    run = run_ref[...]
    onehots = [jnp.where(iota == idxs[k], 1.0, 0.0) for k in range(TOP_K)]
    incl_all = _dot(jnp.concatenate(onehots, axis=0).astype(BF16), tri_ref[...])
    for k in range(TOP_K):
        incl = incl_all[k * N_EXP:(k + 1) * N_EXP, :]
        rank = jnp.sum(onehots[k] * (incl - 1.0 + run), axis=0, keepdims=True)
        rank_ref[k:k + 1, :] = rank.astype(jnp.int32)
        run = run + incl[:, rows - 1:rows]
    run_ref[...] = run
    cnt_ref[...] = run.astype(jnp.int32)


_EPI_OUT_SHAPES = (
    jax.ShapeDtypeStruct((N, D), F32),
    jax.ShapeDtypeStruct((N * PACK_ROWS, 128), jnp.uint32),
    jax.ShapeDtypeStruct((TOP_K, N), jnp.int32),
    jax.ShapeDtypeStruct((TOP_K, N), F32),
    jax.ShapeDtypeStruct((TOP_K, N), jnp.int32),
    jax.ShapeDtypeStruct((N_EXP, 1), jnp.int32),
)
_EPI_SCRATCH = [pltpu.VMEM((N_EXP, 1), F32)]


def _epi_operands(h, mod, g_ffn, w_router_t, b_router, rows=TILE):
    tri = jnp.asarray(np.triu(np.ones((rows, rows), np.float32)), BF16)
    return (*_stream_specs(rows, N_P // rows, h)[1], mod, g_ffn.reshape(1, D), w_router_t,
            b_router.reshape(N_EXP, 1), tri)


def _epi_in_specs(h, rows=TILE):
    p_steps, s_steps = N_P // rows, T_SAMPLE // rows
    h_specs, _ = _stream_specs(rows, p_steps, h)
    mod_row = lambda r: jnp.where(r < p_steps, 0, 1 + (r - p_steps) // s_steps)
    return h_specs + [
        pl.BlockSpec((None, 6, D), lambda r: (mod_row(r), 0, 0)),
        pl.BlockSpec((1, D), lambda r: (0, 0)),
        pl.BlockSpec((N_EXP, D), lambda r: (0, 0)),
        pl.BlockSpec((N_EXP, 1), lambda r: (0, 0)),
        pl.BlockSpec((rows, rows), lambda r: (0, 0)),
    ]


def _epi_out_specs(rows=TILE):
    return [
        pl.BlockSpec((rows, D), lambda r: (r, 0)),
        pl.BlockSpec((rows * PACK_ROWS, 128), lambda r: (r, 0)),
        pl.BlockSpec((TOP_K, rows), lambda r: (0, r)),
        pl.BlockSpec((TOP_K, rows), lambda r: (0, r)),
        pl.BlockSpec((TOP_K, rows), lambda r: (0, r)),
        pl.BlockSpec((N_EXP, 1), lambda r: (0, 0)),
    ]


POST0_ROWS = 512
POST0_P_STEPS = N_P // POST0_ROWS


def _post0_kernel(hs_ref, xg_ref, yfp_ref, yfl_ref, wo_ref, *epilogue_refs):
    y_rec = (hs_ref[0] + hs_ref[1]) * _gelu(xg_ref[...])
    y_four = jnp.where(pl.program_id(0) < POST0_P_STEPS, yfp_ref[...], yfl_ref[...])
    mix = (_dot(y_rec.astype(BF16), wo_ref[0:LRU_W, :])
           + _dot(y_four.astype(BF16), wo_ref[LRU_W:, :]))
    _epilogue(mix, POST0_ROWS, *epilogue_refs)


def _post0(hs, proj0, yf_prompt, yf_latent, w_out_bf16, h, mod, g_ffn, w_router_t, b_router):
    return pl.pallas_call(
        _post0_kernel,
        out_shape=_EPI_OUT_SHAPES,
        grid=(N // POST0_ROWS,),
        in_specs=[
            pl.BlockSpec((2, POST0_ROWS, LRU_W), lambda r: (0, r, 0)),
            pl.BlockSpec((POST0_ROWS, LRU_W), lambda r: (r, 1)),
            pl.BlockSpec((POST0_ROWS, FN_W), lambda r: (jnp.minimum(r, POST0_P_STEPS - 1), 0)),
            pl.BlockSpec((POST0_ROWS, FN_W), lambda r: (jnp.maximum(r - POST0_P_STEPS, 0), 0)),
            pl.BlockSpec((D, D), lambda r: (0, 0)),
        ] + _epi_in_specs(h, POST0_ROWS),
        out_specs=_epi_out_specs(POST0_ROWS),
        scratch_shapes=_EPI_SCRATCH,
        compiler_params=_cp(("arbitrary",), VMEM_LIMIT),
        name="post_rglru_fourier",
    )(hs, proj0, yf_prompt, yf_latent, w_out_bf16,
      *_epi_operands(h, mod, g_ffn, w_router_t, b_router, POST0_ROWS))


POST1_ROWS = 2 * TILE
POST1_P_STEPS = N_P // POST1_ROWS


def _glu(x):
    return x[:, :CONV_W] * _sigmoid(x[:, CONV_W:])


def _post1_kernel(p_ref, prev_ref, next_ref, lng_ref, lnb_ref, ws_ref, bs_ref, dww_ref, dwb_ref,
                  clg_ref, clb_ref, wo_ref, *rest):
    epilogue_refs, (ext_ref, shift_ref, u_ref, v_ref, mix_ref) = rest[:-5], rest[-5:]
    j = pl.program_id(0)
    is_prompt = j < POST1_P_STEPS
    pos_in_seq = (j - POST1_P_STEPS) % (T_SAMPLE // POST1_ROWS)
    first_starts_seq = is_prompt | (pos_in_seq == 0)
    second_ends_seq = is_prompt | (pos_in_seq == T_SAMPLE // POST1_ROWS - 1)

    for c in range(POST1_ROWS // ROW_PIECE):
        rows = slice(c * ROW_PIECE, (c + 1) * ROW_PIECE)
        z = _gelu(p_ref[rows, 0:2 * SGU_W].astype(F32))
        u_ref[rows, :] = z[:, :SGU_W]
        v_ref[rows, :] = _layernorm(z[:, SGU_W:], lng_ref[...], lnb_ref[...]).astype(BF16)
        half, start = divmod(c * ROW_PIECE, TILE)
        ext_ref[half, CONV_HALO + start:CONV_HALO + start + ROW_PIECE, :] = _glu(p_ref[rows, 2 * SGU_W:].astype(F32))
    ext_ref[0, 0:CONV_HALO, :] = jnp.where(first_starts_seq, 0.0, _glu(prev_ref[...].astype(F32)))
    ext_ref[0, CONV_HALO + TILE:, :] = jnp.where(is_prompt, 0.0, ext_ref[1, CONV_HALO:2 * CONV_HALO, :])
    ext_ref[1, 0:CONV_HALO, :] = jnp.where(is_prompt, 0.0, ext_ref[0, TILE:TILE + CONV_HALO, :])
    ext_ref[1, CONV_HALO + TILE:, :] = jnp.where(second_ends_seq, 0.0, _glu(next_ref[...].astype(F32)))

    for n in range(POST1_ROWS // CHUNK):
        for g in range(SGU_G):
            rows, cols = slice(n * CHUNK, (n + 1) * CHUNK), slice(g * CHUNK, (g + 1) * CHUNK)
            s = _dot(ws_ref[g], v_ref[rows, cols]) + bs_ref[:, cols]
            mix_ref[rows, cols] = (u_ref[rows, cols] * s).astype(BF16)

    n_shift_rows = TILE + 2 * CONV_HALO - 8
    for half in range(POST1_ROWS // TILE):
        for phase in range(8):
            shift_ref[half, phase] = ext_ref[half, phase:phase + n_shift_rows, :]
        for c in range(TILE // CONV_ROWS):
            acc = dwb_ref[...] + jnp.zeros((CONV_ROWS // 8, 8, CONV_W), F32)
            for k in range(CONV_K):
                off = CONV_HALO - CONV_PAD + k
                start = off // 8 * 8 + c * CONV_ROWS
                rows = shift_ref[half, off % 8, start:start + CONV_ROWS, :].reshape(CONV_ROWS // 8, 8, CONV_W)
                acc = acc + rows * dww_ref[k]
            ln = _layernorm(acc.reshape(CONV_ROWS, CONV_W), clg_ref[...], clb_ref[...])
            out_rows = slice(half * TILE + c * CONV_ROWS, half * TILE + (c + 1) * CONV_ROWS)
            mix_ref[out_rows, SGU_W:] = (ln * _sigmoid(ln)).astype(BF16)

    _epilogue(_dot(mix_ref[...], wo_ref[...]), POST1_ROWS, *epilogue_refs)


def _post1(proj1, ln_g, ln_b, ws_bf16, bs_full, dw_w, dw_b, cln_g, cln_b, w_out_bf16,
           h, mod, g_ffn, w_router_t, b_router):
    per_step = POST1_ROWS // CONV_HALO
    n_halo_blocks = N // CONV_HALO
    const2 = lambda r: (0, 0)
    row = lambda a: a.reshape(1, -1)
    return pl.pallas_call(
        _post1_kernel,
        out_shape=_EPI_OUT_SHAPES,
        grid=(N // POST1_ROWS,),
        in_specs=[
            pl.BlockSpec((POST1_ROWS, IN1), lambda r: (r, 0)),
            pl.BlockSpec((CONV_HALO, 2 * CONV_W), lambda r: (jnp.maximum(r * per_step - 1, 0), 1)),
            pl.BlockSpec((CONV_HALO, 2 * CONV_W),
                         lambda r: (jnp.minimum((r + 1) * per_step, n_halo_blocks - 1), 1)),
            pl.BlockSpec((1, SGU_W), const2),
            pl.BlockSpec((1, SGU_W), const2),
            pl.BlockSpec((SGU_G, CHUNK, CHUNK), lambda r: (0, 0, 0)),
            pl.BlockSpec((CHUNK, SGU_W), const2),
            pl.BlockSpec((CONV_K, 8, CONV_W), lambda r: (0, 0, 0)),
            pl.BlockSpec((1, CONV_W), const2),
            pl.BlockSpec((1, CONV_W), const2),
            pl.BlockSpec((1, CONV_W), const2),
            pl.BlockSpec((D, D), const2),
        ] + _epi_in_specs(h, POST1_ROWS),
        out_specs=_epi_out_specs(POST1_ROWS),
        scratch_shapes=_EPI_SCRATCH + [
            pltpu.VMEM((POST1_ROWS // TILE, TILE + 2 * CONV_HALO, CONV_W), F32),
            pltpu.VMEM((POST1_ROWS // TILE, 8, TILE + 2 * CONV_HALO - 8, CONV_W), F32),
            pltpu.VMEM((POST1_ROWS, SGU_W), F32),
            pltpu.VMEM((POST1_ROWS, SGU_W), BF16),
            pltpu.VMEM((POST1_ROWS, D), BF16),
        ],
        compiler_params=_cp(("arbitrary",), VMEM_LIMIT),
        name="post_sgu_conformer",
    )(proj1, proj1, proj1, row(ln_g), row(ln_b), ws_bf16, bs_full,
      jnp.broadcast_to(dw_w[:, None, :], (CONV_K, 8, CONV_W)), row(dw_b), row(cln_g), row(cln_b),
      w_out_bf16, *_epi_operands(h, mod, g_ffn, w_router_t, b_router, POST1_ROWS))


PAD_BASE = N_ASSIGN
PLACEHOLDER_BASE = PAD_BASE + 2 * MOE_BLOCK
Y_ROWS = PLACEHOLDER_BASE + 2 * MOE_BLOCK


def _inverse_kernel(pos_ref, pend_ref, out_ref):
    def fill_block(start):
        def fill(j, carry):
            s = start + j
            out_ref[s] = PAD_BASE + (s & (2 * MOE_BLOCK - 1))
            return carry
        lax.fori_loop(0, MOE_BLOCK, fill, 0, unroll=16)

    def per_expert(e, carry):
        fill_block(jnp.maximum(pend_ref[e] - MOE_BLOCK, 0))
        return carry

    def idle_block(b, carry):
        fill_block(b * MOE_BLOCK)
        return carry

    lax.fori_loop(0, N_EXP, per_expert, 0)
    lax.fori_loop(pend_ref[N_EXP - 1] // MOE_BLOCK, N_BLOCKS, idle_block, 0)

    def body(t, carry):
        for k in range(TOP_K):
            a = t * TOP_K + k
            out_ref[pos_ref[a]] = a
        return carry

    lax.fori_loop(0, N, body, 0, unroll=8)


def _inverse(pos_flat, pad_end):
    smem = pl.BlockSpec(memory_space=pltpu.SMEM)
    return pl.pallas_call(
        _inverse_kernel,
        out_shape=jax.ShapeDtypeStruct((N_SLOTS,), jnp.int32),
        in_specs=[smem, smem],
        out_specs=smem,
        name="moe_inverse",
    )(pos_flat, pad_end)


def _routing_tables(top_e, rank, counts):
    counts = counts.reshape(N_EXP)
    padded = (counts + MOE_BLOCK - 1) // MOE_BLOCK * MOE_BLOCK
    pad_end = jnp.cumsum(padded).astype(jnp.int32)
    pad_start = pad_end - padded
    experts = jnp.arange(N_EXP, dtype=jnp.int32)[:, None, None]
    pos = rank + jnp.sum(jnp.where(top_e[None] == experts, pad_start[:, None, None], 0), axis=0)
    pos = pos.astype(jnp.int32).T.reshape(-1)
    n_used = pad_end[-1] // MOE_BLOCK
    blk = jnp.minimum(jnp.arange(N_BLOCKS, dtype=jnp.int32), n_used - 1)
    block_e = jnp.sum(pad_end[None, :] <= (blk * MOE_BLOCK)[:, None], axis=1).astype(jnp.int32)
    slot_a = _inverse(pos, pad_end)
    tok = jnp.minimum(slot_a >> 2, N - 1)
    gather_key = (tok >> 1) * (2 * PACK_PAIR) + (tok & 1)
    out_row = jnp.where(slot_a >= PAD_BASE, slot_a, (slot_a & (TOP_K - 1)) * N + (slot_a >> 2))
    placeholder_rows = PLACEHOLDER_BASE + jnp.arange(MOE_BLOCK, dtype=jnp.int32)
    out_sub_row = jnp.concatenate([placeholder_rows, out_row]) * PACK_ROWS
    ids = jnp.arange(N_EXP, dtype=jnp.int32)
    later_used = (ids[None, :] > ids[:, None]) & (counts[None, :] > 0)
    next_expert = jnp.min(jnp.where(later_used, ids[None, :], N_EXP), axis=1).astype(jnp.int32)
    return gather_key, out_sub_row, block_e, n_used.reshape(1), next_expert


def _moe_kernel(be_ref, nu_ref, gk_ref, sub_row_ref, nxt_ref, x_hbm, w1_hbm, b1g_ref, b1l_ref, w2_hbm, b2_ref, perm_ref,
                y_hbm, xs, xbuf, w1f, w2f, w1s, w2s, sem, ybuf, osem, *, layer):
    i = pl.program_id(0)
    n_used = nu_ref[0]
    slot = i % 2
    e = be_ref[i]
    e_prev = be_ref[jnp.maximum(i - 1, 0)]
    low_half = lax.broadcasted_iota(jnp.int32, (PACK_PAIR, 128), 0) < PACK_ROWS

    def weight_copies(expert):
        return (pltpu.make_async_copy(w1_hbm.at[layer, expert], w1f, sem.at[1]),
                pltpu.make_async_copy(w2_hbm.at[layer, expert], w2f, sem.at[2]))

    def token_tile(key):
        tile = xs[pl.ds(pl.multiple_of(key >> 1, PACK_PAIR), PACK_PAIR), :]
        return tile, pltpu.roll(tile, PACK_ROWS, 0), key & 1

    def gather_pair(blk, jj, dst_slot):
        t0, r0, half0 = token_tile(gk_ref[blk * MOE_BLOCK + 2 * jj])
        t1, r1, half1 = token_tile(gk_ref[blk * MOE_BLOCK + 2 * jj + 1])
        lower = jnp.where(half0 == 0, t0, r0)
        upper = jnp.where(half1 == 1, t1, r1)
        start = jj * PACK_PAIR if isinstance(jj, int) else pl.multiple_of(jj * PACK_PAIR, PACK_PAIR)
        xbuf[dst_slot, pl.ds(start, PACK_PAIR), :] = jnp.where(low_half, lower, upper)

    def out_copy(src_slot, j, sub_row):
        start = j * PACK_ROWS if isinstance(j, int) else pl.multiple_of(j * PACK_ROWS, PACK_ROWS)
        dst = y_hbm.at[pl.ds(pl.multiple_of(sub_row, PACK_ROWS), PACK_ROWS), :]
        return pltpu.make_async_copy(ybuf.at[src_slot, pl.ds(start, PACK_ROWS), :], dst, osem.at[src_slot])

    def out_buffer_wait(src_slot):
        pltpu.make_async_copy(ybuf.at[src_slot], y_hbm.at[pl.ds(0, MOE_BLOCK * PACK_ROWS), :],
                              osem.at[src_slot]).wait()

    @pl.when(i == 0)
    def _():
        xs_copy = pltpu.make_async_copy(x_hbm, xs, sem.at[0])
        xs_copy.start()
        for c in weight_copies(e):
            c.start()
        ybuf[...] = jnp.zeros_like(ybuf)

        def zero_rows(first_row):
            def body(j, carry):
                out_copy(0, j, (first_row + j) * PACK_ROWS).start()
                return carry
            lax.fori_loop(0, MOE_BLOCK, body, 0, unroll=8)

        zero_rows(PAD_BASE)
        zero_rows(PAD_BASE + MOE_BLOCK)
        out_buffer_wait(0)
        out_buffer_wait(0)
        zero_rows(PLACEHOLDER_BASE + MOE_BLOCK)
        xs_copy.wait()

        def body(jj, carry):
            gather_pair(0, jj, 0)
            return carry
        lax.fori_loop(0, MOE_BLOCK // 2, body, 0, unroll=4)

    @pl.when((i < n_used) & ((i == 0) | (e != e_prev)))
    def _():
        for c in weight_copies(e):
            c.wait()
        for c in range(2 * D_FF // 256):
            wc = w1f[:, c * 256:(c + 1) * 256].astype(BF16)
            w1s[:, c * 256:(c + 1) * 256] = _dot(wc, perm_ref[...]).astype(BF16)
        w2s[...] = w2f[...].astype(BF16)
        e_next = nxt_ref[e]

        @pl.when(e_next < N_EXP)
        def _():
            for c in weight_copies(e_next):
                c.start()

    def block(prefetch_next):
        for j in range(MOE_BLOCK):
            out_copy(1 - slot, j, sub_row_ref[i * MOE_BLOCK + j]).start()
        halves = [[], []]
        for q in range(PACK_ROWS):
            hi, lo = _unpack_bf16_pairs(xbuf[slot, pl.ds(q, MOE_BLOCK, stride=PACK_ROWS), :])
            halves[0].append(hi.astype(BF16))
            halves[1].append(lo.astype(BF16))
        x = jnp.concatenate(halves[0] + halves[1], axis=1)
        if prefetch_next:
            for jj in range(MOE_BLOCK // 2):
                gather_pair(i + 1, jj, 1 - slot)
        hid = _dot(x, w1s[...])
        acts = []
        for c in range(D_FF // 128):
            h_glu = hid[:, c * 256:c * 256 + 128] + b1g_ref[:, c * 128:(c + 1) * 128]
            h_lin = hid[:, c * 256 + 128:(c + 1) * 256] + b1l_ref[:, c * 128:(c + 1) * 128]
            h_glu = jnp.minimum(h_glu, LIMIT)
            h_lin = jnp.clip(h_lin, -LIMIT, LIMIT)
            acts.append((h_glu * _sigmoid(ALPHA * h_glu) * (h_lin + 1.0)).astype(BF16))
        y = _dot(jnp.concatenate(acts, axis=1), w2s[...]) + b2_ref[...]
        out_buffer_wait(slot)
        packed = _pack_bf16_pairs(y)
        for q in range(PACK_ROWS):
            ybuf[slot, pl.ds(q, MOE_BLOCK, stride=PACK_ROWS), :] = packed[:, q * 128:(q + 1) * 128]
        if not prefetch_next:
            def send(j, carry):
                out_copy(slot, j, sub_row_ref[(i + 1) * MOE_BLOCK + j]).start()
                return carry
            lax.fori_loop(0, MOE_BLOCK, send, 0, unroll=8)
            out_buffer_wait(1 - slot)
            out_buffer_wait(slot)

    @pl.when(i + 1 < n_used)
    def _():
        block(True)

    @pl.when(i + 1 == n_used)
    def _():
        block(False)


def _deinterleave_matrix():
    p = np.zeros((256, 256), np.float32)
    m = np.arange(128)
    p[2 * m, m] = 1.0
    p[2 * m + 1, 128 + m] = 1.0
    return jnp.asarray(p, BF16)


def _moe_experts(layer, fn_packed, gather_key, out_sub_row, block_e, n_used, next_expert, w1, b1, w2, b2):
    b1g = b1[0][layer].reshape(N_EXP, 1, D_FF)
    b1l = b1[1][layer].reshape(N_EXP, 1, D_FF)
    ex = lambda i, be, *_: (be[i], 0, 0)
    grid_spec = pltpu.PrefetchScalarGridSpec(
        num_scalar_prefetch=5,
        grid=(N_BLOCKS,),
        in_specs=[
            pl.BlockSpec(memory_space=pl.ANY),
            pl.BlockSpec(memory_space=pl.ANY),
            pl.BlockSpec((None, 1, D_FF), ex),
            pl.BlockSpec((None, 1, D_FF), ex),
            pl.BlockSpec(memory_space=pl.ANY),
            pl.BlockSpec((None, 1, D), ex),
            pl.BlockSpec((256, 256), lambda i, *_: (0, 0)),
        ],
        out_specs=pl.BlockSpec(memory_space=pl.ANY),
        scratch_shapes=[
            pltpu.VMEM((N * PACK_ROWS, 128), jnp.uint32),
            pltpu.VMEM((2, MOE_BLOCK * PACK_ROWS, 128), jnp.uint32),
            pltpu.VMEM((D, 2 * D_FF), F32),
            pltpu.VMEM((D_FF, D), F32),
            pltpu.VMEM((D, 2 * D_FF), BF16),
            pltpu.VMEM((D_FF, D), BF16),
            pltpu.SemaphoreType.DMA((3,)),
            pltpu.VMEM((2, MOE_BLOCK * PACK_ROWS, 128), jnp.uint32),
            pltpu.SemaphoreType.DMA((2,)),
        ],
    )
    return pl.pallas_call(
        functools.partial(_moe_kernel, layer=layer),
        out_shape=jax.ShapeDtypeStruct((Y_ROWS * PACK_ROWS, 128), jnp.uint32),
        grid_spec=grid_spec,
        compiler_params=_cp(("arbitrary",), MOE_VMEM_LIMIT),
        name="moe_experts",
    )(block_e, n_used, gather_key, out_sub_row, next_expert, fn_packed, w1, b1g, b1l, w2, b2[layer].reshape(N_EXP, 1, D),
      _deinterleave_matrix())


def _moe_residual(y_refs, g_ref, h_ref, m_ref, rows):
    g = g_ref[...]
    halves = [[], []]
    for q in range(PACK_ROWS):
        acc_hi = acc_lo = None
        for k in range(TOP_K):
            hi, lo = _unpack_bf16_pairs(y_refs[k][pl.ds(q, rows, stride=PACK_ROWS), :])
            gk = g[:, k:k + 1]
            acc_hi = hi * gk if acc_hi is None else acc_hi + hi * gk
            acc_lo = lo * gk if acc_lo is None else acc_lo + lo * gk
        halves[0].append(acc_hi)
        halves[1].append(acc_lo)
    return h_ref[...] + m_ref[5:6, :] * jnp.concatenate(halves[0] + halves[1], axis=1)


def _moe_stream_specs(rows, mod_row):
    steps = N // rows
    return [pl.BlockSpec((rows * PACK_ROWS, 128), lambda i, k=k: (k * steps + i, 0)) for k in range(TOP_K)] + [
        pl.BlockSpec((rows, TOP_K), lambda i: (i, 0)),
        pl.BlockSpec((rows, D), lambda i: (i, 0)),
        pl.BlockSpec((None, 6, D), lambda i: (mod_row(i), 0, 0)),
    ]


def _combine_final_kernel(y0_ref, y1_ref, y2_ref, y3_ref, g_ref, h_ref, m_ref, fg_ref, yp_ref, yl_ref):
    i = pl.program_id(0)
    h_new = _moe_residual((y0_ref, y1_ref, y2_ref, y3_ref), g_ref, h_ref, m_ref, TILE)
    ms = jnp.mean(h_new * h_new, axis=-1, keepdims=True)
    y = h_new * lax.rsqrt(ms + EPS) * fg_ref[...]

    @pl.when(i < P_TILES)
    def _():
        yp_ref[...] = y

    @pl.when(i >= P_TILES)
    def _():
        yl_ref[...] = y


def _moe_combine_final(y_rows, gates_nt, h, mod, final_g):
    return pl.pallas_call(
        _combine_final_kernel,
        out_shape=(jax.ShapeDtypeStruct((N_P, D), F32), jax.ShapeDtypeStruct((N_S, D), F32)),
        grid=(N_TILES,),
        in_specs=_moe_stream_specs(TILE, _tile_mod_row) + [pl.BlockSpec((1, D), lambda i: (0, 0))],
        out_specs=[
            pl.BlockSpec((TILE, D), lambda i: (jnp.minimum(i, P_TILES - 1), 0)),
            pl.BlockSpec((TILE, D), lambda i: (jnp.maximum(i - P_TILES, 0), 0)),
        ],
        compiler_params=_cp(("arbitrary",), VMEM_LIMIT),
        name="moe_combine_final",
    )(y_rows, y_rows, y_rows, y_rows, gates_nt, h, mod, final_g.reshape(1, D))


COMBINE_ROWS = 512


def _combine_inproj_kernel(y0_ref, y1_ref, y2_ref, y3_ref, g_ref, h_ref, m_ref, gmix_ref, mnext_ref, w_ref,
                           hout_ref, proj_ref):
    h_new = _moe_residual((y0_ref, y1_ref, y2_ref, y3_ref), g_ref, h_ref, m_ref, COMBINE_ROWS)
    hout_ref[...] = h_new
    hn = _rms_mod(h_new, gmix_ref[...], mnext_ref[0:1, :], mnext_ref[1:2, :])
    proj_ref[...] = _dot(hn.astype(BF16), w_ref[...]).astype(BF16)


def _moe_combine_inproj(y_rows, gates_nt, h, mod, g_mix_next, mod_next, w_next_bf16):
    n_out = w_next_bf16.shape[1]
    p_steps, s_steps = N_P // COMBINE_ROWS, T_SAMPLE // COMBINE_ROWS

    def mod_row(i):
        return jnp.where(i < p_steps, 0, 1 + (i - p_steps) // s_steps)

    return pl.pallas_call(
        _combine_inproj_kernel,
        out_shape=(jax.ShapeDtypeStruct((N, D), F32), jax.ShapeDtypeStruct((N, n_out), BF16)),
        grid=(N // COMBINE_ROWS,),
        in_specs=_moe_stream_specs(COMBINE_ROWS, mod_row) + [
            pl.BlockSpec((1, D), lambda i: (0, 0)),
            pl.BlockSpec((None, 6, D), lambda i: (mod_row(i), 0, 0)),
            pl.BlockSpec((D, n_out), lambda i: (0, 0)),
        ],
        out_specs=[
            pl.BlockSpec((COMBINE_ROWS, D), lambda i: (i, 0)),
            pl.BlockSpec((COMBINE_ROWS, n_out), lambda i: (i, 0)),
        ],
        compiler_params=_cp(("arbitrary",), VMEM_LIMIT),
        name="moe_combine_inproj",
    )(y_rows, y_rows, y_rows, y_rows, gates_nt, h, mod, g_mix_next.reshape(1, D), mod_next, w_next_bf16)


def _moe_expert_rows(layer, routed, w1, b1, w2, b2):
    fn_packed, top_e, gates, rank, counts = routed
    gather_key, out_sub_row, block_e, n_used, next_expert = _routing_tables(top_e, rank, counts)
    return _moe_experts(layer, fn_packed, gather_key, out_sub_row, block_e, n_used, next_expert, w1, b1, w2, b2)


def _gate_weights(w_r, w_i):
    per = LRU_BLK // LRU_HD
    eye = jnp.eye(per, dtype=F32)

    def blockdiag(w):
        w = w.reshape(2, LRU_HEADS // per, per, LRU_HD, LRU_HD)
        full = jnp.einsum("dgaij,ab->dgaibj", w, eye)
        return full.reshape(2, LRU_HEADS // per, LRU_BLK, LRU_BLK)

    return jnp.concatenate([blockdiag(w_r), blockdiag(w_i)], axis=-1).astype(BF16)


def kernel(x_prompt, x_sample, state_rglru, c, c_ctx, norm_mix_g, norm_ffn_g, w_mod, b_mod, w_in0, lru_conv_w, lru_conv_b, lru_w_r, lru_b_r, lru_w_i, lru_b_i, lru_lambda, w_out0, w_in1, sgu_ln_g, sgu_ln_b, sgu_w_s, sgu_b_s, conv_dw_w, conv_dw_b, conv_ln_g, conv_ln_b, w_out1, w_router, b_router, w1, b1, w2, b2, final_norm_g):
    h = (x_prompt.reshape(N_P, D), x_sample.reshape(N_S, D))
    cond8 = jnp.concatenate([c_ctx[None, :], c, jnp.zeros((N_MOD - 1 - N_SAMPLE_SEQ, D), F32)], axis=0)
    b1 = (b1[:, :, 0::2], b1[:, :, 1::2])
    mod = _adaln(cond8, w_mod, b_mod)

    proj0 = _inproj(h, norm_mix_g[0], mod[0], w_in0[0].astype(BF16))
    st = state_rglru[:, 0].astype(F32)
    h0 = jnp.zeros((2, N_MOD, LRU_W), F32).at[:, 1:1 + N_SAMPLE_SEQ].set(jnp.swapaxes(st, 0, 1))
    hs, ctx_state = _lru(proj0, lru_conv_w[0], lru_conv_b[0], _gate_weights(lru_w_r[0], lru_w_i[0]),
              lru_b_r[0], lru_b_i[0], lru_lambda[0], h0)
    h, *routed = _post0(hs, proj0, _fourier_prompt(proj0), _fourier_sample(proj0), w_out0[0].astype(BF16),
                        h, mod[0], norm_ffn_g[0], w_router[0].T, b_router[0])
    y_rows = _moe_expert_rows(0, routed, w1, b1, w2, b2)
    gates = routed[2]

    h, proj1 = _moe_combine_inproj(y_rows, gates.T, h, mod[0], norm_mix_g[1], mod[1], w_in1[0].astype(BF16))
    bs_full = jnp.repeat(sgu_b_s[0].T, CHUNK, axis=1)
    h, *routed = _post1(proj1, sgu_ln_g[0], sgu_ln_b[0], sgu_w_s[0].astype(BF16), bs_full,
                        conv_dw_w[0], conv_dw_b[0], conv_ln_g[0], conv_ln_b[0],
                        w_out1[0].astype(BF16), h, mod[1], norm_ffn_g[1], w_router[1].T, b_router[1])
    y_rows = _moe_expert_rows(1, routed, w1, b1, w2, b2)
    y_p, y_l = _moe_combine_final(y_rows, routed[2].T, h, mod[1], final_norm_g)

    y_prompt = y_p.reshape(N_PROMPT_SEQ, T_PROMPT, D)
    y_sample = y_l.reshape(N_SAMPLE_SEQ, T_SAMPLE, D)
    new_state = jnp.transpose(ctx_state, (1, 2, 0, 3))
    return (y_prompt, y_sample, new_state.astype(x_prompt.dtype))
```
